```python
import math
import jax, jax.numpy as jnp
from jax import lax
import numpy as np

D_MODEL = 1024
BATCH = 8
SEQ = 4096
DEPTH = 2

N_MIXERS = 2
N_Q_HEADS = 16
N_KV_HEADS = 2
HEAD_DIM = 64
QKV_DIM = (N_Q_HEADS + 2 * N_KV_HEADS) * HEAD_DIM
WINDOW = 128
ATTN_BLOCK = 128
N_REL_BUCKETS = 32
REL_MAX_DISTANCE = 128
CONV_DIM = D_MODEL
CONV_WIDTH = 3
N_GROUPS = 4
EXPERTS_PER_GROUP = 8
N_EXPERTS = N_GROUPS * EXPERTS_PER_GROUP
TOP_K_IN_GROUP = 2
EXPERT_FF = 512
MOE_BLOCK = 128
RMS_EPS = 1e-5
N_ATTN_LAYERS = (DEPTH + 1) // 2
N_CONV_LAYERS = DEPTH // 2

kernel_name = "hybrid_swa_shortconv_hmoe"


def rms_norm(x, g):
    xf = x.astype(jnp.float32)
    y = xf * lax.rsqrt(jnp.mean(xf * xf, axis=-1, keepdims=True) + RMS_EPS)
    return (y * g.astype(jnp.float32)).astype(x.dtype)


def t5_causal_bucket(dist):
    max_exact = N_REL_BUCKETS // 2
    d = jnp.maximum(dist, max_exact).astype(jnp.float32)
    large = max_exact + (jnp.log(d / max_exact) / math.log(REL_MAX_DISTANCE / max_exact)
                         * (N_REL_BUCKETS - max_exact)).astype(jnp.int32)
    large = jnp.minimum(large, N_REL_BUCKETS - 1)
    return jnp.where(dist < max_exact, dist, large)


def sliding_window_attention(h, w_qkv, b_qkv, w_o, b_o, sinks, rel_bias):
    B, S, _ = h.shape
    G = N_Q_HEADS // N_KV_HEADS
    qkv = h @ w_qkv + b_qkv
    q, k, v = jnp.split(qkv, [N_Q_HEADS * HEAD_DIM, (N_Q_HEADS + N_KV_HEADS) * HEAD_DIM], axis=-1)
    q = q.reshape(B, S, N_KV_HEADS, G, HEAD_DIM) * (HEAD_DIM ** -0.5)
    k = k.reshape(B, S, N_KV_HEADS, HEAD_DIM)
    v = v.reshape(B, S, N_KV_HEADS, HEAD_DIM)
    pad = jnp.zeros((B, ATTN_BLOCK, N_KV_HEADS, HEAD_DIM), h.dtype)
    k_pad = jnp.concatenate([pad, k], axis=1)
    v_pad = jnp.concatenate([pad, v], axis=1)
    n_blocks = S // ATTN_BLOCK
    q_blocks = q.reshape(B, n_blocks, ATTN_BLOCK, N_KV_HEADS, G, HEAD_DIM).transpose(1, 0, 2, 3, 4, 5)
    qi = jnp.arange(ATTN_BLOCK)[:, None]
    kj = jnp.arange(2 * ATTN_BLOCK)[None, :]
    dist = qi + ATTN_BLOCK - kj
    in_window = (dist >= 0) & (dist < WINDOW)
    bucket = t5_causal_bucket(jnp.maximum(dist, 0))
    bias = rel_bias[bucket].astype(jnp.float32)
    bias = bias.transpose(2, 0, 1).reshape(N_KV_HEADS, G, ATTN_BLOCK, 2 * ATTN_BLOCK)
    sink = sinks.astype(jnp.float32).reshape(N_KV_HEADS, G)[:, :, None, None]

    def one_block(args):
        n, qb = args
        kb = lax.dynamic_slice_in_dim(k_pad, n * ATTN_BLOCK, 2 * ATTN_BLOCK, axis=1)
        vb = lax.dynamic_slice_in_dim(v_pad, n * ATTN_BLOCK, 2 * ATTN_BLOCK, axis=1)
        s = jnp.einsum('bqhgd,bkhd->bhgqk', qb, kb).astype(jnp.float32) + bias
        valid = in_window & ((n - 1) * ATTN_BLOCK + kj >= 0)
        s = jnp.where(valid, s, -1e30)
        m = jnp.maximum(jnp.max(s, axis=-1, keepdims=True), sink)
        p = jnp.exp(s - m)
        denom = jnp.sum(p, axis=-1, keepdims=True) + jnp.exp(sink - m)
        p = (p / denom).astype(vb.dtype)
        return jnp.einsum('bhgqk,bkhd->bqhgd', p, vb)

    o = lax.map(one_block, (jnp.arange(n_blocks), q_blocks))
    o = o.transpose(1, 0, 2, 3, 4, 5).reshape(B, S, N_Q_HEADS * HEAD_DIM)
    return o @ w_o + b_o


def gated_short_conv(h, w_in, conv_w, w_out):
    S = h.shape[1]
    b_gate, c_gate, u = jnp.split(h @ w_in, 3, axis=-1)
    z = c_gate * u
    zp = jnp.pad(z, ((0, 0), (CONV_WIDTH - 1, 0), (0, 0)))
    conv = zp[:, 0:S] * conv_w[0]
    for j in range(1, CONV_WIDTH):
        conv = conv + zp[:, j:j + S] * conv_w[j]
    return (b_gate * conv) @ w_out


def hierarchical_moe(h, w_group, b_group, w_expert, b_expert, w_gate_up, w_down):
    B, S, D = h.shape
    T = B * S
    xt = h.reshape(T, D)
    group_probs = jax.nn.softmax((xt @ w_group + b_group).astype(jnp.float32), axis=-1)
    g_prob, g_idx = lax.top_k(group_probs, 1)
    exp_logits = (xt @ w_expert + b_expert).astype(jnp.float32).reshape(T, N_GROUPS, EXPERTS_PER_GROUP)
    sel = jnp.broadcast_to(g_idx[:, :, None], (T, 1, EXPERTS_PER_GROUP))
    in_group = jnp.take_along_axis(exp_logits, sel, axis=1)[:, 0]
    e_prob, e_local = lax.top_k(jax.nn.softmax(in_group, axis=-1), TOP_K_IN_GROUP)
    e_prob = e_prob / jnp.sum(e_prob, axis=-1, keepdims=True)
    gates = g_prob * e_prob
    expert_ids = g_idx * EXPERTS_PER_GROUP + e_local

    N = T * TOP_K_IN_GROUP
    flat_e = expert_ids.reshape(N)
    flat_tok = jnp.repeat(jnp.arange(T, dtype=jnp.int32), TOP_K_IN_GROUP)
    flat_gate = gates.reshape(N)
    order = jnp.argsort(flat_e)
    sorted_e = flat_e[order]
    counts = jnp.zeros((N_EXPERTS,), jnp.int32).at[flat_e].add(1)
    padded = (counts + MOE_BLOCK - 1) // MOE_BLOCK * MOE_BLOCK
    start = jnp.cumsum(counts) - counts
    pend = jnp.cumsum(padded)
    pstart = pend - padded
    dest = pstart[sorted_e] + jnp.arange(N, dtype=jnp.int32) - start[sorted_e]
    capacity = -(-N // MOE_BLOCK) * MOE_BLOCK + N_EXPERTS * MOE_BLOCK
    slot_tok = jnp.full((capacity,), T, jnp.int32).at[dest].set(flat_tok[order])
    slot_gate = jnp.zeros((capacity,), jnp.float32).at[dest].set(flat_gate[order])
    n_blk = capacity // MOE_BLOCK
    blk_start = jnp.arange(n_blk, dtype=jnp.int32) * MOE_BLOCK
    blk_expert = jnp.minimum(jnp.searchsorted(pend, blk_start, side='right'), N_EXPERTS - 1)
    x_pad = jnp.concatenate([xt, jnp.zeros((1, D), xt.dtype)], axis=0)
    xs = x_pad[slot_tok].reshape(n_blk, MOE_BLOCK, D)

    def expert_block(args):
        xb, e = args
        g, u = jnp.split(xb @ w_gate_up[e], 2, axis=-1)
        return (jax.nn.silu(g) * u) @ w_down[e]

    ys = lax.map(expert_block, (xs, blk_expert)).reshape(capacity, D)
    out = jax.ops.segment_sum(ys * slot_gate[:, None].astype(ys.dtype), slot_tok, num_segments=T + 1)[:T]
    return out.reshape(B, S, D)


def setup_inputs(seed: int = 0) -> dict:
    key = jax.random.key(seed)
    ks = jax.random.split(key, 24)
    f32 = jnp.float32
    nrm = lambda k, shape, s: (jax.random.normal(k, shape, f32) * s)
    D = D_MODEL
    return {
        "x": nrm(ks[0], (BATCH, SEQ, D), 1.0),
        "norm_mix": 1.0 + nrm(ks[1], (DEPTH, D), 0.02),
        "norm_ffn": 1.0 + nrm(ks[2], (DEPTH, D), 0.02),
        "final_norm": 1.0 + nrm(ks[3], (D,), 0.02),
        "rel_bias": nrm(ks[4], (N_REL_BUCKETS, N_Q_HEADS), 0.5),
        "attn_w_qkv": nrm(ks[5], (N_ATTN_LAYERS, D, QKV_DIM), D ** -0.5),
        "attn_b_qkv": nrm(ks[6], (N_ATTN_LAYERS, QKV_DIM), 0.02),
        "attn_w_o": nrm(ks[7], (N_ATTN_LAYERS, N_Q_HEADS * HEAD_DIM, D), (N_Q_HEADS * HEAD_DIM) ** -0.5),
        "attn_b_o": nrm(ks[8], (N_ATTN_LAYERS, D), 0.02),
        "attn_sinks": nrm(ks[9], (N_ATTN_LAYERS, N_Q_HEADS), 1.0),
        "conv_w_in": nrm(ks[10], (N_CONV_LAYERS, D, 3 * CONV_DIM), D ** -0.5),
        "conv_w": nrm(ks[11], (N_CONV_LAYERS, CONV_WIDTH, CONV_DIM), CONV_WIDTH ** -0.5),
        "conv_w_out": nrm(ks[12], (N_CONV_LAYERS, CONV_DIM, D), CONV_DIM ** -0.5),
        "moe_w_group": nrm(ks[13], (DEPTH, D, N_GROUPS), D ** -0.5),
        "moe_b_group": nrm(ks[14], (DEPTH, N_GROUPS), 0.01),
        "moe_w_expert": nrm(ks[15], (DEPTH, D, N_EXPERTS), D ** -0.5),
        "moe_b_expert": nrm(ks[16], (DEPTH, N_EXPERTS), 0.01),
        "moe_w_gate_up": nrm(ks[17], (DEPTH, N_EXPERTS, D, 2 * EXPERT_FF), D ** -0.5),
        "moe_w_down": nrm(ks[18], (DEPTH, N_EXPERTS, EXPERT_FF, D), EXPERT_FF ** -0.5),
    }


def reference(x, norm_mix, norm_ffn, final_norm, rel_bias, attn_w_qkv, attn_b_qkv, attn_w_o,
              attn_b_o, attn_sinks, conv_w_in, conv_w, conv_w_out, moe_w_group, moe_b_group,
              moe_w_expert, moe_b_expert, moe_w_gate_up, moe_w_down):
    h = x
    for i in range(DEPTH):
        xn = rms_norm(h, norm_mix[i])
        if i % N_MIXERS == 0:
            a = i // N_MIXERS
            mix = sliding_window_attention(xn, attn_w_qkv[a], attn_b_qkv[a], attn_w_o[a],
                                           attn_b_o[a], attn_sinks[a], rel_bias)
        else:
            c = i // N_MIXERS
            mix = gated_short_conv(xn, conv_w_in[c], conv_w[c], conv_w_out[c])
        h = h + mix
        h = h + hierarchical_moe(rms_norm(h, norm_ffn[i]), moe_w_group[i], moe_b_group[i],
                                 moe_w_expert[i], moe_b_expert[i], moe_w_gate_up[i], moe_w_down[i])
    return rms_norm(h, final_norm)
```

```python
import functools
import math

import numpy as np
import jax
import jax.numpy as jnp
from jax import lax
from jax.experimental import pallas as pl
from jax.experimental.pallas import tpu as pltpu

D_MODEL = 1024
N_Q_HEADS = 16
N_KV_HEADS = 2
HEAD_DIM = 64
GROUP = N_Q_HEADS // N_KV_HEADS
WINDOW = 128
ATTN_BLOCK = 128
N_REL_BUCKETS = 32
REL_MAX_DISTANCE = 128
CONV_WIDTH = 3
N_GROUPS = 4
EXPERTS_PER_GROUP = 8
N_EXPERTS = N_GROUPS * EXPERTS_PER_GROUP
EXPERT_FF = 512
RMS_EPS = 1e-5

LANES = 128
SUBLANES = 8
ROW_CHUNKS = D_MODEL // LANES
VMEM_LIMIT = 56 * 1024 * 1024

TM_DENSE = 512
TQ = 512
TM_MIX1 = 256
TM_EXPERT = 256
TM_DISPATCH = 2048
TM_COMBINE = 256
ROUTER_ROWS = 40


def _rms(x, g):
    return x * lax.rsqrt(jnp.mean(x * x, axis=-1, keepdims=True) + RMS_EPS) * g


def _params(sem):
    return pltpu.CompilerParams(dimension_semantics=sem, vmem_limit_bytes=VMEM_LIMIT)


def _qkv_kernel(x_ref, g_ref, w_ref, b_ref, q_ref, kt_ref, v_ref):
    xn = _rms(x_ref[...], g_ref[...]).astype(jnp.bfloat16)
    out = jnp.dot(xn, w_ref[...], preferred_element_type=jnp.float32) + b_ref[...]
    nq = N_Q_HEADS * HEAD_DIM
    nk = N_KV_HEADS * HEAD_DIM
    q_ref[...] = out[:, :nq].astype(jnp.bfloat16)
    kt_ref[...] = out[:, nq:nq + nk].T.astype(jnp.bfloat16)
    v_ref[...] = out[:, nq + nk:].astype(jnp.bfloat16)


def _qkv_call(x2, g, w, b):
    T = x2.shape[0]
    nq = N_Q_HEADS * HEAD_DIM
    nk = N_KV_HEADS * HEAD_DIM
    tm = TM_DENSE
    return pl.pallas_call(
        _qkv_kernel,
        grid=(T // tm,),
        in_specs=[
            pl.BlockSpec((tm, D_MODEL), lambda i: (i, 0)),
            pl.BlockSpec((1, D_MODEL), lambda i: (0, 0)),
            pl.BlockSpec((D_MODEL, nq + 2 * nk), lambda i: (0, 0)),
            pl.BlockSpec((1, nq + 2 * nk), lambda i: (0, 0)),
        ],
        out_specs=[
            pl.BlockSpec((tm, nq), lambda i: (i, 0)),
            pl.BlockSpec((nk, tm), lambda i: (0, i)),
            pl.BlockSpec((tm, nk), lambda i: (i, 0)),
        ],
        out_shape=[
            jax.ShapeDtypeStruct((T, nq), jnp.bfloat16),
            jax.ShapeDtypeStruct((nk, T), jnp.bfloat16),
            jax.ShapeDtypeStruct((T, nk), jnp.bfloat16),
        ],
        compiler_params=_params(("parallel",)),
        name="qkv_proj",
    )(x2, g, w, b)


def _attn_kernel(sink_ref, q_ref, kt_ref, ktp_ref, v_ref, vp_ref, bias_ref, o_ref):
    first_tile = pl.program_id(1) == 0
    nblk = TQ // ATTN_BLOCK
    qi_io = lax.broadcasted_iota(jnp.int32, (ATTN_BLOCK, 2 * ATTN_BLOCK), 0)
    kj_io = lax.broadcasted_iota(jnp.int32, (ATTN_BLOCK, 2 * ATTN_BLOCK), 1)
    dist = qi_io + ATTN_BLOCK - kj_io
    band = (dist >= 0) & (dist < WINDOW)
    low_half = lax.broadcasted_iota(jnp.int32, (ATTN_BLOCK, LANES), 1) < HEAD_DIM
    for qi in range(nblk):
        r0 = qi * ATTN_BLOCK
        if qi == 0:
            kt_blk = jnp.concatenate([ktp_ref[...], kt_ref[:, 0:ATTN_BLOCK]], axis=1)
            v_blk = jnp.concatenate([vp_ref[...], v_ref[0:ATTN_BLOCK, :]], axis=0)
            mask = band & (jnp.logical_not(first_tile) | (kj_io >= ATTN_BLOCK))
        else:
            kt_blk = kt_ref[:, r0 - ATTN_BLOCK:r0 + ATTN_BLOCK]
            v_blk = v_ref[r0 - ATTN_BLOCK:r0 + ATTN_BLOCK, :]
            mask = band
        for m in range(GROUP):
            qg = q_ref[r0:r0 + ATTN_BLOCK, m * LANES:(m + 1) * LANES]
            halves = []
            for half in range(N_KV_HEADS):
                h = m + GROUP * half
                keep = low_half if half == 0 else jnp.logical_not(low_half)
                qpad = jnp.where(keep, qg, jnp.zeros_like(qg))
                s = jnp.dot(qpad, kt_blk, preferred_element_type=jnp.float32)
                s = jnp.where(mask, s + bias_ref[h], -1e30)
                sink = sink_ref[h]
                mx = jnp.maximum(jnp.max(s, axis=-1, keepdims=True), sink)
                p = jnp.exp(s - mx)
                den = jnp.sum(p, axis=-1, keepdims=True) + jnp.exp(sink - mx)
                pv = jnp.dot(p.astype(jnp.bfloat16), v_blk, preferred_element_type=jnp.float32)
                halves.append(pv * (1.0 / den))
            og = jnp.where(low_half, halves[0], halves[1])
            o_ref[r0:r0 + ATTN_BLOCK, m * LANES:(m + 1) * LANES] = og.astype(jnp.bfloat16)


def _attn_call(sinks, q, kt, v, bias, batch, seq):
    T = q.shape[0]
    nq = N_Q_HEADS * HEAD_DIM
    nk = N_KV_HEADS * HEAD_DIM
    tiles = seq // TQ
    per = TQ // ATTN_BLOCK

    def cur(b, j):
        return b * tiles + j

    def prev(b, j):
        return jnp.maximum((b * tiles + j) * per - 1, b * tiles * per)

    return pl.pallas_call(
        _attn_kernel,
        grid=(batch, tiles),
        in_specs=[
            pl.BlockSpec(memory_space=pltpu.SMEM),
            pl.BlockSpec((TQ, nq), lambda b, j: (cur(b, j), 0)),
            pl.BlockSpec((nk, TQ), lambda b, j: (0, cur(b, j))),
            pl.BlockSpec((nk, ATTN_BLOCK), lambda b, j: (0, prev(b, j))),
            pl.BlockSpec((TQ, nk), lambda b, j: (cur(b, j), 0)),
            pl.BlockSpec((ATTN_BLOCK, nk), lambda b, j: (prev(b, j), 0)),
            pl.BlockSpec((N_Q_HEADS, ATTN_BLOCK, 2 * ATTN_BLOCK), lambda b, j: (0, 0, 0)),
        ],
        out_specs=pl.BlockSpec((TQ, nq), lambda b, j: (cur(b, j), 0)),
        out_shape=jax.ShapeDtypeStruct((T, nq), jnp.bfloat16),
        compiler_params=_params(("parallel", "parallel")),
        name="swa_attention",
    )(sinks, q, kt, kt, v, v, bias)


def _router_epilogue(h, gffn_ref, wr_ref, brt_ref, carry_ref,
                     h_out_ref, xrows_ref, ri_ref, rf_ref, cnt_ref):
    tm = h.shape[0]
    h_out_ref[...] = h
    xn = _rms(h, gffn_ref[...])
    for c in range(ROW_CHUNKS):
        xrows_ref[pl.ds(c, tm, stride=ROW_CHUNKS), :] = xn[:, c * LANES:(c + 1) * LANES]

    logits = jnp.dot(xn, wr_ref[...], preferred_element_type=jnp.float32,
                     precision=lax.Precision.HIGHEST)
    lt = logits.T[:ROUTER_ROWS, :] + brt_ref[...]

    gl = [lt[g:g + 1, :] for g in range(N_GROUPS)]
    gmax = functools.reduce(jnp.maximum, gl)
    gexp = [jnp.exp(x - gmax) for x in gl]
    gsum = functools.reduce(lambda a, b: a + b, gexp)
    gprob = [x / gsum for x in gexp]
    g_prob = functools.reduce(jnp.maximum, gprob)
    g_idx = jnp.full(g_prob.shape, N_GROUPS - 1, jnp.int32)
    for g in range(N_GROUPS - 2, -1, -1):
        g_idx = jnp.where(gprob[g] == g_prob, g, g_idx)

    el = []
    for j in range(EXPERTS_PER_GROUP):
        x = lt[SUBLANES + j:SUBLANES + j + 1, :]
        for g in range(1, N_GROUPS):
            r = SUBLANES + g * EXPERTS_PER_GROUP + j
            x = jnp.where(g_idx == g, lt[r:r + 1, :], x)
        el.append(x)
    emax = functools.reduce(jnp.maximum, el)
    eexp = [jnp.exp(x - emax) for x in el]
    esum = functools.reduce(lambda a, b: a + b, eexp)
    eprob = [x / esum for x in eexp]
    p1 = functools.reduce(jnp.maximum, eprob)
    i1 = jnp.full(p1.shape, EXPERTS_PER_GROUP - 1, jnp.int32)
    for j in range(EXPERTS_PER_GROUP - 2, -1, -1):
        i1 = jnp.where(eprob[j] == p1, j, i1)
    rest = [jnp.where(i1 == j, -1.0, eprob[j]) for j in range(EXPERTS_PER_GROUP)]
    p2 = functools.reduce(jnp.maximum, rest)
    i2 = jnp.full(p2.shape, EXPERTS_PER_GROUP - 1, jnp.int32)
    for j in range(EXPERTS_PER_GROUP - 2, -1, -1):
        i2 = jnp.where(rest[j] == p2, j, i2)
    psum = p1 + p2
    gate0 = g_prob * (p1 / psum)
    gate1 = g_prob * (p2 / psum)
    e0 = g_idx * EXPERTS_PER_GROUP + i1
    e1 = g_idx * EXPERTS_PER_GROUP + i2

    eio = lax.broadcasted_iota(jnp.int32, (N_EXPERTS, tm), 0)
    oh0 = (eio == e0).astype(jnp.float32)
    oh1 = (eio == e1).astype(jnp.float32)
    both = oh0 + oh1
    tr = lax.broadcasted_iota(jnp.int32, (tm, tm), 0)
    tc = lax.broadcasted_iota(jnp.int32, (tm, tm), 1)
    upper = (tr < tc).astype(jnp.bfloat16)
    before = jnp.dot(both.astype(jnp.bfloat16), upper, preferred_element_type=jnp.float32)
    before = before + carry_ref[...]
    rank0 = jnp.sum(oh0 * before, axis=0, keepdims=True)
    rank1 = jnp.sum(oh1 * (before + oh0), axis=0, keepdims=True)
    carry_ref[...] = carry_ref[...] + jnp.sum(both, axis=1, keepdims=True)
    cnt_ref[...] = jnp.broadcast_to(carry_ref[...], cnt_ref.shape)

    zi = jnp.zeros_like(e0)
    ri_ref[...] = jnp.concatenate(
        [e0, e1, rank0.astype(jnp.int32), rank1.astype(jnp.int32), zi, zi, zi, zi], axis=0)
    zf = jnp.zeros_like(gate0)
    rf_ref[...] = jnp.concatenate([gate0, gate1, zf, zf, zf, zf, zf, zf], axis=0)


def _epilogue_out_specs(tm):
    return [
        pl.BlockSpec((tm, D_MODEL), lambda i: (i, 0)),
        pl.BlockSpec((tm * ROW_CHUNKS, LANES), lambda i: (i, 0)),
        pl.BlockSpec((SUBLANES, tm), lambda i: (0, i)),
        pl.BlockSpec((SUBLANES, tm), lambda i: (0, i)),
        pl.BlockSpec((N_EXPERTS, LANES), lambda i: (0, 0)),
    ]


def _epilogue_out_shapes(T):
    return [
        jax.ShapeDtypeStruct((T, D_MODEL), jnp.float32),
        jax.ShapeDtypeStruct((T * ROW_CHUNKS, LANES), jnp.float32),
        jax.ShapeDtypeStruct((SUBLANES, T), jnp.int32),
        jax.ShapeDtypeStruct((SUBLANES, T), jnp.float32),
        jax.ShapeDtypeStruct((N_EXPERTS, LANES), jnp.float32),
    ]


def _epilogue_in_specs():
    return [
        pl.BlockSpec((1, D_MODEL), lambda i: (0, 0)),
        pl.BlockSpec((D_MODEL, LANES), lambda i: (0, 0)),
        pl.BlockSpec((ROUTER_ROWS, 1), lambda i: (0, 0)),
    ]


def _attn_out_kernel(a_ref, w_ref, b_ref, res_ref, gffn_ref, wr_ref, brt_ref,
                     h_out_ref, xrows_ref, ri_ref, rf_ref, cnt_ref, carry_ref):
    @pl.when(pl.program_id(0) == 0)
    def _():
        carry_ref[...] = jnp.zeros_like(carry_ref)

    mix = jnp.dot(a_ref[...], w_ref[...], preferred_element_type=jnp.float32) + b_ref[...]
    h = res_ref[...] + mix
    _router_epilogue(h, gffn_ref, wr_ref, brt_ref, carry_ref,
                     h_out_ref, xrows_ref, ri_ref, rf_ref, cnt_ref)


def _attn_out_call(a, w, b, res, gffn, wr, brt):
    T = a.shape[0]
    tm = TM_DENSE
    return pl.pallas_call(
        _attn_out_kernel,
        grid=(T // tm,),
        in_specs=[
            pl.BlockSpec((tm, D_MODEL), lambda i: (i, 0)),
            pl.BlockSpec((D_MODEL, D_MODEL), lambda i: (0, 0)),
            pl.BlockSpec((1, D_MODEL), lambda i: (0, 0)),
            pl.BlockSpec((tm, D_MODEL), lambda i: (i, 0)),
        ] + _epilogue_in_specs(),
        out_specs=_epilogue_out_specs(tm),
        out_shape=_epilogue_out_shapes(T),
        scratch_shapes=[pltpu.VMEM((N_EXPERTS, 1), jnp.float32)],
        compiler_params=_params(("arbitrary",)),
        name="attn_out_router",
    )(a, w, b, res, gffn, wr, brt)


def _conv_mixer_kernel(tiles_per_seq, x_ref, g_ref, win_ref, cw_ref, wout_ref,
                       gffn_ref, wr_ref, brt_ref,
                       h_out_ref, xrows_ref, ri_ref, rf_ref, cnt_ref, carry_ref, tail_ref):
    i = pl.program_id(0)
    tm = x_ref.shape[0]

    @pl.when(i == 0)
    def _():
        carry_ref[...] = jnp.zeros_like(carry_ref)

    @pl.when(i % tiles_per_seq == 0)
    def _():
        tail_ref[...] = jnp.zeros_like(tail_ref)

    x = x_ref[...]
    xn = _rms(x, g_ref[...]).astype(jnp.bfloat16)
    bcu = jnp.dot(xn, win_ref[...], preferred_element_type=jnp.float32)
    b_gate = bcu[:, :D_MODEL]
    z = bcu[:, D_MODEL:2 * D_MODEL] * bcu[:, 2 * D_MODEL:]
    row = lax.broadcasted_iota(jnp.int32, (tm, D_MODEL), 0)
    tail = tail_ref[...]
    t1 = tail[SUBLANES - 1:SUBLANES, :]
    t2 = tail[SUBLANES - 2:SUBLANES - 1, :]
    z1 = jnp.where(row == 0, t1, pltpu.roll(z, 1, axis=0))
    z2 = jnp.where(row == 0, t2, jnp.where(row == 1, t1, pltpu.roll(z, 2, axis=0)))
    tail_ref[...] = z[tm - SUBLANES:, :]
    conv = z2 * cw_ref[0:1, :]
    conv = conv + z1 * cw_ref[1:2, :]
    conv = conv + z * cw_ref[2:3, :]
    gated = (b_gate * conv).astype(jnp.bfloat16)
    h = x + jnp.dot(gated, wout_ref[...], preferred_element_type=jnp.float32)
    _router_epilogue(h, gffn_ref, wr_ref, brt_ref, carry_ref,
                     h_out_ref, xrows_ref, ri_ref, rf_ref, cnt_ref)


def _conv_mixer_call(x2, g, win, cw, wout, gffn, wr, brt, seq):
    T = x2.shape[0]
    tm = TM_MIX1
    return pl.pallas_call(
        functools.partial(_conv_mixer_kernel, seq // tm),
        grid=(T // tm,),
        in_specs=[
            pl.BlockSpec((tm, D_MODEL), lambda i: (i, 0)),
            pl.BlockSpec((1, D_MODEL), lambda i: (0, 0)),
            pl.BlockSpec((D_MODEL, 3 * D_MODEL), lambda i: (0, 0)),
            pl.BlockSpec((CONV_WIDTH, D_MODEL), lambda i: (0, 0)),
            pl.BlockSpec((D_MODEL, D_MODEL), lambda i: (0, 0)),
        ] + _epilogue_in_specs(),
        out_specs=_epilogue_out_specs(tm),
        out_shape=_epilogue_out_shapes(T),
        scratch_shapes=[pltpu.VMEM((N_EXPERTS, 1), jnp.float32),
                        pltpu.VMEM((SUBLANES, D_MODEL), jnp.float32)],
        compiler_params=_params(("arbitrary",)),
        name="conv_mixer_router",
    )(x2, g, win, cw, wout, gffn, wr, brt)


def _row_copy(src_hbm, src_row, dst_hbm, dst_row, sem):
    return pltpu.make_async_copy(
        src_hbm.at[pl.ds(pl.multiple_of(src_row, ROW_CHUNKS), ROW_CHUNKS)],
        dst_hbm.at[pl.ds(pl.multiple_of(dst_row, ROW_CHUNKS), ROW_CHUNKS)], sem)


def _dispatch_kernel(d0_ref, d1_ref, padrow_ref, x_hbm, xs_hbm, zero_ref, sem, zsem):
    i = pl.program_id(0)
    tile_rows = TM_EXPERT * ROW_CHUNKS

    @pl.when(i == 0)
    def _():
        zero_ref[...] = jnp.zeros_like(zero_ref)

        def zcopy(e):
            return pltpu.make_async_copy(
                zero_ref,
                xs_hbm.at[pl.ds(pl.multiple_of(padrow_ref[e], tile_rows), tile_rows)], zsem)

        def zstart(e, c):
            @pl.when(padrow_ref[e] >= 0)
            def _():
                zcopy(e).start()
            return c

        def zwait(e, c):
            @pl.when(padrow_ref[e] >= 0)
            def _():
                zcopy(e).wait()
            return c

        lax.fori_loop(0, N_EXPERTS, zstart, 0)
        lax.fori_loop(0, N_EXPERTS, zwait, 0)

    base = i * TM_DISPATCH

    def issue(t, c):
        src = (base + t) * ROW_CHUNKS
        _row_copy(x_hbm, src, xs_hbm, d0_ref[t], sem).start()
        _row_copy(x_hbm, src, xs_hbm, d1_ref[t], sem).start()
        return c

    def drain(t, c):
        src = (base + t) * ROW_CHUNKS
        _row_copy(x_hbm, src, xs_hbm, d0_ref[t], sem).wait()
        _row_copy(x_hbm, src, xs_hbm, d1_ref[t], sem).wait()
        return c

    lax.fori_loop(0, TM_DISPATCH, issue, 0)
    lax.fori_loop(0, TM_DISPATCH, drain, 0)


def _dispatch_call(d0, d1, padrow, xrows, capacity):
    T = d0.shape[0]
    return pl.pallas_call(
        _dispatch_kernel,
        grid=(T // TM_DISPATCH,),
        in_specs=[
            pl.BlockSpec((TM_DISPATCH,), lambda i: (i,), memory_space=pltpu.SMEM),
            pl.BlockSpec((TM_DISPATCH,), lambda i: (i,), memory_space=pltpu.SMEM),
            pl.BlockSpec(memory_space=pltpu.SMEM),
            pl.BlockSpec(memory_space=pl.ANY),
        ],
        out_specs=pl.BlockSpec(memory_space=pl.ANY),
        out_shape=jax.ShapeDtypeStruct((capacity * ROW_CHUNKS, LANES), jnp.float32),
        scratch_shapes=[pltpu.VMEM((TM_EXPERT * ROW_CHUNKS, LANES), jnp.float32),
                        pltpu.SemaphoreType.DMA, pltpu.SemaphoreType.DMA],
        compiler_params=pltpu.CompilerParams(dimension_semantics=("arbitrary",),
                                             vmem_limit_bytes=VMEM_LIMIT,
                                             has_side_effects=True),
        name="moe_dispatch",
    )(d0, d1, padrow, xrows)


def _expert_kernel(be_ref, nu_ref, xs_ref, wgu_ref, wd_ref, ys_ref):
    i = pl.program_id(0)
    tm = TM_EXPERT

    @pl.when(i < nu_ref[0])
    def _():
        x = jnp.concatenate(
            [xs_ref[pl.ds(c, tm, stride=ROW_CHUNKS), :] for c in range(ROW_CHUNKS)], axis=-1)
        gu = jnp.dot(x.astype(jnp.bfloat16), wgu_ref[0], preferred_element_type=jnp.float32)
        g = gu[:, :EXPERT_FF]
        u = gu[:, EXPERT_FF:]
        act = (g / (1.0 + jnp.exp(-g))) * u
        y = jnp.dot(act.astype(jnp.bfloat16), wd_ref[0], preferred_element_type=jnp.float32)
        for c in range(ROW_CHUNKS):
            ys_ref[pl.ds(c, tm, stride=ROW_CHUNKS), :] = y[:, c * LANES:(c + 1) * LANES]


def _expert_call(blk_expert, n_used, xs, wgu, wd, capacity):
    tm = TM_EXPERT
    n_blk = capacity // tm

    def slot_map(i, be, nu):
        return (jnp.minimum(i, nu[0] - 1), 0)

    def w_map(i, be, nu):
        return (be[i], 0, 0)

    return pl.pallas_call(
        _expert_kernel,
        grid_spec=pltpu.PrefetchScalarGridSpec(
            num_scalar_prefetch=2,
            grid=(n_blk,),
            in_specs=[
                pl.BlockSpec((tm * ROW_CHUNKS, LANES), slot_map),
                pl.BlockSpec((1, D_MODEL, 2 * EXPERT_FF), w_map),
                pl.BlockSpec((1, EXPERT_FF, D_MODEL), w_map),
            ],
            out_specs=pl.BlockSpec((tm * ROW_CHUNKS, LANES), slot_map),
        ),
        out_shape=jax.ShapeDtypeStruct((capacity * ROW_CHUNKS, LANES), jnp.float32),
        compiler_params=_params(("arbitrary",)),
        name="moe_experts",
    )(blk_expert, n_used, xs, wgu, wd)


def _combine_kernel(final_norm, d0_ref, d1_ref, ys_hbm, h_ref, gate_ref, gfin_ref,
                    o_ref, y0_ref, y1_ref, sem):
    tm = TM_COMBINE

    def issue(t, c):
        dst = t * ROW_CHUNKS
        _row_copy(ys_hbm, d0_ref[t], y0_ref, dst, sem).start()
        _row_copy(ys_hbm, d1_ref[t], y1_ref, dst, sem).start()
        return c

    def drain(t, c):
        dst = t * ROW_CHUNKS
        _row_copy(ys_hbm, d0_ref[t], y0_ref, dst, sem).wait()
        _row_copy(ys_hbm, d1_ref[t], y1_ref, dst, sem).wait()
        return c

    lax.fori_loop(0, tm, issue, 0)
    lax.fori_loop(0, tm, drain, 0)

    g0 = gate_ref[:, 0:1]
    g1 = gate_ref[:, 1:2]
    cols = []
    for c in range(ROW_CHUNKS):
        y0 = y0_ref[pl.ds(c, tm, stride=ROW_CHUNKS), :]
        y1 = y1_ref[pl.ds(c, tm, stride=ROW_CHUNKS), :]
        cols.append(h_ref[:, c * LANES:(c + 1) * LANES] + (y0 * g0 + y1 * g1))
    h = jnp.concatenate(cols, axis=-1)
    if final_norm:
        h = _rms(h, gfin_ref[...])
    o_ref[...] = h


def _combine_call(d0, d1, ys, h, gates, gfin, final_norm):
    T = h.shape[0]
    tm = TM_COMBINE
    return pl.pallas_call(
        functools.partial(_combine_kernel, final_norm),
        grid=(T // tm,),
        in_specs=[
            pl.BlockSpec((tm,), lambda i: (i,), memory_space=pltpu.SMEM),
            pl.BlockSpec((tm,), lambda i: (i,), memory_space=pltpu.SMEM),
            pl.BlockSpec(memory_space=pl.ANY),
            pl.BlockSpec((tm, D_MODEL), lambda i: (i, 0)),
            pl.BlockSpec((tm, 2), lambda i: (i, 0)),
            pl.BlockSpec((1, D_MODEL), lambda i: (0, 0)),
        ],
        out_specs=pl.BlockSpec((tm, D_MODEL), lambda i: (i, 0)),
        out_shape=jax.ShapeDtypeStruct((T, D_MODEL), jnp.float32),
        scratch_shapes=[pltpu.VMEM((tm * ROW_CHUNKS, LANES), jnp.float32),
                        pltpu.VMEM((tm * ROW_CHUNKS, LANES), jnp.float32),
                        pltpu.SemaphoreType.DMA],
        compiler_params=_params(("arbitrary",)),
        name="moe_combine",
    )(d0, d1, ys, h, gates, gfin)


def _head_perm():
    perm = []
    for m in range(GROUP):
        perm += list(range(m * HEAD_DIM, (m + 1) * HEAD_DIM))
        perm += list(range((GROUP + m) * HEAD_DIM, (GROUP + m + 1) * HEAD_DIM))
    return np.asarray(perm, np.int32)


def _rel_bucket_table():
    qi = np.arange(ATTN_BLOCK)[:, None]
    kj = np.arange(2 * ATTN_BLOCK)[None, :]
    dist = np.maximum(qi + ATTN_BLOCK - kj, 0)
    max_exact = N_REL_BUCKETS // 2
    d = np.maximum(dist, max_exact).astype(np.float32)
    large = max_exact + (np.log(d / np.float32(max_exact)) / np.float32(math.log(REL_MAX_DISTANCE / max_exact))
                         * np.float32(N_REL_BUCKETS - max_exact)).astype(np.int32)
    large = np.minimum(large, N_REL_BUCKETS - 1)
    return np.where(dist < max_exact, dist, large).astype(np.int32)


def _router_weights(w_group, b_group, w_expert, b_expert):
    wr = jnp.zeros((D_MODEL, LANES), jnp.float32)
    wr = wr.at[:, :N_GROUPS].set(w_group)
    wr = wr.at[:, SUBLANES:SUBLANES + N_EXPERTS].set(w_expert)
    brt = jnp.zeros((ROUTER_ROWS, 1), jnp.float32)
    brt = brt.at[:N_GROUPS, 0].set(b_group)
    brt = brt.at[SUBLANES:SUBLANES + N_EXPERTS, 0].set(b_expert)
    return wr, brt


def _moe(h, xrows, ri, rf, cnt, w_gate_up, w_down, gfin, final_norm):
    T = h.shape[0]
    tm = TM_EXPERT
    capacity = -(-(2 * T) // tm) * tm + N_EXPERTS * tm
    counts = cnt[:, 0].astype(jnp.int32)
    ntile = (counts + tm - 1) // tm
    tend = jnp.cumsum(ntile)
    pstart = (tend - ntile) * tm
    n_used = tend[-1:]
    dest0 = (pstart[ri[0]] + ri[2]) * ROW_CHUNKS
    dest1 = (pstart[ri[1]] + ri[3]) * ROW_CHUNKS
    padrow = jnp.where(ntile > 0, (tend - 1) * (tm * ROW_CHUNKS), -1).astype(jnp.int32)
    blk = jnp.minimum(jnp.arange(capacity // tm, dtype=jnp.int32), n_used[0] - 1)
    blk_expert = jnp.minimum(jnp.searchsorted(tend, blk, side="right"), N_EXPERTS - 1).astype(jnp.int32)

    xs = _dispatch_call(dest0, dest1, padrow, xrows, capacity)
    ys = _expert_call(blk_expert, n_used.astype(jnp.int32), xs,
                      w_gate_up.astype(jnp.bfloat16), w_down.astype(jnp.bfloat16), capacity)
    gates = rf[:2].T
    return _combine_call(dest0, dest1, ys, h, gates, gfin, final_norm)


def kernel(x, norm_mix, norm_ffn, final_norm, rel_bias, attn_w_qkv, attn_b_qkv, attn_w_o, attn_b_o,
           attn_sinks, conv_w_in, conv_w, conv_w_out, moe_w_group, moe_b_group, moe_w_expert,
           moe_b_expert, moe_w_gate_up, moe_w_down):
    B, S, D = x.shape
    T = B * S
    x2 = x.reshape(T, D)
    nq = N_Q_HEADS * HEAD_DIM
    perm = _head_perm()
    scale = HEAD_DIM ** -0.5

    wqkv = attn_w_qkv[0]
    bqkv = attn_b_qkv[0]
    w_all = jnp.concatenate([wqkv[:, :nq][:, perm] * scale, wqkv[:, nq:]], axis=1).astype(jnp.bfloat16)
    b_all = jnp.concatenate([bqkv[:nq][perm] * scale, bqkv[nq:]])[None, :]
    q, kt, v = _qkv_call(x2, norm_mix[0][None, :], w_all, b_all)
    bias = rel_bias[_rel_bucket_table()].astype(jnp.float32).transpose(2, 0, 1)
    a = _attn_call(attn_sinks[0].astype(jnp.float32), q, kt, v, bias, B, S)
    wr, brt = _router_weights(moe_w_group[0], moe_b_group[0], moe_w_expert[0], moe_b_expert[0])
    h, xrows, ri, rf, cnt = _attn_out_call(
        a, attn_w_o[0][perm, :].astype(jnp.bfloat16), attn_b_o[0][None, :], x2,
        norm_ffn[0][None, :], wr, brt)
    h = _moe(h, xrows, ri, rf, cnt, moe_w_gate_up[0], moe_w_down[0], final_norm[None, :], False)

    wr, brt = _router_weights(moe_w_group[1], moe_b_group[1], moe_w_expert[1], moe_b_expert[1])
    h, xrows, ri, rf, cnt = _conv_mixer_call(
        h, norm_mix[1][None, :], conv_w_in[0].astype(jnp.bfloat16), conv_w[0],
        conv_w_out[0].astype(jnp.bfloat16), norm_ffn[1][None, :], wr, brt, S)
    out = _moe(h, xrows, ri, rf, cnt, moe_w_gate_up[1], moe_w_down[1], final_norm[None, :], True)
    return out.reshape(B, S, D)
```

```python
import functools
import math

import numpy as np
import jax
import jax.numpy as jnp
from jax import lax
from jax.experimental import pallas as pl
from jax.experimental.pallas import tpu as pltpu

D_MODEL = 1024
N_Q_HEADS = 16
N_KV_HEADS = 2
HEAD_DIM = 64
GROUP = N_Q_HEADS // N_KV_HEADS
WINDOW = 128
ATTN_BLOCK = 128
N_REL_BUCKETS = 32
REL_MAX_DISTANCE = 128
CONV_WIDTH = 3
N_GROUPS = 4
EXPERTS_PER_GROUP = 8
N_EXPERTS = N_GROUPS * EXPERTS_PER_GROUP
EXPERT_FF = 512
RMS_EPS = 1e-5

LANES = 128
SUBLANES = 8
ROW_CHUNKS = D_MODEL // LANES
VMEM_LIMIT = 56 * 1024 * 1024

TM_DENSE = 512
TQ = 512
TM_MIX1 = 256
TM_EXPERT = 256
TM_DISPATCH = 512
TM_COMBINE = 256
DMA_UNROLL = 8
ROUTER_ROWS = 40


def _rms(x, g):
    return x * lax.rsqrt(jnp.mean(x * x, axis=-1, keepdims=True) + RMS_EPS) * g


def _params(sem):
    return pltpu.CompilerParams(dimension_semantics=sem, vmem_limit_bytes=VMEM_LIMIT)


def _qkv_kernel(x_ref, g_ref, w_ref, b_ref, q_ref, kt_ref, v_ref):
    xn = _rms(x_ref[...], g_ref[...]).astype(jnp.bfloat16)
    out = jnp.dot(xn, w_ref[...], preferred_element_type=jnp.float32) + b_ref[...]
    nq = N_Q_HEADS * HEAD_DIM
    nk = N_KV_HEADS * HEAD_DIM
    q_ref[...] = out[:, :nq].astype(jnp.bfloat16)
    kt_ref[...] = out[:, nq:nq + nk].T.astype(jnp.bfloat16)
    v_ref[...] = out[:, nq + nk:].astype(jnp.bfloat16)


def _qkv_call(x2, g, w, b):
    T = x2.shape[0]
    nq = N_Q_HEADS * HEAD_DIM
    nk = N_KV_HEADS * HEAD_DIM
    tm = TM_DENSE
    return pl.pallas_call(
        _qkv_kernel,
        grid=(T // tm,),
        in_specs=[
            pl.BlockSpec((tm, D_MODEL), lambda i: (i, 0)),
            pl.BlockSpec((1, D_MODEL), lambda i: (0, 0)),
            pl.BlockSpec((D_MODEL, nq + 2 * nk), lambda i: (0, 0)),
            pl.BlockSpec((1, nq + 2 * nk), lambda i: (0, 0)),
        ],
        out_specs=[
            pl.BlockSpec((tm, nq), lambda i: (i, 0)),
            pl.BlockSpec((nk, tm), lambda i: (0, i)),
            pl.BlockSpec((tm, nk), lambda i: (i, 0)),
        ],
        out_shape=[
            jax.ShapeDtypeStruct((T, nq), jnp.bfloat16),
            jax.ShapeDtypeStruct((nk, T), jnp.bfloat16),
            jax.ShapeDtypeStruct((T, nk), jnp.bfloat16),
        ],
        compiler_params=_params(("parallel",)),
        name="qkv_proj",
    )(x2, g, w, b)


def _attn_kernel(sink_ref, q_ref, kt_ref, ktp_ref, v_ref, vp_ref, bias_ref, o_ref):
    first_tile = pl.program_id(1) == 0
    nblk = TQ // ATTN_BLOCK
    qi_io = lax.broadcasted_iota(jnp.int32, (ATTN_BLOCK, 2 * ATTN_BLOCK), 0)
    kj_io = lax.broadcasted_iota(jnp.int32, (ATTN_BLOCK, 2 * ATTN_BLOCK), 1)
    dist = qi_io + ATTN_BLOCK - kj_io
    band = (dist >= 0) & (dist < WINDOW)
    low_half = lax.broadcasted_iota(jnp.int32, (ATTN_BLOCK, LANES), 1) < HEAD_DIM
    for qi in range(nblk):
        r0 = qi * ATTN_BLOCK
        if qi == 0:
            kt_blk = jnp.concatenate([ktp_ref[...], kt_ref[:, 0:ATTN_BLOCK]], axis=1)
            v_blk = jnp.concatenate([vp_ref[...], v_ref[0:ATTN_BLOCK, :]], axis=0)
            mask = band & (jnp.logical_not(first_tile) | (kj_io >= ATTN_BLOCK))
        else:
            kt_blk = kt_ref[:, r0 - ATTN_BLOCK:r0 + ATTN_BLOCK]
            v_blk = v_ref[r0 - ATTN_BLOCK:r0 + ATTN_BLOCK, :]
            mask = band
        for m in range(GROUP):
            qg = q_ref[r0:r0 + ATTN_BLOCK, m * LANES:(m + 1) * LANES]
            halves = []
            for half in range(N_KV_HEADS):
                h = m + GROUP * half
                keep = low_half if half == 0 else jnp.logical_not(low_half)
                qpad = jnp.where(keep, qg, jnp.zeros_like(qg))
                s = jnp.dot(qpad, kt_blk, preferred_element_type=jnp.float32)
                s = jnp.where(mask, s + bias_ref[h], -1e30)
                sink = sink_ref[h]
                mx = jnp.maximum(jnp.max(s, axis=-1, keepdims=True), sink)
                p = jnp.exp(s - mx)
                den = jnp.sum(p, axis=-1, keepdims=True) + jnp.exp(sink - mx)
                pv = jnp.dot(p.astype(jnp.bfloat16), v_blk, preferred_element_type=jnp.float32)
                halves.append(pv * (1.0 / den))
            og = jnp.where(low_half, halves[0], halves[1])
            o_ref[r0:r0 + ATTN_BLOCK, m * LANES:(m + 1) * LANES] = og.astype(jnp.bfloat16)


def _attn_call(sinks, q, kt, v, bias, batch, seq):
    T = q.shape[0]
    nq = N_Q_HEADS * HEAD_DIM
    nk = N_KV_HEADS * HEAD_DIM
    tiles = seq // TQ
    per = TQ // ATTN_BLOCK

    def cur(b, j):
        return b * tiles + j

    def prev(b, j):
        return jnp.maximum((b * tiles + j) * per - 1, b * tiles * per)

    return pl.pallas_call(
        _attn_kernel,
        grid=(batch, tiles),
        in_specs=[
            pl.BlockSpec(memory_space=pltpu.SMEM),
            pl.BlockSpec((TQ, nq), lambda b, j: (cur(b, j), 0)),
            pl.BlockSpec((nk, TQ), lambda b, j: (0, cur(b, j))),
            pl.BlockSpec((nk, ATTN_BLOCK), lambda b, j: (0, prev(b, j))),
            pl.BlockSpec((TQ, nk), lambda b, j: (cur(b, j), 0)),
            pl.BlockSpec((ATTN_BLOCK, nk), lambda b, j: (prev(b, j), 0)),
            pl.BlockSpec((N_Q_HEADS, ATTN_BLOCK, 2 * ATTN_BLOCK), lambda b, j: (0, 0, 0)),
        ],
        out_specs=pl.BlockSpec((TQ, nq), lambda b, j: (cur(b, j), 0)),
        out_shape=jax.ShapeDtypeStruct((T, nq), jnp.bfloat16),
        compiler_params=_params(("parallel", "parallel")),
        name="swa_attention",
    )(sinks, q, kt, kt, v, v, bias)


def _router_epilogue(h, gffn_ref, wr_ref, brt_ref, carry_ref,
                     h_out_ref, xrows_ref, ri_ref, rf_ref, cnt_ref):
    tm = h.shape[0]
    h_out_ref[...] = h
    xn = _rms(h, gffn_ref[...])
    for c in range(ROW_CHUNKS):
        xrows_ref[pl.ds(c, tm, stride=ROW_CHUNKS), :] = xn[:, c * LANES:(c + 1) * LANES]

    logits = jnp.dot(xn, wr_ref[...], preferred_element_type=jnp.float32,
                     precision=lax.Precision.HIGHEST)
    lt = logits.T[:ROUTER_ROWS, :] + brt_ref[...]

    gl = [lt[g:g + 1, :] for g in range(N_GROUPS)]
    gmax = functools.reduce(jnp.maximum, gl)
    gexp = [jnp.exp(x - gmax) for x in gl]
    gsum = functools.reduce(lambda a, b: a + b, gexp)
    gprob = [x / gsum for x in gexp]
    g_prob = functools.reduce(jnp.maximum, gprob)
    g_idx = jnp.full(g_prob.shape, N_GROUPS - 1, jnp.int32)
    for g in range(N_GROUPS - 2, -1, -1):
        g_idx = jnp.where(gprob[g] == g_prob, g, g_idx)

    el = []
    for j in range(EXPERTS_PER_GROUP):
        x = lt[SUBLANES + j:SUBLANES + j + 1, :]
        for g in range(1, N_GROUPS):
            r = SUBLANES + g * EXPERTS_PER_GROUP + j
            x = jnp.where(g_idx == g, lt[r:r + 1, :], x)
        el.append(x)
    emax = functools.reduce(jnp.maximum, el)
    eexp = [jnp.exp(x - emax) for x in el]
    esum = functools.reduce(lambda a, b: a + b, eexp)
    eprob = [x / esum for x in eexp]
    p1 = functools.reduce(jnp.maximum, eprob)
    i1 = jnp.full(p1.shape, EXPERTS_PER_GROUP - 1, jnp.int32)
    for j in range(EXPERTS_PER_GROUP - 2, -1, -1):
        i1 = jnp.where(eprob[j] == p1, j, i1)
    rest = [jnp.where(i1 == j, -1.0, eprob[j]) for j in range(EXPERTS_PER_GROUP)]
    p2 = functools.reduce(jnp.maximum, rest)
    i2 = jnp.full(p2.shape, EXPERTS_PER_GROUP - 1, jnp.int32)
    for j in range(EXPERTS_PER_GROUP - 2, -1, -1):
        i2 = jnp.where(rest[j] == p2, j, i2)
    psum = p1 + p2
    gate0 = g_prob * (p1 / psum)
    gate1 = g_prob * (p2 / psum)
    e0 = g_idx * EXPERTS_PER_GROUP + i1
    e1 = g_idx * EXPERTS_PER_GROUP + i2

    eio = lax.broadcasted_iota(jnp.int32, (N_EXPERTS, tm), 0)
    oh0 = (eio == e0).astype(jnp.float32)
    oh1 = (eio == e1).astype(jnp.float32)
    both = oh0 + oh1
    tr = lax.broadcasted_iota(jnp.int32, (tm, tm), 0)
    tc = lax.broadcasted_iota(jnp.int32, (tm, tm), 1)
    upper = (tr < tc).astype(jnp.bfloat16)
    before = jnp.dot(both.astype(jnp.bfloat16), upper, preferred_element_type=jnp.float32)
    before = before + carry_ref[...]
    rank0 = jnp.sum(oh0 * before, axis=0, keepdims=True)
    rank1 = jnp.sum(oh1 * (before + oh0), axis=0, keepdims=True)
    carry_ref[...] = carry_ref[...] + jnp.sum(both, axis=1, keepdims=True)
    cnt_ref[...] = jnp.broadcast_to(carry_ref[...], cnt_ref.shape)

    zi = jnp.zeros_like(e0)
    ri_ref[...] = jnp.concatenate(
        [e0, e1, rank0.astype(jnp.int32), rank1.astype(jnp.int32), zi, zi, zi, zi], axis=0)
    zf = jnp.zeros_like(gate0)
    rf_ref[...] = jnp.concatenate([gate0, gate1, zf, zf, zf, zf, zf, zf], axis=0)


def _epilogue_out_specs(tm):
    return [
        pl.BlockSpec((tm, D_MODEL), lambda i: (i, 0)),
        pl.BlockSpec((tm * ROW_CHUNKS, LANES), lambda i: (i, 0)),
        pl.BlockSpec((SUBLANES, tm), lambda i: (0, i)),
        pl.BlockSpec((SUBLANES, tm), lambda i: (0, i)),
        pl.BlockSpec((N_EXPERTS, LANES), lambda i: (0, 0)),
    ]


def _epilogue_out_shapes(T):
    return [
        jax.ShapeDtypeStruct((T, D_MODEL), jnp.float32),
        jax.ShapeDtypeStruct((T * ROW_CHUNKS, LANES), jnp.float32),
        jax.ShapeDtypeStruct((SUBLANES, T), jnp.int32),
        jax.ShapeDtypeStruct((SUBLANES, T), jnp.float32),
        jax.ShapeDtypeStruct((N_EXPERTS, LANES), jnp.float32),
    ]


def _epilogue_in_specs():
    return [
        pl.BlockSpec((1, D_MODEL), lambda i: (0, 0)),
        pl.BlockSpec((D_MODEL, LANES), lambda i: (0, 0)),
        pl.BlockSpec((ROUTER_ROWS, 1), lambda i: (0, 0)),
    ]


def _attn_out_kernel(a_ref, w_ref, b_ref, res_ref, gffn_ref, wr_ref, brt_ref,
                     h_out_ref, xrows_ref, ri_ref, rf_ref, cnt_ref, carry_ref):
    @pl.when(pl.program_id(0) == 0)
    def _():
        carry_ref[...] = jnp.zeros_like(carry_ref)

    mix = jnp.dot(a_ref[...], w_ref[...], preferred_element_type=jnp.float32) + b_ref[...]
    h = res_ref[...] + mix
    _router_epilogue(h, gffn_ref, wr_ref, brt_ref, carry_ref,
                     h_out_ref, xrows_ref, ri_ref, rf_ref, cnt_ref)


def _attn_out_call(a, w, b, res, gffn, wr, brt):
    T = a.shape[0]
    tm = TM_DENSE
    return pl.pallas_call(
        _attn_out_kernel,
        grid=(T // tm,),
        in_specs=[
            pl.BlockSpec((tm, D_MODEL), lambda i: (i, 0)),
            pl.BlockSpec((D_MODEL, D_MODEL), lambda i: (0, 0)),
            pl.BlockSpec((1, D_MODEL), lambda i: (0, 0)),
            pl.BlockSpec((tm, D_MODEL), lambda i: (i, 0)),
        ] + _epilogue_in_specs(),
        out_specs=_epilogue_out_specs(tm),
        out_shape=_epilogue_out_shapes(T),
        scratch_shapes=[pltpu.VMEM((N_EXPERTS, 1), jnp.float32)],
        compiler_params=_params(("arbitrary",)),
        name="attn_out_router",
    )(a, w, b, res, gffn, wr, brt)


def _conv_mixer_kernel(tiles_per_seq, x_ref, g_ref, win_ref, cw_ref, wout_ref,
                       gffn_ref, wr_ref, brt_ref,
                       h_out_ref, xrows_ref, ri_ref, rf_ref, cnt_ref, carry_ref, tail_ref):
    i = pl.program_id(0)
    tm = x_ref.shape[0]

    @pl.when(i == 0)
    def _():
        carry_ref[...] = jnp.zeros_like(carry_ref)

    @pl.when(i % tiles_per_seq == 0)
    def _():
        tail_ref[...] = jnp.zeros_like(tail_ref)

    x = x_ref[...]
    xn = _rms(x, g_ref[...]).astype(jnp.bfloat16)
    bcu = jnp.dot(xn, win_ref[...], preferred_element_type=jnp.float32)
    b_gate = bcu[:, :D_MODEL]
    z = bcu[:, D_MODEL:2 * D_MODEL] * bcu[:, 2 * D_MODEL:]
    row = lax.broadcasted_iota(jnp.int32, (tm, D_MODEL), 0)
    tail = tail_ref[...]
    t1 = tail[SUBLANES - 1:SUBLANES, :]
    t2 = tail[SUBLANES - 2:SUBLANES - 1, :]
    z1 = jnp.where(row == 0, t1, pltpu.roll(z, 1, axis=0))
    z2 = jnp.where(row == 0, t2, jnp.where(row == 1, t1, pltpu.roll(z, 2, axis=0)))
    tail_ref[...] = z[tm - SUBLANES:, :]
    conv = z2 * cw_ref[0:1, :]
    conv = conv + z1 * cw_ref[1:2, :]
    conv = conv + z * cw_ref[2:3, :]
    gated = (b_gate * conv).astype(jnp.bfloat16)
    h = x + jnp.dot(gated, wout_ref[...], preferred_element_type=jnp.float32)
    _router_epilogue(h, gffn_ref, wr_ref, brt_ref, carry_ref,
                     h_out_ref, xrows_ref, ri_ref, rf_ref, cnt_ref)


def _conv_mixer_call(x2, g, win, cw, wout, gffn, wr, brt, seq):
    T = x2.shape[0]
    tm = TM_MIX1
    return pl.pallas_call(
        functools.partial(_conv_mixer_kernel, seq // tm),
        grid=(T // tm,),
        in_specs=[
            pl.BlockSpec((tm, D_MODEL), lambda i: (i, 0)),
            pl.BlockSpec((1, D_MODEL), lambda i: (0, 0)),
            pl.BlockSpec((D_MODEL, 3 * D_MODEL), lambda i: (0, 0)),
            pl.BlockSpec((CONV_WIDTH, D_MODEL), lambda i: (0, 0)),
            pl.BlockSpec((D_MODEL, D_MODEL), lambda i: (0, 0)),
        ] + _epilogue_in_specs(),
        out_specs=_epilogue_out_specs(tm),
        out_shape=_epilogue_out_shapes(T),
        scratch_shapes=[pltpu.VMEM((N_EXPERTS, 1), jnp.float32),
                        pltpu.VMEM((SUBLANES, D_MODEL), jnp.float32)],
        compiler_params=_params(("arbitrary",)),
        name="conv_mixer_router",
    )(x2, g, win, cw, wout, gffn, wr, brt)


def _row_copy(src_hbm, src_row, dst_hbm, dst_row, sem):
    return pltpu.make_async_copy(
        src_hbm.at[pl.ds(pl.multiple_of(src_row, ROW_CHUNKS), ROW_CHUNKS)],
        dst_hbm.at[pl.ds(pl.multiple_of(dst_row, ROW_CHUNKS), ROW_CHUNKS)], sem)


def _dispatch_kernel(d0_ref, d1_ref, padrow_ref, x_ref, xs_hbm, zero_ref, sem, zsem):
    i = pl.program_id(0)
    tile_rows = TM_EXPERT * ROW_CHUNKS

    @pl.when(i == 0)
    def _():
        zero_ref[...] = jnp.zeros_like(zero_ref)

        def zcopy(e):
            return pltpu.make_async_copy(
                zero_ref,
                xs_hbm.at[pl.ds(pl.multiple_of(padrow_ref[e], tile_rows), tile_rows)], zsem)

        def zstart(e, c):
            @pl.when(padrow_ref[e] >= 0)
            def _():
                zcopy(e).start()
            return c

        def zwait(e, c):
            @pl.when(padrow_ref[e] >= 0)
            def _():
                zcopy(e).wait()
            return c

        lax.fori_loop(0, N_EXPERTS, zstart, 0)
        lax.fori_loop(0, N_EXPERTS, zwait, 0)

        def tcopy(b):
            return pltpu.make_async_copy(
                zero_ref, xs_hbm.at[pl.ds(pl.multiple_of(b * tile_rows, tile_rows), tile_rows)], zsem)

        n_used = padrow_ref[N_EXPERTS]
        n_blk = xs_hbm.shape[0] // tile_rows
        lax.fori_loop(n_used, n_blk, lambda b, c: (tcopy(b).start(), c)[1], 0)
        lax.fori_loop(n_used, n_blk, lambda b, c: (tcopy(b).wait(), c)[1], 0)

    def issue(t, c):
        src = t * ROW_CHUNKS
        _row_copy(x_ref, src, xs_hbm, d0_ref[t], sem).start()
        _row_copy(x_ref, src, xs_hbm, d1_ref[t], sem).start()
        return c

    lax.fori_loop(0, TM_DISPATCH, issue, 0, unroll=DMA_UNROLL)
    for _ in range(2):
        pltpu.make_async_copy(x_ref, xs_hbm.at[pl.ds(0, TM_DISPATCH * ROW_CHUNKS)], sem).wait()


def _dispatch_call(d0, d1, padrow, xrows, capacity):
    T = d0.shape[0]
    return pl.pallas_call(
        _dispatch_kernel,
        grid=(T // TM_DISPATCH,),
        in_specs=[
            pl.BlockSpec((TM_DISPATCH,), lambda i: (i,), memory_space=pltpu.SMEM),
            pl.BlockSpec((TM_DISPATCH,), lambda i: (i,), memory_space=pltpu.SMEM),
            pl.BlockSpec(memory_space=pltpu.SMEM),
            pl.BlockSpec((TM_DISPATCH * ROW_CHUNKS, LANES), lambda i: (i, 0)),
        ],
        out_specs=pl.BlockSpec(memory_space=pl.ANY),
        out_shape=jax.ShapeDtypeStruct((capacity * ROW_CHUNKS, LANES), jnp.float32),
        scratch_shapes=[pltpu.VMEM((TM_EXPERT * ROW_CHUNKS, LANES), jnp.float32),
                        pltpu.SemaphoreType.DMA, pltpu.SemaphoreType.DMA],
        compiler_params=_params(("arbitrary",)),
        name="moe_dispatch",
    )(d0, d1, padrow, xrows)


def _expert_kernel(be_ref, first_ref, nu_ref, xs_ref, wgu_ref, wd_ref, ys_ref, wgu_bf, wd_bf):
    i = pl.program_id(0)
    tm = TM_EXPERT
    live = i < nu_ref[0]

    @pl.when(live & (first_ref[i] == 1))
    def _():
        wgu_bf[...] = wgu_ref[0, 0].astype(jnp.bfloat16)
        wd_bf[...] = wd_ref[0, 0].astype(jnp.bfloat16)

    @pl.when(live)
    def _():
        x = jnp.concatenate(
            [xs_ref[pl.ds(c, tm, stride=ROW_CHUNKS), :] for c in range(ROW_CHUNKS)], axis=-1)
        gu = jnp.dot(x.astype(jnp.bfloat16), wgu_bf[...], preferred_element_type=jnp.float32)
        g = gu[:, :EXPERT_FF]
        u = gu[:, EXPERT_FF:]
        act = (g / (1.0 + jnp.exp(-g))) * u
        y = jnp.dot(act.astype(jnp.bfloat16), wd_bf[...], preferred_element_type=jnp.float32)
        for c in range(ROW_CHUNKS):
            ys_ref[pl.ds(c, tm, stride=ROW_CHUNKS), :] = y[:, c * LANES:(c + 1) * LANES]

    @pl.when(jnp.logical_not(live))
    def _():
        ys_ref[...] = jnp.zeros_like(ys_ref)


def _expert_call(layer, blk_expert, blk_first, n_used, xs, wgu, wd, capacity):
    tm = TM_EXPERT
    n_blk = capacity // tm

    def slot_map(i, be, bf, nu):
        return (jnp.minimum(i, nu[0] - 1), 0)

    def w_map(i, be, bf, nu):
        return (layer, be[i], 0, 0)

    return pl.pallas_call(
        _expert_kernel,
        grid_spec=pltpu.PrefetchScalarGridSpec(
            num_scalar_prefetch=3,
            grid=(n_blk,),
            in_specs=[
                pl.BlockSpec((tm * ROW_CHUNKS, LANES), slot_map),
                pl.BlockSpec((1, 1, D_MODEL, 2 * EXPERT_FF), w_map),
                pl.BlockSpec((1, 1, EXPERT_FF, D_MODEL), w_map),
            ],
            out_specs=pl.BlockSpec((tm * ROW_CHUNKS, LANES), lambda i, be, bf, nu: (i, 0)),
            scratch_shapes=[pltpu.VMEM((D_MODEL, 2 * EXPERT_FF), jnp.bfloat16),
                            pltpu.VMEM((EXPERT_FF, D_MODEL), jnp.bfloat16)],
        ),
        out_shape=jax.ShapeDtypeStruct((capacity * ROW_CHUNKS, LANES), jnp.float32),
        compiler_params=_params(("arbitrary",)),
        name="moe_experts",
    )(blk_expert, blk_first, n_used, xs, wgu, wd)


def _combine_kernel(final_norm, d0_ref, d1_ref, d0n_ref, d1n_ref, ys_hbm, h_ref, gate_ref, gfin_ref,
                    o_ref, y0_ref, y1_ref, sem):
    tm = TM_COMBINE
    i = pl.program_id(0)
    slot = i % 2

    def issue_tile(d0r, d1r, s):
        def issue(t, c):
            dst = t * ROW_CHUNKS
            _row_copy(ys_hbm, d0r[t], y0_ref.at[s], dst, sem.at[s]).start()
            _row_copy(ys_hbm, d1r[t], y1_ref.at[s], dst, sem.at[s]).start()
            return c
        lax.fori_loop(0, tm, issue, 0, unroll=DMA_UNROLL)

    @pl.when(i == 0)
    def _():
        issue_tile(d0_ref, d1_ref, 0)

    @pl.when(i + 1 < pl.num_programs(0))
    def _():
        issue_tile(d0n_ref, d1n_ref, 1 - slot)

    for buf in (y0_ref, y1_ref):
        pltpu.make_async_copy(ys_hbm.at[pl.ds(0, tm * ROW_CHUNKS)], buf.at[slot], sem.at[slot]).wait()

    g0 = gate_ref[:, 0:1]
    g1 = gate_ref[:, 1:2]
    cols = []
    for c in range(ROW_CHUNKS):
        y0 = y0_ref[slot, pl.ds(c, tm, stride=ROW_CHUNKS), :]
        y1 = y1_ref[slot, pl.ds(c, tm, stride=ROW_CHUNKS), :]
        cols.append(h_ref[:, c * LANES:(c + 1) * LANES] + (y0 * g0 + y1 * g1))
    h = jnp.concatenate(cols, axis=-1)
    if final_norm:
        h = _rms(h, gfin_ref[...])
    o_ref[...] = h


def _combine_call(d0, d1, ys, h, gates, gfin, final_norm):
    T = h.shape[0]
    tm = TM_COMBINE
    last = T // tm - 1
    return pl.pallas_call(
        functools.partial(_combine_kernel, final_norm),
        grid=(T // tm,),
        in_specs=[
            pl.BlockSpec((tm,), lambda i: (i,), memory_space=pltpu.SMEM),
            pl.BlockSpec((tm,), lambda i: (i,), memory_space=pltpu.SMEM),
            pl.BlockSpec((tm,), lambda i: (jnp.minimum(i + 1, last),), memory_space=pltpu.SMEM),
            pl.BlockSpec((tm,), lambda i: (jnp.minimum(i + 1, last),), memory_space=pltpu.SMEM),
            pl.BlockSpec(memory_space=pl.ANY),
            pl.BlockSpec((tm, D_MODEL), lambda i: (i, 0)),
            pl.BlockSpec((tm, 2), lambda i: (i, 0)),
            pl.BlockSpec((1, D_MODEL), lambda i: (0, 0)),
        ],
        out_specs=pl.BlockSpec((tm, D_MODEL), lambda i: (i, 0)),
        out_shape=jax.ShapeDtypeStruct((T, D_MODEL), jnp.float32),
        scratch_shapes=[pltpu.VMEM((2, tm * ROW_CHUNKS, LANES), jnp.float32),
                        pltpu.VMEM((2, tm * ROW_CHUNKS, LANES), jnp.float32),
                        pltpu.SemaphoreType.DMA((2,))],
        compiler_params=_params(("arbitrary",)),
        name="moe_combine",
    )(d0, d1, d0, d1, ys, h, gates, gfin)


def _group_heads_last(w):
    lead = w.shape[:-1]
    w = w.reshape(lead + (N_KV_HEADS, GROUP, HEAD_DIM))
    w = jnp.swapaxes(w, -3, -2)
    return w.reshape(lead + (N_Q_HEADS * HEAD_DIM,))


def _rel_bucket_table():
    qi = np.arange(ATTN_BLOCK)[:, None]
    kj = np.arange(2 * ATTN_BLOCK)[None, :]
    dist = np.maximum(qi + ATTN_BLOCK - kj, 0)
    max_exact = N_REL_BUCKETS // 2
    d = np.maximum(dist, max_exact).astype(np.float32)
    large = max_exact + (np.log(d / np.float32(max_exact)) / np.float32(math.log(REL_MAX_DISTANCE / max_exact))
                         * np.float32(N_REL_BUCKETS - max_exact)).astype(np.int32)
    large = np.minimum(large, N_REL_BUCKETS - 1)
    return np.where(dist < max_exact, dist, large).astype(np.int32)


def _rel_bias_blocks(rel_bias):
    onehot = (jnp.asarray(_rel_bucket_table())[..., None] == jnp.arange(N_REL_BUCKETS)).astype(jnp.float32)
    return jnp.einsum("qkb,bh->hqk", onehot, rel_bias.astype(jnp.float32),
                      precision=lax.Precision.HIGHEST)


def _router_weights(w_group, b_group, w_expert, b_expert):
    zw = jnp.zeros((D_MODEL, SUBLANES - N_GROUPS), jnp.float32)
    zw2 = jnp.zeros((D_MODEL, LANES - SUBLANES - N_EXPERTS), jnp.float32)
    wr = jnp.concatenate([w_group, zw, w_expert, zw2], axis=1)
    brt = jnp.concatenate([b_group, jnp.zeros((SUBLANES - N_GROUPS,), jnp.float32), b_expert])[:, None]
    return wr, brt


def _moe(layer, h, xrows, ri, rf, cnt, w_gate_up, w_down, gfin, final_norm):
    T = h.shape[0]
    tm = TM_EXPERT
    capacity = -(-(2 * T) // tm) * tm + N_EXPERTS * tm
    counts = cnt[:, 0].astype(jnp.int32)
    ntile = (counts + tm - 1) // tm
    tend = jnp.cumsum(ntile)
    pstart = (tend - ntile) * tm
    n_used = tend[-1:]
    dest0 = (pstart[ri[0]] + ri[2]) * ROW_CHUNKS
    dest1 = (pstart[ri[1]] + ri[3]) * ROW_CHUNKS
    padrow = jnp.concatenate([jnp.where(ntile > 0, (tend - 1) * (tm * ROW_CHUNKS), -1), n_used]
                             ).astype(jnp.int32)
    blk = jnp.minimum(jnp.arange(capacity // tm, dtype=jnp.int32), n_used[0] - 1)
    blk_expert = jnp.minimum(jnp.sum((tend[None, :] <= blk[:, None]).astype(jnp.int32), axis=1),
                             N_EXPERTS - 1)
    blk_first = jnp.concatenate([jnp.ones((1,), jnp.int32),
                                 (blk_expert[1:] != blk_expert[:-1]).astype(jnp.int32)])

    xs = _dispatch_call(dest0, dest1, padrow, xrows, capacity)
    ys = _expert_call(layer, blk_expert, blk_first, n_used.astype(jnp.int32), xs, w_gate_up, w_down,
                      capacity)
    gates = rf[:2].T
    return _combine_call(dest0, dest1, ys, h, gates, gfin, final_norm)


def kernel(x, norm_mix, norm_ffn, final_norm, rel_bias, attn_w_qkv, attn_b_qkv, attn_w_o, attn_b_o,
           attn_sinks, conv_w_in, conv_w, conv_w_out, moe_w_group, moe_b_group, moe_w_expert,
           moe_b_expert, moe_w_gate_up, moe_w_down):
    B, S, D = x.shape
    T = B * S
    x2 = x.reshape(T, D)
    nq = N_Q_HEADS * HEAD_DIM
    scale = HEAD_DIM ** -0.5

    wqkv = attn_w_qkv[0]
    bqkv = attn_b_qkv[0]
    w_all = jnp.concatenate([_group_heads_last(wqkv[:, :nq]) * scale, wqkv[:, nq:]],
                            axis=1).astype(jnp.bfloat16)
    b_all = jnp.concatenate([_group_heads_last(bqkv[:nq]) * scale, bqkv[nq:]])[None, :]
    q, kt, v = _qkv_call(x2, norm_mix[0][None, :], w_all, b_all)
    a = _attn_call(attn_sinks[0].astype(jnp.float32), q, kt, v, _rel_bias_blocks(rel_bias), B, S)
    wr, brt = _router_weights(moe_w_group[0], moe_b_group[0], moe_w_expert[0], moe_b_expert[0])
    w_o = _group_heads_last(attn_w_o[0].T).T.astype(jnp.bfloat16)
    h, xrows, ri, rf, cnt = _attn_out_call(
        a, w_o, attn_b_o[0][None, :], x2, norm_ffn[0][None, :], wr, brt)
    h = _moe(0, h, xrows, ri, rf, cnt, moe_w_gate_up, moe_w_down, final_norm[None, :], False)

    wr, brt = _router_weights(moe_w_group[1], moe_b_group[1], moe_w_expert[1], moe_b_expert[1])
    h, xrows, ri, rf, cnt = _conv_mixer_call(
        h, norm_mix[1][None, :], conv_w_in[0].astype(jnp.bfloat16), conv_w[0],
        conv_w_out[0].astype(jnp.bfloat16), norm_ffn[1][None, :], wr, brt, S)
    out = _moe(1, h, xrows, ri, rf, cnt, moe_w_gate_up, moe_w_down, final_norm[None, :], True)
    return out.reshape(B, S, D)
```

```python
import functools
import math

import numpy as np
import jax
import jax.numpy as jnp
from jax import lax
from jax.experimental import pallas as pl
from jax.experimental.pallas import tpu as pltpu

D_MODEL = 1024
N_Q_HEADS = 16
N_KV_HEADS = 2
HEAD_DIM = 64
GROUP = N_Q_HEADS // N_KV_HEADS
WINDOW = 128
ATTN_BLOCK = 128
N_REL_BUCKETS = 32
REL_MAX_DISTANCE = 128
CONV_WIDTH = 3
N_GROUPS = 4
EXPERTS_PER_GROUP = 8
N_EXPERTS = N_GROUPS * EXPERTS_PER_GROUP
EXPERT_FF = 512
RMS_EPS = 1e-5

LANES = 128
SUBLANES = 8
ROW_CHUNKS = D_MODEL // LANES
VMEM_LIMIT = 56 * 1024 * 1024

TM_DENSE = 512
TQ = 512
TM_MIX1 = 512
TM_EXPERT = 256
TM_DISPATCH = 512
TM_COMBINE = 256
DMA_UNROLL = 8
ROUTER_ROWS = 40


def _rms(x, g):
    return x * lax.rsqrt(jnp.mean(x * x, axis=-1, keepdims=True) + RMS_EPS) * g


def _params(sem):
    return pltpu.CompilerParams(dimension_semantics=sem, vmem_limit_bytes=VMEM_LIMIT)


def _qkv_kernel(x_ref, g_ref, w_ref, b_ref, q_ref, kt_ref, v_ref):
    xn = _rms(x_ref[...], g_ref[...]).astype(jnp.bfloat16)
    out = jnp.dot(xn, w_ref[...], preferred_element_type=jnp.float32) + b_ref[...]
    nq = N_Q_HEADS * HEAD_DIM
    nk = N_KV_HEADS * HEAD_DIM
    q_ref[...] = out[:, :nq].astype(jnp.bfloat16)
    kt_ref[...] = out[:, nq:nq + nk].T.astype(jnp.bfloat16)
    v_ref[...] = out[:, nq + nk:].astype(jnp.bfloat16)


def _qkv_call(x2, g, w, b):
    T = x2.shape[0]
    nq = N_Q_HEADS * HEAD_DIM
    nk = N_KV_HEADS * HEAD_DIM
    tm = TM_DENSE
    return pl.pallas_call(
        _qkv_kernel,
        grid=(T // tm,),
        in_specs=[
            pl.BlockSpec((tm, D_MODEL), lambda i: (i, 0)),
            pl.BlockSpec((1, D_MODEL), lambda i: (0, 0)),
            pl.BlockSpec((D_MODEL, nq + 2 * nk), lambda i: (0, 0)),
            pl.BlockSpec((1, nq + 2 * nk), lambda i: (0, 0)),
        ],
        out_specs=[
            pl.BlockSpec((tm, nq), lambda i: (i, 0)),
            pl.BlockSpec((nk, tm), lambda i: (0, i)),
            pl.BlockSpec((tm, nk), lambda i: (i, 0)),
        ],
        out_shape=[
            jax.ShapeDtypeStruct((T, nq), jnp.bfloat16),
            jax.ShapeDtypeStruct((nk, T), jnp.bfloat16),
            jax.ShapeDtypeStruct((T, nk), jnp.bfloat16),
        ],
        compiler_params=_params(("parallel",)),
        name="qkv_proj",
    )(x2, g, w, b)


def _attn_kernel(sink_ref, q_ref, kt_ref, ktp_ref, v_ref, vp_ref, bias_ref, o_ref):
    first_tile = pl.program_id(1) == 0
    nblk = TQ // ATTN_BLOCK
    qi_io = lax.broadcasted_iota(jnp.int32, (ATTN_BLOCK, 2 * ATTN_BLOCK), 0)
    kj_io = lax.broadcasted_iota(jnp.int32, (ATTN_BLOCK, 2 * ATTN_BLOCK), 1)
    dist = qi_io + ATTN_BLOCK - kj_io
    band = (dist >= 0) & (dist < WINDOW)
    low_half = lax.broadcasted_iota(jnp.int32, (ATTN_BLOCK, LANES), 1) < HEAD_DIM
    for qi in range(nblk):
        r0 = qi * ATTN_BLOCK
        if qi == 0:
            kt_blk = jnp.concatenate([ktp_ref[...], kt_ref[:, 0:ATTN_BLOCK]], axis=1)
            v_blk = jnp.concatenate([vp_ref[...], v_ref[0:ATTN_BLOCK, :]], axis=0)
            mask = band & (jnp.logical_not(first_tile) | (kj_io >= ATTN_BLOCK))
        else:
            kt_blk = kt_ref[:, r0 - ATTN_BLOCK:r0 + ATTN_BLOCK]
            v_blk = v_ref[r0 - ATTN_BLOCK:r0 + ATTN_BLOCK, :]
            mask = band
        for m in range(GROUP):
            qg = q_ref[r0:r0 + ATTN_BLOCK, m * LANES:(m + 1) * LANES]
            halves = []
            for half in range(N_KV_HEADS):
                h = m + GROUP * half
                keep = low_half if half == 0 else jnp.logical_not(low_half)
                qpad = jnp.where(keep, qg, jnp.zeros_like(qg))
                s = jnp.dot(qpad, kt_blk, preferred_element_type=jnp.float32)
                s = jnp.where(mask, s + bias_ref[h], -1e30)
                sink = sink_ref[h]
                mx = jnp.maximum(jnp.max(s, axis=-1, keepdims=True), sink)
                p = jnp.exp(s - mx)
                den = jnp.sum(p, axis=-1, keepdims=True) + jnp.exp(sink - mx)
                pv = jnp.dot(p.astype(jnp.bfloat16), v_blk, preferred_element_type=jnp.float32)
                halves.append(pv * (1.0 / den))
            og = jnp.where(low_half, halves[0], halves[1])
            o_ref[r0:r0 + ATTN_BLOCK, m * LANES:(m + 1) * LANES] = og.astype(jnp.bfloat16)


def _attn_call(sinks, q, kt, v, bias, batch, seq):
    T = q.shape[0]
    nq = N_Q_HEADS * HEAD_DIM
    nk = N_KV_HEADS * HEAD_DIM
    tiles = seq // TQ
    per = TQ // ATTN_BLOCK

    def cur(b, j):
        return b * tiles + j

    def prev(b, j):
        return jnp.maximum((b * tiles + j) * per - 1, b * tiles * per)

    return pl.pallas_call(
        _attn_kernel,
        grid=(batch, tiles),
        in_specs=[
            pl.BlockSpec(memory_space=pltpu.SMEM),
            pl.BlockSpec((TQ, nq), lambda b, j: (cur(b, j), 0)),
            pl.BlockSpec((nk, TQ), lambda b, j: (0, cur(b, j))),
            pl.BlockSpec((nk, ATTN_BLOCK), lambda b, j: (0, prev(b, j))),
            pl.BlockSpec((TQ, nk), lambda b, j: (cur(b, j), 0)),
            pl.BlockSpec((ATTN_BLOCK, nk), lambda b, j: (prev(b, j), 0)),
            pl.BlockSpec((N_Q_HEADS, ATTN_BLOCK, 2 * ATTN_BLOCK), lambda b, j: (0, 0, 0)),
        ],
        out_specs=pl.BlockSpec((TQ, nq), lambda b, j: (cur(b, j), 0)),
        out_shape=jax.ShapeDtypeStruct((T, nq), jnp.bfloat16),
        compiler_params=_params(("parallel", "parallel")),
        name="swa_attention",
    )(sinks, q, kt, kt, v, v, bias)


def _router_epilogue(h, gffn_ref, wr_ref, brt_ref, carry_ref,
                     h_out_ref, xrows_ref, ri_ref, rf_ref, cnt_ref):
    tm = h.shape[0]
    h_out_ref[...] = h
    xn = _rms(h, gffn_ref[...])
    for c in range(ROW_CHUNKS):
        xrows_ref[pl.ds(c, tm, stride=ROW_CHUNKS), :] = xn[:, c * LANES:(c + 1) * LANES]

    xh = xn.astype(jnp.bfloat16)
    xl = (xn - xh.astype(jnp.float32)).astype(jnp.bfloat16)
    wcat = wr_ref[...]
    prod = jnp.dot(xh, wcat, preferred_element_type=jnp.float32)
    logits = (prod[:, :LANES] + prod[:, LANES:]) + jnp.dot(
        xl, wcat[:, :LANES], preferred_element_type=jnp.float32)
    lt = logits.T[:ROUTER_ROWS, :] + brt_ref[...]

    gl = [lt[g:g + 1, :] for g in range(N_GROUPS)]
    gmax = functools.reduce(jnp.maximum, gl)
    gexp = [jnp.exp(x - gmax) for x in gl]
    gsum = functools.reduce(lambda a, b: a + b, gexp)
    gprob = [x / gsum for x in gexp]
    g_prob = functools.reduce(jnp.maximum, gprob)
    g_idx = jnp.full(g_prob.shape, N_GROUPS - 1, jnp.int32)
    for g in range(N_GROUPS - 2, -1, -1):
        g_idx = jnp.where(gprob[g] == g_prob, g, g_idx)

    el = []
    for j in range(EXPERTS_PER_GROUP):
        x = lt[SUBLANES + j:SUBLANES + j + 1, :]
        for g in range(1, N_GROUPS):
            r = SUBLANES + g * EXPERTS_PER_GROUP + j
            x = jnp.where(g_idx == g, lt[r:r + 1, :], x)
        el.append(x)
    emax = functools.reduce(jnp.maximum, el)
    eexp = [jnp.exp(x - emax) for x in el]
    esum = functools.reduce(lambda a, b: a + b, eexp)
    eprob = [x / esum for x in eexp]
    p1 = functools.reduce(jnp.maximum, eprob)
    i1 = jnp.full(p1.shape, EXPERTS_PER_GROUP - 1, jnp.int32)
    for j in range(EXPERTS_PER_GROUP - 2, -1, -1):
        i1 = jnp.where(eprob[j] == p1, j, i1)
    rest = [jnp.where(i1 == j, -1.0, eprob[j]) for j in range(EXPERTS_PER_GROUP)]
    p2 = functools.reduce(jnp.maximum, rest)
    i2 = jnp.full(p2.shape, EXPERTS_PER_GROUP - 1, jnp.int32)
    for j in range(EXPERTS_PER_GROUP - 2, -1, -1):
        i2 = jnp.where(rest[j] == p2, j, i2)
    psum = p1 + p2
    gate0 = g_prob * (p1 / psum)
    gate1 = g_prob * (p2 / psum)
    e0 = g_idx * EXPERTS_PER_GROUP + i1
    e1 = g_idx * EXPERTS_PER_GROUP + i2

    eio = lax.broadcasted_iota(jnp.int32, (N_EXPERTS, tm), 0)
    oh0 = (eio == e0).astype(jnp.float32)
    oh1 = (eio == e1).astype(jnp.float32)
    both = oh0 + oh1
    tr = lax.broadcasted_iota(jnp.int32, (tm, tm), 0)
    tc = lax.broadcasted_iota(jnp.int32, (tm, tm), 1)
    upper = (tr < tc).astype(jnp.bfloat16)
    before = jnp.dot(both.astype(jnp.bfloat16), upper, preferred_element_type=jnp.float32)
    before = before + carry_ref[...]
    rank0 = jnp.sum(oh0 * before, axis=0, keepdims=True)
    rank1 = jnp.sum(oh1 * (before + oh0), axis=0, keepdims=True)
    carry_ref[...] = carry_ref[...] + jnp.sum(both, axis=1, keepdims=True)
    cnt_ref[...] = jnp.broadcast_to(carry_ref[...], cnt_ref.shape)

    zi = jnp.zeros_like(e0)
    ri_ref[...] = jnp.concatenate(
        [e0, e1, rank0.astype(jnp.int32), rank1.astype(jnp.int32), zi, zi, zi, zi], axis=0)
    zf = jnp.zeros_like(gate0)
    rf_ref[...] = jnp.concatenate([gate0, gate1, zf, zf, zf, zf, zf, zf], axis=0)


def _epilogue_out_specs(tm):
    return [
        pl.BlockSpec((tm, D_MODEL), lambda i: (i, 0)),
        pl.BlockSpec((tm * ROW_CHUNKS, LANES), lambda i: (i, 0)),
        pl.BlockSpec((SUBLANES, tm), lambda i: (0, i)),
        pl.BlockSpec((SUBLANES, tm), lambda i: (0, i)),
        pl.BlockSpec((N_EXPERTS, LANES), lambda i: (0, 0)),
    ]


def _epilogue_out_shapes(T):
    return [
        jax.ShapeDtypeStruct((T, D_MODEL), jnp.float32),
        jax.ShapeDtypeStruct((T * ROW_CHUNKS, LANES), jnp.float32),
        jax.ShapeDtypeStruct((SUBLANES, T), jnp.int32),
        jax.ShapeDtypeStruct((SUBLANES, T), jnp.float32),
        jax.ShapeDtypeStruct((N_EXPERTS, LANES), jnp.float32),
    ]


def _epilogue_in_specs():
    return [
        pl.BlockSpec((1, D_MODEL), lambda i: (0, 0)),
        pl.BlockSpec((D_MODEL, 2 * LANES), lambda i: (0, 0)),
        pl.BlockSpec((ROUTER_ROWS, 1), lambda i: (0, 0)),
    ]


def _attn_out_kernel(a_ref, w_ref, b_ref, res_ref, gffn_ref, wr_ref, brt_ref,
                     h_out_ref, xrows_ref, ri_ref, rf_ref, cnt_ref, carry_ref):
    @pl.when(pl.program_id(0) == 0)
    def _():
        carry_ref[...] = jnp.zeros_like(carry_ref)

    mix = jnp.dot(a_ref[...], w_ref[...], preferred_element_type=jnp.float32) + b_ref[...]
    h = res_ref[...] + mix
    _router_epilogue(h, gffn_ref, wr_ref, brt_ref, carry_ref,
                     h_out_ref, xrows_ref, ri_ref, rf_ref, cnt_ref)


def _attn_out_call(a, w, b, res, gffn, wr, brt):
    T = a.shape[0]
    tm = TM_DENSE
    return pl.pallas_call(
        _attn_out_kernel,
        grid=(T // tm,),
        in_specs=[
            pl.BlockSpec((tm, D_MODEL), lambda i: (i, 0)),
            pl.BlockSpec((D_MODEL, D_MODEL), lambda i: (0, 0)),
            pl.BlockSpec((1, D_MODEL), lambda i: (0, 0)),
            pl.BlockSpec((tm, D_MODEL), lambda i: (i, 0)),
        ] + _epilogue_in_specs(),
        out_specs=_epilogue_out_specs(tm),
        out_shape=_epilogue_out_shapes(T),
        scratch_shapes=[pltpu.VMEM((N_EXPERTS, 1), jnp.float32)],
        compiler_params=_params(("arbitrary",)),
        name="attn_out_router",
    )(a, w, b, res, gffn, wr, brt)


def _conv_mixer_kernel(tiles_per_seq, x_ref, g_ref, win_ref, cw_ref, wout_ref,
                       gffn_ref, wr_ref, brt_ref,
                       h_out_ref, xrows_ref, ri_ref, rf_ref, cnt_ref, carry_ref, tail_ref):
    i = pl.program_id(0)
    tm = x_ref.shape[0]

    @pl.when(i == 0)
    def _():
        carry_ref[...] = jnp.zeros_like(carry_ref)

    @pl.when(i % tiles_per_seq == 0)
    def _():
        tail_ref[...] = jnp.zeros_like(tail_ref)

    x = x_ref[...]
    xn = _rms(x, g_ref[...]).astype(jnp.bfloat16)
    bcu = jnp.dot(xn, win_ref[...], preferred_element_type=jnp.float32)
    b_gate = bcu[:, :D_MODEL]
    z = bcu[:, D_MODEL:2 * D_MODEL] * bcu[:, 2 * D_MODEL:]
    row = lax.broadcasted_iota(jnp.int32, (tm, D_MODEL), 0)
    tail = tail_ref[...]
    t1 = tail[SUBLANES - 1:SUBLANES, :]
    t2 = tail[SUBLANES - 2:SUBLANES - 1, :]
    z1 = jnp.where(row == 0, t1, pltpu.roll(z, 1, axis=0))
    z2 = jnp.where(row == 0, t2, jnp.where(row == 1, t1, pltpu.roll(z, 2, axis=0)))
    tail_ref[...] = z[tm - SUBLANES:, :]
    conv = z2 * cw_ref[0:1, :]
    conv = conv + z1 * cw_ref[1:2, :]
    conv = conv + z * cw_ref[2:3, :]
    gated = (b_gate * conv).astype(jnp.bfloat16)
    h = x + jnp.dot(gated, wout_ref[...], preferred_element_type=jnp.float32)
    _router_epilogue(h, gffn_ref, wr_ref, brt_ref, carry_ref,
                     h_out_ref, xrows_ref, ri_ref, rf_ref, cnt_ref)


def _conv_mixer_call(x2, g, win, cw, wout, gffn, wr, brt, seq):
    T = x2.shape[0]
    tm = TM_MIX1
    return pl.pallas_call(
        functools.partial(_conv_mixer_kernel, seq // tm),
        grid=(T // tm,),
        in_specs=[
            pl.BlockSpec((tm, D_MODEL), lambda i: (i, 0)),
            pl.BlockSpec((1, D_MODEL), lambda i: (0, 0)),
            pl.BlockSpec((D_MODEL, 3 * D_MODEL), lambda i: (0, 0)),
            pl.BlockSpec((CONV_WIDTH, D_MODEL), lambda i: (0, 0)),
            pl.BlockSpec((D_MODEL, D_MODEL), lambda i: (0, 0)),
        ] + _epilogue_in_specs(),
        out_specs=_epilogue_out_specs(tm),
        out_shape=_epilogue_out_shapes(T),
        scratch_shapes=[pltpu.VMEM((N_EXPERTS, 1), jnp.float32),
                        pltpu.VMEM((SUBLANES, D_MODEL), jnp.float32)],
        compiler_params=_params(("arbitrary",)),
        name="conv_mixer_router",
    )(x2, g, win, cw, wout, gffn, wr, brt)


def _row_copy(src_hbm, src_row, dst_hbm, dst_row, sem):
    return pltpu.make_async_copy(
        src_hbm.at[pl.ds(pl.multiple_of(src_row, ROW_CHUNKS), ROW_CHUNKS)],
        dst_hbm.at[pl.ds(pl.multiple_of(dst_row, ROW_CHUNKS), ROW_CHUNKS)], sem)


def _dispatch_kernel(d0_ref, d1_ref, padrow_ref, x_ref, xs_hbm, zero_ref, sem, zsem):
    i = pl.program_id(0)
    tile_rows = TM_EXPERT * ROW_CHUNKS

    @pl.when(i == 0)
    def _():
        zero_ref[...] = jnp.zeros_like(zero_ref)

        def zcopy(e):
            return pltpu.make_async_copy(
                zero_ref,
                xs_hbm.at[pl.ds(pl.multiple_of(padrow_ref[e], tile_rows), tile_rows)], zsem)

        def zstart(e, c):
            @pl.when(padrow_ref[e] >= 0)
            def _():
                zcopy(e).start()
            return c

        def zwait(e, c):
            @pl.when(padrow_ref[e] >= 0)
            def _():
                zcopy(e).wait()
            return c

        lax.fori_loop(0, N_EXPERTS, zstart, 0)
        lax.fori_loop(0, N_EXPERTS, zwait, 0)

        def tcopy(b):
            return pltpu.make_async_copy(
                zero_ref, xs_hbm.at[pl.ds(pl.multiple_of(b * tile_rows, tile_rows), tile_rows)], zsem)

        n_used = padrow_ref[N_EXPERTS]
        n_blk = xs_hbm.shape[0] // tile_rows
        lax.fori_loop(n_used, n_blk, lambda b, c: (tcopy(b).start(), c)[1], 0)
        lax.fori_loop(n_used, n_blk, lambda b, c: (tcopy(b).wait(), c)[1], 0)

    def issue(t, c):
        src = t * ROW_CHUNKS
        _row_copy(x_ref, src, xs_hbm, d0_ref[t], sem).start()
        _row_copy(x_ref, src, xs_hbm, d1_ref[t], sem).start()
        return c

    lax.fori_loop(0, TM_DISPATCH, issue, 0, unroll=DMA_UNROLL)
    for _ in range(2):
        pltpu.make_async_copy(x_ref, xs_hbm.at[pl.ds(0, TM_DISPATCH * ROW_CHUNKS)], sem).wait()


def _dispatch_call(d0, d1, padrow, xrows, capacity):
    T = d0.shape[0]
    return pl.pallas_call(
        _dispatch_kernel,
        grid=(T // TM_DISPATCH,),
        in_specs=[
            pl.BlockSpec((TM_DISPATCH,), lambda i: (i,), memory_space=pltpu.SMEM),
            pl.BlockSpec((TM_DISPATCH,), lambda i: (i,), memory_space=pltpu.SMEM),
            pl.BlockSpec(memory_space=pltpu.SMEM),
            pl.BlockSpec((TM_DISPATCH * ROW_CHUNKS, LANES), lambda i: (i, 0)),
        ],
        out_specs=pl.BlockSpec(memory_space=pl.ANY),
        out_shape=jax.ShapeDtypeStruct((capacity * ROW_CHUNKS, LANES), jnp.float32),
        scratch_shapes=[pltpu.VMEM((TM_EXPERT * ROW_CHUNKS, LANES), jnp.float32),
                        pltpu.SemaphoreType.DMA, pltpu.SemaphoreType.DMA],
        compiler_params=_params(("arbitrary",)),
        name="moe_dispatch",
    )(d0, d1, padrow, xrows)


def _expert_kernel(be_ref, first_ref, nu_ref, xs_ref, wgu_ref, wd_ref, ys_ref, wgu_bf, wd_bf):
    i = pl.program_id(0)
    tm = TM_EXPERT
    live = i < nu_ref[0]

    @pl.when(live & (first_ref[i] == 1))
    def _():
        wgu_bf[...] = wgu_ref[0, 0].astype(jnp.bfloat16)
        wd_bf[...] = wd_ref[0, 0].astype(jnp.bfloat16)

    @pl.when(live)
    def _():
        x = jnp.concatenate(
            [xs_ref[pl.ds(c, tm, stride=ROW_CHUNKS), :] for c in range(ROW_CHUNKS)], axis=-1)
        gu = jnp.dot(x.astype(jnp.bfloat16), wgu_bf[...], preferred_element_type=jnp.float32)
        g = gu[:, :EXPERT_FF]
        u = gu[:, EXPERT_FF:]
        act = (g / (1.0 + jnp.exp(-g))) * u
        y = jnp.dot(act.astype(jnp.bfloat16), wd_bf[...], preferred_element_type=jnp.float32)
        for c in range(ROW_CHUNKS):
            ys_ref[pl.ds(c, tm, stride=ROW_CHUNKS), :] = y[:, c * LANES:(c + 1) * LANES]

    @pl.when(jnp.logical_not(live))
    def _():
        ys_ref[...] = jnp.zeros_like(ys_ref)


def _expert_call(layer, blk_expert, blk_first, n_used, xs, wgu, wd, capacity):
    tm = TM_EXPERT
    n_blk = capacity // tm

    def slot_map(i, be, bf, nu):
        return (jnp.minimum(i, nu[0] - 1), 0)

    def w_map(i, be, bf, nu):
        return (layer, be[i], 0, 0)

    return pl.pallas_call(
        _expert_kernel,
        grid_spec=pltpu.PrefetchScalarGridSpec(
            num_scalar_prefetch=3,
            grid=(n_blk,),
            in_specs=[
                pl.BlockSpec((tm * ROW_CHUNKS, LANES), slot_map),
                pl.BlockSpec((1, 1, D_MODEL, 2 * EXPERT_FF), w_map),
                pl.BlockSpec((1, 1, EXPERT_FF, D_MODEL), w_map),
            ],
            out_specs=pl.BlockSpec((tm * ROW_CHUNKS, LANES), lambda i, be, bf, nu: (i, 0)),
            scratch_shapes=[pltpu.VMEM((D_MODEL, 2 * EXPERT_FF), jnp.bfloat16),
                            pltpu.VMEM((EXPERT_FF, D_MODEL), jnp.bfloat16)],
        ),
        out_shape=jax.ShapeDtypeStruct((capacity * ROW_CHUNKS, LANES), jnp.float32),
        compiler_params=_params(("arbitrary",)),
        name="moe_experts",
    )(blk_expert, blk_first, n_used, xs, wgu, wd)


def _combine_kernel(final_norm, d0_ref, d1_ref, d0n_ref, d1n_ref, ys_hbm, h_ref, gate_ref, gfin_ref,
                    o_ref, y0_ref, y1_ref, sem):
    tm = TM_COMBINE
    i = pl.program_id(0)
    slot = i % 2

    def issue_tile(d0r, d1r, s):
        def issue(t, c):
            dst = t * ROW_CHUNKS
            _row_copy(ys_hbm, d0r[t], y0_ref.at[s], dst, sem.at[s]).start()
            _row_copy(ys_hbm, d1r[t], y1_ref.at[s], dst, sem.at[s]).start()
            return c
        lax.fori_loop(0, tm, issue, 0, unroll=DMA_UNROLL)

    @pl.when(i == 0)
    def _():
        issue_tile(d0_ref, d1_ref, 0)

    @pl.when(i + 1 < pl.num_programs(0))
    def _():
        issue_tile(d0n_ref, d1n_ref, 1 - slot)

    for buf in (y0_ref, y1_ref):
        pltpu.make_async_copy(ys_hbm.at[pl.ds(0, tm * ROW_CHUNKS)], buf.at[slot], sem.at[slot]).wait()

    g0 = gate_ref[:, 0:1]
    g1 = gate_ref[:, 1:2]
    cols = []
    for c in range(ROW_CHUNKS):
        y0 = y0_ref[slot, pl.ds(c, tm, stride=ROW_CHUNKS), :]
        y1 = y1_ref[slot, pl.ds(c, tm, stride=ROW_CHUNKS), :]
        cols.append(h_ref[:, c * LANES:(c + 1) * LANES] + (y0 * g0 + y1 * g1))
    h = jnp.concatenate(cols, axis=-1)
    if final_norm:
        h = _rms(h, gfin_ref[...])
    o_ref[...] = h


def _combine_call(d0, d1, ys, h, gates, gfin, final_norm):
    T = h.shape[0]
    tm = TM_COMBINE
    last = T // tm - 1
    return pl.pallas_call(
        functools.partial(_combine_kernel, final_norm),
        grid=(T // tm,),
        in_specs=[
            pl.BlockSpec((tm,), lambda i: (i,), memory_space=pltpu.SMEM),
            pl.BlockSpec((tm,), lambda i: (i,), memory_space=pltpu.SMEM),
            pl.BlockSpec((tm,), lambda i: (jnp.minimum(i + 1, last),), memory_space=pltpu.SMEM),
            pl.BlockSpec((tm,), lambda i: (jnp.minimum(i + 1, last),), memory_space=pltpu.SMEM),
            pl.BlockSpec(memory_space=pl.ANY),
            pl.BlockSpec((tm, D_MODEL), lambda i: (i, 0)),
            pl.BlockSpec((tm, 2), lambda i: (i, 0)),
            pl.BlockSpec((1, D_MODEL), lambda i: (0, 0)),
        ],
        out_specs=pl.BlockSpec((tm, D_MODEL), lambda i: (i, 0)),
        out_shape=jax.ShapeDtypeStruct((T, D_MODEL), jnp.float32),
        scratch_shapes=[pltpu.VMEM((2, tm * ROW_CHUNKS, LANES), jnp.float32),
                        pltpu.VMEM((2, tm * ROW_CHUNKS, LANES), jnp.float32),
                        pltpu.SemaphoreType.DMA((2,))],
        compiler_params=_params(("arbitrary",)),
        name="moe_combine",
    )(d0, d1, d0, d1, ys, h, gates, gfin)


def _group_heads_last(w):
    lead = w.shape[:-1]
    w = w.reshape(lead + (N_KV_HEADS, GROUP, HEAD_DIM))
    w = jnp.swapaxes(w, -3, -2)
    return w.reshape(lead + (N_Q_HEADS * HEAD_DIM,))


def _rel_bucket_table():
    qi = np.arange(ATTN_BLOCK)[:, None]
    kj = np.arange(2 * ATTN_BLOCK)[None, :]
    dist = np.maximum(qi + ATTN_BLOCK - kj, 0)
    max_exact = N_REL_BUCKETS // 2
    d = np.maximum(dist, max_exact).astype(np.float32)
    large = max_exact + (np.log(d / np.float32(max_exact)) / np.float32(math.log(REL_MAX_DISTANCE / max_exact))
                         * np.float32(N_REL_BUCKETS - max_exact)).astype(np.int32)
    large = np.minimum(large, N_REL_BUCKETS - 1)
    return np.where(dist < max_exact, dist, large).astype(np.int32)


def _rel_bias_blocks(rel_bias):
    onehot = (jnp.asarray(_rel_bucket_table())[..., None] == jnp.arange(N_REL_BUCKETS)).astype(jnp.float32)
    return jnp.einsum("qkb,bh->hqk", onehot, rel_bias.astype(jnp.float32),
                      precision=lax.Precision.HIGHEST)


def _router_weights(w_group, b_group, w_expert, b_expert):
    zw = jnp.zeros((D_MODEL, SUBLANES - N_GROUPS), jnp.float32)
    zw2 = jnp.zeros((D_MODEL, LANES - SUBLANES - N_EXPERTS), jnp.float32)
    wr = jnp.concatenate([w_group, zw, w_expert, zw2], axis=1)
    wh = wr.astype(jnp.bfloat16)
    wl = (wr - wh.astype(jnp.float32)).astype(jnp.bfloat16)
    wr = jnp.concatenate([wh, wl], axis=1)
    brt = jnp.concatenate([b_group, jnp.zeros((SUBLANES - N_GROUPS,), jnp.float32), b_expert])[:, None]
    return wr, brt


def _moe(layer, h, xrows, ri, rf, cnt, w_gate_up, w_down, gfin, final_norm):
    T = h.shape[0]
    tm = TM_EXPERT
    capacity = -(-(2 * T) // tm) * tm + N_EXPERTS * tm
    counts = cnt[:, 0].astype(jnp.int32)
    ntile = (counts + tm - 1) // tm
    tend = jnp.cumsum(ntile)
    pstart = (tend - ntile) * tm
    n_used = tend[-1:]
    dest0 = (pstart[ri[0]] + ri[2]) * ROW_CHUNKS
    dest1 = (pstart[ri[1]] + ri[3]) * ROW_CHUNKS
    padrow = jnp.concatenate([jnp.where(ntile > 0, (tend - 1) * (tm * ROW_CHUNKS), -1), n_used]
                             ).astype(jnp.int32)
    blk = jnp.minimum(jnp.arange(capacity // tm, dtype=jnp.int32), n_used[0] - 1)
    blk_expert = jnp.minimum(jnp.sum((tend[None, :] <= blk[:, None]).astype(jnp.int32), axis=1),
                             N_EXPERTS - 1)
    blk_first = jnp.concatenate([jnp.ones((1,), jnp.int32),
                                 (blk_expert[1:] != blk_expert[:-1]).astype(jnp.int32)])

    xs = _dispatch_call(dest0, dest1, padrow, xrows, capacity)
    ys = _expert_call(layer, blk_expert, blk_first, n_used.astype(jnp.int32), xs, w_gate_up, w_down,
                      capacity)
    gates = rf[:2].T
    return _combine_call(dest0, dest1, ys, h, gates, gfin, final_norm)


def kernel(x, norm_mix, norm_ffn, final_norm, rel_bias, attn_w_qkv, attn_b_qkv, attn_w_o, attn_b_o,
           attn_sinks, conv_w_in, conv_w, conv_w_out, moe_w_group, moe_b_group, moe_w_expert,
           moe_b_expert, moe_w_gate_up, moe_w_down):
    B, S, D = x.shape
    T = B * S
    x2 = x.reshape(T, D)
    nq = N_Q_HEADS * HEAD_DIM
    scale = HEAD_DIM ** -0.5

    wqkv = attn_w_qkv[0]
    bqkv = attn_b_qkv[0]
    w_all = jnp.concatenate([_group_heads_last(wqkv[:, :nq]) * scale, wqkv[:, nq:]],
                            axis=1).astype(jnp.bfloat16)
    b_all = jnp.concatenate([_group_heads_last(bqkv[:nq]) * scale, bqkv[nq:]])[None, :]
    q, kt, v = _qkv_call(x2, norm_mix[0][None, :], w_all, b_all)
    a = _attn_call(attn_sinks[0].astype(jnp.float32), q, kt, v, _rel_bias_blocks(rel_bias), B, S)
    wr, brt = _router_weights(moe_w_group[0], moe_b_group[0], moe_w_expert[0], moe_b_expert[0])
    w_o = _group_heads_last(attn_w_o[0].T).T.astype(jnp.bfloat16)
    h, xrows, ri, rf, cnt = _attn_out_call(
        a, w_o, attn_b_o[0][None, :], x2, norm_ffn[0][None, :], wr, brt)
    h = _moe(0, h, xrows, ri, rf, cnt, moe_w_gate_up, moe_w_down, final_norm[None, :], False)

    wr, brt = _router_weights(moe_w_group[1], moe_b_group[1], moe_w_expert[1], moe_b_expert[1])
    h, xrows, ri, rf, cnt = _conv_mixer_call(
        h, norm_mix[1][None, :], conv_w_in[0].astype(jnp.bfloat16), conv_w[0],
        conv_w_out[0].astype(jnp.bfloat16), norm_ffn[1][None, :], wr, brt, S)
    out = _moe(1, h, xrows, ri, rf, cnt, moe_w_gate_up, moe_w_down, final_norm[None, :], True)
    return out.reshape(B, S, D)
```

```python
import functools
import math

import numpy as np
import jax
import jax.numpy as jnp
from jax import lax
from jax.experimental import pallas as pl
from jax.experimental.pallas import tpu as pltpu

D_MODEL = 1024
N_Q_HEADS = 16
N_KV_HEADS = 2
HEAD_DIM = 64
GROUP = N_Q_HEADS // N_KV_HEADS
WINDOW = 128
ATTN_BLOCK = 128
N_REL_BUCKETS = 32
REL_MAX_DISTANCE = 128
CONV_WIDTH = 3
N_GROUPS = 4
EXPERTS_PER_GROUP = 8
N_EXPERTS = N_GROUPS * EXPERTS_PER_GROUP
EXPERT_FF = 512
RMS_EPS = 1e-5

LANES = 128
SUBLANES = 8
ROW_CHUNKS = D_MODEL // LANES
VMEM_LIMIT = 56 * 1024 * 1024

TM_DENSE = 512
TQ = 512
TM_MIX1 = 512
TM_EXPERT = 512
TM_DISPATCH = 512
TM_COMBINE = 256
DMA_UNROLL = 8
ROUTER_ROWS = 40


def _rms(x, g):
    return x * lax.rsqrt(jnp.mean(x * x, axis=-1, keepdims=True) + RMS_EPS) * g


def _params(sem):
    return pltpu.CompilerParams(dimension_semantics=sem, vmem_limit_bytes=VMEM_LIMIT)


def _qkv_kernel(x_ref, g_ref, w_ref, b_ref, q_ref, kt_ref, v_ref):
    xn = _rms(x_ref[...], g_ref[...]).astype(jnp.bfloat16)
    out = jnp.dot(xn, w_ref[...], preferred_element_type=jnp.float32) + b_ref[...]
    nq = N_Q_HEADS * HEAD_DIM
    nk = N_KV_HEADS * HEAD_DIM
    q_ref[...] = out[:, :nq].astype(jnp.bfloat16)
    kt_ref[...] = out[:, nq:nq + nk].T.astype(jnp.bfloat16)
    v_ref[...] = out[:, nq + nk:].astype(jnp.bfloat16)


def _qkv_call(x2, g, w, b):
    T = x2.shape[0]
    nq = N_Q_HEADS * HEAD_DIM
    nk = N_KV_HEADS * HEAD_DIM
    tm = TM_DENSE
    return pl.pallas_call(
        _qkv_kernel,
        grid=(T // tm,),
        in_specs=[
            pl.BlockSpec((tm, D_MODEL), lambda i: (i, 0)),
            pl.BlockSpec((1, D_MODEL), lambda i: (0, 0)),
            pl.BlockSpec((D_MODEL, nq + 2 * nk), lambda i: (0, 0)),
            pl.BlockSpec((1, nq + 2 * nk), lambda i: (0, 0)),
        ],
        out_specs=[
            pl.BlockSpec((tm, nq), lambda i: (i, 0)),
            pl.BlockSpec((nk, tm), lambda i: (0, i)),
            pl.BlockSpec((tm, nk), lambda i: (i, 0)),
        ],
        out_shape=[
            jax.ShapeDtypeStruct((T, nq), jnp.bfloat16),
            jax.ShapeDtypeStruct((nk, T), jnp.bfloat16),
            jax.ShapeDtypeStruct((T, nk), jnp.bfloat16),
        ],
        compiler_params=_params(("parallel",)),
        name="qkv_proj",
    )(x2, g, w, b)


def _attn_kernel(q_ref, kt_ref, ktp_ref, v_ref, vp_ref, bias_ref, o_ref):
    first_tile = pl.program_id(1) == 0
    nblk = TQ // ATTN_BLOCK
    qi_io = lax.broadcasted_iota(jnp.int32, (ATTN_BLOCK, 2 * ATTN_BLOCK), 0)
    kj_io = lax.broadcasted_iota(jnp.int32, (ATTN_BLOCK, 2 * ATTN_BLOCK), 1)
    dist = qi_io + ATTN_BLOCK - kj_io
    band = (dist >= 0) & (dist < WINDOW)
    sink_col = kj_io == 0
    kt_keep = lax.broadcasted_iota(jnp.int32, (N_KV_HEADS * HEAD_DIM, 2 * ATTN_BLOCK), 1) > 0
    v_keep = lax.broadcasted_iota(jnp.int32, (2 * ATTN_BLOCK, N_KV_HEADS * HEAD_DIM), 0) > 0
    low_half = lax.broadcasted_iota(jnp.int32, (ATTN_BLOCK, LANES), 1) < HEAD_DIM
    for qi in range(nblk):
        r0 = qi * ATTN_BLOCK
        if qi == 0:
            kt_blk = jnp.concatenate([ktp_ref[...], kt_ref[:, 0:ATTN_BLOCK]], axis=1)
            v_blk = jnp.concatenate([vp_ref[...], v_ref[0:ATTN_BLOCK, :]], axis=0)
            mask = (band & (jnp.logical_not(first_tile) | (kj_io >= ATTN_BLOCK))) | sink_col
        else:
            kt_blk = kt_ref[:, r0 - ATTN_BLOCK:r0 + ATTN_BLOCK]
            v_blk = v_ref[r0 - ATTN_BLOCK:r0 + ATTN_BLOCK, :]
            mask = band | sink_col
        kt_blk = jnp.where(kt_keep, kt_blk, jnp.zeros_like(kt_blk))
        v_blk = jnp.where(v_keep, v_blk, jnp.zeros_like(v_blk))
        for m in range(GROUP):
            qg = q_ref[r0:r0 + ATTN_BLOCK, m * LANES:(m + 1) * LANES]
            halves = []
            for half in range(N_KV_HEADS):
                h = m + GROUP * half
                keep = low_half if half == 0 else jnp.logical_not(low_half)
                qpad = jnp.where(keep, qg, jnp.zeros_like(qg))
                s = jnp.dot(qpad, kt_blk, preferred_element_type=jnp.float32)
                s = jnp.where(mask, s + bias_ref[h], -1e30)
                mx = jnp.max(s, axis=-1, keepdims=True)
                p = jnp.exp(s - mx)
                den = jnp.sum(p, axis=-1, keepdims=True)
                pv = jnp.dot(p.astype(jnp.bfloat16), v_blk, preferred_element_type=jnp.float32)
                halves.append(pv * (1.0 / den))
            og = jnp.where(low_half, halves[0], halves[1])
            o_ref[r0:r0 + ATTN_BLOCK, m * LANES:(m + 1) * LANES] = og.astype(jnp.bfloat16)


def _attn_call(q, kt, v, bias, batch, seq):
    T = q.shape[0]
    nq = N_Q_HEADS * HEAD_DIM
    nk = N_KV_HEADS * HEAD_DIM
    tiles = seq // TQ
    per = TQ // ATTN_BLOCK

    def cur(b, j):
        return b * tiles + j

    def prev(b, j):
        return jnp.maximum((b * tiles + j) * per - 1, b * tiles * per)

    return pl.pallas_call(
        _attn_kernel,
        grid=(batch, tiles),
        in_specs=[
            pl.BlockSpec((TQ, nq), lambda b, j: (cur(b, j), 0)),
            pl.BlockSpec((nk, TQ), lambda b, j: (0, cur(b, j))),
            pl.BlockSpec((nk, ATTN_BLOCK), lambda b, j: (0, prev(b, j))),
            pl.BlockSpec((TQ, nk), lambda b, j: (cur(b, j), 0)),
            pl.BlockSpec((ATTN_BLOCK, nk), lambda b, j: (prev(b, j), 0)),
            pl.BlockSpec((N_Q_HEADS, ATTN_BLOCK, 2 * ATTN_BLOCK), lambda b, j: (0, 0, 0)),
        ],
        out_specs=pl.BlockSpec((TQ, nq), lambda b, j: (cur(b, j), 0)),
        out_shape=jax.ShapeDtypeStruct((T, nq), jnp.bfloat16),
        compiler_params=_params(("parallel", "parallel")),
        name="swa_attention",
    )(q, kt, kt, v, v, bias)


def _router_epilogue(h, gffn_ref, wr_ref, brt_ref, carry_ref,
                     h_out_ref, xrows_ref, ri_ref, rf_ref, cnt_ref):
    tm = h.shape[0]
    h_out_ref[...] = h
    xn = _rms(h, gffn_ref[...])
    for c in range(ROW_CHUNKS):
        xrows_ref[pl.ds(c, tm, stride=ROW_CHUNKS), :] = xn[:, c * LANES:(c + 1) * LANES]

    xh = xn.astype(jnp.bfloat16)
    xl = (xn - xh.astype(jnp.float32)).astype(jnp.bfloat16)
    wcat = wr_ref[...]
    prod = jnp.dot(xh, wcat, preferred_element_type=jnp.float32)
    logits = (prod[:, :LANES] + prod[:, LANES:]) + jnp.dot(
        xl, wcat[:, :LANES], preferred_element_type=jnp.float32)
    lt = logits.T[:ROUTER_ROWS, :] + brt_ref[...]

    gl = [lt[g:g + 1, :] for g in range(N_GROUPS)]
    gmax = functools.reduce(jnp.maximum, gl)
    gexp = [jnp.exp(x - gmax) for x in gl]
    gsum = functools.reduce(lambda a, b: a + b, gexp)
    gprob = [x / gsum for x in gexp]
    g_prob = functools.reduce(jnp.maximum, gprob)
    g_idx = jnp.full(g_prob.shape, N_GROUPS - 1, jnp.int32)
    for g in range(N_GROUPS - 2, -1, -1):
        g_idx = jnp.where(gprob[g] == g_prob, g, g_idx)

    el = []
    for j in range(EXPERTS_PER_GROUP):
        x = lt[SUBLANES + j:SUBLANES + j + 1, :]
        for g in range(1, N_GROUPS):
            r = SUBLANES + g * EXPERTS_PER_GROUP + j
            x = jnp.where(g_idx == g, lt[r:r + 1, :], x)
        el.append(x)
    emax = functools.reduce(jnp.maximum, el)
    eexp = [jnp.exp(x - emax) for x in el]
    esum = functools.reduce(lambda a, b: a + b, eexp)
    eprob = [x / esum for x in eexp]
    p1 = functools.reduce(jnp.maximum, eprob)
    i1 = jnp.full(p1.shape, EXPERTS_PER_GROUP - 1, jnp.int32)
    for j in range(EXPERTS_PER_GROUP - 2, -1, -1):
        i1 = jnp.where(eprob[j] == p1, j, i1)
    rest = [jnp.where(i1 == j, -1.0, eprob[j]) for j in range(EXPERTS_PER_GROUP)]
    p2 = functools.reduce(jnp.maximum, rest)
    i2 = jnp.full(p2.shape, EXPERTS_PER_GROUP - 1, jnp.int32)
    for j in range(EXPERTS_PER_GROUP - 2, -1, -1):
        i2 = jnp.where(rest[j] == p2, j, i2)
    psum = p1 + p2
    gate0 = g_prob * (p1 / psum)
    gate1 = g_prob * (p2 / psum)
    e0 = g_idx * EXPERTS_PER_GROUP + i1
    e1 = g_idx * EXPERTS_PER_GROUP + i2

    eio = lax.broadcasted_iota(jnp.int32, (N_EXPERTS, tm), 0)
    oh0 = (eio == e0).astype(jnp.float32)
    oh1 = (eio == e1).astype(jnp.float32)
    both = oh0 + oh1
    tr = lax.broadcasted_iota(jnp.int32, (tm, tm), 0)
    tc = lax.broadcasted_iota(jnp.int32, (tm, tm), 1)
    upper = (tr < tc).astype(jnp.bfloat16)
    before = jnp.dot(both.astype(jnp.bfloat16), upper, preferred_element_type=jnp.float32)
    before = before + carry_ref[...]
    rank0 = jnp.sum(oh0 * before, axis=0, keepdims=True)
    rank1 = jnp.sum(oh1 * (before + oh0), axis=0, keepdims=True)
    carry_ref[...] = carry_ref[...] + jnp.sum(both, axis=1, keepdims=True)
    cnt_ref[...] = jnp.broadcast_to(carry_ref[...], cnt_ref.shape)

    zi = jnp.zeros_like(e0)
    ri_ref[...] = jnp.concatenate(
        [e0, e1, rank0.astype(jnp.int32), rank1.astype(jnp.int32), zi, zi, zi, zi], axis=0)
    zf = jnp.zeros_like(gate0)
    rf_ref[...] = jnp.concatenate([gate0, gate1, zf, zf, zf, zf, zf, zf], axis=0)


def _epilogue_out_specs(tm):
    return [
        pl.BlockSpec((tm, D_MODEL), lambda i: (i, 0)),
        pl.BlockSpec((tm * ROW_CHUNKS, LANES), lambda i: (i, 0)),
        pl.BlockSpec((SUBLANES, tm), lambda i: (0, i)),
        pl.BlockSpec((SUBLANES, tm), lambda i: (0, i)),
        pl.BlockSpec((N_EXPERTS, LANES), lambda i: (0, 0)),
    ]


def _epilogue_out_shapes(T):
    return [
        jax.ShapeDtypeStruct((T, D_MODEL), jnp.float32),
        jax.ShapeDtypeStruct((T * ROW_CHUNKS, LANES), jnp.float32),
        jax.ShapeDtypeStruct((SUBLANES, T), jnp.int32),
        jax.ShapeDtypeStruct((SUBLANES, T), jnp.float32),
        jax.ShapeDtypeStruct((N_EXPERTS, LANES), jnp.float32),
    ]


def _epilogue_in_specs():
    return [
        pl.BlockSpec((1, D_MODEL), lambda i: (0, 0)),
        pl.BlockSpec((D_MODEL, 2 * LANES), lambda i: (0, 0)),
        pl.BlockSpec((ROUTER_ROWS, 1), lambda i: (0, 0)),
    ]


def _attn_out_kernel(a_ref, w_ref, b_ref, res_ref, gffn_ref, wr_ref, brt_ref,
                     h_out_ref, xrows_ref, ri_ref, rf_ref, cnt_ref, carry_ref):
    @pl.when(pl.program_id(0) == 0)
    def _():
        carry_ref[...] = jnp.zeros_like(carry_ref)

    mix = jnp.dot(a_ref[...], w_ref[...], preferred_element_type=jnp.float32) + b_ref[...]
    h = res_ref[...] + mix
    _router_epilogue(h, gffn_ref, wr_ref, brt_ref, carry_ref,
                     h_out_ref, xrows_ref, ri_ref, rf_ref, cnt_ref)


def _attn_out_call(a, w, b, res, gffn, wr, brt):
    T = a.shape[0]
    tm = TM_DENSE
    return pl.pallas_call(
        _attn_out_kernel,
        grid=(T // tm,),
        in_specs=[
            pl.BlockSpec((tm, D_MODEL), lambda i: (i, 0)),
            pl.BlockSpec((D_MODEL, D_MODEL), lambda i: (0, 0)),
            pl.BlockSpec((1, D_MODEL), lambda i: (0, 0)),
            pl.BlockSpec((tm, D_MODEL), lambda i: (i, 0)),
        ] + _epilogue_in_specs(),
        out_specs=_epilogue_out_specs(tm),
        out_shape=_epilogue_out_shapes(T),
        scratch_shapes=[pltpu.VMEM((N_EXPERTS, 1), jnp.float32)],
        compiler_params=_params(("arbitrary",)),
        name="attn_out_router",
    )(a, w, b, res, gffn, wr, brt)


def _conv_mixer_kernel(tiles_per_seq, x_ref, g_ref, win_ref, cw_ref, wout_ref,
                       gffn_ref, wr_ref, brt_ref,
                       h_out_ref, xrows_ref, ri_ref, rf_ref, cnt_ref, carry_ref, tail_ref):
    i = pl.program_id(0)
    tm = x_ref.shape[0]

    @pl.when(i == 0)
    def _():
        carry_ref[...] = jnp.zeros_like(carry_ref)

    @pl.when(i % tiles_per_seq == 0)
    def _():
        tail_ref[...] = jnp.zeros_like(tail_ref)

    x = x_ref[...]
    xn = _rms(x, g_ref[...]).astype(jnp.bfloat16)
    bcu = jnp.dot(xn, win_ref[...], preferred_element_type=jnp.float32)
    b_gate = bcu[:, :D_MODEL]
    z = bcu[:, D_MODEL:2 * D_MODEL] * bcu[:, 2 * D_MODEL:]
    row = lax.broadcasted_iota(jnp.int32, (tm, D_MODEL), 0)
    tail = tail_ref[...]
    t1 = tail[SUBLANES - 1:SUBLANES, :]
    t2 = tail[SUBLANES - 2:SUBLANES - 1, :]
    z1 = jnp.where(row == 0, t1, pltpu.roll(z, 1, axis=0))
    z2 = jnp.where(row == 0, t2, jnp.where(row == 1, t1, pltpu.roll(z, 2, axis=0)))
    tail_ref[...] = z[tm - SUBLANES:, :]
    conv = z2 * cw_ref[0:1, :]
    conv = conv + z1 * cw_ref[1:2, :]
    conv = conv + z * cw_ref[2:3, :]
    gated = (b_gate * conv).astype(jnp.bfloat16)
    h = x + jnp.dot(gated, wout_ref[...], preferred_element_type=jnp.float32)
    _router_epilogue(h, gffn_ref, wr_ref, brt_ref, carry_ref,
                     h_out_ref, xrows_ref, ri_ref, rf_ref, cnt_ref)


def _conv_mixer_call(x2, g, win, cw, wout, gffn, wr, brt, seq):
    T = x2.shape[0]
    tm = TM_MIX1
    return pl.pallas_call(
        functools.partial(_conv_mixer_kernel, seq // tm),
        grid=(T // tm,),
        in_specs=[
            pl.BlockSpec((tm, D_MODEL), lambda i: (i, 0)),
            pl.BlockSpec((1, D_MODEL), lambda i: (0, 0)),
            pl.BlockSpec((D_MODEL, 3 * D_MODEL), lambda i: (0, 0)),
            pl.BlockSpec((CONV_WIDTH, D_MODEL), lambda i: (0, 0)),
            pl.BlockSpec((D_MODEL, D_MODEL), lambda i: (0, 0)),
        ] + _epilogue_in_specs(),
        out_specs=_epilogue_out_specs(tm),
        out_shape=_epilogue_out_shapes(T),
        scratch_shapes=[pltpu.VMEM((N_EXPERTS, 1), jnp.float32),
                        pltpu.VMEM((SUBLANES, D_MODEL), jnp.float32)],
        compiler_params=_params(("arbitrary",)),
        name="conv_mixer_router",
    )(x2, g, win, cw, wout, gffn, wr, brt)


def _row_copy(src_hbm, src_row, dst_hbm, dst_row, sem):
    return pltpu.make_async_copy(
        src_hbm.at[pl.ds(pl.multiple_of(src_row, ROW_CHUNKS), ROW_CHUNKS)],
        dst_hbm.at[pl.ds(pl.multiple_of(dst_row, ROW_CHUNKS), ROW_CHUNKS)], sem)


def _dispatch_kernel(d0_ref, d1_ref, padrow_ref, x_ref, xs_hbm, zero_ref, sem, zsem):
    i = pl.program_id(0)
    tile_rows = TM_EXPERT * ROW_CHUNKS

    @pl.when(i == 0)
    def _():
        zero_ref[...] = jnp.zeros_like(zero_ref)

        def zcopy(e):
            return pltpu.make_async_copy(
                zero_ref,
                xs_hbm.at[pl.ds(pl.multiple_of(padrow_ref[e], tile_rows), tile_rows)], zsem)

        def zstart(e, c):
            @pl.when(padrow_ref[e] >= 0)
            def _():
                zcopy(e).start()
            return c

        def zwait(e, c):
            @pl.when(padrow_ref[e] >= 0)
            def _():
                zcopy(e).wait()
            return c

        lax.fori_loop(0, N_EXPERTS, zstart, 0)
        lax.fori_loop(0, N_EXPERTS, zwait, 0)

        def tcopy(b):
            return pltpu.make_async_copy(
                zero_ref, xs_hbm.at[pl.ds(pl.multiple_of(b * tile_rows, tile_rows), tile_rows)], zsem)

        n_used = padrow_ref[N_EXPERTS]
        n_blk = xs_hbm.shape[0] // tile_rows
        lax.fori_loop(n_used, n_blk, lambda b, c: (tcopy(b).start(), c)[1], 0)
        lax.fori_loop(n_used, n_blk, lambda b, c: (tcopy(b).wait(), c)[1], 0)

    def issue(t, c):
        src = t * ROW_CHUNKS
        _row_copy(x_ref, src, xs_hbm, d0_ref[t], sem).start(priority=0)
        _row_copy(x_ref, src, xs_hbm, d1_ref[t], sem).start(priority=1)
        return c

    lax.fori_loop(0, TM_DISPATCH, issue, 0, unroll=DMA_UNROLL)
    for _ in range(2):
        pltpu.make_async_copy(x_ref, xs_hbm.at[pl.ds(0, TM_DISPATCH * ROW_CHUNKS)], sem).wait()


def _dispatch_call(d0, d1, padrow, xrows, capacity):
    T = d0.shape[0]
    return pl.pallas_call(
        _dispatch_kernel,
        grid=(T // TM_DISPATCH,),
        in_specs=[
            pl.BlockSpec((TM_DISPATCH,), lambda i: (i,), memory_space=pltpu.SMEM),
            pl.BlockSpec((TM_DISPATCH,), lambda i: (i,), memory_space=pltpu.SMEM),
            pl.BlockSpec(memory_space=pltpu.SMEM),
            pl.BlockSpec((TM_DISPATCH * ROW_CHUNKS, LANES), lambda i: (i, 0)),
        ],
        out_specs=pl.BlockSpec(memory_space=pl.ANY),
        out_shape=jax.ShapeDtypeStruct((capacity * ROW_CHUNKS, LANES), jnp.float32),
        scratch_shapes=[pltpu.VMEM((TM_EXPERT * ROW_CHUNKS, LANES), jnp.float32),
                        pltpu.SemaphoreType.DMA, pltpu.SemaphoreType.DMA],
        compiler_params=_params(("arbitrary",)),
        name="moe_dispatch",
    )(d0, d1, padrow, xrows)


def _expert_kernel(be_ref, first_ref, nu_ref, xs_ref, wgu_ref, wd_ref, ys_ref, wgu_bf, wd_bf):
    i = pl.program_id(0)
    tm = TM_EXPERT
    live = i < nu_ref[0]

    @pl.when(live & (first_ref[i] == 1))
    def _():
        wgu_bf[...] = wgu_ref[0, 0].astype(jnp.bfloat16)
        wd_bf[...] = wd_ref[0, 0].astype(jnp.bfloat16)

    @pl.when(live)
    def _():
        x = jnp.concatenate(
            [xs_ref[pl.ds(c, tm, stride=ROW_CHUNKS), :] for c in range(ROW_CHUNKS)], axis=-1)
        gu = jnp.dot(x.astype(jnp.bfloat16), wgu_bf[...], preferred_element_type=jnp.float32)
        g = gu[:, :EXPERT_FF]
        u = gu[:, EXPERT_FF:]
        act = (g / (1.0 + jnp.exp(-g))) * u
        y = jnp.dot(act.astype(jnp.bfloat16), wd_bf[...], preferred_element_type=jnp.float32)
        for c in range(ROW_CHUNKS):
            ys_ref[pl.ds(c, tm, stride=ROW_CHUNKS), :] = y[:, c * LANES:(c + 1) * LANES]

    @pl.when(jnp.logical_not(live))
    def _():
        ys_ref[...] = jnp.zeros_like(ys_ref)


def _expert_call(layer, blk_expert, blk_first, n_used, xs, wgu, wd, capacity):
    tm = TM_EXPERT
    n_blk = capacity // tm

    def slot_map(i, be, bf, nu):
        return (jnp.minimum(i, nu[0] - 1), 0)

    def w_map(i, be, bf, nu):
        return (layer, be[i], 0, 0)

    return pl.pallas_call(
        _expert_kernel,
        grid_spec=pltpu.PrefetchScalarGridSpec(
            num_scalar_prefetch=3,
            grid=(n_blk,),
            in_specs=[
                pl.BlockSpec((tm * ROW_CHUNKS, LANES), slot_map),
                pl.BlockSpec((1, 1, D_MODEL, 2 * EXPERT_FF), w_map),
                pl.BlockSpec((1, 1, EXPERT_FF, D_MODEL), w_map),
            ],
            out_specs=pl.BlockSpec((tm * ROW_CHUNKS, LANES), lambda i, be, bf, nu: (i, 0)),
            scratch_shapes=[pltpu.VMEM((D_MODEL, 2 * EXPERT_FF), jnp.bfloat16),
                            pltpu.VMEM((EXPERT_FF, D_MODEL), jnp.bfloat16)],
        ),
        out_shape=jax.ShapeDtypeStruct((capacity * ROW_CHUNKS, LANES), jnp.float32),
        compiler_params=_params(("arbitrary",)),
        name="moe_experts",
    )(blk_expert, blk_first, n_used, xs, wgu, wd)


def _combine_kernel(final_norm, d0_ref, d1_ref, d0n_ref, d1n_ref, ys_hbm, h_ref, gate_ref, gfin_ref,
                    o_ref, y0_ref, y1_ref, sem):
    tm = TM_COMBINE
    i = pl.program_id(0)
    slot = i % 2

    def issue_tile(d0r, d1r, s):
        def issue(t, c):
            dst = t * ROW_CHUNKS
            _row_copy(ys_hbm, d0r[t], y0_ref.at[s], dst, sem.at[s]).start(priority=0)
            _row_copy(ys_hbm, d1r[t], y1_ref.at[s], dst, sem.at[s]).start(priority=1)
            return c
        lax.fori_loop(0, tm, issue, 0, unroll=DMA_UNROLL)

    @pl.when(i == 0)
    def _():
        issue_tile(d0_ref, d1_ref, 0)

    @pl.when(i + 1 < pl.num_programs(0))
    def _():
        issue_tile(d0n_ref, d1n_ref, 1 - slot)

    for buf in (y0_ref, y1_ref):
        pltpu.make_async_copy(ys_hbm.at[pl.ds(0, tm * ROW_CHUNKS)], buf.at[slot], sem.at[slot]).wait()

    g0 = gate_ref[:, 0:1]
    g1 = gate_ref[:, 1:2]
    cols = []
    for c in range(ROW_CHUNKS):
        y0 = y0_ref[slot, pl.ds(c, tm, stride=ROW_CHUNKS), :]
        y1 = y1_ref[slot, pl.ds(c, tm, stride=ROW_CHUNKS), :]
        cols.append(h_ref[:, c * LANES:(c + 1) * LANES] + (y0 * g0 + y1 * g1))
    h = jnp.concatenate(cols, axis=-1)
    if final_norm:
        h = _rms(h, gfin_ref[...])
    o_ref[...] = h


def _combine_call(d0, d1, ys, h, gates, gfin, final_norm):
    T = h.shape[0]
    tm = TM_COMBINE
    last = T // tm - 1
    return pl.pallas_call(
        functools.partial(_combine_kernel, final_norm),
        grid=(T // tm,),
        in_specs=[
            pl.BlockSpec((tm,), lambda i: (i,), memory_space=pltpu.SMEM),
            pl.BlockSpec((tm,), lambda i: (i,), memory_space=pltpu.SMEM),
            pl.BlockSpec((tm,), lambda i: (jnp.minimum(i + 1, last),), memory_space=pltpu.SMEM),
            pl.BlockSpec((tm,), lambda i: (jnp.minimum(i + 1, last),), memory_space=pltpu.SMEM),
            pl.BlockSpec(memory_space=pl.ANY),
            pl.BlockSpec((tm, D_MODEL), lambda i: (i, 0)),
            pl.BlockSpec((tm, 2), lambda i: (i, 0)),
            pl.BlockSpec((1, D_MODEL), lambda i: (0, 0)),
        ],
        out_specs=pl.BlockSpec((tm, D_MODEL), lambda i: (i, 0)),
        out_shape=jax.ShapeDtypeStruct((T, D_MODEL), jnp.float32),
        scratch_shapes=[pltpu.VMEM((2, tm * ROW_CHUNKS, LANES), jnp.float32),
                        pltpu.VMEM((2, tm * ROW_CHUNKS, LANES), jnp.float32),
                        pltpu.SemaphoreType.DMA((2,))],
        compiler_params=_params(("arbitrary",)),
        name="moe_combine",
    )(d0, d1, d0, d1, ys, h, gates, gfin)


def _group_heads_last(w):
    lead = w.shape[:-1]
    w = w.reshape(lead + (N_KV_HEADS, GROUP, HEAD_DIM))
    w = jnp.swapaxes(w, -3, -2)
    return w.reshape(lead + (N_Q_HEADS * HEAD_DIM,))


def _rel_bucket_table():
    qi = np.arange(ATTN_BLOCK)[:, None]
    kj = np.arange(2 * ATTN_BLOCK)[None, :]
    dist = np.maximum(qi + ATTN_BLOCK - kj, 0)
    max_exact = N_REL_BUCKETS // 2
    d = np.maximum(dist, max_exact).astype(np.float32)
    large = max_exact + (np.log(d / np.float32(max_exact)) / np.float32(math.log(REL_MAX_DISTANCE / max_exact))
                         * np.float32(N_REL_BUCKETS - max_exact)).astype(np.int32)
    large = np.minimum(large, N_REL_BUCKETS - 1)
    return np.where(dist < max_exact, dist, large).astype(np.int32)


def _rel_bias_blocks(rel_bias, sinks):
    onehot = (jnp.asarray(_rel_bucket_table())[..., None] == jnp.arange(N_REL_BUCKETS)).astype(jnp.float32)
    bias = jnp.einsum("qkb,bh->hqk", onehot, rel_bias.astype(jnp.float32),
                      precision=lax.Precision.HIGHEST)
    sink_col = jnp.broadcast_to(sinks.astype(jnp.float32)[:, None, None], (N_Q_HEADS, ATTN_BLOCK, 1))
    return jnp.concatenate([sink_col, bias[:, :, 1:]], axis=2)


def _router_weights(w_group, b_group, w_expert, b_expert):
    zw = jnp.zeros((D_MODEL, SUBLANES - N_GROUPS), jnp.float32)
    zw2 = jnp.zeros((D_MODEL, LANES - SUBLANES - N_EXPERTS), jnp.float32)
    wr = jnp.concatenate([w_group, zw, w_expert, zw2], axis=1)
    wh = wr.astype(jnp.bfloat16)
    wl = (wr - wh.astype(jnp.float32)).astype(jnp.bfloat16)
    wr = jnp.concatenate([wh, wl], axis=1)
    brt = jnp.concatenate([b_group, jnp.zeros((SUBLANES - N_GROUPS,), jnp.float32), b_expert])[:, None]
    return wr, brt


def _moe(layer, h, xrows, ri, rf, cnt, w_gate_up, w_down, gfin, final_norm):
    T = h.shape[0]
    tm = TM_EXPERT
    capacity = -(-(2 * T) // tm) * tm + N_EXPERTS * tm
    counts = cnt[:, 0].astype(jnp.int32)
    ntile = (counts + tm - 1) // tm
    tend = jnp.cumsum(ntile)
    pstart = (tend - ntile) * tm
    n_used = tend[-1:]
    dest0 = (pstart[ri[0]] + ri[2]) * ROW_CHUNKS
    dest1 = (pstart[ri[1]] + ri[3]) * ROW_CHUNKS
    padrow = jnp.concatenate([jnp.where(ntile > 0, (tend - 1) * (tm * ROW_CHUNKS), -1), n_used]
                             ).astype(jnp.int32)
    blk = jnp.minimum(jnp.arange(capacity // tm, dtype=jnp.int32), n_used[0] - 1)
    blk_expert = jnp.minimum(jnp.sum((tend[None, :] <= blk[:, None]).astype(jnp.int32), axis=1),
                             N_EXPERTS - 1)
    blk_first = jnp.concatenate([jnp.ones((1,), jnp.int32),
                                 (blk_expert[1:] != blk_expert[:-1]).astype(jnp.int32)])

    xs = _dispatch_call(dest0, dest1, padrow, xrows, capacity)
    ys = _expert_call(layer, blk_expert, blk_first, n_used.astype(jnp.int32), xs, w_gate_up, w_down,
                      capacity)
    gates = rf[:2].T
    return _combine_call(dest0, dest1, ys, h, gates, gfin, final_norm)


def kernel(x, norm_mix, norm_ffn, final_norm, rel_bias, attn_w_qkv, attn_b_qkv, attn_w_o, attn_b_o,
           attn_sinks, conv_w_in, conv_w, conv_w_out, moe_w_group, moe_b_group, moe_w_expert,
           moe_b_expert, moe_w_gate_up, moe_w_down):
    B, S, D = x.shape
    T = B * S
    x2 = x.reshape(T, D)
    nq = N_Q_HEADS * HEAD_DIM
    scale = HEAD_DIM ** -0.5

    wqkv = attn_w_qkv[0]
    bqkv = attn_b_qkv[0]
    w_all = jnp.concatenate([_group_heads_last(wqkv[:, :nq]) * scale, wqkv[:, nq:]],
                            axis=1).astype(jnp.bfloat16)
    b_all = jnp.concatenate([_group_heads_last(bqkv[:nq]) * scale, bqkv[nq:]])[None, :]
    q, kt, v = _qkv_call(x2, norm_mix[0][None, :], w_all, b_all)
    a = _attn_call(q, kt, v, _rel_bias_blocks(rel_bias, attn_sinks[0]), B, S)
    wr, brt = _router_weights(moe_w_group[0], moe_b_group[0], moe_w_expert[0], moe_b_expert[0])
    w_o = _group_heads_last(attn_w_o[0].T).T.astype(jnp.bfloat16)
    h, xrows, ri, rf, cnt = _attn_out_call(
        a, w_o, attn_b_o[0][None, :], x2, norm_ffn[0][None, :], wr, brt)
    h = _moe(0, h, xrows, ri, rf, cnt, moe_w_gate_up, moe_w_down, final_norm[None, :], False)

    wr, brt = _router_weights(moe_w_group[1], moe_b_group[1], moe_w_expert[1], moe_b_expert[1])
    h, xrows, ri, rf, cnt = _conv_mixer_call(
        h, norm_mix[1][None, :], conv_w_in[0].astype(jnp.bfloat16), conv_w[0],
        conv_w_out[0].astype(jnp.bfloat16), norm_ffn[1][None, :], wr, brt, S)
    out = _moe(1, h, xrows, ri, rf, cnt, moe_w_gate_up, moe_w_down, final_norm[None, :], True)
    return out.reshape(B, S, D)
```

```python
import functools
import math

import numpy as np
import jax
import jax.numpy as jnp
from jax import lax
from jax.experimental import pallas as pl
from jax.experimental.pallas import tpu as pltpu

D_MODEL = 1024
N_Q_HEADS = 16
N_KV_HEADS = 2
HEAD_DIM = 64
GROUP = N_Q_HEADS // N_KV_HEADS
WINDOW = 128
ATTN_BLOCK = 128
N_REL_BUCKETS = 32
REL_MAX_DISTANCE = 128
CONV_WIDTH = 3
N_GROUPS = 4
EXPERTS_PER_GROUP = 8
N_EXPERTS = N_GROUPS * EXPERTS_PER_GROUP
EXPERT_FF = 512
RMS_EPS = 1e-5

LANES = 128
SUBLANES = 8
ROW_CHUNKS = D_MODEL // LANES
VMEM_LIMIT = 56 * 1024 * 1024

TM_DENSE = 512
TQ = 512
TM_MIX1 = 512
TM_EXPERT = 512
TM_DISPATCH = 512
TM_COMBINE = 256
DMA_UNROLL = 8
TM_SLOT_ROWS = 4096
ROUTER_ROWS = 40


def _rms(x, g):
    return x * lax.rsqrt(jnp.mean(x * x, axis=-1, keepdims=True) + RMS_EPS) * g


def _params(sem):
    return pltpu.CompilerParams(dimension_semantics=sem, vmem_limit_bytes=VMEM_LIMIT)


def _qkv_kernel(x_ref, g_ref, w_ref, b_ref, q_ref, kt_ref, v_ref):
    xn = _rms(x_ref[...], g_ref[...]).astype(jnp.bfloat16)
    out = jnp.dot(xn, w_ref[...], preferred_element_type=jnp.float32) + b_ref[...]
    nq = N_Q_HEADS * HEAD_DIM
    nk = N_KV_HEADS * HEAD_DIM
    q_ref[...] = out[:, :nq].astype(jnp.bfloat16)
    kt_ref[...] = out[:, nq:nq + nk].T.astype(jnp.bfloat16)
    v_ref[...] = out[:, nq + nk:].astype(jnp.bfloat16)


def _qkv_call(x2, g, w, b):
    T = x2.shape[0]
    nq = N_Q_HEADS * HEAD_DIM
    nk = N_KV_HEADS * HEAD_DIM
    tm = TM_DENSE
    return pl.pallas_call(
        _qkv_kernel,
        grid=(T // tm,),
        in_specs=[
            pl.BlockSpec((tm, D_MODEL), lambda i: (i, 0)),
            pl.BlockSpec((1, D_MODEL), lambda i: (0, 0)),
            pl.BlockSpec((D_MODEL, nq + 2 * nk), lambda i: (0, 0)),
            pl.BlockSpec((1, nq + 2 * nk), lambda i: (0, 0)),
        ],
        out_specs=[
            pl.BlockSpec((tm, nq), lambda i: (i, 0)),
            pl.BlockSpec((nk, tm), lambda i: (0, i)),
            pl.BlockSpec((tm, nk), lambda i: (i, 0)),
        ],
        out_shape=[
            jax.ShapeDtypeStruct((T, nq), jnp.bfloat16),
            jax.ShapeDtypeStruct((nk, T), jnp.bfloat16),
            jax.ShapeDtypeStruct((T, nk), jnp.bfloat16),
        ],
        compiler_params=_params(("parallel",)),
        name="qkv_proj",
    )(x2, g, w, b)


def _attn_kernel(q_ref, kt_ref, ktp_ref, v_ref, vp_ref, bias_ref, o_ref):
    first_tile = pl.program_id(1) == 0
    nblk = TQ // ATTN_BLOCK
    qi_io = lax.broadcasted_iota(jnp.int32, (ATTN_BLOCK, 2 * ATTN_BLOCK), 0)
    kj_io = lax.broadcasted_iota(jnp.int32, (ATTN_BLOCK, 2 * ATTN_BLOCK), 1)
    dist = qi_io + ATTN_BLOCK - kj_io
    band = (dist >= 0) & (dist < WINDOW)
    sink_col = kj_io == 0
    kt_keep = lax.broadcasted_iota(jnp.int32, (N_KV_HEADS * HEAD_DIM, 2 * ATTN_BLOCK), 1) > 0
    v_keep = lax.broadcasted_iota(jnp.int32, (2 * ATTN_BLOCK, N_KV_HEADS * HEAD_DIM), 0) > 0
    low_half = lax.broadcasted_iota(jnp.int32, (ATTN_BLOCK, LANES), 1) < HEAD_DIM
    for qi in range(nblk):
        r0 = qi * ATTN_BLOCK
        if qi == 0:
            kt_blk = jnp.concatenate([ktp_ref[...], kt_ref[:, 0:ATTN_BLOCK]], axis=1)
            v_blk = jnp.concatenate([vp_ref[...], v_ref[0:ATTN_BLOCK, :]], axis=0)
            mask = (band & (jnp.logical_not(first_tile) | (kj_io >= ATTN_BLOCK))) | sink_col
        else:
            kt_blk = kt_ref[:, r0 - ATTN_BLOCK:r0 + ATTN_BLOCK]
            v_blk = v_ref[r0 - ATTN_BLOCK:r0 + ATTN_BLOCK, :]
            mask = band | sink_col
        kt_blk = jnp.where(kt_keep, kt_blk, jnp.zeros_like(kt_blk))
        v_blk = jnp.where(v_keep, v_blk, jnp.zeros_like(v_blk))
        for m in range(GROUP):
            qg = q_ref[r0:r0 + ATTN_BLOCK, m * LANES:(m + 1) * LANES]
            halves = []
            for half in range(N_KV_HEADS):
                h = m + GROUP * half
                keep = low_half if half == 0 else jnp.logical_not(low_half)
                qpad = jnp.where(keep, qg, jnp.zeros_like(qg))
                s = jnp.dot(qpad, kt_blk, preferred_element_type=jnp.float32)
                s = jnp.where(mask, s + bias_ref[h], -1e30)
                mx = jnp.max(s, axis=-1, keepdims=True)
                p = jnp.exp(s - mx)
                den = jnp.sum(p, axis=-1, keepdims=True)
                pv = jnp.dot(p.astype(jnp.bfloat16), v_blk, preferred_element_type=jnp.float32)
                halves.append(pv * (1.0 / den))
            og = jnp.where(low_half, halves[0], halves[1])
            o_ref[r0:r0 + ATTN_BLOCK, m * LANES:(m + 1) * LANES] = og.astype(jnp.bfloat16)


def _attn_call(q, kt, v, bias, batch, seq):
    T = q.shape[0]
    nq = N_Q_HEADS * HEAD_DIM
    nk = N_KV_HEADS * HEAD_DIM
    tiles = seq // TQ
    per = TQ // ATTN_BLOCK

    def cur(b, j):
        return b * tiles + j

    def prev(b, j):
        return jnp.maximum((b * tiles + j) * per - 1, b * tiles * per)

    return pl.pallas_call(
        _attn_kernel,
        grid=(batch, tiles),
        in_specs=[
            pl.BlockSpec((TQ, nq), lambda b, j: (cur(b, j), 0)),
            pl.BlockSpec((nk, TQ), lambda b, j: (0, cur(b, j))),
            pl.BlockSpec((nk, ATTN_BLOCK), lambda b, j: (0, prev(b, j))),
            pl.BlockSpec((TQ, nk), lambda b, j: (cur(b, j), 0)),
            pl.BlockSpec((ATTN_BLOCK, nk), lambda b, j: (prev(b, j), 0)),
            pl.BlockSpec((N_Q_HEADS, ATTN_BLOCK, 2 * ATTN_BLOCK), lambda b, j: (0, 0, 0)),
        ],
        out_specs=pl.BlockSpec((TQ, nq), lambda b, j: (cur(b, j), 0)),
        out_shape=jax.ShapeDtypeStruct((T, nq), jnp.bfloat16),
        compiler_params=_params(("parallel", "parallel")),
        name="swa_attention",
    )(q, kt, kt, v, v, bias)


def _router_epilogue(h, gffn_ref, wr_ref, brt_ref, carry_ref,
                     h_out_ref, xrows_ref, ri_ref, rf_ref, cnt_ref):
    tm = h.shape[0]
    h_out_ref[...] = h
    xn = _rms(h, gffn_ref[...])
    for c in range(ROW_CHUNKS):
        xrows_ref[pl.ds(c, tm, stride=ROW_CHUNKS), :] = xn[:, c * LANES:(c + 1) * LANES]

    xh = xn.astype(jnp.bfloat16)
    xl = (xn - xh.astype(jnp.float32)).astype(jnp.bfloat16)
    wcat = wr_ref[...]
    prod = jnp.dot(xh, wcat, preferred_element_type=jnp.float32)
    logits = (prod[:, :LANES] + prod[:, LANES:]) + jnp.dot(
        xl, wcat[:, :LANES], preferred_element_type=jnp.float32)
    lt = logits.T[:ROUTER_ROWS, :] + brt_ref[...]

    gl = [lt[g:g + 1, :] for g in range(N_GROUPS)]
    gmax = functools.reduce(jnp.maximum, gl)
    gexp = [jnp.exp(x - gmax) for x in gl]
    gsum = functools.reduce(lambda a, b: a + b, gexp)
    gprob = [x / gsum for x in gexp]
    g_prob = functools.reduce(jnp.maximum, gprob)
    g_idx = jnp.full(g_prob.shape, N_GROUPS - 1, jnp.int32)
    for g in range(N_GROUPS - 2, -1, -1):
        g_idx = jnp.where(gprob[g] == g_prob, g, g_idx)

    el = []
    for j in range(EXPERTS_PER_GROUP):
        x = lt[SUBLANES + j:SUBLANES + j + 1, :]
        for g in range(1, N_GROUPS):
            r = SUBLANES + g * EXPERTS_PER_GROUP + j
            x = jnp.where(g_idx == g, lt[r:r + 1, :], x)
        el.append(x)
    emax = functools.reduce(jnp.maximum, el)
    eexp = [jnp.exp(x - emax) for x in el]
    esum = functools.reduce(lambda a, b: a + b, eexp)
    eprob = [x / esum for x in eexp]
    p1 = functools.reduce(jnp.maximum, eprob)
    i1 = jnp.full(p1.shape, EXPERTS_PER_GROUP - 1, jnp.int32)
    for j in range(EXPERTS_PER_GROUP - 2, -1, -1):
        i1 = jnp.where(eprob[j] == p1, j, i1)
    rest = [jnp.where(i1 == j, -1.0, eprob[j]) for j in range(EXPERTS_PER_GROUP)]
    p2 = functools.reduce(jnp.maximum, rest)
    i2 = jnp.full(p2.shape, EXPERTS_PER_GROUP - 1, jnp.int32)
    for j in range(EXPERTS_PER_GROUP - 2, -1, -1):
        i2 = jnp.where(rest[j] == p2, j, i2)
    psum = p1 + p2
    gate0 = g_prob * (p1 / psum)
    gate1 = g_prob * (p2 / psum)
    e0 = g_idx * EXPERTS_PER_GROUP + i1
    e1 = g_idx * EXPERTS_PER_GROUP + i2

    eio = lax.broadcasted_iota(jnp.int32, (N_EXPERTS, tm), 0)
    oh0 = (eio == e0).astype(jnp.float32)
    oh1 = (eio == e1).astype(jnp.float32)
    both = oh0 + oh1
    tr = lax.broadcasted_iota(jnp.int32, (tm, tm), 0)
    tc = lax.broadcasted_iota(jnp.int32, (tm, tm), 1)
    upper = (tr < tc).astype(jnp.bfloat16)
    before = jnp.dot(both.astype(jnp.bfloat16), upper, preferred_element_type=jnp.float32)
    before = before + carry_ref[...]
    rank0 = jnp.sum(oh0 * before, axis=0, keepdims=True)
    rank1 = jnp.sum(oh1 * (before + oh0), axis=0, keepdims=True)
    carry_ref[...] = carry_ref[...] + jnp.sum(both, axis=1, keepdims=True)
    cnt_ref[...] = jnp.broadcast_to(carry_ref[...], cnt_ref.shape)

    zi = jnp.zeros_like(e0)
    ri_ref[...] = jnp.concatenate(
        [e0, e1, rank0.astype(jnp.int32), rank1.astype(jnp.int32), zi, zi, zi, zi], axis=0)
    zf = jnp.zeros_like(gate0)
    rf_ref[...] = jnp.concatenate([gate0, gate1, zf, zf, zf, zf, zf, zf], axis=0)


def _epilogue_out_specs(tm):
    return [
        pl.BlockSpec((tm, D_MODEL), lambda i: (i, 0)),
        pl.BlockSpec((tm * ROW_CHUNKS, LANES), lambda i: (i, 0)),
        pl.BlockSpec((SUBLANES, tm), lambda i: (0, i)),
        pl.BlockSpec((SUBLANES, tm), lambda i: (0, i)),
        pl.BlockSpec((N_EXPERTS, LANES), lambda i: (0, 0)),
    ]


def _epilogue_out_shapes(T):
    return [
        jax.ShapeDtypeStruct((T, D_MODEL), jnp.float32),
        jax.ShapeDtypeStruct((T * ROW_CHUNKS, LANES), jnp.float32),
        jax.ShapeDtypeStruct((SUBLANES, T), jnp.int32),
        jax.ShapeDtypeStruct((SUBLANES, T), jnp.float32),
        jax.ShapeDtypeStruct((N_EXPERTS, LANES), jnp.float32),
    ]


def _epilogue_in_specs():
    return [
        pl.BlockSpec((1, D_MODEL), lambda i: (0, 0)),
        pl.BlockSpec((D_MODEL, 2 * LANES), lambda i: (0, 0)),
        pl.BlockSpec((ROUTER_ROWS, 1), lambda i: (0, 0)),
    ]


def _attn_out_kernel(a_ref, w_ref, b_ref, res_ref, gffn_ref, wr_ref, brt_ref,
                     h_out_ref, xrows_ref, ri_ref, rf_ref, cnt_ref, carry_ref):
    @pl.when(pl.program_id(0) == 0)
    def _():
        carry_ref[...] = jnp.zeros_like(carry_ref)

    mix = jnp.dot(a_ref[...], w_ref[...], preferred_element_type=jnp.float32) + b_ref[...]
    h = res_ref[...] + mix
    _router_epilogue(h, gffn_ref, wr_ref, brt_ref, carry_ref,
                     h_out_ref, xrows_ref, ri_ref, rf_ref, cnt_ref)


def _attn_out_call(a, w, b, res, gffn, wr, brt):
    T = a.shape[0]
    tm = TM_DENSE
    return pl.pallas_call(
        _attn_out_kernel,
        grid=(T // tm,),
        in_specs=[
            pl.BlockSpec((tm, D_MODEL), lambda i: (i, 0)),
            pl.BlockSpec((D_MODEL, D_MODEL), lambda i: (0, 0)),
            pl.BlockSpec((1, D_MODEL), lambda i: (0, 0)),
            pl.BlockSpec((tm, D_MODEL), lambda i: (i, 0)),
        ] + _epilogue_in_specs(),
        out_specs=_epilogue_out_specs(tm),
        out_shape=_epilogue_out_shapes(T),
        scratch_shapes=[pltpu.VMEM((N_EXPERTS, 1), jnp.float32)],
        compiler_params=_params(("arbitrary",)),
        name="attn_out_router",
    )(a, w, b, res, gffn, wr, brt)


def _conv_mixer_kernel(tiles_per_seq, x_ref, g_ref, win_ref, cw_ref, wout_ref,
                       gffn_ref, wr_ref, brt_ref,
                       h_out_ref, xrows_ref, ri_ref, rf_ref, cnt_ref, carry_ref, tail_ref):
    i = pl.program_id(0)
    tm = x_ref.shape[0]

    @pl.when(i == 0)
    def _():
        carry_ref[...] = jnp.zeros_like(carry_ref)

    @pl.when(i % tiles_per_seq == 0)
    def _():
        tail_ref[...] = jnp.zeros_like(tail_ref)

    x = x_ref[...]
    xn = _rms(x, g_ref[...]).astype(jnp.bfloat16)
    bcu = jnp.dot(xn, win_ref[...], preferred_element_type=jnp.float32)
    b_gate = bcu[:, :D_MODEL]
    z = bcu[:, D_MODEL:2 * D_MODEL] * bcu[:, 2 * D_MODEL:]
    row = lax.broadcasted_iota(jnp.int32, (tm, D_MODEL), 0)
    tail = tail_ref[...]
    t1 = tail[SUBLANES - 1:SUBLANES, :]
    t2 = tail[SUBLANES - 2:SUBLANES - 1, :]
    z1 = jnp.where(row == 0, t1, pltpu.roll(z, 1, axis=0))
    z2 = jnp.where(row == 0, t2, jnp.where(row == 1, t1, pltpu.roll(z, 2, axis=0)))
    tail_ref[...] = z[tm - SUBLANES:, :]
    conv = z2 * cw_ref[0:1, :]
    conv = conv + z1 * cw_ref[1:2, :]
    conv = conv + z * cw_ref[2:3, :]
    gated = (b_gate * conv).astype(jnp.bfloat16)
    h = x + jnp.dot(gated, wout_ref[...], preferred_element_type=jnp.float32)
    _router_epilogue(h, gffn_ref, wr_ref, brt_ref, carry_ref,
                     h_out_ref, xrows_ref, ri_ref, rf_ref, cnt_ref)


def _conv_mixer_call(x2, g, win, cw, wout, gffn, wr, brt, seq):
    T = x2.shape[0]
    tm = TM_MIX1
    return pl.pallas_call(
        functools.partial(_conv_mixer_kernel, seq // tm),
        grid=(T // tm,),
        in_specs=[
            pl.BlockSpec((tm, D_MODEL), lambda i: (i, 0)),
            pl.BlockSpec((1, D_MODEL), lambda i: (0, 0)),
            pl.BlockSpec((D_MODEL, 3 * D_MODEL), lambda i: (0, 0)),
            pl.BlockSpec((CONV_WIDTH, D_MODEL), lambda i: (0, 0)),
            pl.BlockSpec((D_MODEL, D_MODEL), lambda i: (0, 0)),
        ] + _epilogue_in_specs(),
        out_specs=_epilogue_out_specs(tm),
        out_shape=_epilogue_out_shapes(T),
        scratch_shapes=[pltpu.VMEM((N_EXPERTS, 1), jnp.float32),
                        pltpu.VMEM((SUBLANES, D_MODEL), jnp.float32)],
        compiler_params=_params(("arbitrary",)),
        name="conv_mixer_router",
    )(x2, g, win, cw, wout, gffn, wr, brt)


def _row_copy(src_hbm, src_row, dst_hbm, dst_row, sem):
    return pltpu.make_async_copy(
        src_hbm.at[pl.ds(pl.multiple_of(src_row, ROW_CHUNKS), ROW_CHUNKS)],
        dst_hbm.at[pl.ds(pl.multiple_of(dst_row, ROW_CHUNKS), ROW_CHUNKS)], sem)


def _slot_rows_kernel(ri_ref, pstart_ref, o_ref):
    tb = ri_ref.shape[1]
    eio = lax.broadcasted_iota(jnp.int32, (N_EXPERTS, tb), 0)
    pstart = pstart_ref[...]
    rows = []
    for k in range(2):
        start = jnp.sum(jnp.where(eio == ri_ref[k:k + 1, :], pstart, 0), axis=0, keepdims=True)
        rows.append((start + ri_ref[2 + k:3 + k, :]) * ROW_CHUNKS)
    zi = jnp.zeros_like(rows[0])
    o_ref[...] = jnp.concatenate(rows + [zi] * (SUBLANES - 2), axis=0)


def _slot_rows_call(ri, pstart):
    T = ri.shape[1]
    tb = TM_SLOT_ROWS
    return pl.pallas_call(
        _slot_rows_kernel,
        grid=(T // tb,),
        in_specs=[pl.BlockSpec((SUBLANES, tb), lambda i: (0, i)),
                  pl.BlockSpec((N_EXPERTS, 1), lambda i: (0, 0))],
        out_specs=pl.BlockSpec((SUBLANES, tb), lambda i: (0, i)),
        out_shape=jax.ShapeDtypeStruct((SUBLANES, T), jnp.int32),
        compiler_params=_params(("parallel",)),
        name="moe_slot_rows",
    )(ri, pstart[:, None])


def _dispatch_kernel(dst_ref, padrow_ref, x_ref, xs_hbm, zero_ref, sem, zsem):
    i = pl.program_id(0)
    tile_rows = TM_EXPERT * ROW_CHUNKS

    @pl.when(i == 0)
    def _():
        zero_ref[...] = jnp.zeros_like(zero_ref)

        def zcopy(e):
            return pltpu.make_async_copy(
                zero_ref,
                xs_hbm.at[pl.ds(pl.multiple_of(padrow_ref[e], tile_rows), tile_rows)], zsem)

        def zstart(e, c):
            @pl.when(padrow_ref[e] >= 0)
            def _():
                zcopy(e).start()
            return c

        def zwait(e, c):
            @pl.when(padrow_ref[e] >= 0)
            def _():
                zcopy(e).wait()
            return c

        lax.fori_loop(0, N_EXPERTS, zstart, 0)
        lax.fori_loop(0, N_EXPERTS, zwait, 0)

        def tcopy(b):
            return pltpu.make_async_copy(
                zero_ref, xs_hbm.at[pl.ds(pl.multiple_of(b * tile_rows, tile_rows), tile_rows)], zsem)

        n_used = padrow_ref[N_EXPERTS]
        n_blk = xs_hbm.shape[0] // tile_rows
        lax.fori_loop(n_used, n_blk, lambda b, c: (tcopy(b).start(), c)[1], 0)
        lax.fori_loop(n_used, n_blk, lambda b, c: (tcopy(b).wait(), c)[1], 0)

    def issue(t, c):
        src = t * ROW_CHUNKS
        _row_copy(x_ref, src, xs_hbm, dst_ref[0, t], sem).start(priority=0)
        _row_copy(x_ref, src, xs_hbm, dst_ref[1, t], sem).start(priority=1)
        return c

    lax.fori_loop(0, TM_DISPATCH, issue, 0, unroll=DMA_UNROLL)
    for _ in range(2):
        pltpu.make_async_copy(x_ref, xs_hbm.at[pl.ds(0, TM_DISPATCH * ROW_CHUNKS)], sem).wait()


def _dispatch_call(dst, padrow, xrows, capacity):
    T = dst.shape[1]
    return pl.pallas_call(
        _dispatch_kernel,
        grid=(T // TM_DISPATCH,),
        in_specs=[
            pl.BlockSpec((SUBLANES, TM_DISPATCH), lambda i: (0, i), memory_space=pltpu.SMEM),
            pl.BlockSpec(memory_space=pltpu.SMEM),
            pl.BlockSpec((TM_DISPATCH * ROW_CHUNKS, LANES), lambda i: (i, 0)),
        ],
        out_specs=pl.BlockSpec(memory_space=pl.ANY),
        out_shape=jax.ShapeDtypeStruct((capacity * ROW_CHUNKS, LANES), jnp.float32),
        scratch_shapes=[pltpu.VMEM((TM_EXPERT * ROW_CHUNKS, LANES), jnp.float32),
                        pltpu.SemaphoreType.DMA, pltpu.SemaphoreType.DMA],
        compiler_params=_params(("arbitrary",)),
        name="moe_dispatch",
    )(dst, padrow, xrows)


def _expert_kernel(be_ref, first_ref, nu_ref, xs_ref, wgu_ref, wd_ref, ys_ref, wgu_bf, wd_bf):
    i = pl.program_id(0)
    tm = TM_EXPERT
    live = i < nu_ref[0]

    @pl.when(live & (first_ref[i] == 1))
    def _():
        wgu_bf[...] = wgu_ref[0, 0].astype(jnp.bfloat16)
        wd_bf[...] = wd_ref[0, 0].astype(jnp.bfloat16)

    @pl.when(live)
    def _():
        x = jnp.concatenate(
            [xs_ref[pl.ds(c, tm, stride=ROW_CHUNKS), :] for c in range(ROW_CHUNKS)], axis=-1)
        gu = jnp.dot(x.astype(jnp.bfloat16), wgu_bf[...], preferred_element_type=jnp.float32)
        g = gu[:, :EXPERT_FF]
        u = gu[:, EXPERT_FF:]
        act = (g / (1.0 + jnp.exp(-g))) * u
        y = jnp.dot(act.astype(jnp.bfloat16), wd_bf[...], preferred_element_type=jnp.float32)
        for c in range(ROW_CHUNKS):
            ys_ref[pl.ds(c, tm, stride=ROW_CHUNKS), :] = y[:, c * LANES:(c + 1) * LANES]

    @pl.when(jnp.logical_not(live))
    def _():
        ys_ref[...] = jnp.zeros_like(ys_ref)


def _expert_call(layer, blk_expert, blk_first, n_used, xs, wgu, wd, capacity):
    tm = TM_EXPERT
    n_blk = capacity // tm

    def slot_map(i, be, bf, nu):
        return (jnp.minimum(i, nu[0] - 1), 0)

    def w_map(i, be, bf, nu):
        return (layer, be[i], 0, 0)

    return pl.pallas_call(
        _expert_kernel,
        grid_spec=pltpu.PrefetchScalarGridSpec(
            num_scalar_prefetch=3,
            grid=(n_blk,),
            in_specs=[
                pl.BlockSpec((tm * ROW_CHUNKS, LANES), slot_map),
                pl.BlockSpec((1, 1, D_MODEL, 2 * EXPERT_FF), w_map),
                pl.BlockSpec((1, 1, EXPERT_FF, D_MODEL), w_map),
            ],
            out_specs=pl.BlockSpec((tm * ROW_CHUNKS, LANES), lambda i, be, bf, nu: (i, 0)),
            scratch_shapes=[pltpu.VMEM((D_MODEL, 2 * EXPERT_FF), jnp.bfloat16),
                            pltpu.VMEM((EXPERT_FF, D_MODEL), jnp.bfloat16)],
        ),
        out_shape=jax.ShapeDtypeStruct((capacity * ROW_CHUNKS, LANES), jnp.float32),
        compiler_params=_params(("arbitrary",)),
        name="moe_experts",
    )(blk_expert, blk_first, n_used, xs, wgu, wd)


def _combine_kernel(final_norm, dst_ref, dstn_ref, ys_hbm, h_ref, gate_ref, gfin_ref,
                    o_ref, y0_ref, y1_ref, sem):
    tm = TM_COMBINE
    i = pl.program_id(0)
    slot = i % 2

    def issue_tile(src_ref, s):
        def issue(t, c):
            dst = t * ROW_CHUNKS
            _row_copy(ys_hbm, src_ref[0, t], y0_ref.at[s], dst, sem.at[s]).start(priority=0)
            _row_copy(ys_hbm, src_ref[1, t], y1_ref.at[s], dst, sem.at[s]).start(priority=1)
            return c
        lax.fori_loop(0, tm, issue, 0, unroll=DMA_UNROLL)

    @pl.when(i == 0)
    def _():
        issue_tile(dst_ref, 0)

    @pl.when(i + 1 < pl.num_programs(0))
    def _():
        issue_tile(dstn_ref, 1 - slot)

    for buf in (y0_ref, y1_ref):
        pltpu.make_async_copy(ys_hbm.at[pl.ds(0, tm * ROW_CHUNKS)], buf.at[slot], sem.at[slot]).wait()

    g0 = gate_ref[:, 0:1]
    g1 = gate_ref[:, 1:2]
    cols = []
    for c in range(ROW_CHUNKS):
        y0 = y0_ref[slot, pl.ds(c, tm, stride=ROW_CHUNKS), :]
        y1 = y1_ref[slot, pl.ds(c, tm, stride=ROW_CHUNKS), :]
        cols.append(h_ref[:, c * LANES:(c + 1) * LANES] + (y0 * g0 + y1 * g1))
    h = jnp.concatenate(cols, axis=-1)
    if final_norm:
        h = _rms(h, gfin_ref[...])
    o_ref[...] = h


def _combine_call(dst, ys, h, gates, gfin, final_norm):
    T = h.shape[0]
    tm = TM_COMBINE
    last = T // tm - 1
    return pl.pallas_call(
        functools.partial(_combine_kernel, final_norm),
        grid=(T // tm,),
        in_specs=[
            pl.BlockSpec((SUBLANES, tm), lambda i: (0, i), memory_space=pltpu.SMEM),
            pl.BlockSpec((SUBLANES, tm), lambda i: (0, jnp.minimum(i + 1, last)),
                         memory_space=pltpu.SMEM),
            pl.BlockSpec(memory_space=pl.ANY),
            pl.BlockSpec((tm, D_MODEL), lambda i: (i, 0)),
            pl.BlockSpec((tm, 2), lambda i: (i, 0)),
            pl.BlockSpec((1, D_MODEL), lambda i: (0, 0)),
        ],
        out_specs=pl.BlockSpec((tm, D_MODEL), lambda i: (i, 0)),
        out_shape=jax.ShapeDtypeStruct((T, D_MODEL), jnp.float32),
        scratch_shapes=[pltpu.VMEM((2, tm * ROW_CHUNKS, LANES), jnp.float32),
                        pltpu.VMEM((2, tm * ROW_CHUNKS, LANES), jnp.float32),
                        pltpu.SemaphoreType.DMA((2,))],
        compiler_params=_params(("arbitrary",)),
        name="moe_combine",
    )(dst, dst, ys, h, gates, gfin)


def _group_heads_last(w):
    lead = w.shape[:-1]
    w = w.reshape(lead + (N_KV_HEADS, GROUP, HEAD_DIM))
    w = jnp.swapaxes(w, -3, -2)
    return w.reshape(lead + (N_Q_HEADS * HEAD_DIM,))


def _rel_bucket_table():
    qi = np.arange(ATTN_BLOCK)[:, None]
    kj = np.arange(2 * ATTN_BLOCK)[None, :]
    dist = np.maximum(qi + ATTN_BLOCK - kj, 0)
    max_exact = N_REL_BUCKETS // 2
    d = np.maximum(dist, max_exact).astype(np.float32)
    large = max_exact + (np.log(d / np.float32(max_exact)) / np.float32(math.log(REL_MAX_DISTANCE / max_exact))
                         * np.float32(N_REL_BUCKETS - max_exact)).astype(np.int32)
    large = np.minimum(large, N_REL_BUCKETS - 1)
    return np.where(dist < max_exact, dist, large).astype(np.int32)


def _rel_bias_blocks(rel_bias, sinks):
    onehot = (jnp.asarray(_rel_bucket_table())[..., None] == jnp.arange(N_REL_BUCKETS)).astype(jnp.float32)
    bias = jnp.einsum("qkb,bh->hqk", onehot, rel_bias.astype(jnp.float32),
                      precision=lax.Precision.HIGHEST)
    sink_col = jnp.broadcast_to(sinks.astype(jnp.float32)[:, None, None], (N_Q_HEADS, ATTN_BLOCK, 1))
    return jnp.concatenate([sink_col, bias[:, :, 1:]], axis=2)


def _router_weights(w_group, b_group, w_expert, b_expert):
    zw = jnp.zeros((D_MODEL, SUBLANES - N_GROUPS), jnp.float32)
    zw2 = jnp.zeros((D_MODEL, LANES - SUBLANES - N_EXPERTS), jnp.float32)
    wr = jnp.concatenate([w_group, zw, w_expert, zw2], axis=1)
    wh = wr.astype(jnp.bfloat16)
    wl = (wr - wh.astype(jnp.float32)).astype(jnp.bfloat16)
    wr = jnp.concatenate([wh, wl], axis=1)
    brt = jnp.concatenate([b_group, jnp.zeros((SUBLANES - N_GROUPS,), jnp.float32), b_expert])[:, None]
    return wr, brt


def _moe(layer, h, xrows, ri, rf, cnt, w_gate_up, w_down, gfin, final_norm):
    T = h.shape[0]
    tm = TM_EXPERT
    capacity = -(-(2 * T) // tm) * tm + N_EXPERTS * tm
    counts = cnt[:, 0].astype(jnp.int32)
    ntile = (counts + tm - 1) // tm
    tend = jnp.cumsum(ntile)
    pstart = (tend - ntile) * tm
    n_used = tend[-1:]
    padrow = jnp.concatenate([jnp.where(ntile > 0, (tend - 1) * (tm * ROW_CHUNKS), -1), n_used]
                             ).astype(jnp.int32)
    blk = jnp.minimum(jnp.arange(capacity // tm, dtype=jnp.int32), n_used[0] - 1)
    blk_expert = jnp.minimum(jnp.sum((tend[None, :] <= blk[:, None]).astype(jnp.int32), axis=1),
                             N_EXPERTS - 1)
    blk_first = jnp.concatenate([jnp.ones((1,), jnp.int32),
                                 (blk_expert[1:] != blk_expert[:-1]).astype(jnp.int32)])

    dst = _slot_rows_call(ri, pstart)
    xs = _dispatch_call(dst, padrow, xrows, capacity)
    ys = _expert_call(layer, blk_expert, blk_first, n_used.astype(jnp.int32), xs, w_gate_up, w_down,
                      capacity)
    gates = rf[:2].T
    return _combine_call(dst, ys, h, gates, gfin, final_norm)


def kernel(x, norm_mix, norm_ffn, final_norm, rel_bias, attn_w_qkv, attn_b_qkv, attn_w_o, attn_b_o,
           attn_sinks, conv_w_in, conv_w, conv_w_out, moe_w_group, moe_b_group, moe_w_expert,
           moe_b_expert, moe_w_gate_up, moe_w_down):
    B, S, D = x.shape
    T = B * S
    x2 = x.reshape(T, D)
    nq = N_Q_HEADS * HEAD_DIM
    scale = HEAD_DIM ** -0.5

    wqkv = attn_w_qkv[0]
    bqkv = attn_b_qkv[0]
    w_all = jnp.concatenate([_group_heads_last(wqkv[:, :nq]) * scale, wqkv[:, nq:]],
                            axis=1).astype(jnp.bfloat16)
    b_all = jnp.concatenate([_group_heads_last(bqkv[:nq]) * scale, bqkv[nq:]])[None, :]
    q, kt, v = _qkv_call(x2, norm_mix[0][None, :], w_all, b_all)
    a = _attn_call(q, kt, v, _rel_bias_blocks(rel_bias, attn_sinks[0]), B, S)
    wr, brt = _router_weights(moe_w_group[0], moe_b_group[0], moe_w_expert[0], moe_b_expert[0])
    w_o = _group_heads_last(attn_w_o[0].T).T.astype(jnp.bfloat16)
    h, xrows, ri, rf, cnt = _attn_out_call(
        a, w_o, attn_b_o[0][None, :], x2, norm_ffn[0][None, :], wr, brt)
    h = _moe(0, h, xrows, ri, rf, cnt, moe_w_gate_up, moe_w_down, final_norm[None, :], False)

    wr, brt = _router_weights(moe_w_group[1], moe_b_group[1], moe_w_expert[1], moe_b_expert[1])
    h, xrows, ri, rf, cnt = _conv_mixer_call(
        h, norm_mix[1][None, :], conv_w_in[0].astype(jnp.bfloat16), conv_w[0],
        conv_w_out[0].astype(jnp.bfloat16), norm_ffn[1][None, :], wr, brt, S)
    out = _moe(1, h, xrows, ri, rf, cnt, moe_w_gate_up, moe_w_down, final_norm[None, :], True)
    return out.reshape(B, S, D)
```

```python
import functools
import math

import numpy as np
import jax
import jax.numpy as jnp
from jax import lax
from jax.experimental import pallas as pl
from jax.experimental.pallas import tpu as pltpu

D_MODEL = 1024
N_Q_HEADS = 16
N_KV_HEADS = 2
HEAD_DIM = 64
GROUP = N_Q_HEADS // N_KV_HEADS
WINDOW = 128
ATTN_BLOCK = 128
N_REL_BUCKETS = 32
REL_MAX_DISTANCE = 128
CONV_WIDTH = 3
N_GROUPS = 4
EXPERTS_PER_GROUP = 8
N_EXPERTS = N_GROUPS * EXPERTS_PER_GROUP
EXPERT_FF = 512
RMS_EPS = 1e-5

LANES = 128
SUBLANES = 8
ROW_CHUNKS = D_MODEL // LANES
VMEM_LIMIT = 56 * 1024 * 1024

TM_DENSE = 512
TQ = 512
TM_MIX1 = 512
TM_EXPERT = 512
TM_DISPATCH = 512
TM_COMBINE = 256
DMA_UNROLL = 8
TM_SLOT_ROWS = 4096
ROUTER_ROWS = 40


def _rms(x, g):
    return x * lax.rsqrt(jnp.mean(x * x, axis=-1, keepdims=True) + RMS_EPS) * g


def _params(sem):
    return pltpu.CompilerParams(dimension_semantics=sem, vmem_limit_bytes=VMEM_LIMIT)


def _qkv_kernel(x_ref, g_ref, w_ref, b_ref, q_ref, kt_ref, v_ref):
    xn = _rms(x_ref[...], g_ref[...]).astype(jnp.bfloat16)
    out = jnp.dot(xn, w_ref[...], preferred_element_type=jnp.float32) + b_ref[...]
    nq = N_Q_HEADS * HEAD_DIM
    nk = N_KV_HEADS * HEAD_DIM
    q_ref[...] = out[:, :nq].astype(jnp.bfloat16)
    kt_ref[...] = out[:, nq:nq + nk].T.astype(jnp.bfloat16)
    v_ref[...] = out[:, nq + nk:].astype(jnp.bfloat16)


def _qkv_call(x2, g, w, b):
    T = x2.shape[0]
    nq = N_Q_HEADS * HEAD_DIM
    nk = N_KV_HEADS * HEAD_DIM
    tm = TM_DENSE
    return pl.pallas_call(
        _qkv_kernel,
        grid=(T // tm,),
        in_specs=[
            pl.BlockSpec((tm, D_MODEL), lambda i: (i, 0)),
            pl.BlockSpec((1, D_MODEL), lambda i: (0, 0)),
            pl.BlockSpec((D_MODEL, nq + 2 * nk), lambda i: (0, 0)),
            pl.BlockSpec((1, nq + 2 * nk), lambda i: (0, 0)),
        ],
        out_specs=[
            pl.BlockSpec((tm, nq), lambda i: (i, 0)),
            pl.BlockSpec((nk, tm), lambda i: (0, i)),
            pl.BlockSpec((tm, nk), lambda i: (i, 0)),
        ],
        out_shape=[
            jax.ShapeDtypeStruct((T, nq), jnp.bfloat16),
            jax.ShapeDtypeStruct((nk, T), jnp.bfloat16),
            jax.ShapeDtypeStruct((T, nk), jnp.bfloat16),
        ],
        compiler_params=_params(("parallel",)),
        name="qkv_proj",
    )(x2, g, w, b)


def _attn_kernel(q_ref, kt_ref, ktp_ref, v_ref, vp_ref, bias_ref, o_ref):
    first_tile = pl.program_id(1) == 0
    nblk = TQ // ATTN_BLOCK
    qi_io = lax.broadcasted_iota(jnp.int32, (ATTN_BLOCK, 2 * ATTN_BLOCK), 0)
    kj_io = lax.broadcasted_iota(jnp.int32, (ATTN_BLOCK, 2 * ATTN_BLOCK), 1)
    dist = qi_io + ATTN_BLOCK - kj_io
    band = (dist >= 0) & (dist < WINDOW)
    sink_col = kj_io == 0
    kt_keep = lax.broadcasted_iota(jnp.int32, (N_KV_HEADS * HEAD_DIM, 2 * ATTN_BLOCK), 1) > 0
    v_keep = lax.broadcasted_iota(jnp.int32, (2 * ATTN_BLOCK, N_KV_HEADS * HEAD_DIM), 0) > 0
    low_half = lax.broadcasted_iota(jnp.int32, (ATTN_BLOCK, LANES), 1) < HEAD_DIM
    for qi in range(nblk):
        r0 = qi * ATTN_BLOCK
        if qi == 0:
            kt_blk = jnp.concatenate([ktp_ref[...], kt_ref[:, 0:ATTN_BLOCK]], axis=1)
            v_blk = jnp.concatenate([vp_ref[...], v_ref[0:ATTN_BLOCK, :]], axis=0)
            mask = (band & (jnp.logical_not(first_tile) | (kj_io >= ATTN_BLOCK))) | sink_col
        else:
            kt_blk = kt_ref[:, r0 - ATTN_BLOCK:r0 + ATTN_BLOCK]
            v_blk = v_ref[r0 - ATTN_BLOCK:r0 + ATTN_BLOCK, :]
            mask = band | sink_col
        kt_blk = jnp.where(kt_keep, kt_blk, jnp.zeros_like(kt_blk))
        v_blk = jnp.where(v_keep, v_blk, jnp.zeros_like(v_blk))
        for m in range(GROUP):
            qg = q_ref[r0:r0 + ATTN_BLOCK, m * LANES:(m + 1) * LANES]
            halves = []
            for half in range(N_KV_HEADS):
                h = m + GROUP * half
                keep = low_half if half == 0 else jnp.logical_not(low_half)
                qpad = jnp.where(keep, qg, jnp.zeros_like(qg))
                s = jnp.dot(qpad, kt_blk, preferred_element_type=jnp.float32)
                s = jnp.where(mask, s + bias_ref[h], -1e30)
                mx = jnp.max(s, axis=-1, keepdims=True)
                p = jnp.exp(s - mx)
                den = jnp.sum(p, axis=-1, keepdims=True)
                pv = jnp.dot(p.astype(jnp.bfloat16), v_blk, preferred_element_type=jnp.float32)
                halves.append(pv * (1.0 / den))
            og = jnp.where(low_half, halves[0], halves[1])
            o_ref[r0:r0 + ATTN_BLOCK, m * LANES:(m + 1) * LANES] = og.astype(jnp.bfloat16)


def _attn_call(q, kt, v, bias, batch, seq):
    T = q.shape[0]
    nq = N_Q_HEADS * HEAD_DIM
    nk = N_KV_HEADS * HEAD_DIM
    tiles = seq // TQ
    per = TQ // ATTN_BLOCK

    def cur(b, j):
        return b * tiles + j

    def prev(b, j):
        return jnp.maximum((b * tiles + j) * per - 1, b * tiles * per)

    return pl.pallas_call(
        _attn_kernel,
        grid=(batch, tiles),
        in_specs=[
            pl.BlockSpec((TQ, nq), lambda b, j: (cur(b, j), 0)),
            pl.BlockSpec((nk, TQ), lambda b, j: (0, cur(b, j))),
            pl.BlockSpec((nk, ATTN_BLOCK), lambda b, j: (0, prev(b, j))),
            pl.BlockSpec((TQ, nk), lambda b, j: (cur(b, j), 0)),
            pl.BlockSpec((ATTN_BLOCK, nk), lambda b, j: (prev(b, j), 0)),
            pl.BlockSpec((N_Q_HEADS, ATTN_BLOCK, 2 * ATTN_BLOCK), lambda b, j: (0, 0, 0)),
        ],
        out_specs=pl.BlockSpec((TQ, nq), lambda b, j: (cur(b, j), 0)),
        out_shape=jax.ShapeDtypeStruct((T, nq), jnp.bfloat16),
        compiler_params=_params(("parallel", "parallel")),
        name="swa_attention",
    )(q, kt, kt, v, v, bias)


def _router_epilogue(h, gffn_ref, wr_ref, brt_ref, carry_ref,
                     h_out_ref, xrows_ref, ri_ref, rf_ref, cnt_ref):
    tm = h.shape[0]
    h_out_ref[...] = h
    xn = _rms(h, gffn_ref[...])
    for c in range(ROW_CHUNKS):
        xrows_ref[pl.ds(c, tm, stride=ROW_CHUNKS), :] = xn[:, c * LANES:(c + 1) * LANES]

    xh = xn.astype(jnp.bfloat16)
    xl = (xn - xh.astype(jnp.float32)).astype(jnp.bfloat16)
    wcat = wr_ref[...]
    prod = jnp.dot(xh, wcat, preferred_element_type=jnp.float32)
    logits = (prod[:, :LANES] + prod[:, LANES:]) + jnp.dot(
        xl, wcat[:, :LANES], preferred_element_type=jnp.float32)
    lt = logits.T[:ROUTER_ROWS, :] + brt_ref[...]

    gl = [lt[g:g + 1, :] for g in range(N_GROUPS)]
    gmax = functools.reduce(jnp.maximum, gl)
    gexp = [jnp.exp(x - gmax) for x in gl]
    gsum = functools.reduce(lambda a, b: a + b, gexp)
    gprob = [x / gsum for x in gexp]
    g_prob = functools.reduce(jnp.maximum, gprob)
    g_idx = jnp.full(g_prob.shape, N_GROUPS - 1, jnp.int32)
    for g in range(N_GROUPS - 2, -1, -1):
        g_idx = jnp.where(gprob[g] == g_prob, g, g_idx)

    el = []
    for j in range(EXPERTS_PER_GROUP):
        x = lt[SUBLANES + j:SUBLANES + j + 1, :]
        for g in range(1, N_GROUPS):
            r = SUBLANES + g * EXPERTS_PER_GROUP + j
            x = jnp.where(g_idx == g, lt[r:r + 1, :], x)
        el.append(x)
    emax = functools.reduce(jnp.maximum, el)
    eexp = [jnp.exp(x - emax) for x in el]
    esum = functools.reduce(lambda a, b: a + b, eexp)
    eprob = [x / esum for x in eexp]
    p1 = functools.reduce(jnp.maximum, eprob)
    i1 = jnp.full(p1.shape, EXPERTS_PER_GROUP - 1, jnp.int32)
    for j in range(EXPERTS_PER_GROUP - 2, -1, -1):
        i1 = jnp.where(eprob[j] == p1, j, i1)
    rest = [jnp.where(i1 == j, -1.0, eprob[j]) for j in range(EXPERTS_PER_GROUP)]
    p2 = functools.reduce(jnp.maximum, rest)
    i2 = jnp.full(p2.shape, EXPERTS_PER_GROUP - 1, jnp.int32)
    for j in range(EXPERTS_PER_GROUP - 2, -1, -1):
        i2 = jnp.where(rest[j] == p2, j, i2)
    psum = p1 + p2
    gate0 = g_prob * (p1 / psum)
    gate1 = g_prob * (p2 / psum)
    e0 = g_idx * EXPERTS_PER_GROUP + i1
    e1 = g_idx * EXPERTS_PER_GROUP + i2

    eio = lax.broadcasted_iota(jnp.int32, (N_EXPERTS, tm), 0)
    oh0 = (eio == e0).astype(jnp.float32)
    oh1 = (eio == e1).astype(jnp.float32)
    both = oh0 + oh1
    tr = lax.broadcasted_iota(jnp.int32, (tm, tm), 0)
    tc = lax.broadcasted_iota(jnp.int32, (tm, tm), 1)
    upper = (tr < tc).astype(jnp.bfloat16)
    before = jnp.dot(both.astype(jnp.bfloat16), upper, preferred_element_type=jnp.float32)
    before = before + carry_ref[...]
    rank0 = jnp.sum(oh0 * before, axis=0, keepdims=True)
    rank1 = jnp.sum(oh1 * (before + oh0), axis=0, keepdims=True)
    carry_ref[...] = carry_ref[...] + jnp.sum(both, axis=1, keepdims=True)
    cnt_ref[...] = jnp.broadcast_to(carry_ref[...], cnt_ref.shape)

    zi = jnp.zeros_like(e0)
    ri_ref[...] = jnp.concatenate(
        [e0, e1, rank0.astype(jnp.int32), rank1.astype(jnp.int32), zi, zi, zi, zi], axis=0)
    zf = jnp.zeros_like(gate0)
    rf_ref[...] = jnp.concatenate([gate0, gate1, zf, zf, zf, zf, zf, zf], axis=0)


def _epilogue_out_specs(tm):
    return [
        pl.BlockSpec((tm, D_MODEL), lambda i: (i, 0)),
        pl.BlockSpec((tm * ROW_CHUNKS, LANES), lambda i: (i, 0)),
        pl.BlockSpec((SUBLANES, tm), lambda i: (0, i)),
        pl.BlockSpec((SUBLANES, tm), lambda i: (0, i)),
        pl.BlockSpec((N_EXPERTS, LANES), lambda i: (0, 0)),
    ]


def _epilogue_out_shapes(T):
    return [
        jax.ShapeDtypeStruct((T, D_MODEL), jnp.float32),
        jax.ShapeDtypeStruct((T * ROW_CHUNKS, LANES), jnp.float32),
        jax.ShapeDtypeStruct((SUBLANES, T), jnp.int32),
        jax.ShapeDtypeStruct((SUBLANES, T), jnp.float32),
        jax.ShapeDtypeStruct((N_EXPERTS, LANES), jnp.float32),
    ]


def _epilogue_in_specs():
    return [
        pl.BlockSpec((1, D_MODEL), lambda i: (0, 0)),
        pl.BlockSpec((D_MODEL, 2 * LANES), lambda i: (0, 0)),
        pl.BlockSpec((ROUTER_ROWS, 1), lambda i: (0, 0)),
    ]


def _attn_out_kernel(a_ref, w_ref, b_ref, res_ref, gffn_ref, wr_ref, brt_ref,
                     h_out_ref, xrows_ref, ri_ref, rf_ref, cnt_ref, carry_ref):
    @pl.when(pl.program_id(0) == 0)
    def _():
        carry_ref[...] = jnp.zeros_like(carry_ref)

    mix = jnp.dot(a_ref[...], w_ref[...], preferred_element_type=jnp.float32) + b_ref[...]
    h = res_ref[...] + mix
    _router_epilogue(h, gffn_ref, wr_ref, brt_ref, carry_ref,
                     h_out_ref, xrows_ref, ri_ref, rf_ref, cnt_ref)


def _attn_out_call(a, w, b, res, gffn, wr, brt):
    T = a.shape[0]
    tm = TM_DENSE
    return pl.pallas_call(
        _attn_out_kernel,
        grid=(T // tm,),
        in_specs=[
            pl.BlockSpec((tm, D_MODEL), lambda i: (i, 0)),
            pl.BlockSpec((D_MODEL, D_MODEL), lambda i: (0, 0)),
            pl.BlockSpec((1, D_MODEL), lambda i: (0, 0)),
            pl.BlockSpec((tm, D_MODEL), lambda i: (i, 0)),
        ] + _epilogue_in_specs(),
        out_specs=_epilogue_out_specs(tm),
        out_shape=_epilogue_out_shapes(T),
        scratch_shapes=[pltpu.VMEM((N_EXPERTS, 1), jnp.float32)],
        compiler_params=_params(("arbitrary",)),
        name="attn_out_router",
    )(a, w, b, res, gffn, wr, brt)


def _conv_mixer_kernel(tiles_per_seq, x_ref, g_ref, win_ref, cw_ref, wout_ref,
                       gffn_ref, wr_ref, brt_ref,
                       h_out_ref, xrows_ref, ri_ref, rf_ref, cnt_ref, carry_ref, tail_ref):
    i = pl.program_id(0)
    tm = x_ref.shape[0]

    @pl.when(i == 0)
    def _():
        carry_ref[...] = jnp.zeros_like(carry_ref)

    @pl.when(i % tiles_per_seq == 0)
    def _():
        tail_ref[...] = jnp.zeros_like(tail_ref)

    x = x_ref[...]
    xn = _rms(x, g_ref[...]).astype(jnp.bfloat16)
    bcu = jnp.dot(xn, win_ref[...], preferred_element_type=jnp.float32)
    b_gate = bcu[:, :D_MODEL]
    z = bcu[:, D_MODEL:2 * D_MODEL] * bcu[:, 2 * D_MODEL:]
    row = lax.broadcasted_iota(jnp.int32, (tm, D_MODEL), 0)
    tail = tail_ref[...]
    t1 = tail[SUBLANES - 1:SUBLANES, :]
    t2 = tail[SUBLANES - 2:SUBLANES - 1, :]
    z1 = jnp.where(row == 0, t1, pltpu.roll(z, 1, axis=0))
    z2 = jnp.where(row == 0, t2, jnp.where(row == 1, t1, pltpu.roll(z, 2, axis=0)))
    tail_ref[...] = z[tm - SUBLANES:, :]
    conv = z2 * cw_ref[0:1, :]
    conv = conv + z1 * cw_ref[1:2, :]
    conv = conv + z * cw_ref[2:3, :]
    gated = (b_gate * conv).astype(jnp.bfloat16)
    h = x + jnp.dot(gated, wout_ref[...], preferred_element_type=jnp.float32)
    _router_epilogue(h, gffn_ref, wr_ref, brt_ref, carry_ref,
                     h_out_ref, xrows_ref, ri_ref, rf_ref, cnt_ref)


def _conv_mixer_call(x2, g, win, cw, wout, gffn, wr, brt, seq):
    T = x2.shape[0]
    tm = TM_MIX1
    return pl.pallas_call(
        functools.partial(_conv_mixer_kernel, seq // tm),
        grid=(T // tm,),
        in_specs=[
            pl.BlockSpec((tm, D_MODEL), lambda i: (i, 0)),
            pl.BlockSpec((1, D_MODEL), lambda i: (0, 0)),
            pl.BlockSpec((D_MODEL, 3 * D_MODEL), lambda i: (0, 0)),
            pl.BlockSpec((CONV_WIDTH, D_MODEL), lambda i: (0, 0)),
            pl.BlockSpec((D_MODEL, D_MODEL), lambda i: (0, 0)),
        ] + _epilogue_in_specs(),
        out_specs=_epilogue_out_specs(tm),
        out_shape=_epilogue_out_shapes(T),
        scratch_shapes=[pltpu.VMEM((N_EXPERTS, 1), jnp.float32),
                        pltpu.VMEM((SUBLANES, D_MODEL), jnp.float32)],
        compiler_params=_params(("arbitrary",)),
        name="conv_mixer_router",
    )(x2, g, win, cw, wout, gffn, wr, brt)


def _row_copy(src_hbm, src_row, dst_hbm, dst_row, sem):
    return pltpu.make_async_copy(
        src_hbm.at[pl.ds(pl.multiple_of(src_row, ROW_CHUNKS), ROW_CHUNKS)],
        dst_hbm.at[pl.ds(pl.multiple_of(dst_row, ROW_CHUNKS), ROW_CHUNKS)], sem)


def _slot_rows_kernel(ri_ref, pstart_ref, o_ref):
    tb = ri_ref.shape[1]
    eio = lax.broadcasted_iota(jnp.int32, (N_EXPERTS, tb), 0)
    pstart = pstart_ref[...]
    rows = []
    for k in range(2):
        start = jnp.sum(jnp.where(eio == ri_ref[k:k + 1, :], pstart, 0), axis=0, keepdims=True)
        rows.append((start + ri_ref[2 + k:3 + k, :]) * ROW_CHUNKS)
    zi = jnp.zeros_like(rows[0])
    o_ref[...] = jnp.concatenate(rows + [zi] * (SUBLANES - 2), axis=0)


def _slot_rows_call(ri, pstart):
    T = ri.shape[1]
    tb = TM_SLOT_ROWS
    return pl.pallas_call(
        _slot_rows_kernel,
        grid=(T // tb,),
        in_specs=[pl.BlockSpec((SUBLANES, tb), lambda i: (0, i)),
                  pl.BlockSpec((N_EXPERTS, 1), lambda i: (0, 0))],
        out_specs=pl.BlockSpec((SUBLANES, tb), lambda i: (0, i)),
        out_shape=jax.ShapeDtypeStruct((SUBLANES, T), jnp.int32),
        compiler_params=_params(("parallel",)),
        name="moe_slot_rows",
    )(ri, pstart[:, None])


def _dispatch_kernel(d0_ref, d1_ref, padrow_ref, x_ref, xs_hbm, zero_ref, sem, zsem):
    i = pl.program_id(0)
    tile_rows = TM_EXPERT * ROW_CHUNKS

    @pl.when(i == 0)
    def _():
        zero_ref[...] = jnp.zeros_like(zero_ref)

        def zcopy(e):
            return pltpu.make_async_copy(
                zero_ref,
                xs_hbm.at[pl.ds(pl.multiple_of(padrow_ref[e], tile_rows), tile_rows)], zsem)

        def zstart(e, c):
            @pl.when(padrow_ref[e] >= 0)
            def _():
                zcopy(e).start()
            return c

        def zwait(e, c):
            @pl.when(padrow_ref[e] >= 0)
            def _():
                zcopy(e).wait()
            return c

        lax.fori_loop(0, N_EXPERTS, zstart, 0)
        lax.fori_loop(0, N_EXPERTS, zwait, 0)

        def tcopy(b):
            return pltpu.make_async_copy(
                zero_ref, xs_hbm.at[pl.ds(pl.multiple_of(b * tile_rows, tile_rows), tile_rows)], zsem)

        n_used = padrow_ref[N_EXPERTS]
        n_blk = xs_hbm.shape[0] // tile_rows
        lax.fori_loop(n_used, n_blk, lambda b, c: (tcopy(b).start(), c)[1], 0)
        lax.fori_loop(n_used, n_blk, lambda b, c: (tcopy(b).wait(), c)[1], 0)

    def issue(t, c):
        src = t * ROW_CHUNKS
        _row_copy(x_ref, src, xs_hbm, d0_ref[t], sem).start(priority=0)
        _row_copy(x_ref, src, xs_hbm, d1_ref[t], sem).start(priority=1)
        return c

    lax.fori_loop(0, TM_DISPATCH, issue, 0, unroll=DMA_UNROLL)
    for _ in range(2):
        pltpu.make_async_copy(x_ref, xs_hbm.at[pl.ds(0, TM_DISPATCH * ROW_CHUNKS)], sem).wait()


def _dispatch_call(d0, d1, padrow, xrows, capacity):
    T = d0.shape[0]
    return pl.pallas_call(
        _dispatch_kernel,
        grid=(T // TM_DISPATCH,),
        in_specs=[
            pl.BlockSpec((TM_DISPATCH,), lambda i: (i,), memory_space=pltpu.SMEM),
            pl.BlockSpec((TM_DISPATCH,), lambda i: (i,), memory_space=pltpu.SMEM),
            pl.BlockSpec(memory_space=pltpu.SMEM),
            pl.BlockSpec((TM_DISPATCH * ROW_CHUNKS, LANES), lambda i: (i, 0)),
        ],
        out_specs=pl.BlockSpec(memory_space=pl.ANY),
        out_shape=jax.ShapeDtypeStruct((capacity * ROW_CHUNKS, LANES), jnp.float32),
        scratch_shapes=[pltpu.VMEM((TM_EXPERT * ROW_CHUNKS, LANES), jnp.float32),
                        pltpu.SemaphoreType.DMA, pltpu.SemaphoreType.DMA],
        compiler_params=_params(("arbitrary",)),
        name="moe_dispatch",
    )(d0, d1, padrow, xrows)


def _expert_kernel(tstart_ref, ntile_ref, xs_hbm, wgu_ref, wd_ref, ys_hbm,
                   xbuf, ybuf, wgu_bf, wd_bf, sem_in, sem_out):
    e = pl.program_id(0)
    tm = TM_EXPERT
    tile_rows = tm * ROW_CHUNKS
    t0 = tstart_ref[e]
    nt = ntile_ref[e]
    n_used = tstart_ref[N_EXPERTS - 1] + ntile_ref[N_EXPERTS - 1]

    def tile_at(ref, g):
        return ref.at[pl.ds(pl.multiple_of(g * tile_rows, tile_rows), tile_rows)]

    def in_copy(g, slot):
        return pltpu.make_async_copy(tile_at(xs_hbm, g), xbuf.at[slot], sem_in.at[slot])

    def out_copy(g, slot):
        return pltpu.make_async_copy(ybuf.at[slot], tile_at(ys_hbm, g), sem_out.at[slot])

    @pl.when(e == 0)
    def _():
        in_copy(0, 0).start()

    @pl.when(nt > 0)
    def _():
        wgu_bf[...] = wgu_ref[0, 0].astype(jnp.bfloat16)
        wd_bf[...] = wd_ref[0, 0].astype(jnp.bfloat16)

    def tile_body(g, carry):
        slot = g % 2
        in_copy(g, slot).wait()

        @pl.when(g + 1 < n_used)
        def _():
            in_copy(g + 1, 1 - slot).start()

        @pl.when(g >= 2)
        def _():
            out_copy(g - 2, slot).wait()

        x = jnp.concatenate(
            [xbuf[slot, pl.ds(c, tm, stride=ROW_CHUNKS), :] for c in range(ROW_CHUNKS)], axis=-1)
        gu = jnp.dot(x.astype(jnp.bfloat16), wgu_bf[...], preferred_element_type=jnp.float32)
        g_act = gu[:, :EXPERT_FF]
        u = gu[:, EXPERT_FF:]
        act = (g_act / (1.0 + jnp.exp(-g_act))) * u
        y = jnp.dot(act.astype(jnp.bfloat16), wd_bf[...], preferred_element_type=jnp.float32)
        for c in range(ROW_CHUNKS):
            ybuf[slot, pl.ds(c, tm, stride=ROW_CHUNKS), :] = y[:, c * LANES:(c + 1) * LANES]
        out_copy(g, slot).start()
        return carry

    lax.fori_loop(t0, t0 + nt, tile_body, 0)

    @pl.when(e == N_EXPERTS - 1)
    def _():
        @pl.when(n_used >= 2)
        def _():
            out_copy(n_used - 2, n_used % 2).wait()

        out_copy(n_used - 1, (n_used - 1) % 2).wait()

        n_blk = ys_hbm.shape[0] // tile_rows
        ybuf[0] = jnp.zeros(ybuf.shape[1:], ybuf.dtype)
        lax.fori_loop(n_used, n_blk, lambda b, c: (out_copy(b, 0).start(), c)[1], 0)
        lax.fori_loop(n_used, n_blk, lambda b, c: (out_copy(b, 0).wait(), c)[1], 0)


def _expert_call(layer, tstart, ntile, xs, wgu, wd, capacity):
    tile_rows = TM_EXPERT * ROW_CHUNKS

    def w_map(e, ts, nt):
        return (layer, e, 0, 0)

    return pl.pallas_call(
        _expert_kernel,
        grid_spec=pltpu.PrefetchScalarGridSpec(
            num_scalar_prefetch=2,
            grid=(N_EXPERTS,),
            in_specs=[
                pl.BlockSpec(memory_space=pl.ANY),
                pl.BlockSpec((1, 1, D_MODEL, 2 * EXPERT_FF), w_map),
                pl.BlockSpec((1, 1, EXPERT_FF, D_MODEL), w_map),
            ],
            out_specs=pl.BlockSpec(memory_space=pl.ANY),
            scratch_shapes=[pltpu.VMEM((2, tile_rows, LANES), jnp.float32),
                            pltpu.VMEM((2, tile_rows, LANES), jnp.float32),
                            pltpu.VMEM((D_MODEL, 2 * EXPERT_FF), jnp.bfloat16),
                            pltpu.VMEM((EXPERT_FF, D_MODEL), jnp.bfloat16),
                            pltpu.SemaphoreType.DMA((2,)),
                            pltpu.SemaphoreType.DMA((2,))],
        ),
        out_shape=jax.ShapeDtypeStruct((capacity * ROW_CHUNKS, LANES), jnp.float32),
        compiler_params=_params(("arbitrary",)),
        name="moe_experts",
    )(tstart, ntile, xs, wgu, wd)


def _combine_kernel(final_norm, d0_ref, d1_ref, d0n_ref, d1n_ref, ys_hbm, h_ref, gate_ref, gfin_ref,
                    o_ref, y0_ref, y1_ref, sem):
    tm = TM_COMBINE
    i = pl.program_id(0)
    slot = i % 2

    def issue_tile(d0r, d1r, s):
        def issue(t, c):
            dst = t * ROW_CHUNKS
            _row_copy(ys_hbm, d0r[t], y0_ref.at[s], dst, sem.at[s]).start(priority=0)
            _row_copy(ys_hbm, d1r[t], y1_ref.at[s], dst, sem.at[s]).start(priority=1)
            return c
        lax.fori_loop(0, tm, issue, 0, unroll=DMA_UNROLL)

    @pl.when(i == 0)
    def _():
        issue_tile(d0_ref, d1_ref, 0)

    @pl.when(i + 1 < pl.num_programs(0))
    def _():
        issue_tile(d0n_ref, d1n_ref, 1 - slot)

    for buf in (y0_ref, y1_ref):
        pltpu.make_async_copy(ys_hbm.at[pl.ds(0, tm * ROW_CHUNKS)], buf.at[slot], sem.at[slot]).wait()

    g0 = gate_ref[:, 0:1]
    g1 = gate_ref[:, 1:2]
    cols = []
    for c in range(ROW_CHUNKS):
        y0 = y0_ref[slot, pl.ds(c, tm, stride=ROW_CHUNKS), :]
        y1 = y1_ref[slot, pl.ds(c, tm, stride=ROW_CHUNKS), :]
        cols.append(h_ref[:, c * LANES:(c + 1) * LANES] + (y0 * g0 + y1 * g1))
    h = jnp.concatenate(cols, axis=-1)
    if final_norm:
        h = _rms(h, gfin_ref[...])
    o_ref[...] = h


def _combine_call(d0, d1, ys, h, gates, gfin, final_norm):
    T = h.shape[0]
    tm = TM_COMBINE
    last = T // tm - 1
    return pl.pallas_call(
        functools.partial(_combine_kernel, final_norm),
        grid=(T // tm,),
        in_specs=[
            pl.BlockSpec((tm,), lambda i: (i,), memory_space=pltpu.SMEM),
            pl.BlockSpec((tm,), lambda i: (i,), memory_space=pltpu.SMEM),
            pl.BlockSpec((tm,), lambda i: (jnp.minimum(i + 1, last),), memory_space=pltpu.SMEM),
            pl.BlockSpec((tm,), lambda i: (jnp.minimum(i + 1, last),), memory_space=pltpu.SMEM),
            pl.BlockSpec(memory_space=pl.ANY),
            pl.BlockSpec((tm, D_MODEL), lambda i: (i, 0)),
            pl.BlockSpec((tm, 2), lambda i: (i, 0)),
            pl.BlockSpec((1, D_MODEL), lambda i: (0, 0)),
        ],
        out_specs=pl.BlockSpec((tm, D_MODEL), lambda i: (i, 0)),
        out_shape=jax.ShapeDtypeStruct((T, D_MODEL), jnp.float32),
        scratch_shapes=[pltpu.VMEM((2, tm * ROW_CHUNKS, LANES), jnp.float32),
                        pltpu.VMEM((2, tm * ROW_CHUNKS, LANES), jnp.float32),
                        pltpu.SemaphoreType.DMA((2,))],
        compiler_params=_params(("arbitrary",)),
        name="moe_combine",
    )(d0, d1, d0, d1, ys, h, gates, gfin)


def _group_heads_last(w):
    lead = w.shape[:-1]
    w = w.reshape(lead + (N_KV_HEADS, GROUP, HEAD_DIM))
    w = jnp.swapaxes(w, -3, -2)
    return w.reshape(lead + (N_Q_HEADS * HEAD_DIM,))


def _rel_bucket_table():
    qi = np.arange(ATTN_BLOCK)[:, None]
    kj = np.arange(2 * ATTN_BLOCK)[None, :]
    dist = np.maximum(qi + ATTN_BLOCK - kj, 0)
    max_exact = N_REL_BUCKETS // 2
    d = np.maximum(dist, max_exact).astype(np.float32)
    large = max_exact + (np.log(d / np.float32(max_exact)) / np.float32(math.log(REL_MAX_DISTANCE / max_exact))
                         * np.float32(N_REL_BUCKETS - max_exact)).astype(np.int32)
    large = np.minimum(large, N_REL_BUCKETS - 1)
    return np.where(dist < max_exact, dist, large).astype(np.int32)


def _rel_bias_blocks(rel_bias, sinks):
    onehot = (jnp.asarray(_rel_bucket_table())[..., None] == jnp.arange(N_REL_BUCKETS)).astype(jnp.float32)
    bias = jnp.einsum("qkb,bh->hqk", onehot, rel_bias.astype(jnp.float32),
                      precision=lax.Precision.HIGHEST)
    sink_col = jnp.broadcast_to(sinks.astype(jnp.float32)[:, None, None], (N_Q_HEADS, ATTN_BLOCK, 1))
    return jnp.concatenate([sink_col, bias[:, :, 1:]], axis=2)


def _router_weights(w_group, b_group, w_expert, b_expert):
    zw = jnp.zeros((D_MODEL, SUBLANES - N_GROUPS), jnp.float32)
    zw2 = jnp.zeros((D_MODEL, LANES - SUBLANES - N_EXPERTS), jnp.float32)
    wr = jnp.concatenate([w_group, zw, w_expert, zw2], axis=1)
    wh = wr.astype(jnp.bfloat16)
    wl = (wr - wh.astype(jnp.float32)).astype(jnp.bfloat16)
    wr = jnp.concatenate([wh, wl], axis=1)
    brt = jnp.concatenate([b_group, jnp.zeros((SUBLANES - N_GROUPS,), jnp.float32), b_expert])[:, None]
    return wr, brt


def _moe(layer, h, xrows, ri, rf, cnt, w_gate_up, w_down, gfin, final_norm):
    T = h.shape[0]
    tm = TM_EXPERT
    capacity = -(-(2 * T) // tm) * tm + N_EXPERTS * tm
    counts = cnt[:, 0].astype(jnp.int32)
    ntile = (counts + tm - 1) // tm
    tend = jnp.cumsum(ntile)
    pstart = (tend - ntile) * tm
    n_used = tend[-1:]
    padrow = jnp.concatenate([jnp.where(ntile > 0, (tend - 1) * (tm * ROW_CHUNKS), -1), n_used]
                             ).astype(jnp.int32)

    dst = _slot_rows_call(ri, pstart)
    d0, d1 = dst[0], dst[1]
    xs = _dispatch_call(d0, d1, padrow, xrows, capacity)
    ys = _expert_call(layer, (tend - ntile).astype(jnp.int32), ntile.astype(jnp.int32), xs,
                      w_gate_up, w_down, capacity)
    gates = rf[:2].T
    return _combine_call(d0, d1, ys, h, gates, gfin, final_norm)


def kernel(x, norm_mix, norm_ffn, final_norm, rel_bias, attn_w_qkv, attn_b_qkv, attn_w_o, attn_b_o,
           attn_sinks, conv_w_in, conv_w, conv_w_out, moe_w_group, moe_b_group, moe_w_expert,
           moe_b_expert, moe_w_gate_up, moe_w_down):
    B, S, D = x.shape
    T = B * S
    x2 = x.reshape(T, D)
    nq = N_Q_HEADS * HEAD_DIM
    scale = HEAD_DIM ** -0.5

    wqkv = attn_w_qkv[0]
    bqkv = attn_b_qkv[0]
    w_all = jnp.concatenate([_group_heads_last(wqkv[:, :nq]) * scale, wqkv[:, nq:]],
                            axis=1).astype(jnp.bfloat16)
    b_all = jnp.concatenate([_group_heads_last(bqkv[:nq]) * scale, bqkv[nq:]])[None, :]
    q, kt, v = _qkv_call(x2, norm_mix[0][None, :], w_all, b_all)
    a = _attn_call(q, kt, v, _rel_bias_blocks(rel_bias, attn_sinks[0]), B, S)
    wr, brt = _router_weights(moe_w_group[0], moe_b_group[0], moe_w_expert[0], moe_b_expert[0])
    w_o = _group_heads_last(attn_w_o[0].T).T.astype(jnp.bfloat16)
    h, xrows, ri, rf, cnt = _attn_out_call(
        a, w_o, attn_b_o[0][None, :], x2, norm_ffn[0][None, :], wr, brt)
    h = _moe(0, h, xrows, ri, rf, cnt, moe_w_gate_up, moe_w_down, final_norm[None, :], False)

    wr, brt = _router_weights(moe_w_group[1], moe_b_group[1], moe_w_expert[1], moe_b_expert[1])
    h, xrows, ri, rf, cnt = _conv_mixer_call(
        h, norm_mix[1][None, :], conv_w_in[0].astype(jnp.bfloat16), conv_w[0],
        conv_w_out[0].astype(jnp.bfloat16), norm_ffn[1][None, :], wr, brt, S)
    out = _moe(1, h, xrows, ri, rf, cnt, moe_w_gate_up, moe_w_down, final_norm[None, :], True)
    return out.reshape(B, S, D)
```

```python
import functools
import math

import numpy as np
import jax
import jax.numpy as jnp
from jax import lax
from jax.experimental import pallas as pl
from jax.experimental.pallas import tpu as pltpu

D_MODEL = 1024
N_Q_HEADS = 16
N_KV_HEADS = 2
HEAD_DIM = 64
GROUP = N_Q_HEADS // N_KV_HEADS
WINDOW = 128
ATTN_BLOCK = 128
N_REL_BUCKETS = 32
REL_MAX_DISTANCE = 128
CONV_WIDTH = 3
N_GROUPS = 4
EXPERTS_PER_GROUP = 8
N_EXPERTS = N_GROUPS * EXPERTS_PER_GROUP
EXPERT_FF = 512
RMS_EPS = 1e-5

LANES = 128
SUBLANES = 8
ROW_CHUNKS = D_MODEL // LANES
XS_CHUNKS = ROW_CHUNKS // 2
VMEM_LIMIT = 56 * 1024 * 1024

TM_DENSE = 512
TQ = 512
TM_MIX1 = 512
TM_EXPERT = 512
TM_DISPATCH = 512
TM_COMBINE = 256
DMA_UNROLL = 8
TM_SLOT_ROWS = 4096
ROUTER_ROWS = 40


def _rms(x, g):
    return x * lax.rsqrt(jnp.mean(x * x, axis=-1, keepdims=True) + RMS_EPS) * g


def _params(sem):
    return pltpu.CompilerParams(dimension_semantics=sem, vmem_limit_bytes=VMEM_LIMIT)


def _qkv_kernel(x_ref, g_ref, w_ref, b_ref, q_ref, kt_ref, v_ref):
    xn = _rms(x_ref[...], g_ref[...]).astype(jnp.bfloat16)
    out = jnp.dot(xn, w_ref[...], preferred_element_type=jnp.float32) + b_ref[...]
    nq = N_Q_HEADS * HEAD_DIM
    nk = N_KV_HEADS * HEAD_DIM
    q_ref[...] = out[:, :nq].astype(jnp.bfloat16)
    kt_ref[...] = out[:, nq:nq + nk].T.astype(jnp.bfloat16)
    v_ref[...] = out[:, nq + nk:].astype(jnp.bfloat16)


def _qkv_call(x2, g, w, b):
    T = x2.shape[0]
    nq = N_Q_HEADS * HEAD_DIM
    nk = N_KV_HEADS * HEAD_DIM
    tm = TM_DENSE
    return pl.pallas_call(
        _qkv_kernel,
        grid=(T // tm,),
        in_specs=[
            pl.BlockSpec((tm, D_MODEL), lambda i: (i, 0)),
            pl.BlockSpec((1, D_MODEL), lambda i: (0, 0)),
            pl.BlockSpec((D_MODEL, nq + 2 * nk), lambda i: (0, 0)),
            pl.BlockSpec((1, nq + 2 * nk), lambda i: (0, 0)),
        ],
        out_specs=[
            pl.BlockSpec((tm, nq), lambda i: (i, 0)),
            pl.BlockSpec((nk, tm), lambda i: (0, i)),
            pl.BlockSpec((tm, nk), lambda i: (i, 0)),
        ],
        out_shape=[
            jax.ShapeDtypeStruct((T, nq), jnp.bfloat16),
            jax.ShapeDtypeStruct((nk, T), jnp.bfloat16),
            jax.ShapeDtypeStruct((T, nk), jnp.bfloat16),
        ],
        compiler_params=_params(("parallel",)),
        name="qkv_proj",
    )(x2, g, w, b)


def _attn_kernel(q_ref, kt_ref, ktp_ref, v_ref, vp_ref, bias_ref, o_ref):
    first_tile = pl.program_id(1) == 0
    nblk = TQ // ATTN_BLOCK
    qi_io = lax.broadcasted_iota(jnp.int32, (ATTN_BLOCK, 2 * ATTN_BLOCK), 0)
    kj_io = lax.broadcasted_iota(jnp.int32, (ATTN_BLOCK, 2 * ATTN_BLOCK), 1)
    dist = qi_io + ATTN_BLOCK - kj_io
    band = (dist >= 0) & (dist < WINDOW)
    sink_col = kj_io == 0
    kt_keep = lax.broadcasted_iota(jnp.int32, (N_KV_HEADS * HEAD_DIM, 2 * ATTN_BLOCK), 1) > 0
    v_keep = lax.broadcasted_iota(jnp.int32, (2 * ATTN_BLOCK, N_KV_HEADS * HEAD_DIM), 0) > 0
    low_half = lax.broadcasted_iota(jnp.int32, (ATTN_BLOCK, LANES), 1) < HEAD_DIM
    for qi in range(nblk):
        r0 = qi * ATTN_BLOCK
        if qi == 0:
            kt_blk = jnp.concatenate([ktp_ref[...], kt_ref[:, 0:ATTN_BLOCK]], axis=1)
            v_blk = jnp.concatenate([vp_ref[...], v_ref[0:ATTN_BLOCK, :]], axis=0)
            mask = (band & (jnp.logical_not(first_tile) | (kj_io >= ATTN_BLOCK))) | sink_col
        else:
            kt_blk = kt_ref[:, r0 - ATTN_BLOCK:r0 + ATTN_BLOCK]
            v_blk = v_ref[r0 - ATTN_BLOCK:r0 + ATTN_BLOCK, :]
            mask = band | sink_col
        kt_blk = jnp.where(kt_keep, kt_blk, jnp.zeros_like(kt_blk))
        v_blk = jnp.where(v_keep, v_blk, jnp.zeros_like(v_blk))
        for m in range(GROUP):
            qg = q_ref[r0:r0 + ATTN_BLOCK, m * LANES:(m + 1) * LANES]
            halves = []
            for half in range(N_KV_HEADS):
                h = m + GROUP * half
                keep = low_half if half == 0 else jnp.logical_not(low_half)
                qpad = jnp.where(keep, qg, jnp.zeros_like(qg))
                s = jnp.dot(qpad, kt_blk, preferred_element_type=jnp.float32)
                s = jnp.where(mask, s + bias_ref[h], -1e30)
                mx = jnp.max(s, axis=-1, keepdims=True)
                p = jnp.exp(s - mx)
                den = jnp.sum(p, axis=-1, keepdims=True)
                pv = jnp.dot(p.astype(jnp.bfloat16), v_blk, preferred_element_type=jnp.float32)
                halves.append(pv * (1.0 / den))
            og = jnp.where(low_half, halves[0], halves[1])
            o_ref[r0:r0 + ATTN_BLOCK, m * LANES:(m + 1) * LANES] = og.astype(jnp.bfloat16)


def _attn_call(q, kt, v, bias, batch, seq):
    T = q.shape[0]
    nq = N_Q_HEADS * HEAD_DIM
    nk = N_KV_HEADS * HEAD_DIM
    tiles = seq // TQ
    per = TQ // ATTN_BLOCK

    def cur(b, j):
        return b * tiles + j

    def prev(b, j):
        return jnp.maximum((b * tiles + j) * per - 1, b * tiles * per)

    return pl.pallas_call(
        _attn_kernel,
        grid=(batch, tiles),
        in_specs=[
            pl.BlockSpec((TQ, nq), lambda b, j: (cur(b, j), 0)),
            pl.BlockSpec((nk, TQ), lambda b, j: (0, cur(b, j))),
            pl.BlockSpec((nk, ATTN_BLOCK), lambda b, j: (0, prev(b, j))),
            pl.BlockSpec((TQ, nk), lambda b, j: (cur(b, j), 0)),
            pl.BlockSpec((ATTN_BLOCK, nk), lambda b, j: (prev(b, j), 0)),
            pl.BlockSpec((N_Q_HEADS, ATTN_BLOCK, 2 * ATTN_BLOCK), lambda b, j: (0, 0, 0)),
        ],
        out_specs=pl.BlockSpec((TQ, nq), lambda b, j: (cur(b, j), 0)),
        out_shape=jax.ShapeDtypeStruct((T, nq), jnp.bfloat16),
        compiler_params=_params(("parallel", "parallel")),
        name="swa_attention",
    )(q, kt, kt, v, v, bias)


def _router_epilogue(h, gffn_ref, wr_ref, brt_ref, carry_ref,
                     h_out_ref, xrows_ref, ri_ref, rf_ref, cnt_ref):
    tm = h.shape[0]
    h_out_ref[...] = h
    xn = _rms(h, gffn_ref[...])
    xh = xn.astype(jnp.bfloat16)
    xbits = lax.bitcast_convert_type(xh.astype(jnp.float32), jnp.uint32)
    for c in range(XS_CHUNKS):
        lo = xbits[:, (2 * c) * LANES:(2 * c + 1) * LANES] >> 16
        hi = xbits[:, (2 * c + 1) * LANES:(2 * c + 2) * LANES]
        xrows_ref[pl.ds(c, tm, stride=XS_CHUNKS), :] = lo | hi

    xl = (xn - xh.astype(jnp.float32)).astype(jnp.bfloat16)
    wcat = wr_ref[...]
    prod = jnp.dot(xh, wcat, preferred_element_type=jnp.float32)
    logits = (prod[:, :LANES] + prod[:, LANES:]) + jnp.dot(
        xl, wcat[:, :LANES], preferred_element_type=jnp.float32)
    lt = logits.T[:ROUTER_ROWS, :] + brt_ref[...]

    gl = [lt[g:g + 1, :] for g in range(N_GROUPS)]
    gmax = functools.reduce(jnp.maximum, gl)
    gexp = [jnp.exp(x - gmax) for x in gl]
    gsum = functools.reduce(lambda a, b: a + b, gexp)
    gprob = [x / gsum for x in gexp]
    g_prob = functools.reduce(jnp.maximum, gprob)
    g_idx = jnp.full(g_prob.shape, N_GROUPS - 1, jnp.int32)
    for g in range(N_GROUPS - 2, -1, -1):
        g_idx = jnp.where(gprob[g] == g_prob, g, g_idx)

    el = []
    for j in range(EXPERTS_PER_GROUP):
        x = lt[SUBLANES + j:SUBLANES + j + 1, :]
        for g in range(1, N_GROUPS):
            r = SUBLANES + g * EXPERTS_PER_GROUP + j
            x = jnp.where(g_idx == g, lt[r:r + 1, :], x)
        el.append(x)
    emax = functools.reduce(jnp.maximum, el)
    eexp = [jnp.exp(x - emax) for x in el]
    esum = functools.reduce(lambda a, b: a + b, eexp)
    eprob = [x / esum for x in eexp]
    p1 = functools.reduce(jnp.maximum, eprob)
    i1 = jnp.full(p1.shape, EXPERTS_PER_GROUP - 1, jnp.int32)
    for j in range(EXPERTS_PER_GROUP - 2, -1, -1):
        i1 = jnp.where(eprob[j] == p1, j, i1)
    rest = [jnp.where(i1 == j, -1.0, eprob[j]) for j in range(EXPERTS_PER_GROUP)]
    p2 = functools.reduce(jnp.maximum, rest)
    i2 = jnp.full(p2.shape, EXPERTS_PER_GROUP - 1, jnp.int32)
    for j in range(EXPERTS_PER_GROUP - 2, -1, -1):
        i2 = jnp.where(rest[j] == p2, j, i2)
    psum = p1 + p2
    gate0 = g_prob * (p1 / psum)
    gate1 = g_prob * (p2 / psum)
    e0 = g_idx * EXPERTS_PER_GROUP + i1
    e1 = g_idx * EXPERTS_PER_GROUP + i2

    eio = lax.broadcasted_iota(jnp.int32, (N_EXPERTS, tm), 0)
    oh0 = (eio == e0).astype(jnp.float32)
    oh1 = (eio == e1).astype(jnp.float32)
    both = oh0 + oh1
    tr = lax.broadcasted_iota(jnp.int32, (tm, tm), 0)
    tc = lax.broadcasted_iota(jnp.int32, (tm, tm), 1)
    upper = (tr < tc).astype(jnp.bfloat16)
    before = jnp.dot(both.astype(jnp.bfloat16), upper, preferred_element_type=jnp.float32)
    before = before + carry_ref[...]
    rank0 = jnp.sum(oh0 * before, axis=0, keepdims=True)
    rank1 = jnp.sum(oh1 * (before + oh0), axis=0, keepdims=True)
    carry_ref[...] = carry_ref[...] + jnp.sum(both, axis=1, keepdims=True)
    cnt_ref[...] = jnp.broadcast_to(carry_ref[...], cnt_ref.shape)

    zi = jnp.zeros_like(e0)
    ri_ref[...] = jnp.concatenate(
        [e0, e1, rank0.astype(jnp.int32), rank1.astype(jnp.int32), zi, zi, zi, zi], axis=0)
    zf = jnp.zeros_like(gate0)
    rf_ref[...] = jnp.concatenate([gate0, gate1, zf, zf, zf, zf, zf, zf], axis=0)


def _epilogue_out_specs(tm):
    return [
        pl.BlockSpec((tm, D_MODEL), lambda i: (i, 0)),
        pl.BlockSpec((tm * XS_CHUNKS, LANES), lambda i: (i, 0)),
        pl.BlockSpec((SUBLANES, tm), lambda i: (0, i)),
        pl.BlockSpec((SUBLANES, tm), lambda i: (0, i)),
        pl.BlockSpec((N_EXPERTS, LANES), lambda i: (0, 0)),
    ]


def _epilogue_out_shapes(T):
    return [
        jax.ShapeDtypeStruct((T, D_MODEL), jnp.float32),
        jax.ShapeDtypeStruct((T * XS_CHUNKS, LANES), jnp.uint32),
        jax.ShapeDtypeStruct((SUBLANES, T), jnp.int32),
        jax.ShapeDtypeStruct((SUBLANES, T), jnp.float32),
        jax.ShapeDtypeStruct((N_EXPERTS, LANES), jnp.float32),
    ]


def _epilogue_in_specs():
    return [
        pl.BlockSpec((1, D_MODEL), lambda i: (0, 0)),
        pl.BlockSpec((D_MODEL, 2 * LANES), lambda i: (0, 0)),
        pl.BlockSpec((ROUTER_ROWS, 1), lambda i: (0, 0)),
    ]


def _attn_out_kernel(a_ref, w_ref, b_ref, res_ref, gffn_ref, wr_ref, brt_ref,
                     h_out_ref, xrows_ref, ri_ref, rf_ref, cnt_ref, carry_ref):
    @pl.when(pl.program_id(0) == 0)
    def _():
        carry_ref[...] = jnp.zeros_like(carry_ref)

    mix = jnp.dot(a_ref[...], w_ref[...], preferred_element_type=jnp.float32) + b_ref[...]
    h = res_ref[...] + mix
    _router_epilogue(h, gffn_ref, wr_ref, brt_ref, carry_ref,
                     h_out_ref, xrows_ref, ri_ref, rf_ref, cnt_ref)


def _attn_out_call(a, w, b, res, gffn, wr, brt):
    T = a.shape[0]
    tm = TM_DENSE
    return pl.pallas_call(
        _attn_out_kernel,
        grid=(T // tm,),
        in_specs=[
            pl.BlockSpec((tm, D_MODEL), lambda i: (i, 0)),
            pl.BlockSpec((D_MODEL, D_MODEL), lambda i: (0, 0)),
            pl.BlockSpec((1, D_MODEL), lambda i: (0, 0)),
            pl.BlockSpec((tm, D_MODEL), lambda i: (i, 0)),
        ] + _epilogue_in_specs(),
        out_specs=_epilogue_out_specs(tm),
        out_shape=_epilogue_out_shapes(T),
        scratch_shapes=[pltpu.VMEM((N_EXPERTS, 1), jnp.float32)],
        compiler_params=_params(("arbitrary",)),
        name="attn_out_router",
    )(a, w, b, res, gffn, wr, brt)


def _conv_mixer_kernel(tiles_per_seq, x_ref, g_ref, win_ref, cw_ref, wout_ref,
                       gffn_ref, wr_ref, brt_ref,
                       h_out_ref, xrows_ref, ri_ref, rf_ref, cnt_ref, carry_ref, tail_ref):
    i = pl.program_id(0)
    tm = x_ref.shape[0]

    @pl.when(i == 0)
    def _():
        carry_ref[...] = jnp.zeros_like(carry_ref)

    @pl.when(i % tiles_per_seq == 0)
    def _():
        tail_ref[...] = jnp.zeros_like(tail_ref)

    x = x_ref[...]
    xn = _rms(x, g_ref[...]).astype(jnp.bfloat16)
    bcu = jnp.dot(xn, win_ref[...], preferred_element_type=jnp.float32)
    b_gate = bcu[:, :D_MODEL]
    z = bcu[:, D_MODEL:2 * D_MODEL] * bcu[:, 2 * D_MODEL:]
    row = lax.broadcasted_iota(jnp.int32, (tm, D_MODEL), 0)
    tail = tail_ref[...]
    t1 = tail[SUBLANES - 1:SUBLANES, :]
    t2 = tail[SUBLANES - 2:SUBLANES - 1, :]
    z1 = jnp.where(row == 0, t1, pltpu.roll(z, 1, axis=0))
    z2 = jnp.where(row == 0, t2, jnp.where(row == 1, t1, pltpu.roll(z, 2, axis=0)))
    tail_ref[...] = z[tm - SUBLANES:, :]
    conv = z2 * cw_ref[0:1, :]
    conv = conv + z1 * cw_ref[1:2, :]
    conv = conv + z * cw_ref[2:3, :]
    gated = (b_gate * conv).astype(jnp.bfloat16)
    h = x + jnp.dot(gated, wout_ref[...], preferred_element_type=jnp.float32)
    _router_epilogue(h, gffn_ref, wr_ref, brt_ref, carry_ref,
                     h_out_ref, xrows_ref, ri_ref, rf_ref, cnt_ref)


def _conv_mixer_call(x2, g, win, cw, wout, gffn, wr, brt, seq):
    T = x2.shape[0]
    tm = TM_MIX1
    return pl.pallas_call(
        functools.partial(_conv_mixer_kernel, seq // tm),
        grid=(T // tm,),
        in_specs=[
            pl.BlockSpec((tm, D_MODEL), lambda i: (i, 0)),
            pl.BlockSpec((1, D_MODEL), lambda i: (0, 0)),
            pl.BlockSpec((D_MODEL, 3 * D_MODEL), lambda i: (0, 0)),
            pl.BlockSpec((CONV_WIDTH, D_MODEL), lambda i: (0, 0)),
            pl.BlockSpec((D_MODEL, D_MODEL), lambda i: (0, 0)),
        ] + _epilogue_in_specs(),
        out_specs=_epilogue_out_specs(tm),
        out_shape=_epilogue_out_shapes(T),
        scratch_shapes=[pltpu.VMEM((N_EXPERTS, 1), jnp.float32),
                        pltpu.VMEM((SUBLANES, D_MODEL), jnp.float32)],
        compiler_params=_params(("arbitrary",)),
        name="conv_mixer_router",
    )(x2, g, win, cw, wout, gffn, wr, brt)


def _row_copy(src_hbm, src_row, dst_hbm, dst_row, sem, chunks):
    return pltpu.make_async_copy(
        src_hbm.at[pl.ds(pl.multiple_of(src_row, chunks), chunks)],
        dst_hbm.at[pl.ds(pl.multiple_of(dst_row, chunks), chunks)], sem)


def _slot_rows_kernel(ri_ref, pstart_ref, o_ref):
    tb = ri_ref.shape[1]
    eio = lax.broadcasted_iota(jnp.int32, (N_EXPERTS, tb), 0)
    pstart = pstart_ref[...]
    slots = []
    for k in range(2):
        start = jnp.sum(jnp.where(eio == ri_ref[k:k + 1, :], pstart, 0), axis=0, keepdims=True)
        slots.append(start + ri_ref[2 + k:3 + k, :])
    zi = jnp.zeros_like(slots[0])
    o_ref[...] = jnp.concatenate(
        [s * XS_CHUNKS for s in slots] + [s * ROW_CHUNKS for s in slots] + [zi] * (SUBLANES - 4),
        axis=0)


def _slot_rows_call(ri, pstart):
    T = ri.shape[1]
    tb = TM_SLOT_ROWS
    return pl.pallas_call(
        _slot_rows_kernel,
        grid=(T // tb,),
        in_specs=[pl.BlockSpec((SUBLANES, tb), lambda i: (0, i)),
                  pl.BlockSpec((N_EXPERTS, 1), lambda i: (0, 0))],
        out_specs=pl.BlockSpec((SUBLANES, tb), lambda i: (0, i)),
        out_shape=jax.ShapeDtypeStruct((SUBLANES, T), jnp.int32),
        compiler_params=_params(("parallel",)),
        name="moe_slot_rows",
    )(ri, pstart[:, None])


def _dispatch_kernel(d0_ref, d1_ref, padrow_ref, x_ref, xs_hbm, zero_ref, sem, zsem):
    i = pl.program_id(0)
    tile_rows = TM_EXPERT * XS_CHUNKS

    @pl.when(i == 0)
    def _():
        zero_ref[...] = jnp.zeros_like(zero_ref)

        def zcopy(e):
            return pltpu.make_async_copy(
                zero_ref,
                xs_hbm.at[pl.ds(pl.multiple_of(padrow_ref[e], tile_rows), tile_rows)], zsem)

        def zstart(e, c):
            @pl.when(padrow_ref[e] >= 0)
            def _():
                zcopy(e).start()
            return c

        def zwait(e, c):
            @pl.when(padrow_ref[e] >= 0)
            def _():
                zcopy(e).wait()
            return c

        lax.fori_loop(0, N_EXPERTS, zstart, 0)
        lax.fori_loop(0, N_EXPERTS, zwait, 0)

        def tcopy(b):
            return pltpu.make_async_copy(
                zero_ref, xs_hbm.at[pl.ds(pl.multiple_of(b * tile_rows, tile_rows), tile_rows)], zsem)

        n_used = padrow_ref[N_EXPERTS]
        n_blk = xs_hbm.shape[0] // tile_rows
        lax.fori_loop(n_used, n_blk, lambda b, c: (tcopy(b).start(), c)[1], 0)
        lax.fori_loop(n_used, n_blk, lambda b, c: (tcopy(b).wait(), c)[1], 0)

    def issue(t, c):
        src = t * XS_CHUNKS
        _row_copy(x_ref, src, xs_hbm, d0_ref[t], sem, XS_CHUNKS).start(priority=0)
        _row_copy(x_ref, src, xs_hbm, d1_ref[t], sem, XS_CHUNKS).start(priority=1)
        return c

    lax.fori_loop(0, TM_DISPATCH, issue, 0, unroll=DMA_UNROLL)
    for _ in range(2):
        pltpu.make_async_copy(x_ref, xs_hbm.at[pl.ds(0, TM_DISPATCH * XS_CHUNKS)], sem).wait()


def _dispatch_call(d0, d1, padrow, xrows, capacity):
    T = d0.shape[0]
    return pl.pallas_call(
        _dispatch_kernel,
        grid=(T // TM_DISPATCH,),
        in_specs=[
            pl.BlockSpec((TM_DISPATCH,), lambda i: (i,), memory_space=pltpu.SMEM),
            pl.BlockSpec((TM_DISPATCH,), lambda i: (i,), memory_space=pltpu.SMEM),
            pl.BlockSpec(memory_space=pltpu.SMEM),
            pl.BlockSpec((TM_DISPATCH * XS_CHUNKS, LANES), lambda i: (i, 0)),
        ],
        out_specs=pl.BlockSpec(memory_space=pl.ANY),
        out_shape=jax.ShapeDtypeStruct((capacity * XS_CHUNKS, LANES), jnp.uint32),
        scratch_shapes=[pltpu.VMEM((TM_EXPERT * XS_CHUNKS, LANES), jnp.uint32),
                        pltpu.SemaphoreType.DMA, pltpu.SemaphoreType.DMA],
        compiler_params=_params(("arbitrary",)),
        name="moe_dispatch",
    )(d0, d1, padrow, xrows)


def _expert_kernel(tstart_ref, ntile_ref, xs_hbm, wgu_ref, wd_ref, ys_hbm,
                   xbuf, ybuf, wgu_bf, wd_bf, sem_in, sem_out):
    e = pl.program_id(0)
    tm = TM_EXPERT
    tile_rows = tm * ROW_CHUNKS
    t0 = tstart_ref[e]
    nt = ntile_ref[e]
    n_used = tstart_ref[N_EXPERTS - 1] + ntile_ref[N_EXPERTS - 1]

    def tile_at(ref, g, rows):
        return ref.at[pl.ds(pl.multiple_of(g * rows, rows), rows)]

    def in_copy(g, slot):
        return pltpu.make_async_copy(tile_at(xs_hbm, g, tm * XS_CHUNKS), xbuf.at[slot], sem_in.at[slot])

    def out_copy(g, slot):
        return pltpu.make_async_copy(ybuf.at[slot], tile_at(ys_hbm, g, tile_rows), sem_out.at[slot])

    @pl.when(e == 0)
    def _():
        in_copy(0, 0).start()

    @pl.when(nt > 0)
    def _():
        wgu_bf[...] = wgu_ref[0, 0].astype(jnp.bfloat16)
        wd_bf[...] = wd_ref[0, 0].astype(jnp.bfloat16)

    def tile_body(g, carry):
        slot = g % 2
        in_copy(g, slot).wait()

        @pl.when(g + 1 < n_used)
        def _():
            in_copy(g + 1, 1 - slot).start()

        @pl.when(g >= 2)
        def _():
            out_copy(g - 2, slot).wait()

        cols = []
        for c in range(XS_CHUNKS):
            w = xbuf[slot, pl.ds(c, tm, stride=XS_CHUNKS), :]
            cols.append(lax.bitcast_convert_type(w << 16, jnp.float32).astype(jnp.bfloat16))
            cols.append(lax.bitcast_convert_type(w & jnp.uint32(0xFFFF0000), jnp.float32
                                                 ).astype(jnp.bfloat16))
        x = jnp.concatenate(cols, axis=-1)
        gu = jnp.dot(x, wgu_bf[...], preferred_element_type=jnp.float32)
        g_act = gu[:, :EXPERT_FF]
        u = gu[:, EXPERT_FF:]
        act = (g_act / (1.0 + jnp.exp(-g_act))) * u
        y = jnp.dot(act.astype(jnp.bfloat16), wd_bf[...], preferred_element_type=jnp.float32)
        for c in range(ROW_CHUNKS):
            ybuf[slot, pl.ds(c, tm, stride=ROW_CHUNKS), :] = y[:, c * LANES:(c + 1) * LANES]
        out_copy(g, slot).start()
        return carry

    lax.fori_loop(t0, t0 + nt, tile_body, 0)

    @pl.when(e == N_EXPERTS - 1)
    def _():
        @pl.when(n_used >= 2)
        def _():
            out_copy(n_used - 2, n_used % 2).wait()

        out_copy(n_used - 1, (n_used - 1) % 2).wait()

        n_blk = ys_hbm.shape[0] // tile_rows
        ybuf[0] = jnp.zeros(ybuf.shape[1:], ybuf.dtype)
        lax.fori_loop(n_used, n_blk, lambda b, c: (out_copy(b, 0).start(), c)[1], 0)
        lax.fori_loop(n_used, n_blk, lambda b, c: (out_copy(b, 0).wait(), c)[1], 0)


def _expert_call(layer, tstart, ntile, xs, wgu, wd, capacity):
    tile_rows = TM_EXPERT * ROW_CHUNKS

    def w_map(e, ts, nt):
        return (layer, e, 0, 0)

    return pl.pallas_call(
        _expert_kernel,
        grid_spec=pltpu.PrefetchScalarGridSpec(
            num_scalar_prefetch=2,
            grid=(N_EXPERTS,),
            in_specs=[
                pl.BlockSpec(memory_space=pl.ANY),
                pl.BlockSpec((1, 1, D_MODEL, 2 * EXPERT_FF), w_map),
                pl.BlockSpec((1, 1, EXPERT_FF, D_MODEL), w_map),
            ],
            out_specs=pl.BlockSpec(memory_space=pl.ANY),
            scratch_shapes=[pltpu.VMEM((2, TM_EXPERT * XS_CHUNKS, LANES), jnp.uint32),
                            pltpu.VMEM((2, tile_rows, LANES), jnp.float32),
                            pltpu.VMEM((D_MODEL, 2 * EXPERT_FF), jnp.bfloat16),
                            pltpu.VMEM((EXPERT_FF, D_MODEL), jnp.bfloat16),
                            pltpu.SemaphoreType.DMA((2,)),
                            pltpu.SemaphoreType.DMA((2,))],
        ),
        out_shape=jax.ShapeDtypeStruct((capacity * ROW_CHUNKS, LANES), jnp.float32),
        compiler_params=_params(("arbitrary",)),
        name="moe_experts",
    )(tstart, ntile, xs, wgu, wd)


def _combine_kernel(final_norm, d0_ref, d1_ref, d0n_ref, d1n_ref, ys_hbm, h_ref, gate_ref, gfin_ref,
                    o_ref, y0_ref, y1_ref, sem):
    tm = TM_COMBINE
    i = pl.program_id(0)
    slot = i % 2

    def issue_tile(d0r, d1r, s):
        def issue(t, c):
            dst = t * ROW_CHUNKS
            _row_copy(ys_hbm, d0r[t], y0_ref.at[s], dst, sem.at[s], ROW_CHUNKS).start(priority=0)
            _row_copy(ys_hbm, d1r[t], y1_ref.at[s], dst, sem.at[s], ROW_CHUNKS).start(priority=1)
            return c
        lax.fori_loop(0, tm, issue, 0, unroll=DMA_UNROLL)

    @pl.when(i == 0)
    def _():
        issue_tile(d0_ref, d1_ref, 0)

    @pl.when(i + 1 < pl.num_programs(0))
    def _():
        issue_tile(d0n_ref, d1n_ref, 1 - slot)

    for buf in (y0_ref, y1_ref):
        pltpu.make_async_copy(ys_hbm.at[pl.ds(0, tm * ROW_CHUNKS)], buf.at[slot], sem.at[slot]).wait()

    g0 = gate_ref[:, 0:1]
    g1 = gate_ref[:, 1:2]
    cols = []
    for c in range(ROW_CHUNKS):
        y0 = y0_ref[slot, pl.ds(c, tm, stride=ROW_CHUNKS), :]
        y1 = y1_ref[slot, pl.ds(c, tm, stride=ROW_CHUNKS), :]
        cols.append(h_ref[:, c * LANES:(c + 1) * LANES] + (y0 * g0 + y1 * g1))
    h = jnp.concatenate(cols, axis=-1)
    if final_norm:
        h = _rms(h, gfin_ref[...])
    o_ref[...] = h


def _combine_call(d0, d1, ys, h, gates, gfin, final_norm):
    T = h.shape[0]
    tm = TM_COMBINE
    last = T // tm - 1
    return pl.pallas_call(
        functools.partial(_combine_kernel, final_norm),
        grid=(T // tm,),
        in_specs=[
            pl.BlockSpec((tm,), lambda i: (i,), memory_space=pltpu.SMEM),
            pl.BlockSpec((tm,), lambda i: (i,), memory_space=pltpu.SMEM),
            pl.BlockSpec((tm,), lambda i: (jnp.minimum(i + 1, last),), memory_space=pltpu.SMEM),
            pl.BlockSpec((tm,), lambda i: (jnp.minimum(i + 1, last),), memory_space=pltpu.SMEM),
            pl.BlockSpec(memory_space=pl.ANY),
            pl.BlockSpec((tm, D_MODEL), lambda i: (i, 0)),
            pl.BlockSpec((tm, 2), lambda i: (i, 0)),
            pl.BlockSpec((1, D_MODEL), lambda i: (0, 0)),
        ],
        out_specs=pl.BlockSpec((tm, D_MODEL), lambda i: (i, 0)),
        out_shape=jax.ShapeDtypeStruct((T, D_MODEL), jnp.float32),
        scratch_shapes=[pltpu.VMEM((2, tm * ROW_CHUNKS, LANES), jnp.float32),
                        pltpu.VMEM((2, tm * ROW_CHUNKS, LANES), jnp.float32),
                        pltpu.SemaphoreType.DMA((2,))],
        compiler_params=_params(("arbitrary",)),
        name="moe_combine",
    )(d0, d1, d0, d1, ys, h, gates, gfin)


def _group_heads_last(w):
    lead = w.shape[:-1]
    w = w.reshape(lead + (N_KV_HEADS, GROUP, HEAD_DIM))
    w = jnp.swapaxes(w, -3, -2)
    return w.reshape(lead + (N_Q_HEADS * HEAD_DIM,))


def _rel_bucket_table():
    qi = np.arange(ATTN_BLOCK)[:, None]
    kj = np.arange(2 * ATTN_BLOCK)[None, :]
    dist = np.maximum(qi + ATTN_BLOCK - kj, 0)
    max_exact = N_REL_BUCKETS // 2
    d = np.maximum(dist, max_exact).astype(np.float32)
    large = max_exact + (np.log(d / np.float32(max_exact)) / np.float32(math.log(REL_MAX_DISTANCE / max_exact))
                         * np.float32(N_REL_BUCKETS - max_exact)).astype(np.int32)
    large = np.minimum(large, N_REL_BUCKETS - 1)
    return np.where(dist < max_exact, dist, large).astype(np.int32)


def _rel_bias_blocks(rel_bias, sinks):
    onehot = (jnp.asarray(_rel_bucket_table())[..., None] == jnp.arange(N_REL_BUCKETS)).astype(jnp.float32)
    bias = jnp.einsum("qkb,bh->hqk", onehot, rel_bias.astype(jnp.float32),
                      precision=lax.Precision.HIGHEST)
    sink_col = jnp.broadcast_to(sinks.astype(jnp.float32)[:, None, None], (N_Q_HEADS, ATTN_BLOCK, 1))
    return jnp.concatenate([sink_col, bias[:, :, 1:]], axis=2)


def _router_weights(w_group, b_group, w_expert, b_expert):
    zw = jnp.zeros((D_MODEL, SUBLANES - N_GROUPS), jnp.float32)
    zw2 = jnp.zeros((D_MODEL, LANES - SUBLANES - N_EXPERTS), jnp.float32)
    wr = jnp.concatenate([w_group, zw, w_expert, zw2], axis=1)
    wh = wr.astype(jnp.bfloat16)
    wl = (wr - wh.astype(jnp.float32)).astype(jnp.bfloat16)
    wr = jnp.concatenate([wh, wl], axis=1)
    brt = jnp.concatenate([b_group, jnp.zeros((SUBLANES - N_GROUPS,), jnp.float32), b_expert])[:, None]
    return wr, brt


def _moe(layer, h, xrows, ri, rf, cnt, w_gate_up, w_down, gfin, final_norm):
    T = h.shape[0]
    tm = TM_EXPERT
    capacity = -(-(2 * T) // tm) * tm + N_EXPERTS * tm
    counts = cnt[:, 0].astype(jnp.int32)
    ntile = (counts + tm - 1) // tm
    tend = jnp.cumsum(ntile)
    pstart = (tend - ntile) * tm
    n_used = tend[-1:]
    padrow = jnp.concatenate([jnp.where(ntile > 0, (tend - 1) * (tm * XS_CHUNKS), -1), n_used]
                             ).astype(jnp.int32)

    dst = _slot_rows_call(ri, pstart)
    xs = _dispatch_call(dst[0], dst[1], padrow, xrows, capacity)
    ys = _expert_call(layer, (tend - ntile).astype(jnp.int32), ntile.astype(jnp.int32), xs,
                      w_gate_up, w_down, capacity)
    gates = rf[:2].T
    return _combine_call(dst[2], dst[3], ys, h, gates, gfin, final_norm)


def kernel(x, norm_mix, norm_ffn, final_norm, rel_bias, attn_w_qkv, attn_b_qkv, attn_w_o, attn_b_o,
           attn_sinks, conv_w_in, conv_w, conv_w_out, moe_w_group, moe_b_group, moe_w_expert,
           moe_b_expert, moe_w_gate_up, moe_w_down):
    B, S, D = x.shape
    T = B * S
    x2 = x.reshape(T, D)
    nq = N_Q_HEADS * HEAD_DIM
    scale = HEAD_DIM ** -0.5

    wqkv = attn_w_qkv[0]
    bqkv = attn_b_qkv[0]
    w_all = jnp.concatenate([_group_heads_last(wqkv[:, :nq]) * scale, wqkv[:, nq:]],
                            axis=1).astype(jnp.bfloat16)
    b_all = jnp.concatenate([_group_heads_last(bqkv[:nq]) * scale, bqkv[nq:]])[None, :]
    q, kt, v = _qkv_call(x2, norm_mix[0][None, :], w_all, b_all)
    a = _attn_call(q, kt, v, _rel_bias_blocks(rel_bias, attn_sinks[0]), B, S)
    wr, brt = _router_weights(moe_w_group[0], moe_b_group[0], moe_w_expert[0], moe_b_expert[0])
    w_o = _group_heads_last(attn_w_o[0].T).T.astype(jnp.bfloat16)
    h, xrows, ri, rf, cnt = _attn_out_call(
        a, w_o, attn_b_o[0][None, :], x2, norm_ffn[0][None, :], wr, brt)
    h = _moe(0, h, xrows, ri, rf, cnt, moe_w_gate_up, moe_w_down, final_norm[None, :], False)

    wr, brt = _router_weights(moe_w_group[1], moe_b_group[1], moe_w_expert[1], moe_b_expert[1])
    h, xrows, ri, rf, cnt = _conv_mixer_call(
        h, norm_mix[1][None, :], conv_w_in[0].astype(jnp.bfloat16), conv_w[0],
        conv_w_out[0].astype(jnp.bfloat16), norm_ffn[1][None, :], wr, brt, S)
    out = _moe(1, h, xrows, ri, rf, cnt, moe_w_gate_up, moe_w_down, final_norm[None, :], True)
    return out.reshape(B, S, D)
```

```python
import functools
import math

import numpy as np
import jax
import jax.numpy as jnp
from jax import lax
from jax.experimental import pallas as pl
from jax.experimental.pallas import tpu as pltpu

D_MODEL = 1024
N_Q_HEADS = 16
N_KV_HEADS = 2
HEAD_DIM = 64
GROUP = N_Q_HEADS // N_KV_HEADS
WINDOW = 128
ATTN_BLOCK = 128
N_REL_BUCKETS = 32
REL_MAX_DISTANCE = 128
CONV_WIDTH = 3
N_GROUPS = 4
EXPERTS_PER_GROUP = 8
N_EXPERTS = N_GROUPS * EXPERTS_PER_GROUP
EXPERT_FF = 512
RMS_EPS = 1e-5

LANES = 128
SUBLANES = 8
ROW_CHUNKS = D_MODEL // LANES
XS_CHUNKS = ROW_CHUNKS // 2
VMEM_LIMIT = 56 * 1024 * 1024

TM_DENSE = 1024
TQ = 1024
TM_MIX1 = 512
TM_EXPERT = 512
TM_DISPATCH = 1024
TM_COMBINE = 512
DMA_UNROLL = 8
TM_SLOT_ROWS = 4096
ROUTER_ROWS = 40


def _rms(x, g):
    return x * lax.rsqrt(jnp.mean(x * x, axis=-1, keepdims=True) + RMS_EPS) * g


def _params(sem):
    return pltpu.CompilerParams(dimension_semantics=sem, vmem_limit_bytes=VMEM_LIMIT)


def _qkv_kernel(x_ref, g_ref, w_ref, b_ref, q_ref, kt_ref, v_ref):
    xn = _rms(x_ref[...], g_ref[...]).astype(jnp.bfloat16)
    out = jnp.dot(xn, w_ref[...], preferred_element_type=jnp.float32) + b_ref[...]
    nq = N_Q_HEADS * HEAD_DIM
    nk = N_KV_HEADS * HEAD_DIM
    q_ref[...] = out[:, :nq].astype(jnp.bfloat16)
    kt_ref[...] = out[:, nq:nq + nk].T.astype(jnp.bfloat16)
    v_ref[...] = out[:, nq + nk:].astype(jnp.bfloat16)


def _qkv_call(x2, g, w, b):
    T = x2.shape[0]
    nq = N_Q_HEADS * HEAD_DIM
    nk = N_KV_HEADS * HEAD_DIM
    tm = TM_DENSE
    return pl.pallas_call(
        _qkv_kernel,
        grid=(T // tm,),
        in_specs=[
            pl.BlockSpec((tm, D_MODEL), lambda i: (i, 0)),
            pl.BlockSpec((1, D_MODEL), lambda i: (0, 0)),
            pl.BlockSpec((D_MODEL, nq + 2 * nk), lambda i: (0, 0)),
            pl.BlockSpec((1, nq + 2 * nk), lambda i: (0, 0)),
        ],
        out_specs=[
            pl.BlockSpec((tm, nq), lambda i: (i, 0)),
            pl.BlockSpec((nk, tm), lambda i: (0, i)),
            pl.BlockSpec((tm, nk), lambda i: (i, 0)),
        ],
        out_shape=[
            jax.ShapeDtypeStruct((T, nq), jnp.bfloat16),
            jax.ShapeDtypeStruct((nk, T), jnp.bfloat16),
            jax.ShapeDtypeStruct((T, nk), jnp.bfloat16),
        ],
        compiler_params=_params(("parallel",)),
        name="qkv_proj",
    )(x2, g, w, b)


def _attn_kernel(q_ref, kt_ref, ktp_ref, v_ref, vp_ref, bias_ref, o_ref):
    first_tile = pl.program_id(1) == 0
    nblk = TQ // ATTN_BLOCK
    qi_io = lax.broadcasted_iota(jnp.int32, (ATTN_BLOCK, 2 * ATTN_BLOCK), 0)
    kj_io = lax.broadcasted_iota(jnp.int32, (ATTN_BLOCK, 2 * ATTN_BLOCK), 1)
    dist = qi_io + ATTN_BLOCK - kj_io
    band = (dist >= 0) & (dist < WINDOW)
    sink_col = kj_io == 0
    kt_keep = lax.broadcasted_iota(jnp.int32, (N_KV_HEADS * HEAD_DIM, 2 * ATTN_BLOCK), 1) > 0
    v_keep = lax.broadcasted_iota(jnp.int32, (2 * ATTN_BLOCK, N_KV_HEADS * HEAD_DIM), 0) > 0
    low_half = lax.broadcasted_iota(jnp.int32, (ATTN_BLOCK, LANES), 1) < HEAD_DIM
    for qi in range(nblk):
        r0 = qi * ATTN_BLOCK
        if qi == 0:
            kt_blk = jnp.concatenate([ktp_ref[...], kt_ref[:, 0:ATTN_BLOCK]], axis=1)
            v_blk = jnp.concatenate([vp_ref[...], v_ref[0:ATTN_BLOCK, :]], axis=0)
            mask = (band & (jnp.logical_not(first_tile) | (kj_io >= ATTN_BLOCK))) | sink_col
        else:
            kt_blk = kt_ref[:, r0 - ATTN_BLOCK:r0 + ATTN_BLOCK]
            v_blk = v_ref[r0 - ATTN_BLOCK:r0 + ATTN_BLOCK, :]
            mask = band | sink_col
        kt_blk = jnp.where(kt_keep, kt_blk, jnp.zeros_like(kt_blk))
        v_blk = jnp.where(v_keep, v_blk, jnp.zeros_like(v_blk))
        for m in range(GROUP):
            qg = q_ref[r0:r0 + ATTN_BLOCK, m * LANES:(m + 1) * LANES]
            halves = []
            for half in range(N_KV_HEADS):
                h = m + GROUP * half
                keep = low_half if half == 0 else jnp.logical_not(low_half)
                qpad = jnp.where(keep, qg, jnp.zeros_like(qg))
                s = jnp.dot(qpad, kt_blk, preferred_element_type=jnp.float32)
                s = jnp.where(mask, s + bias_ref[h], -1e30)
                mx = jnp.max(s, axis=-1, keepdims=True)
                p = jnp.exp(s - mx)
                den = jnp.sum(p, axis=-1, keepdims=True)
                pv = jnp.dot(p.astype(jnp.bfloat16), v_blk, preferred_element_type=jnp.float32)
                halves.append(pv * (1.0 / den))
            og = jnp.where(low_half, halves[0], halves[1])
            o_ref[r0:r0 + ATTN_BLOCK, m * LANES:(m + 1) * LANES] = og.astype(jnp.bfloat16)


def _attn_call(q, kt, v, bias, batch, seq):
    T = q.shape[0]
    nq = N_Q_HEADS * HEAD_DIM
    nk = N_KV_HEADS * HEAD_DIM
    tiles = seq // TQ
    per = TQ // ATTN_BLOCK

    def cur(b, j):
        return b * tiles + j

    def prev(b, j):
        return jnp.maximum((b * tiles + j) * per - 1, b * tiles * per)

    return pl.pallas_call(
        _attn_kernel,
        grid=(batch, tiles),
        in_specs=[
            pl.BlockSpec((TQ, nq), lambda b, j: (cur(b, j), 0)),
            pl.BlockSpec((nk, TQ), lambda b, j: (0, cur(b, j))),
            pl.BlockSpec((nk, ATTN_BLOCK), lambda b, j: (0, prev(b, j))),
            pl.BlockSpec((TQ, nk), lambda b, j: (cur(b, j), 0)),
            pl.BlockSpec((ATTN_BLOCK, nk), lambda b, j: (prev(b, j), 0)),
            pl.BlockSpec((N_Q_HEADS, ATTN_BLOCK, 2 * ATTN_BLOCK), lambda b, j: (0, 0, 0)),
        ],
        out_specs=pl.BlockSpec((TQ, nq), lambda b, j: (cur(b, j), 0)),
        out_shape=jax.ShapeDtypeStruct((T, nq), jnp.bfloat16),
        compiler_params=_params(("parallel", "parallel")),
        name="swa_attention",
    )(q, kt, kt, v, v, bias)


def _router_epilogue(h, gffn_ref, wr_ref, brt_ref, carry_ref,
                     h_out_ref, xrows_ref, ri_ref, rf_ref, cnt_ref):
    tm = h.shape[0]
    h_out_ref[...] = h
    xn = _rms(h, gffn_ref[...])
    xh = xn.astype(jnp.bfloat16)
    xbits = lax.bitcast_convert_type(xh.astype(jnp.float32), jnp.uint32)
    for c in range(XS_CHUNKS):
        lo = xbits[:, (2 * c) * LANES:(2 * c + 1) * LANES] >> 16
        hi = xbits[:, (2 * c + 1) * LANES:(2 * c + 2) * LANES]
        xrows_ref[pl.ds(c, tm, stride=XS_CHUNKS), :] = lo | hi

    xl = (xn - xh.astype(jnp.float32)).astype(jnp.bfloat16)
    wcat = wr_ref[...]
    prod = jnp.dot(xh, wcat, preferred_element_type=jnp.float32)
    logits = (prod[:, :LANES] + prod[:, LANES:]) + jnp.dot(
        xl, wcat[:, :LANES], preferred_element_type=jnp.float32)
    lt = logits.T[:ROUTER_ROWS, :] + brt_ref[...]

    gl = [lt[g:g + 1, :] for g in range(N_GROUPS)]
    gmax = functools.reduce(jnp.maximum, gl)
    gexp = [jnp.exp(x - gmax) for x in gl]
    gsum = functools.reduce(lambda a, b: a + b, gexp)
    gprob = [x / gsum for x in gexp]
    g_prob = functools.reduce(jnp.maximum, gprob)
    g_idx = jnp.full(g_prob.shape, N_GROUPS - 1, jnp.int32)
    for g in range(N_GROUPS - 2, -1, -1):
        g_idx = jnp.where(gprob[g] == g_prob, g, g_idx)

    el = []
    for j in range(EXPERTS_PER_GROUP):
        x = lt[SUBLANES + j:SUBLANES + j + 1, :]
        for g in range(1, N_GROUPS):
            r = SUBLANES + g * EXPERTS_PER_GROUP + j
            x = jnp.where(g_idx == g, lt[r:r + 1, :], x)
        el.append(x)
    emax = functools.reduce(jnp.maximum, el)
    eexp = [jnp.exp(x - emax) for x in el]
    esum = functools.reduce(lambda a, b: a + b, eexp)
    eprob = [x / esum for x in eexp]
    p1 = functools.reduce(jnp.maximum, eprob)
    i1 = jnp.full(p1.shape, EXPERTS_PER_GROUP - 1, jnp.int32)
    for j in range(EXPERTS_PER_GROUP - 2, -1, -1):
        i1 = jnp.where(eprob[j] == p1, j, i1)
    rest = [jnp.where(i1 == j, -1.0, eprob[j]) for j in range(EXPERTS_PER_GROUP)]
    p2 = functools.reduce(jnp.maximum, rest)
    i2 = jnp.full(p2.shape, EXPERTS_PER_GROUP - 1, jnp.int32)
    for j in range(EXPERTS_PER_GROUP - 2, -1, -1):
        i2 = jnp.where(rest[j] == p2, j, i2)
    psum = p1 + p2
    gate0 = g_prob * (p1 / psum)
    gate1 = g_prob * (p2 / psum)
    e0 = g_idx * EXPERTS_PER_GROUP + i1
    e1 = g_idx * EXPERTS_PER_GROUP + i2

    eio = lax.broadcasted_iota(jnp.int32, (N_EXPERTS, tm), 0)
    oh0 = (eio == e0).astype(jnp.float32)
    oh1 = (eio == e1).astype(jnp.float32)
    both = oh0 + oh1
    tr = lax.broadcasted_iota(jnp.int32, (tm, tm), 0)
    tc = lax.broadcasted_iota(jnp.int32, (tm, tm), 1)
    upper = (tr < tc).astype(jnp.bfloat16)
    before = jnp.dot(both.astype(jnp.bfloat16), upper, preferred_element_type=jnp.float32)
    before = before + carry_ref[...]
    rank0 = jnp.sum(oh0 * before, axis=0, keepdims=True)
    rank1 = jnp.sum(oh1 * (before + oh0), axis=0, keepdims=True)
    carry_ref[...] = carry_ref[...] + jnp.sum(both, axis=1, keepdims=True)
    cnt_ref[...] = jnp.broadcast_to(carry_ref[...], cnt_ref.shape)

    zi = jnp.zeros_like(e0)
    ri_ref[...] = jnp.concatenate(
        [e0, e1, rank0.astype(jnp.int32), rank1.astype(jnp.int32), zi, zi, zi, zi], axis=0)
    zf = jnp.zeros_like(gate0)
    rf_ref[...] = jnp.concatenate([gate0, gate1, zf, zf, zf, zf, zf, zf], axis=0)


def _epilogue_out_specs(tm):
    return [
        pl.BlockSpec((tm, D_MODEL), lambda i: (i, 0)),
        pl.BlockSpec((tm * XS_CHUNKS, LANES), lambda i: (i, 0)),
        pl.BlockSpec((SUBLANES, tm), lambda i: (0, i)),
        pl.BlockSpec((SUBLANES, tm), lambda i: (0, i)),
        pl.BlockSpec((N_EXPERTS, LANES), lambda i: (0, 0)),
    ]


def _epilogue_out_shapes(T):
    return [
        jax.ShapeDtypeStruct((T, D_MODEL), jnp.float32),
        jax.ShapeDtypeStruct((T * XS_CHUNKS, LANES), jnp.uint32),
        jax.ShapeDtypeStruct((SUBLANES, T), jnp.int32),
        jax.ShapeDtypeStruct((SUBLANES, T), jnp.float32),
        jax.ShapeDtypeStruct((N_EXPERTS, LANES), jnp.float32),
    ]


def _epilogue_in_specs():
    return [
        pl.BlockSpec((1, D_MODEL), lambda i: (0, 0)),
        pl.BlockSpec((D_MODEL, 2 * LANES), lambda i: (0, 0)),
        pl.BlockSpec((ROUTER_ROWS, 1), lambda i: (0, 0)),
    ]


def _attn_out_kernel(a_ref, w_ref, b_ref, res_ref, gffn_ref, wr_ref, brt_ref,
                     h_out_ref, xrows_ref, ri_ref, rf_ref, cnt_ref, carry_ref):
    @pl.when(pl.program_id(0) == 0)
    def _():
        carry_ref[...] = jnp.zeros_like(carry_ref)

    mix = jnp.dot(a_ref[...], w_ref[...], preferred_element_type=jnp.float32) + b_ref[...]
    h = res_ref[...] + mix
    _router_epilogue(h, gffn_ref, wr_ref, brt_ref, carry_ref,
                     h_out_ref, xrows_ref, ri_ref, rf_ref, cnt_ref)


def _attn_out_call(a, w, b, res, gffn, wr, brt):
    T = a.shape[0]
    tm = TM_DENSE
    return pl.pallas_call(
        _attn_out_kernel,
        grid=(T // tm,),
        in_specs=[
            pl.BlockSpec((tm, D_MODEL), lambda i: (i, 0)),
            pl.BlockSpec((D_MODEL, D_MODEL), lambda i: (0, 0)),
            pl.BlockSpec((1, D_MODEL), lambda i: (0, 0)),
            pl.BlockSpec((tm, D_MODEL), lambda i: (i, 0)),
        ] + _epilogue_in_specs(),
        out_specs=_epilogue_out_specs(tm),
        out_shape=_epilogue_out_shapes(T),
        scratch_shapes=[pltpu.VMEM((N_EXPERTS, 1), jnp.float32)],
        compiler_params=_params(("arbitrary",)),
        name="attn_out_router",
    )(a, w, b, res, gffn, wr, brt)


def _conv_mixer_kernel(tiles_per_seq, x_ref, g_ref, win_ref, cw_ref, wout_ref,
                       gffn_ref, wr_ref, brt_ref,
                       h_out_ref, xrows_ref, ri_ref, rf_ref, cnt_ref, carry_ref, tail_ref):
    i = pl.program_id(0)
    tm = x_ref.shape[0]

    @pl.when(i == 0)
    def _():
        carry_ref[...] = jnp.zeros_like(carry_ref)

    @pl.when(i % tiles_per_seq == 0)
    def _():
        tail_ref[...] = jnp.zeros_like(tail_ref)

    x = x_ref[...]
    xn = _rms(x, g_ref[...]).astype(jnp.bfloat16)
    bcu = jnp.dot(xn, win_ref[...], preferred_element_type=jnp.float32)
    b_gate = bcu[:, :D_MODEL]
    z = bcu[:, D_MODEL:2 * D_MODEL] * bcu[:, 2 * D_MODEL:]
    row = lax.broadcasted_iota(jnp.int32, (tm, D_MODEL), 0)
    tail = tail_ref[...]
    t1 = tail[SUBLANES - 1:SUBLANES, :]
    t2 = tail[SUBLANES - 2:SUBLANES - 1, :]
    z1 = jnp.where(row == 0, t1, pltpu.roll(z, 1, axis=0))
    z2 = jnp.where(row == 0, t2, jnp.where(row == 1, t1, pltpu.roll(z, 2, axis=0)))
    tail_ref[...] = z[tm - SUBLANES:, :]
    conv = z2 * cw_ref[0:1, :]
    conv = conv + z1 * cw_ref[1:2, :]
    conv = conv + z * cw_ref[2:3, :]
    gated = (b_gate * conv).astype(jnp.bfloat16)
    h = x + jnp.dot(gated, wout_ref[...], preferred_element_type=jnp.float32)
    _router_epilogue(h, gffn_ref, wr_ref, brt_ref, carry_ref,
                     h_out_ref, xrows_ref, ri_ref, rf_ref, cnt_ref)


def _conv_mixer_call(x2, g, win, cw, wout, gffn, wr, brt, seq):
    T = x2.shape[0]
    tm = TM_MIX1
    return pl.pallas_call(
        functools.partial(_conv_mixer_kernel, seq // tm),
        grid=(T // tm,),
        in_specs=[
            pl.BlockSpec((tm, D_MODEL), lambda i: (i, 0)),
            pl.BlockSpec((1, D_MODEL), lambda i: (0, 0)),
            pl.BlockSpec((D_MODEL, 3 * D_MODEL), lambda i: (0, 0)),
            pl.BlockSpec((CONV_WIDTH, D_MODEL), lambda i: (0, 0)),
            pl.BlockSpec((D_MODEL, D_MODEL), lambda i: (0, 0)),
        ] + _epilogue_in_specs(),
        out_specs=_epilogue_out_specs(tm),
        out_shape=_epilogue_out_shapes(T),
        scratch_shapes=[pltpu.VMEM((N_EXPERTS, 1), jnp.float32),
                        pltpu.VMEM((SUBLANES, D_MODEL), jnp.float32)],
        compiler_params=_params(("arbitrary",)),
        name="conv_mixer_router",
    )(x2, g, win, cw, wout, gffn, wr, brt)


def _row_copy(src_hbm, src_row, dst_hbm, dst_row, sem, chunks):
    return pltpu.make_async_copy(
        src_hbm.at[pl.ds(pl.multiple_of(src_row, chunks), chunks)],
        dst_hbm.at[pl.ds(pl.multiple_of(dst_row, chunks), chunks)], sem)


def _slot_rows_kernel(ri_ref, pstart_ref, o_ref):
    tb = ri_ref.shape[1]
    eio = lax.broadcasted_iota(jnp.int32, (N_EXPERTS, tb), 0)
    pstart = pstart_ref[...]
    slots = []
    for k in range(2):
        start = jnp.sum(jnp.where(eio == ri_ref[k:k + 1, :], pstart, 0), axis=0, keepdims=True)
        slots.append(start + ri_ref[2 + k:3 + k, :])
    zi = jnp.zeros_like(slots[0])
    o_ref[...] = jnp.concatenate(
        [s * XS_CHUNKS for s in slots] + [s * ROW_CHUNKS for s in slots] + [zi] * (SUBLANES - 4),
        axis=0)


def _slot_rows_call(ri, pstart):
    T = ri.shape[1]
    tb = TM_SLOT_ROWS
    return pl.pallas_call(
        _slot_rows_kernel,
        grid=(T // tb,),
        in_specs=[pl.BlockSpec((SUBLANES, tb), lambda i: (0, i)),
                  pl.BlockSpec((N_EXPERTS, 1), lambda i: (0, 0))],
        out_specs=pl.BlockSpec((SUBLANES, tb), lambda i: (0, i)),
        out_shape=jax.ShapeDtypeStruct((SUBLANES, T), jnp.int32),
        compiler_params=_params(("parallel",)),
        name="moe_slot_rows",
    )(ri, pstart[:, None])


def _dispatch_kernel(d0_ref, d1_ref, padrow_ref, x_ref, xs_hbm, zero_ref, sem, zsem):
    i = pl.program_id(0)
    tile_rows = TM_EXPERT * XS_CHUNKS

    @pl.when(i == 0)
    def _():
        zero_ref[...] = jnp.zeros_like(zero_ref)

        def zcopy(e):
            return pltpu.make_async_copy(
                zero_ref,
                xs_hbm.at[pl.ds(pl.multiple_of(padrow_ref[e], tile_rows), tile_rows)], zsem)

        def zstart(e, c):
            @pl.when(padrow_ref[e] >= 0)
            def _():
                zcopy(e).start()
            return c

        def zwait(e, c):
            @pl.when(padrow_ref[e] >= 0)
            def _():
                zcopy(e).wait()
            return c

        lax.fori_loop(0, N_EXPERTS, zstart, 0)
        lax.fori_loop(0, N_EXPERTS, zwait, 0)

        def tcopy(b):
            return pltpu.make_async_copy(
                zero_ref, xs_hbm.at[pl.ds(pl.multiple_of(b * tile_rows, tile_rows), tile_rows)], zsem)

        n_used = padrow_ref[N_EXPERTS]
        n_blk = xs_hbm.shape[0] // tile_rows
        lax.fori_loop(n_used, n_blk, lambda b, c: (tcopy(b).start(), c)[1], 0)
        lax.fori_loop(n_used, n_blk, lambda b, c: (tcopy(b).wait(), c)[1], 0)

    def issue(t, c):
        src = t * XS_CHUNKS
        _row_copy(x_ref, src, xs_hbm, d0_ref[t], sem, XS_CHUNKS).start(priority=0)
        _row_copy(x_ref, src, xs_hbm, d1_ref[t], sem, XS_CHUNKS).start(priority=1)
        return c

    lax.fori_loop(0, TM_DISPATCH, issue, 0, unroll=DMA_UNROLL)
    for _ in range(2):
        pltpu.make_async_copy(x_ref, xs_hbm.at[pl.ds(0, TM_DISPATCH * XS_CHUNKS)], sem).wait()


def _dispatch_call(d0, d1, padrow, xrows, capacity):
    T = d0.shape[0]
    return pl.pallas_call(
        _dispatch_kernel,
        grid=(T // TM_DISPATCH,),
        in_specs=[
            pl.BlockSpec((TM_DISPATCH,), lambda i: (i,), memory_space=pltpu.SMEM),
            pl.BlockSpec((TM_DISPATCH,), lambda i: (i,), memory_space=pltpu.SMEM),
            pl.BlockSpec(memory_space=pltpu.SMEM),
            pl.BlockSpec((TM_DISPATCH * XS_CHUNKS, LANES), lambda i: (i, 0)),
        ],
        out_specs=pl.BlockSpec(memory_space=pl.ANY),
        out_shape=jax.ShapeDtypeStruct((capacity * XS_CHUNKS, LANES), jnp.uint32),
        scratch_shapes=[pltpu.VMEM((TM_EXPERT * XS_CHUNKS, LANES), jnp.uint32),
                        pltpu.SemaphoreType.DMA, pltpu.SemaphoreType.DMA],
        compiler_params=_params(("arbitrary",)),
        name="moe_dispatch",
    )(d0, d1, padrow, xrows)


def _expert_kernel(tstart_ref, ntile_ref, xs_hbm, wgu_ref, wd_ref, ys_hbm,
                   xbuf, ybuf, wgu_bf, wd_bf, sem_in, sem_out):
    e = pl.program_id(0)
    tm = TM_EXPERT
    tile_rows = tm * ROW_CHUNKS
    t0 = tstart_ref[e]
    nt = ntile_ref[e]
    n_used = tstart_ref[N_EXPERTS - 1] + ntile_ref[N_EXPERTS - 1]

    def tile_at(ref, g, rows):
        return ref.at[pl.ds(pl.multiple_of(g * rows, rows), rows)]

    def in_copy(g, slot):
        return pltpu.make_async_copy(tile_at(xs_hbm, g, tm * XS_CHUNKS), xbuf.at[slot], sem_in.at[slot])

    def out_copy(g, slot):
        return pltpu.make_async_copy(ybuf.at[slot], tile_at(ys_hbm, g, tile_rows), sem_out.at[slot])

    @pl.when(e == 0)
    def _():
        in_copy(0, 0).start()

    @pl.when(nt > 0)
    def _():
        wgu_bf[...] = wgu_ref[0, 0].astype(jnp.bfloat16)
        wd_bf[...] = wd_ref[0, 0].astype(jnp.bfloat16)

    def tile_body(g, carry):
        slot = g % 2
        in_copy(g, slot).wait()

        @pl.when(g + 1 < n_used)
        def _():
            in_copy(g + 1, 1 - slot).start(priority=1)

        @pl.when(g >= 2)
        def _():
            out_copy(g - 2, slot).wait()

        cols = []
        for c in range(XS_CHUNKS):
            w = xbuf[slot, pl.ds(c, tm, stride=XS_CHUNKS), :]
            cols.append(lax.bitcast_convert_type(w << 16, jnp.float32).astype(jnp.bfloat16))
            cols.append(lax.bitcast_convert_type(w & jnp.uint32(0xFFFF0000), jnp.float32
                                                 ).astype(jnp.bfloat16))
        x = jnp.concatenate(cols, axis=-1)
        gu = jnp.dot(x, wgu_bf[...], preferred_element_type=jnp.float32)
        g_act = gu[:, :EXPERT_FF]
        u = gu[:, EXPERT_FF:]
        act = (g_act / (1.0 + jnp.exp(-g_act))) * u
        y = jnp.dot(act.astype(jnp.bfloat16), wd_bf[...], preferred_element_type=jnp.float32)
        for c in range(ROW_CHUNKS):
            ybuf[slot, pl.ds(c, tm, stride=ROW_CHUNKS), :] = y[:, c * LANES:(c + 1) * LANES]
        out_copy(g, slot).start(priority=1)
        return carry

    lax.fori_loop(t0, t0 + nt, tile_body, 0)

    @pl.when(e == N_EXPERTS - 1)
    def _():
        @pl.when(n_used >= 2)
        def _():
            out_copy(n_used - 2, n_used % 2).wait()

        out_copy(n_used - 1, (n_used - 1) % 2).wait()

        n_blk = ys_hbm.shape[0] // tile_rows
        ybuf[0] = jnp.zeros(ybuf.shape[1:], ybuf.dtype)
        lax.fori_loop(n_used, n_blk, lambda b, c: (out_copy(b, 0).start(), c)[1], 0)
        lax.fori_loop(n_used, n_blk, lambda b, c: (out_copy(b, 0).wait(), c)[1], 0)


def _expert_call(layer, tstart, ntile, xs, wgu, wd, capacity):
    tile_rows = TM_EXPERT * ROW_CHUNKS

    def w_map(e, ts, nt):
        return (layer, e, 0, 0)

    return pl.pallas_call(
        _expert_kernel,
        grid_spec=pltpu.PrefetchScalarGridSpec(
            num_scalar_prefetch=2,
            grid=(N_EXPERTS,),
            in_specs=[
                pl.BlockSpec(memory_space=pl.ANY),
                pl.BlockSpec((1, 1, D_MODEL, 2 * EXPERT_FF), w_map),
                pl.BlockSpec((1, 1, EXPERT_FF, D_MODEL), w_map),
            ],
            out_specs=pl.BlockSpec(memory_space=pl.ANY),
            scratch_shapes=[pltpu.VMEM((2, TM_EXPERT * XS_CHUNKS, LANES), jnp.uint32),
                            pltpu.VMEM((2, tile_rows, LANES), jnp.float32),
                            pltpu.VMEM((D_MODEL, 2 * EXPERT_FF), jnp.bfloat16),
                            pltpu.VMEM((EXPERT_FF, D_MODEL), jnp.bfloat16),
                            pltpu.SemaphoreType.DMA((2,)),
                            pltpu.SemaphoreType.DMA((2,))],
        ),
        out_shape=jax.ShapeDtypeStruct((capacity * ROW_CHUNKS, LANES), jnp.float32),
        compiler_params=_params(("arbitrary",)),
        name="moe_experts",
    )(tstart, ntile, xs, wgu, wd)


def _combine_kernel(final_norm, d0_ref, d1_ref, d0n_ref, d1n_ref, ys_hbm, h_ref, gate_ref, gfin_ref,
                    o_ref, y0_ref, y1_ref, sem):
    tm = TM_COMBINE
    i = pl.program_id(0)
    slot = i % 2

    def issue_tile(d0r, d1r, s):
        def issue(t, c):
            dst = t * ROW_CHUNKS
            _row_copy(ys_hbm, d0r[t], y0_ref.at[s], dst, sem.at[s], ROW_CHUNKS).start(priority=0)
            _row_copy(ys_hbm, d1r[t], y1_ref.at[s], dst, sem.at[s], ROW_CHUNKS).start(priority=1)
            return c
        lax.fori_loop(0, tm, issue, 0, unroll=DMA_UNROLL)

    @pl.when(i == 0)
    def _():
        issue_tile(d0_ref, d1_ref, 0)

    @pl.when(i + 1 < pl.num_programs(0))
    def _():
        issue_tile(d0n_ref, d1n_ref, 1 - slot)

    for buf in (y0_ref, y1_ref):
        pltpu.make_async_copy(ys_hbm.at[pl.ds(0, tm * ROW_CHUNKS)], buf.at[slot], sem.at[slot]).wait()

    g0 = gate_ref[:, 0:1]
    g1 = gate_ref[:, 1:2]
    cols = []
    for c in range(ROW_CHUNKS):
        y0 = y0_ref[slot, pl.ds(c, tm, stride=ROW_CHUNKS), :]
        y1 = y1_ref[slot, pl.ds(c, tm, stride=ROW_CHUNKS), :]
        cols.append(h_ref[:, c * LANES:(c + 1) * LANES] + (y0 * g0 + y1 * g1))
    h = jnp.concatenate(cols, axis=-1)
    if final_norm:
        h = _rms(h, gfin_ref[...])
    o_ref[...] = h


def _combine_call(d0, d1, ys, h, gates, gfin, final_norm):
    T = h.shape[0]
    tm = TM_COMBINE
    last = T // tm - 1
    return pl.pallas_call(
        functools.partial(_combine_kernel, final_norm),
        grid=(T // tm,),
        in_specs=[
            pl.BlockSpec((tm,), lambda i: (i,), memory_space=pltpu.SMEM),
            pl.BlockSpec((tm,), lambda i: (i,), memory_space=pltpu.SMEM),
            pl.BlockSpec((tm,), lambda i: (jnp.minimum(i + 1, last),), memory_space=pltpu.SMEM),
            pl.BlockSpec((tm,), lambda i: (jnp.minimum(i + 1, last),), memory_space=pltpu.SMEM),
            pl.BlockSpec(memory_space=pl.ANY),
            pl.BlockSpec((tm, D_MODEL), lambda i: (i, 0)),
            pl.BlockSpec((tm, 2), lambda i: (i, 0)),
            pl.BlockSpec((1, D_MODEL), lambda i: (0, 0)),
        ],
        out_specs=pl.BlockSpec((tm, D_MODEL), lambda i: (i, 0)),
        out_shape=jax.ShapeDtypeStruct((T, D_MODEL), jnp.float32),
        scratch_shapes=[pltpu.VMEM((2, tm * ROW_CHUNKS, LANES), jnp.float32),
                        pltpu.VMEM((2, tm * ROW_CHUNKS, LANES), jnp.float32),
                        pltpu.SemaphoreType.DMA((2,))],
        compiler_params=_params(("arbitrary",)),
        name="moe_combine",
    )(d0, d1, d0, d1, ys, h, gates, gfin)


def _group_heads_last(w):
    lead = w.shape[:-1]
    w = w.reshape(lead + (N_KV_HEADS, GROUP, HEAD_DIM))
    w = jnp.swapaxes(w, -3, -2)
    return w.reshape(lead + (N_Q_HEADS * HEAD_DIM,))


def _rel_bucket_table():
    qi = np.arange(ATTN_BLOCK)[:, None]
    kj = np.arange(2 * ATTN_BLOCK)[None, :]
    dist = np.maximum(qi + ATTN_BLOCK - kj, 0)
    max_exact = N_REL_BUCKETS // 2
    d = np.maximum(dist, max_exact).astype(np.float32)
    large = max_exact + (np.log(d / np.float32(max_exact)) / np.float32(math.log(REL_MAX_DISTANCE / max_exact))
                         * np.float32(N_REL_BUCKETS - max_exact)).astype(np.int32)
    large = np.minimum(large, N_REL_BUCKETS - 1)
    return np.where(dist < max_exact, dist, large).astype(np.int32)


def _rel_bias_blocks(rel_bias, sinks):
    onehot = (jnp.asarray(_rel_bucket_table())[..., None] == jnp.arange(N_REL_BUCKETS)).astype(jnp.float32)
    bias = jnp.einsum("qkb,bh->hqk", onehot, rel_bias.astype(jnp.float32),
                      precision=lax.Precision.HIGHEST)
    sink_col = jnp.broadcast_to(sinks.astype(jnp.float32)[:, None, None], (N_Q_HEADS, ATTN_BLOCK, 1))
    return jnp.concatenate([sink_col, bias[:, :, 1:]], axis=2)


def _router_weights(w_group, b_group, w_expert, b_expert):
    zw = jnp.zeros((D_MODEL, SUBLANES - N_GROUPS), jnp.float32)
    zw2 = jnp.zeros((D_MODEL, LANES - SUBLANES - N_EXPERTS), jnp.float32)
    wr = jnp.concatenate([w_group, zw, w_expert, zw2], axis=1)
    wh = wr.astype(jnp.bfloat16)
    wl = (wr - wh.astype(jnp.float32)).astype(jnp.bfloat16)
    wr = jnp.concatenate([wh, wl], axis=1)
    brt = jnp.concatenate([b_group, jnp.zeros((SUBLANES - N_GROUPS,), jnp.float32), b_expert])[:, None]
    return wr, brt


def _moe(layer, h, xrows, ri, rf, cnt, w_gate_up, w_down, gfin, final_norm):
    T = h.shape[0]
    tm = TM_EXPERT
    capacity = -(-(2 * T) // tm) * tm + N_EXPERTS * tm
    counts = cnt[:, 0].astype(jnp.int32)
    ntile = (counts + tm - 1) // tm
    tend = jnp.cumsum(ntile)
    pstart = (tend - ntile) * tm
    n_used = tend[-1:]
    padrow = jnp.concatenate([jnp.where(ntile > 0, (tend - 1) * (tm * XS_CHUNKS), -1), n_used]
                             ).astype(jnp.int32)

    dst = _slot_rows_call(ri, pstart)
    xs = _dispatch_call(dst[0], dst[1], padrow, xrows, capacity)
    ys = _expert_call(layer, (tend - ntile).astype(jnp.int32), ntile.astype(jnp.int32), xs,
                      w_gate_up, w_down, capacity)
    gates = rf[:2].T
    return _combine_call(dst[2], dst[3], ys, h, gates, gfin, final_norm)


def kernel(x, norm_mix, norm_ffn, final_norm, rel_bias, attn_w_qkv, attn_b_qkv, attn_w_o, attn_b_o,
           attn_sinks, conv_w_in, conv_w, conv_w_out, moe_w_group, moe_b_group, moe_w_expert,
           moe_b_expert, moe_w_gate_up, moe_w_down):
    B, S, D = x.shape
    T = B * S
    x2 = x.reshape(T, D)
    nq = N_Q_HEADS * HEAD_DIM
    scale = HEAD_DIM ** -0.5

    wqkv = attn_w_qkv[0]
    bqkv = attn_b_qkv[0]
    w_all = jnp.concatenate([_group_heads_last(wqkv[:, :nq]) * scale, wqkv[:, nq:]],
                            axis=1).astype(jnp.bfloat16)
    b_all = jnp.concatenate([_group_heads_last(bqkv[:nq]) * scale, bqkv[nq:]])[None, :]
    q, kt, v = _qkv_call(x2, norm_mix[0][None, :], w_all, b_all)
    a = _attn_call(q, kt, v, _rel_bias_blocks(rel_bias, attn_sinks[0]), B, S)
    wr, brt = _router_weights(moe_w_group[0], moe_b_group[0], moe_w_expert[0], moe_b_expert[0])
    w_o = _group_heads_last(attn_w_o[0].T).T.astype(jnp.bfloat16)
    h, xrows, ri, rf, cnt = _attn_out_call(
        a, w_o, attn_b_o[0][None, :], x2, norm_ffn[0][None, :], wr, brt)
    h = _moe(0, h, xrows, ri, rf, cnt, moe_w_gate_up, moe_w_down, final_norm[None, :], False)

    wr, brt = _router_weights(moe_w_group[1], moe_b_group[1], moe_w_expert[1], moe_b_expert[1])
    h, xrows, ri, rf, cnt = _conv_mixer_call(
        h, norm_mix[1][None, :], conv_w_in[0].astype(jnp.bfloat16), conv_w[0],
        conv_w_out[0].astype(jnp.bfloat16), norm_ffn[1][None, :], wr, brt, S)
    out = _moe(1, h, xrows, ri, rf, cnt, moe_w_gate_up, moe_w_down, final_norm[None, :], True)
    return out.reshape(B, S, D)
```

```python
import functools
import math

import numpy as np
import jax
import jax.numpy as jnp
from jax import lax
from jax.experimental import pallas as pl
from jax.experimental.pallas import tpu as pltpu

D_MODEL = 1024
N_Q_HEADS = 16
N_KV_HEADS = 2
HEAD_DIM = 64
GROUP = N_Q_HEADS // N_KV_HEADS
WINDOW = 128
ATTN_BLOCK = 128
N_REL_BUCKETS = 32
REL_MAX_DISTANCE = 128
CONV_WIDTH = 3
N_GROUPS = 4
EXPERTS_PER_GROUP = 8
N_EXPERTS = N_GROUPS * EXPERTS_PER_GROUP
EXPERT_FF = 512
RMS_EPS = 1e-5

LANES = 128
SUBLANES = 8
ROW_CHUNKS = D_MODEL // LANES
XS_CHUNKS = ROW_CHUNKS // 2
VMEM_LIMIT = 56 * 1024 * 1024

TM_DENSE = 1024
TQ = 1024
TM_MIX1 = 512
TM_EXPERT = 512
TM_DISPATCH = 1024
TM_COMBINE = 256
DMA_UNROLL = 8
TM_SLOT_ROWS = 4096
ROUTER_ROWS = 40


def _rms(x, g):
    return x * lax.rsqrt(jnp.mean(x * x, axis=-1, keepdims=True) + RMS_EPS) * g


def _params(sem):
    return pltpu.CompilerParams(dimension_semantics=sem, vmem_limit_bytes=VMEM_LIMIT)


def _qkv_kernel(x_ref, g_ref, w_ref, b_ref, q_ref, kt_ref, v_ref):
    xn = _rms(x_ref[...], g_ref[...]).astype(jnp.bfloat16)
    out = jnp.dot(xn, w_ref[...], preferred_element_type=jnp.float32) + b_ref[...]
    nq = N_Q_HEADS * HEAD_DIM
    nk = N_KV_HEADS * HEAD_DIM
    q_ref[...] = out[:, :nq].astype(jnp.bfloat16)
    kt_ref[...] = out[:, nq:nq + nk].T.astype(jnp.bfloat16)
    v_ref[...] = out[:, nq + nk:].astype(jnp.bfloat16)


def _qkv_call(x2, g, w, b):
    T = x2.shape[0]
    nq = N_Q_HEADS * HEAD_DIM
    nk = N_KV_HEADS * HEAD_DIM
    tm = TM_DENSE
    return pl.pallas_call(
        _qkv_kernel,
        grid=(T // tm,),
        in_specs=[
            pl.BlockSpec((tm, D_MODEL), lambda i: (i, 0)),
            pl.BlockSpec((1, D_MODEL), lambda i: (0, 0)),
            pl.BlockSpec((D_MODEL, nq + 2 * nk), lambda i: (0, 0)),
            pl.BlockSpec((1, nq + 2 * nk), lambda i: (0, 0)),
        ],
        out_specs=[
            pl.BlockSpec((tm, nq), lambda i: (i, 0)),
            pl.BlockSpec((nk, tm), lambda i: (0, i)),
            pl.BlockSpec((tm, nk), lambda i: (i, 0)),
        ],
        out_shape=[
            jax.ShapeDtypeStruct((T, nq), jnp.bfloat16),
            jax.ShapeDtypeStruct((nk, T), jnp.bfloat16),
            jax.ShapeDtypeStruct((T, nk), jnp.bfloat16),
        ],
        compiler_params=_params(("parallel",)),
        name="qkv_proj",
    )(x2, g, w, b)


def _attn_kernel(q_ref, kt_ref, ktp_ref, v_ref, vp_ref, bias_ref, o_ref):
    first_tile = pl.program_id(1) == 0
    nblk = TQ // ATTN_BLOCK
    qi_io = lax.broadcasted_iota(jnp.int32, (ATTN_BLOCK, 2 * ATTN_BLOCK), 0)
    kj_io = lax.broadcasted_iota(jnp.int32, (ATTN_BLOCK, 2 * ATTN_BLOCK), 1)
    dist = qi_io + ATTN_BLOCK - kj_io
    band = (dist >= 0) & (dist < WINDOW)
    sink_col = kj_io == 0
    kt_keep = lax.broadcasted_iota(jnp.int32, (N_KV_HEADS * HEAD_DIM, 2 * ATTN_BLOCK), 1) > 0
    v_keep = lax.broadcasted_iota(jnp.int32, (2 * ATTN_BLOCK, N_KV_HEADS * HEAD_DIM), 0) > 0
    low_half = lax.broadcasted_iota(jnp.int32, (ATTN_BLOCK, LANES), 1) < HEAD_DIM
    for qi in range(nblk):
        r0 = qi * ATTN_BLOCK
        if qi == 0:
            kt_blk = jnp.concatenate([ktp_ref[...], kt_ref[:, 0:ATTN_BLOCK]], axis=1)
            v_blk = jnp.concatenate([vp_ref[...], v_ref[0:ATTN_BLOCK, :]], axis=0)
            mask = (band & (jnp.logical_not(first_tile) | (kj_io >= ATTN_BLOCK))) | sink_col
        else:
            kt_blk = kt_ref[:, r0 - ATTN_BLOCK:r0 + ATTN_BLOCK]
            v_blk = v_ref[r0 - ATTN_BLOCK:r0 + ATTN_BLOCK, :]
            mask = band | sink_col
        kt_blk = jnp.where(kt_keep, kt_blk, jnp.zeros_like(kt_blk))
        v_blk = jnp.where(v_keep, v_blk, jnp.zeros_like(v_blk))
        for m in range(GROUP):
            qg = q_ref[r0:r0 + ATTN_BLOCK, m * LANES:(m + 1) * LANES]
            halves = []
            for half in range(N_KV_HEADS):
                h = m + GROUP * half
                keep = low_half if half == 0 else jnp.logical_not(low_half)
                qpad = jnp.where(keep, qg, jnp.zeros_like(qg))
                s = jnp.dot(qpad, kt_blk, preferred_element_type=jnp.float32)
                s = jnp.where(mask, s + bias_ref[h], -1e30)
                mx = jnp.max(s, axis=-1, keepdims=True)
                p = jnp.exp(s - mx)
                den = jnp.sum(p, axis=-1, keepdims=True)
                pv = jnp.dot(p.astype(jnp.bfloat16), v_blk, preferred_element_type=jnp.float32)
                halves.append(pv * (1.0 / den))
            og = jnp.where(low_half, halves[0], halves[1])
            o_ref[r0:r0 + ATTN_BLOCK, m * LANES:(m + 1) * LANES] = og.astype(jnp.bfloat16)


def _attn_call(q, kt, v, bias, batch, seq):
    T = q.shape[0]
    nq = N_Q_HEADS * HEAD_DIM
    nk = N_KV_HEADS * HEAD_DIM
    tiles = seq // TQ
    per = TQ // ATTN_BLOCK

    def cur(b, j):
        return b * tiles + j

    def prev(b, j):
        return jnp.maximum((b * tiles + j) * per - 1, b * tiles * per)

    return pl.pallas_call(
        _attn_kernel,
        grid=(batch, tiles),
        in_specs=[
            pl.BlockSpec((TQ, nq), lambda b, j: (cur(b, j), 0)),
            pl.BlockSpec((nk, TQ), lambda b, j: (0, cur(b, j))),
            pl.BlockSpec((nk, ATTN_BLOCK), lambda b, j: (0, prev(b, j))),
            pl.BlockSpec((TQ, nk), lambda b, j: (cur(b, j), 0)),
            pl.BlockSpec((ATTN_BLOCK, nk), lambda b, j: (prev(b, j), 0)),
            pl.BlockSpec((N_Q_HEADS, ATTN_BLOCK, 2 * ATTN_BLOCK), lambda b, j: (0, 0, 0)),
        ],
        out_specs=pl.BlockSpec((TQ, nq), lambda b, j: (cur(b, j), 0)),
        out_shape=jax.ShapeDtypeStruct((T, nq), jnp.bfloat16),
        compiler_params=_params(("parallel", "parallel")),
        name="swa_attention",
    )(q, kt, kt, v, v, bias)


def _router_epilogue(h, gffn_ref, wr_ref, brt_ref, carry_ref,
                     h_out_ref, xrows_ref, ri_ref, rf_ref, cnt_ref):
    tm = h.shape[0]
    h_out_ref[...] = h
    xn = _rms(h, gffn_ref[...])
    xh = xn.astype(jnp.bfloat16)
    xbits = lax.bitcast_convert_type(xh.astype(jnp.float32), jnp.uint32)
    for c in range(XS_CHUNKS):
        lo = xbits[:, (2 * c) * LANES:(2 * c + 1) * LANES] >> 16
        hi = xbits[:, (2 * c + 1) * LANES:(2 * c + 2) * LANES]
        xrows_ref[pl.ds(c, tm, stride=XS_CHUNKS), :] = lo | hi

    xl = (xn - xh.astype(jnp.float32)).astype(jnp.bfloat16)
    wcat = wr_ref[...]
    prod = jnp.dot(xh, wcat, preferred_element_type=jnp.float32)
    logits = (prod[:, :LANES] + prod[:, LANES:]) + jnp.dot(
        xl, wcat[:, :LANES], preferred_element_type=jnp.float32)
    lt = logits.T[:ROUTER_ROWS, :] + brt_ref[...]

    gl = [lt[g:g + 1, :] for g in range(N_GROUPS)]
    gmax = functools.reduce(jnp.maximum, gl)
    gexp = [jnp.exp(x - gmax) for x in gl]
    gsum = functools.reduce(lambda a, b: a + b, gexp)
    gprob = [x / gsum for x in gexp]
    g_prob = functools.reduce(jnp.maximum, gprob)
    g_idx = jnp.full(g_prob.shape, N_GROUPS - 1, jnp.int32)
    for g in range(N_GROUPS - 2, -1, -1):
        g_idx = jnp.where(gprob[g] == g_prob, g, g_idx)

    el = []
    for j in range(EXPERTS_PER_GROUP):
        x = lt[SUBLANES + j:SUBLANES + j + 1, :]
        for g in range(1, N_GROUPS):
            r = SUBLANES + g * EXPERTS_PER_GROUP + j
            x = jnp.where(g_idx == g, lt[r:r + 1, :], x)
        el.append(x)
    emax = functools.reduce(jnp.maximum, el)
    eexp = [jnp.exp(x - emax) for x in el]
    esum = functools.reduce(lambda a, b: a + b, eexp)
    eprob = [x / esum for x in eexp]
    p1 = functools.reduce(jnp.maximum, eprob)
    i1 = jnp.full(p1.shape, EXPERTS_PER_GROUP - 1, jnp.int32)
    for j in range(EXPERTS_PER_GROUP - 2, -1, -1):
        i1 = jnp.where(eprob[j] == p1, j, i1)
    rest = [jnp.where(i1 == j, -1.0, eprob[j]) for j in range(EXPERTS_PER_GROUP)]
    p2 = functools.reduce(jnp.maximum, rest)
    i2 = jnp.full(p2.shape, EXPERTS_PER_GROUP - 1, jnp.int32)
    for j in range(EXPERTS_PER_GROUP - 2, -1, -1):
        i2 = jnp.where(rest[j] == p2, j, i2)
    psum = p1 + p2
    gate0 = g_prob * (p1 / psum)
    gate1 = g_prob * (p2 / psum)
    e0 = g_idx * EXPERTS_PER_GROUP + i1
    e1 = g_idx * EXPERTS_PER_GROUP + i2

    eio = lax.broadcasted_iota(jnp.int32, (N_EXPERTS, tm), 0)
    oh0 = (eio == e0).astype(jnp.float32)
    oh1 = (eio == e1).astype(jnp.float32)
    both = oh0 + oh1
    tr = lax.broadcasted_iota(jnp.int32, (tm, tm), 0)
    tc = lax.broadcasted_iota(jnp.int32, (tm, tm), 1)
    upper = (tr < tc).astype(jnp.bfloat16)
    before = jnp.dot(both.astype(jnp.bfloat16), upper, preferred_element_type=jnp.float32)
    before = before + carry_ref[...]
    rank0 = jnp.sum(oh0 * before, axis=0, keepdims=True)
    rank1 = jnp.sum(oh1 * (before + oh0), axis=0, keepdims=True)
    carry_ref[...] = carry_ref[...] + jnp.sum(both, axis=1, keepdims=True)
    cnt_ref[...] = jnp.broadcast_to(carry_ref[...], cnt_ref.shape)

    zi = jnp.zeros_like(e0)
    ri_ref[...] = jnp.concatenate(
        [e0, e1, rank0.astype(jnp.int32), rank1.astype(jnp.int32), zi, zi, zi, zi], axis=0)
    zf = jnp.zeros_like(gate0)
    rf_ref[...] = jnp.concatenate([gate0, gate1, zf, zf, zf, zf, zf, zf], axis=0)


def _epilogue_out_specs(tm):
    return [
        pl.BlockSpec((tm, D_MODEL), lambda i: (i, 0)),
        pl.BlockSpec((tm * XS_CHUNKS, LANES), lambda i: (i, 0)),
        pl.BlockSpec((SUBLANES, tm), lambda i: (0, i)),
        pl.BlockSpec((SUBLANES, tm), lambda i: (0, i)),
        pl.BlockSpec((N_EXPERTS, LANES), lambda i: (0, 0)),
    ]


def _epilogue_out_shapes(T):
    return [
        jax.ShapeDtypeStruct((T, D_MODEL), jnp.float32),
        jax.ShapeDtypeStruct((T * XS_CHUNKS, LANES), jnp.uint32),
        jax.ShapeDtypeStruct((SUBLANES, T), jnp.int32),
        jax.ShapeDtypeStruct((SUBLANES, T), jnp.float32),
        jax.ShapeDtypeStruct((N_EXPERTS, LANES), jnp.float32),
    ]


def _epilogue_in_specs():
    return [
        pl.BlockSpec((1, D_MODEL), lambda i: (0, 0)),
        pl.BlockSpec((D_MODEL, 2 * LANES), lambda i: (0, 0)),
        pl.BlockSpec((ROUTER_ROWS, 1), lambda i: (0, 0)),
    ]


def _attn_out_kernel(a_ref, w_ref, b_ref, res_ref, gffn_ref, wr_ref, brt_ref,
                     h_out_ref, xrows_ref, ri_ref, rf_ref, cnt_ref, carry_ref):
    @pl.when(pl.program_id(0) == 0)
    def _():
        carry_ref[...] = jnp.zeros_like(carry_ref)

    mix = jnp.dot(a_ref[...], w_ref[...], preferred_element_type=jnp.float32) + b_ref[...]
    h = res_ref[...] + mix
    _router_epilogue(h, gffn_ref, wr_ref, brt_ref, carry_ref,
                     h_out_ref, xrows_ref, ri_ref, rf_ref, cnt_ref)


def _attn_out_call(a, w, b, res, gffn, wr, brt):
    T = a.shape[0]
    tm = TM_DENSE
    return pl.pallas_call(
        _attn_out_kernel,
        grid=(T // tm,),
        in_specs=[
            pl.BlockSpec((tm, D_MODEL), lambda i: (i, 0)),
            pl.BlockSpec((D_MODEL, D_MODEL), lambda i: (0, 0)),
            pl.BlockSpec((1, D_MODEL), lambda i: (0, 0)),
            pl.BlockSpec((tm, D_MODEL), lambda i: (i, 0)),
        ] + _epilogue_in_specs(),
        out_specs=_epilogue_out_specs(tm),
        out_shape=_epilogue_out_shapes(T),
        scratch_shapes=[pltpu.VMEM((N_EXPERTS, 1), jnp.float32)],
        compiler_params=_params(("arbitrary",)),
        name="attn_out_router",
    )(a, w, b, res, gffn, wr, brt)


def _conv_mixer_kernel(tiles_per_seq, d0_ref, d1_ref, d0n_ref, d1n_ref, ys_hbm, hin_ref, gate_ref,
                       g_ref, win_ref, cw_ref, wout_ref, gffn_ref, wr_ref, brt_ref,
                       h_out_ref, xrows_ref, ri_ref, rf_ref, cnt_ref,
                       carry_ref, tail_ref, y0_ref, y1_ref, sem):
    i = pl.program_id(0)
    tm = hin_ref.shape[0]

    @pl.when(i == 0)
    def _():
        carry_ref[...] = jnp.zeros_like(carry_ref)

    @pl.when(i % tiles_per_seq == 0)
    def _():
        tail_ref[...] = jnp.zeros_like(tail_ref)

    x = _gathered_residual(tm, d0_ref, d1_ref, d0n_ref, d1n_ref, ys_hbm, hin_ref, gate_ref,
                           y0_ref, y1_ref, sem)
    xn = _rms(x, g_ref[...]).astype(jnp.bfloat16)
    bcu = jnp.dot(xn, win_ref[...], preferred_element_type=jnp.float32)
    b_gate = bcu[:, :D_MODEL]
    z = bcu[:, D_MODEL:2 * D_MODEL] * bcu[:, 2 * D_MODEL:]
    row = lax.broadcasted_iota(jnp.int32, (tm, D_MODEL), 0)
    tail = tail_ref[...]
    t1 = tail[SUBLANES - 1:SUBLANES, :]
    t2 = tail[SUBLANES - 2:SUBLANES - 1, :]
    z1 = jnp.where(row == 0, t1, pltpu.roll(z, 1, axis=0))
    z2 = jnp.where(row == 0, t2, jnp.where(row == 1, t1, pltpu.roll(z, 2, axis=0)))
    tail_ref[...] = z[tm - SUBLANES:, :]
    conv = z2 * cw_ref[0:1, :]
    conv = conv + z1 * cw_ref[1:2, :]
    conv = conv + z * cw_ref[2:3, :]
    gated = (b_gate * conv).astype(jnp.bfloat16)
    h = x + jnp.dot(gated, wout_ref[...], preferred_element_type=jnp.float32)
    _router_epilogue(h, gffn_ref, wr_ref, brt_ref, carry_ref,
                     h_out_ref, xrows_ref, ri_ref, rf_ref, cnt_ref)


def _conv_mixer_call(d0, d1, ys, hin, gates, g, win, cw, wout, gffn, wr, brt, seq):
    T = hin.shape[0]
    tm = TM_MIX1
    return pl.pallas_call(
        functools.partial(_conv_mixer_kernel, seq // tm),
        grid=(T // tm,),
        in_specs=_gather_in_specs(tm, T // tm) + [
            pl.BlockSpec((1, D_MODEL), lambda i: (0, 0)),
            pl.BlockSpec((D_MODEL, 3 * D_MODEL), lambda i: (0, 0)),
            pl.BlockSpec((CONV_WIDTH, D_MODEL), lambda i: (0, 0)),
            pl.BlockSpec((D_MODEL, D_MODEL), lambda i: (0, 0)),
        ] + _epilogue_in_specs(),
        out_specs=_epilogue_out_specs(tm),
        out_shape=_epilogue_out_shapes(T),
        scratch_shapes=[pltpu.VMEM((N_EXPERTS, 1), jnp.float32),
                        pltpu.VMEM((SUBLANES, D_MODEL), jnp.float32)] + _gather_scratch(tm),
        compiler_params=_params(("arbitrary",)),
        name="conv_mixer_router",
    )(d0, d1, d0, d1, ys, hin, gates, g, win, cw, wout, gffn, wr, brt)


def _row_copy(src_hbm, src_row, dst_hbm, dst_row, sem, chunks):
    return pltpu.make_async_copy(
        src_hbm.at[pl.ds(pl.multiple_of(src_row, chunks), chunks)],
        dst_hbm.at[pl.ds(pl.multiple_of(dst_row, chunks), chunks)], sem)


def _slot_rows_kernel(ri_ref, pstart_ref, o_ref):
    tb = ri_ref.shape[1]
    eio = lax.broadcasted_iota(jnp.int32, (N_EXPERTS, tb), 0)
    pstart = pstart_ref[...]
    slots = []
    for k in range(2):
        start = jnp.sum(jnp.where(eio == ri_ref[k:k + 1, :], pstart, 0), axis=0, keepdims=True)
        slots.append(start + ri_ref[2 + k:3 + k, :])
    zi = jnp.zeros_like(slots[0])
    o_ref[...] = jnp.concatenate(
        [s * XS_CHUNKS for s in slots] + [s * ROW_CHUNKS for s in slots] + [zi] * (SUBLANES - 4),
        axis=0)


def _slot_rows_call(ri, pstart):
    T = ri.shape[1]
    tb = TM_SLOT_ROWS
    return pl.pallas_call(
        _slot_rows_kernel,
        grid=(T // tb,),
        in_specs=[pl.BlockSpec((SUBLANES, tb), lambda i: (0, i)),
                  pl.BlockSpec((N_EXPERTS, 1), lambda i: (0, 0))],
        out_specs=pl.BlockSpec((SUBLANES, tb), lambda i: (0, i)),
        out_shape=jax.ShapeDtypeStruct((SUBLANES, T), jnp.int32),
        compiler_params=_params(("parallel",)),
        name="moe_slot_rows",
    )(ri, pstart[:, None])


def _dispatch_kernel(d0_ref, d1_ref, padrow_ref, x_ref, xs_hbm, zero_ref, sem, zsem):
    i = pl.program_id(0)
    tile_rows = TM_EXPERT * XS_CHUNKS

    @pl.when(i == 0)
    def _():
        zero_ref[...] = jnp.zeros_like(zero_ref)

        def zcopy(e):
            return pltpu.make_async_copy(
                zero_ref,
                xs_hbm.at[pl.ds(pl.multiple_of(padrow_ref[e], tile_rows), tile_rows)], zsem)

        def zstart(e, c):
            @pl.when(padrow_ref[e] >= 0)
            def _():
                zcopy(e).start()
            return c

        def zwait(e, c):
            @pl.when(padrow_ref[e] >= 0)
            def _():
                zcopy(e).wait()
            return c

        lax.fori_loop(0, N_EXPERTS, zstart, 0)
        lax.fori_loop(0, N_EXPERTS, zwait, 0)

        def tcopy(b):
            return pltpu.make_async_copy(
                zero_ref, xs_hbm.at[pl.ds(pl.multiple_of(b * tile_rows, tile_rows), tile_rows)], zsem)

        n_used = padrow_ref[N_EXPERTS]
        n_blk = xs_hbm.shape[0] // tile_rows
        lax.fori_loop(n_used, n_blk, lambda b, c: (tcopy(b).start(), c)[1], 0)
        lax.fori_loop(n_used, n_blk, lambda b, c: (tcopy(b).wait(), c)[1], 0)

    def issue(t, c):
        src = t * XS_CHUNKS
        _row_copy(x_ref, src, xs_hbm, d0_ref[t], sem, XS_CHUNKS).start(priority=0)
        _row_copy(x_ref, src, xs_hbm, d1_ref[t], sem, XS_CHUNKS).start(priority=1)
        return c

    lax.fori_loop(0, TM_DISPATCH, issue, 0, unroll=DMA_UNROLL)
    for _ in range(2):
        pltpu.make_async_copy(x_ref, xs_hbm.at[pl.ds(0, TM_DISPATCH * XS_CHUNKS)], sem).wait()


def _dispatch_call(d0, d1, padrow, xrows, capacity):
    T = d0.shape[0]
    return pl.pallas_call(
        _dispatch_kernel,
        grid=(T // TM_DISPATCH,),
        in_specs=[
            pl.BlockSpec((TM_DISPATCH,), lambda i: (i,), memory_space=pltpu.SMEM),
            pl.BlockSpec((TM_DISPATCH,), lambda i: (i,), memory_space=pltpu.SMEM),
            pl.BlockSpec(memory_space=pltpu.SMEM),
            pl.BlockSpec((TM_DISPATCH * XS_CHUNKS, LANES), lambda i: (i, 0)),
        ],
        out_specs=pl.BlockSpec(memory_space=pl.ANY),
        out_shape=jax.ShapeDtypeStruct((capacity * XS_CHUNKS, LANES), jnp.uint32),
        scratch_shapes=[pltpu.VMEM((TM_EXPERT * XS_CHUNKS, LANES), jnp.uint32),
                        pltpu.SemaphoreType.DMA, pltpu.SemaphoreType.DMA],
        compiler_params=_params(("arbitrary",)),
        name="moe_dispatch",
    )(d0, d1, padrow, xrows)


def _expert_kernel(tstart_ref, ntile_ref, xs_hbm, wgu_ref, wd_ref, ys_hbm,
                   xbuf, ybuf, wgu_bf, wd_bf, sem_in, sem_out):
    e = pl.program_id(0)
    tm = TM_EXPERT
    tile_rows = tm * ROW_CHUNKS
    t0 = tstart_ref[e]
    nt = ntile_ref[e]
    n_used = tstart_ref[N_EXPERTS - 1] + ntile_ref[N_EXPERTS - 1]

    def tile_at(ref, g, rows):
        return ref.at[pl.ds(pl.multiple_of(g * rows, rows), rows)]

    def in_copy(g, slot):
        return pltpu.make_async_copy(tile_at(xs_hbm, g, tm * XS_CHUNKS), xbuf.at[slot], sem_in.at[slot])

    def out_copy(g, slot):
        return pltpu.make_async_copy(ybuf.at[slot], tile_at(ys_hbm, g, tile_rows), sem_out.at[slot])

    @pl.when(e == 0)
    def _():
        in_copy(0, 0).start()

    @pl.when(nt > 0)
    def _():
        wgu_bf[...] = wgu_ref[0, 0].astype(jnp.bfloat16)
        wd_bf[...] = wd_ref[0, 0].astype(jnp.bfloat16)

    def tile_body(g, carry):
        slot = g % 2
        in_copy(g, slot).wait()

        @pl.when(g + 1 < n_used)
        def _():
            in_copy(g + 1, 1 - slot).start(priority=1)

        @pl.when(g >= 2)
        def _():
            out_copy(g - 2, slot).wait()

        cols = []
        for c in range(XS_CHUNKS):
            w = xbuf[slot, pl.ds(c, tm, stride=XS_CHUNKS), :]
            cols.append(lax.bitcast_convert_type(w << 16, jnp.float32).astype(jnp.bfloat16))
            cols.append(lax.bitcast_convert_type(w & jnp.uint32(0xFFFF0000), jnp.float32
                                                 ).astype(jnp.bfloat16))
        x = jnp.concatenate(cols, axis=-1)
        gu = jnp.dot(x, wgu_bf[...], preferred_element_type=jnp.float32)
        g_act = gu[:, :EXPERT_FF]
        u = gu[:, EXPERT_FF:]
        act = (g_act / (1.0 + jnp.exp(-g_act))) * u
        y = jnp.dot(act.astype(jnp.bfloat16), wd_bf[...], preferred_element_type=jnp.float32)
        for c in range(ROW_CHUNKS):
            ybuf[slot, pl.ds(c, tm, stride=ROW_CHUNKS), :] = y[:, c * LANES:(c + 1) * LANES]
        out_copy(g, slot).start(priority=1)
        return carry

    lax.fori_loop(t0, t0 + nt, tile_body, 0)

    @pl.when(e == N_EXPERTS - 1)
    def _():
        @pl.when(n_used >= 2)
        def _():
            out_copy(n_used - 2, n_used % 2).wait()

        out_copy(n_used - 1, (n_used - 1) % 2).wait()

        n_blk = ys_hbm.shape[0] // tile_rows
        ybuf[0] = jnp.zeros(ybuf.shape[1:], ybuf.dtype)
        lax.fori_loop(n_used, n_blk, lambda b, c: (out_copy(b, 0).start(), c)[1], 0)
        lax.fori_loop(n_used, n_blk, lambda b, c: (out_copy(b, 0).wait(), c)[1], 0)


def _expert_call(layer, tstart, ntile, xs, wgu, wd, capacity):
    tile_rows = TM_EXPERT * ROW_CHUNKS

    def w_map(e, ts, nt):
        return (layer, e, 0, 0)

    return pl.pallas_call(
        _expert_kernel,
        grid_spec=pltpu.PrefetchScalarGridSpec(
            num_scalar_prefetch=2,
            grid=(N_EXPERTS,),
            in_specs=[
                pl.BlockSpec(memory_space=pl.ANY),
                pl.BlockSpec((1, 1, D_MODEL, 2 * EXPERT_FF), w_map),
                pl.BlockSpec((1, 1, EXPERT_FF, D_MODEL), w_map),
            ],
            out_specs=pl.BlockSpec(memory_space=pl.ANY),
            scratch_shapes=[pltpu.VMEM((2, TM_EXPERT * XS_CHUNKS, LANES), jnp.uint32),
                            pltpu.VMEM((2, tile_rows, LANES), jnp.float32),
                            pltpu.VMEM((D_MODEL, 2 * EXPERT_FF), jnp.bfloat16),
                            pltpu.VMEM((EXPERT_FF, D_MODEL), jnp.bfloat16),
                            pltpu.SemaphoreType.DMA((2,)),
                            pltpu.SemaphoreType.DMA((2,))],
        ),
        out_shape=jax.ShapeDtypeStruct((capacity * ROW_CHUNKS, LANES), jnp.float32),
        compiler_params=_params(("arbitrary",)),
        name="moe_experts",
    )(tstart, ntile, xs, wgu, wd)


def _gathered_residual(tm, d0_ref, d1_ref, d0n_ref, d1n_ref, ys_hbm, h_ref, gate_ref,
                       y0_ref, y1_ref, sem):
    i = pl.program_id(0)
    slot = i % 2

    def issue_tile(d0r, d1r, s):
        def issue(t, c):
            dst = t * ROW_CHUNKS
            _row_copy(ys_hbm, d0r[t], y0_ref.at[s], dst, sem.at[s], ROW_CHUNKS).start(priority=0)
            _row_copy(ys_hbm, d1r[t], y1_ref.at[s], dst, sem.at[s], ROW_CHUNKS).start(priority=1)
            return c
        lax.fori_loop(0, tm, issue, 0, unroll=DMA_UNROLL)

    @pl.when(i == 0)
    def _():
        issue_tile(d0_ref, d1_ref, 0)

    @pl.when(i + 1 < pl.num_programs(0))
    def _():
        issue_tile(d0n_ref, d1n_ref, 1 - slot)

    for buf in (y0_ref, y1_ref):
        pltpu.make_async_copy(ys_hbm.at[pl.ds(0, tm * ROW_CHUNKS)], buf.at[slot], sem.at[slot]).wait()

    g0 = gate_ref[:, 0:1]
    g1 = gate_ref[:, 1:2]
    cols = []
    for c in range(ROW_CHUNKS):
        y0 = y0_ref[slot, pl.ds(c, tm, stride=ROW_CHUNKS), :]
        y1 = y1_ref[slot, pl.ds(c, tm, stride=ROW_CHUNKS), :]
        cols.append(h_ref[:, c * LANES:(c + 1) * LANES] + (y0 * g0 + y1 * g1))
    return jnp.concatenate(cols, axis=-1)


def _gather_in_specs(tm, n_tiles):
    last = n_tiles - 1
    return [
        pl.BlockSpec((tm,), lambda i: (i,), memory_space=pltpu.SMEM),
        pl.BlockSpec((tm,), lambda i: (i,), memory_space=pltpu.SMEM),
        pl.BlockSpec((tm,), lambda i: (jnp.minimum(i + 1, last),), memory_space=pltpu.SMEM),
        pl.BlockSpec((tm,), lambda i: (jnp.minimum(i + 1, last),), memory_space=pltpu.SMEM),
        pl.BlockSpec(memory_space=pl.ANY),
        pl.BlockSpec((tm, D_MODEL), lambda i: (i, 0)),
        pl.BlockSpec((tm, 2), lambda i: (i, 0)),
    ]


def _gather_scratch(tm):
    return [pltpu.VMEM((2, tm * ROW_CHUNKS, LANES), jnp.float32),
            pltpu.VMEM((2, tm * ROW_CHUNKS, LANES), jnp.float32),
            pltpu.SemaphoreType.DMA((2,))]


def _final_combine_kernel(d0_ref, d1_ref, d0n_ref, d1n_ref, ys_hbm, h_ref, gate_ref, gfin_ref,
                          o_ref, y0_ref, y1_ref, sem):
    h = _gathered_residual(TM_COMBINE, d0_ref, d1_ref, d0n_ref, d1n_ref, ys_hbm, h_ref, gate_ref,
                           y0_ref, y1_ref, sem)
    o_ref[...] = _rms(h, gfin_ref[...])


def _final_combine_call(d0, d1, ys, h, gates, gfin):
    T = h.shape[0]
    tm = TM_COMBINE
    return pl.pallas_call(
        _final_combine_kernel,
        grid=(T // tm,),
        in_specs=_gather_in_specs(tm, T // tm) + [pl.BlockSpec((1, D_MODEL), lambda i: (0, 0))],
        out_specs=pl.BlockSpec((tm, D_MODEL), lambda i: (i, 0)),
        out_shape=jax.ShapeDtypeStruct((T, D_MODEL), jnp.float32),
        scratch_shapes=_gather_scratch(tm),
        compiler_params=_params(("arbitrary",)),
        name="moe_combine",
    )(d0, d1, d0, d1, ys, h, gates, gfin)


def _group_heads_last(w):
    lead = w.shape[:-1]
    w = w.reshape(lead + (N_KV_HEADS, GROUP, HEAD_DIM))
    w = jnp.swapaxes(w, -3, -2)
    return w.reshape(lead + (N_Q_HEADS * HEAD_DIM,))


def _rel_bucket_table():
    qi = np.arange(ATTN_BLOCK)[:, None]
    kj = np.arange(2 * ATTN_BLOCK)[None, :]
    dist = np.maximum(qi + ATTN_BLOCK - kj, 0)
    max_exact = N_REL_BUCKETS // 2
    d = np.maximum(dist, max_exact).astype(np.float32)
    large = max_exact + (np.log(d / np.float32(max_exact)) / np.float32(math.log(REL_MAX_DISTANCE / max_exact))
                         * np.float32(N_REL_BUCKETS - max_exact)).astype(np.int32)
    large = np.minimum(large, N_REL_BUCKETS - 1)
    return np.where(dist < max_exact, dist, large).astype(np.int32)


def _rel_bias_blocks(rel_bias, sinks):
    onehot = (jnp.asarray(_rel_bucket_table())[..., None] == jnp.arange(N_REL_BUCKETS)).astype(jnp.float32)
    bias = jnp.einsum("qkb,bh->hqk", onehot, rel_bias.astype(jnp.float32),
                      precision=lax.Precision.HIGHEST)
    sink_col = jnp.broadcast_to(sinks.astype(jnp.float32)[:, None, None], (N_Q_HEADS, ATTN_BLOCK, 1))
    return jnp.concatenate([sink_col, bias[:, :, 1:]], axis=2)


def _router_weights(w_group, b_group, w_expert, b_expert):
    zw = jnp.zeros((D_MODEL, SUBLANES - N_GROUPS), jnp.float32)
    zw2 = jnp.zeros((D_MODEL, LANES - SUBLANES - N_EXPERTS), jnp.float32)
    wr = jnp.concatenate([w_group, zw, w_expert, zw2], axis=1)
    wh = wr.astype(jnp.bfloat16)
    wl = (wr - wh.astype(jnp.float32)).astype(jnp.bfloat16)
    wr = jnp.concatenate([wh, wl], axis=1)
    brt = jnp.concatenate([b_group, jnp.zeros((SUBLANES - N_GROUPS,), jnp.float32), b_expert])[:, None]
    return wr, brt


def _moe_experts(layer, xrows, ri, rf, cnt, w_gate_up, w_down):
    T = ri.shape[1]
    tm = TM_EXPERT
    capacity = -(-(2 * T) // tm) * tm + N_EXPERTS * tm
    counts = cnt[:, 0].astype(jnp.int32)
    ntile = (counts + tm - 1) // tm
    tend = jnp.cumsum(ntile)
    pstart = (tend - ntile) * tm
    n_used = tend[-1:]
    padrow = jnp.concatenate([jnp.where(ntile > 0, (tend - 1) * (tm * XS_CHUNKS), -1), n_used]
                             ).astype(jnp.int32)

    dst = _slot_rows_call(ri, pstart)
    xs = _dispatch_call(dst[0], dst[1], padrow, xrows, capacity)
    ys = _expert_call(layer, (tend - ntile).astype(jnp.int32), ntile.astype(jnp.int32), xs,
                      w_gate_up, w_down, capacity)
    return dst[2], dst[3], ys, rf[:2].T


def kernel(x, norm_mix, norm_ffn, final_norm, rel_bias, attn_w_qkv, attn_b_qkv, attn_w_o, attn_b_o,
           attn_sinks, conv_w_in, conv_w, conv_w_out, moe_w_group, moe_b_group, moe_w_expert,
           moe_b_expert, moe_w_gate_up, moe_w_down):
    B, S, D = x.shape
    T = B * S
    x2 = x.reshape(T, D)
    nq = N_Q_HEADS * HEAD_DIM
    scale = HEAD_DIM ** -0.5

    wqkv = attn_w_qkv[0]
    bqkv = attn_b_qkv[0]
    w_all = jnp.concatenate([_group_heads_last(wqkv[:, :nq]) * scale, wqkv[:, nq:]],
                            axis=1).astype(jnp.bfloat16)
    b_all = jnp.concatenate([_group_heads_last(bqkv[:nq]) * scale, bqkv[nq:]])[None, :]
    q, kt, v = _qkv_call(x2, norm_mix[0][None, :], w_all, b_all)
    a = _attn_call(q, kt, v, _rel_bias_blocks(rel_bias, attn_sinks[0]), B, S)
    wr, brt = _router_weights(moe_w_group[0], moe_b_group[0], moe_w_expert[0], moe_b_expert[0])
    w_o = _group_heads_last(attn_w_o[0].T).T.astype(jnp.bfloat16)
    h, xrows, ri, rf, cnt = _attn_out_call(
        a, w_o, attn_b_o[0][None, :], x2, norm_ffn[0][None, :], wr, brt)
    d0, d1, ys, gates = _moe_experts(0, xrows, ri, rf, cnt, moe_w_gate_up, moe_w_down)

    wr, brt = _router_weights(moe_w_group[1], moe_b_group[1], moe_w_expert[1], moe_b_expert[1])
    h, xrows, ri, rf, cnt = _conv_mixer_call(
        d0, d1, ys, h, gates, norm_mix[1][None, :], conv_w_in[0].astype(jnp.bfloat16), conv_w[0],
        conv_w_out[0].astype(jnp.bfloat16), norm_ffn[1][None, :], wr, brt, S)
    d0, d1, ys, gates = _moe_experts(1, xrows, ri, rf, cnt, moe_w_gate_up, moe_w_down)
    out = _final_combine_call(d0, d1, ys, h, gates, final_norm[None, :])
    return out.reshape(B, S, D)
```

```python
import functools
import math

import numpy as np
import jax
import jax.numpy as jnp
from jax import lax
from jax.experimental import pallas as pl
from jax.experimental.pallas import tpu as pltpu

D_MODEL = 1024
N_Q_HEADS = 16
N_KV_HEADS = 2
HEAD_DIM = 64
GROUP = N_Q_HEADS // N_KV_HEADS
WINDOW = 128
ATTN_BLOCK = 128
N_REL_BUCKETS = 32
REL_MAX_DISTANCE = 128
CONV_WIDTH = 3
N_GROUPS = 4
EXPERTS_PER_GROUP = 8
N_EXPERTS = N_GROUPS * EXPERTS_PER_GROUP
EXPERT_FF = 512
RMS_EPS = 1e-5

LANES = 128
SUBLANES = 8
ROW_CHUNKS = D_MODEL // LANES
XS_CHUNKS = ROW_CHUNKS // 2
VMEM_LIMIT = 56 * 1024 * 1024

TM_DENSE = 1024
TQ = 1024
TM_MIX1 = 512
TM_EXPERT = 512
TM_DISPATCH = 1024
TM_COMBINE = 256
DMA_UNROLL = 8
TM_SLOT_ROWS = 4096
EXPERT_BUFS = 3
ROUTER_ROWS = 40


def _rms(x, g):
    return x * lax.rsqrt(jnp.mean(x * x, axis=-1, keepdims=True) + RMS_EPS) * g


def _params(sem):
    return pltpu.CompilerParams(dimension_semantics=sem, vmem_limit_bytes=VMEM_LIMIT)


def _qkv_kernel(x_ref, g_ref, w_ref, b_ref, q_ref, kt_ref, v_ref):
    xn = _rms(x_ref[...], g_ref[...]).astype(jnp.bfloat16)
    out = jnp.dot(xn, w_ref[...], preferred_element_type=jnp.float32) + b_ref[...]
    nq = N_Q_HEADS * HEAD_DIM
    nk = N_KV_HEADS * HEAD_DIM
    q_ref[...] = out[:, :nq].astype(jnp.bfloat16)
    kt_ref[...] = out[:, nq:nq + nk].T.astype(jnp.bfloat16)
    v_ref[...] = out[:, nq + nk:].astype(jnp.bfloat16)


def _qkv_call(x2, g, w, b):
    T = x2.shape[0]
    nq = N_Q_HEADS * HEAD_DIM
    nk = N_KV_HEADS * HEAD_DIM
    tm = TM_DENSE
    return pl.pallas_call(
        _qkv_kernel,
        grid=(T // tm,),
        in_specs=[
            pl.BlockSpec((tm, D_MODEL), lambda i: (i, 0)),
            pl.BlockSpec((1, D_MODEL), lambda i: (0, 0)),
            pl.BlockSpec((D_MODEL, nq + 2 * nk), lambda i: (0, 0)),
            pl.BlockSpec((1, nq + 2 * nk), lambda i: (0, 0)),
        ],
        out_specs=[
            pl.BlockSpec((tm, nq), lambda i: (i, 0)),
            pl.BlockSpec((nk, tm), lambda i: (0, i)),
            pl.BlockSpec((tm, nk), lambda i: (i, 0)),
        ],
        out_shape=[
            jax.ShapeDtypeStruct((T, nq), jnp.bfloat16),
            jax.ShapeDtypeStruct((nk, T), jnp.bfloat16),
            jax.ShapeDtypeStruct((T, nk), jnp.bfloat16),
        ],
        compiler_params=_params(("parallel",)),
        name="qkv_proj",
    )(x2, g, w, b)


def _attn_kernel(q_ref, kt_ref, ktp_ref, v_ref, vp_ref, bias_ref, o_ref):
    first_tile = pl.program_id(1) == 0
    nblk = TQ // ATTN_BLOCK
    qi_io = lax.broadcasted_iota(jnp.int32, (ATTN_BLOCK, 2 * ATTN_BLOCK), 0)
    kj_io = lax.broadcasted_iota(jnp.int32, (ATTN_BLOCK, 2 * ATTN_BLOCK), 1)
    dist = qi_io + ATTN_BLOCK - kj_io
    band = (dist >= 0) & (dist < WINDOW)
    sink_col = kj_io == 0
    kt_keep = lax.broadcasted_iota(jnp.int32, (N_KV_HEADS * HEAD_DIM, 2 * ATTN_BLOCK), 1) > 0
    v_keep = lax.broadcasted_iota(jnp.int32, (2 * ATTN_BLOCK, N_KV_HEADS * HEAD_DIM), 0) > 0
    low_half = lax.broadcasted_iota(jnp.int32, (ATTN_BLOCK, LANES), 1) < HEAD_DIM
    for qi in range(nblk):
        r0 = qi * ATTN_BLOCK
        if qi == 0:
            kt_blk = jnp.concatenate([ktp_ref[...], kt_ref[:, 0:ATTN_BLOCK]], axis=1)
            v_blk = jnp.concatenate([vp_ref[...], v_ref[0:ATTN_BLOCK, :]], axis=0)
            mask = (band & (jnp.logical_not(first_tile) | (kj_io >= ATTN_BLOCK))) | sink_col
        else:
            kt_blk = kt_ref[:, r0 - ATTN_BLOCK:r0 + ATTN_BLOCK]
            v_blk = v_ref[r0 - ATTN_BLOCK:r0 + ATTN_BLOCK, :]
            mask = band | sink_col
        kt_blk = jnp.where(kt_keep, kt_blk, jnp.zeros_like(kt_blk))
        v_blk = jnp.where(v_keep, v_blk, jnp.zeros_like(v_blk))
        for m in range(GROUP):
            qg = q_ref[r0:r0 + ATTN_BLOCK, m * LANES:(m + 1) * LANES]
            halves = []
            for half in range(N_KV_HEADS):
                h = m + GROUP * half
                keep = low_half if half == 0 else jnp.logical_not(low_half)
                qpad = jnp.where(keep, qg, jnp.zeros_like(qg))
                s = jnp.dot(qpad, kt_blk, preferred_element_type=jnp.float32)
                s = jnp.where(mask, s + bias_ref[h], -1e30)
                mx = jnp.max(s, axis=-1, keepdims=True)
                p = jnp.exp(s - mx)
                den = jnp.sum(p, axis=-1, keepdims=True)
                pv = jnp.dot(p.astype(jnp.bfloat16), v_blk, preferred_element_type=jnp.float32)
                halves.append(pv * (1.0 / den))
            og = jnp.where(low_half, halves[0], halves[1])
            o_ref[r0:r0 + ATTN_BLOCK, m * LANES:(m + 1) * LANES] = og.astype(jnp.bfloat16)


def _attn_call(q, kt, v, bias, batch, seq):
    T = q.shape[0]
    nq = N_Q_HEADS * HEAD_DIM
    nk = N_KV_HEADS * HEAD_DIM
    tiles = seq // TQ
    per = TQ // ATTN_BLOCK

    def cur(b, j):
        return b * tiles + j

    def prev(b, j):
        return jnp.maximum((b * tiles + j) * per - 1, b * tiles * per)

    return pl.pallas_call(
        _attn_kernel,
        grid=(batch, tiles),
        in_specs=[
            pl.BlockSpec((TQ, nq), lambda b, j: (cur(b, j), 0)),
            pl.BlockSpec((nk, TQ), lambda b, j: (0, cur(b, j))),
            pl.BlockSpec((nk, ATTN_BLOCK), lambda b, j: (0, prev(b, j))),
            pl.BlockSpec((TQ, nk), lambda b, j: (cur(b, j), 0)),
            pl.BlockSpec((ATTN_BLOCK, nk), lambda b, j: (prev(b, j), 0)),
            pl.BlockSpec((N_Q_HEADS, ATTN_BLOCK, 2 * ATTN_BLOCK), lambda b, j: (0, 0, 0)),
        ],
        out_specs=pl.BlockSpec((TQ, nq), lambda b, j: (cur(b, j), 0)),
        out_shape=jax.ShapeDtypeStruct((T, nq), jnp.bfloat16),
        compiler_params=_params(("parallel", "parallel")),
        name="swa_attention",
    )(q, kt, kt, v, v, bias)


def _router_epilogue(h, gffn_ref, wr_ref, brt_ref, carry_ref,
                     h_out_ref, xrows_ref, ri_ref, rf_ref, cnt_ref):
    tm = h.shape[0]
    h_out_ref[...] = h
    xn = _rms(h, gffn_ref[...])
    xh = xn.astype(jnp.bfloat16)
    xbits = lax.bitcast_convert_type(xh.astype(jnp.float32), jnp.uint32)
    for c in range(XS_CHUNKS):
        lo = xbits[:, (2 * c) * LANES:(2 * c + 1) * LANES] >> 16
        hi = xbits[:, (2 * c + 1) * LANES:(2 * c + 2) * LANES]
        xrows_ref[pl.ds(c, tm, stride=XS_CHUNKS), :] = lo | hi

    xl = (xn - xh.astype(jnp.float32)).astype(jnp.bfloat16)
    wcat = wr_ref[...]
    prod = jnp.dot(xh, wcat, preferred_element_type=jnp.float32)
    logits = (prod[:, :LANES] + prod[:, LANES:]) + jnp.dot(
        xl, wcat[:, :LANES], preferred_element_type=jnp.float32)
    lt = logits.T[:ROUTER_ROWS, :] + brt_ref[...]

    gl = [lt[g:g + 1, :] for g in range(N_GROUPS)]
    gmax = functools.reduce(jnp.maximum, gl)
    gexp = [jnp.exp(x - gmax) for x in gl]
    gsum = functools.reduce(lambda a, b: a + b, gexp)
    gprob = [x / gsum for x in gexp]
    g_prob = functools.reduce(jnp.maximum, gprob)
    g_idx = jnp.full(g_prob.shape, N_GROUPS - 1, jnp.int32)
    for g in range(N_GROUPS - 2, -1, -1):
        g_idx = jnp.where(gprob[g] == g_prob, g, g_idx)

    el = []
    for j in range(EXPERTS_PER_GROUP):
        x = lt[SUBLANES + j:SUBLANES + j + 1, :]
        for g in range(1, N_GROUPS):
            r = SUBLANES + g * EXPERTS_PER_GROUP + j
            x = jnp.where(g_idx == g, lt[r:r + 1, :], x)
        el.append(x)
    emax = functools.reduce(jnp.maximum, el)
    eexp = [jnp.exp(x - emax) for x in el]
    esum = functools.reduce(lambda a, b: a + b, eexp)
    eprob = [x / esum for x in eexp]
    p1 = functools.reduce(jnp.maximum, eprob)
    i1 = jnp.full(p1.shape, EXPERTS_PER_GROUP - 1, jnp.int32)
    for j in range(EXPERTS_PER_GROUP - 2, -1, -1):
        i1 = jnp.where(eprob[j] == p1, j, i1)
    rest = [jnp.where(i1 == j, -1.0, eprob[j]) for j in range(EXPERTS_PER_GROUP)]
    p2 = functools.reduce(jnp.maximum, rest)
    i2 = jnp.full(p2.shape, EXPERTS_PER_GROUP - 1, jnp.int32)
    for j in range(EXPERTS_PER_GROUP - 2, -1, -1):
        i2 = jnp.where(rest[j] == p2, j, i2)
    psum = p1 + p2
    gate0 = g_prob * (p1 / psum)
    gate1 = g_prob * (p2 / psum)
    e0 = g_idx * EXPERTS_PER_GROUP + i1
    e1 = g_idx * EXPERTS_PER_GROUP + i2

    eio = lax.broadcasted_iota(jnp.int32, (N_EXPERTS, tm), 0)
    oh0 = (eio == e0).astype(jnp.float32)
    oh1 = (eio == e1).astype(jnp.float32)
    both = oh0 + oh1
    tr = lax.broadcasted_iota(jnp.int32, (tm, tm), 0)
    tc = lax.broadcasted_iota(jnp.int32, (tm, tm), 1)
    upper = (tr < tc).astype(jnp.bfloat16)
    before = jnp.dot(both.astype(jnp.bfloat16), upper, preferred_element_type=jnp.float32)
    before = before + carry_ref[...]
    rank0 = jnp.sum(oh0 * before, axis=0, keepdims=True)
    rank1 = jnp.sum(oh1 * (before + oh0), axis=0, keepdims=True)
    carry_ref[...] = carry_ref[...] + jnp.sum(both, axis=1, keepdims=True)
    cnt_ref[...] = jnp.broadcast_to(carry_ref[...], cnt_ref.shape)

    zi = jnp.zeros_like(e0)
    ri_ref[...] = jnp.concatenate(
        [e0, e1, rank0.astype(jnp.int32), rank1.astype(jnp.int32), zi, zi, zi, zi], axis=0)
    zf = jnp.zeros_like(gate0)
    rf_ref[...] = jnp.concatenate([gate0, gate1, zf, zf, zf, zf, zf, zf], axis=0)


def _epilogue_out_specs(tm):
    return [
        pl.BlockSpec((tm, D_MODEL), lambda i: (i, 0)),
        pl.BlockSpec((tm * XS_CHUNKS, LANES), lambda i: (i, 0)),
        pl.BlockSpec((SUBLANES, tm), lambda i: (0, i)),
        pl.BlockSpec((SUBLANES, tm), lambda i: (0, i)),
        pl.BlockSpec((N_EXPERTS, LANES), lambda i: (0, 0)),
    ]


def _epilogue_out_shapes(T):
    return [
        jax.ShapeDtypeStruct((T, D_MODEL), jnp.float32),
        jax.ShapeDtypeStruct((T * XS_CHUNKS, LANES), jnp.uint32),
        jax.ShapeDtypeStruct((SUBLANES, T), jnp.int32),
        jax.ShapeDtypeStruct((SUBLANES, T), jnp.float32),
        jax.ShapeDtypeStruct((N_EXPERTS, LANES), jnp.float32),
    ]


def _epilogue_in_specs():
    return [
        pl.BlockSpec((1, D_MODEL), lambda i: (0, 0)),
        pl.BlockSpec((D_MODEL, 2 * LANES), lambda i: (0, 0)),
        pl.BlockSpec((ROUTER_ROWS, 1), lambda i: (0, 0)),
    ]


def _attn_out_kernel(a_ref, w_ref, b_ref, res_ref, gffn_ref, wr_ref, brt_ref,
                     h_out_ref, xrows_ref, ri_ref, rf_ref, cnt_ref, carry_ref):
    @pl.when(pl.program_id(0) == 0)
    def _():
        carry_ref[...] = jnp.zeros_like(carry_ref)

    mix = jnp.dot(a_ref[...], w_ref[...], preferred_element_type=jnp.float32) + b_ref[...]
    h = res_ref[...] + mix
    _router_epilogue(h, gffn_ref, wr_ref, brt_ref, carry_ref,
                     h_out_ref, xrows_ref, ri_ref, rf_ref, cnt_ref)


def _attn_out_call(a, w, b, res, gffn, wr, brt):
    T = a.shape[0]
    tm = TM_DENSE
    return pl.pallas_call(
        _attn_out_kernel,
        grid=(T // tm,),
        in_specs=[
            pl.BlockSpec((tm, D_MODEL), lambda i: (i, 0)),
            pl.BlockSpec((D_MODEL, D_MODEL), lambda i: (0, 0)),
            pl.BlockSpec((1, D_MODEL), lambda i: (0, 0)),
            pl.BlockSpec((tm, D_MODEL), lambda i: (i, 0)),
        ] + _epilogue_in_specs(),
        out_specs=_epilogue_out_specs(tm),
        out_shape=_epilogue_out_shapes(T),
        scratch_shapes=[pltpu.VMEM((N_EXPERTS, 1), jnp.float32)],
        compiler_params=_params(("arbitrary",)),
        name="attn_out_router",
    )(a, w, b, res, gffn, wr, brt)


def _conv_mixer_kernel(tiles_per_seq, x_ref, g_ref, win_ref, cw_ref, wout_ref,
                       gffn_ref, wr_ref, brt_ref,
                       h_out_ref, xrows_ref, ri_ref, rf_ref, cnt_ref, carry_ref, tail_ref):
    i = pl.program_id(0)
    tm = x_ref.shape[0]

    @pl.when(i == 0)
    def _():
        carry_ref[...] = jnp.zeros_like(carry_ref)

    @pl.when(i % tiles_per_seq == 0)
    def _():
        tail_ref[...] = jnp.zeros_like(tail_ref)

    x = x_ref[...]
    xn = _rms(x, g_ref[...]).astype(jnp.bfloat16)
    bcu = jnp.dot(xn, win_ref[...], preferred_element_type=jnp.float32)
    b_gate = bcu[:, :D_MODEL]
    z = bcu[:, D_MODEL:2 * D_MODEL] * bcu[:, 2 * D_MODEL:]
    row = lax.broadcasted_iota(jnp.int32, (tm, D_MODEL), 0)
    tail = tail_ref[...]
    t1 = tail[SUBLANES - 1:SUBLANES, :]
    t2 = tail[SUBLANES - 2:SUBLANES - 1, :]
    z1 = jnp.where(row == 0, t1, pltpu.roll(z, 1, axis=0))
    z2 = jnp.where(row == 0, t2, jnp.where(row == 1, t1, pltpu.roll(z, 2, axis=0)))
    tail_ref[...] = z[tm - SUBLANES:, :]
    conv = z2 * cw_ref[0:1, :]
    conv = conv + z1 * cw_ref[1:2, :]
    conv = conv + z * cw_ref[2:3, :]
    gated = (b_gate * conv).astype(jnp.bfloat16)
    h = x + jnp.dot(gated, wout_ref[...], preferred_element_type=jnp.float32)
    _router_epilogue(h, gffn_ref, wr_ref, brt_ref, carry_ref,
                     h_out_ref, xrows_ref, ri_ref, rf_ref, cnt_ref)


def _conv_mixer_call(x2, g, win, cw, wout, gffn, wr, brt, seq):
    T = x2.shape[0]
    tm = TM_MIX1
    return pl.pallas_call(
        functools.partial(_conv_mixer_kernel, seq // tm),
        grid=(T // tm,),
        in_specs=[
            pl.BlockSpec((tm, D_MODEL), lambda i: (i, 0)),
            pl.BlockSpec((1, D_MODEL), lambda i: (0, 0)),
            pl.BlockSpec((D_MODEL, 3 * D_MODEL), lambda i: (0, 0)),
            pl.BlockSpec((CONV_WIDTH, D_MODEL), lambda i: (0, 0)),
            pl.BlockSpec((D_MODEL, D_MODEL), lambda i: (0, 0)),
        ] + _epilogue_in_specs(),
        out_specs=_epilogue_out_specs(tm),
        out_shape=_epilogue_out_shapes(T),
        scratch_shapes=[pltpu.VMEM((N_EXPERTS, 1), jnp.float32),
                        pltpu.VMEM((SUBLANES, D_MODEL), jnp.float32)],
        compiler_params=_params(("arbitrary",)),
        name="conv_mixer_router",
    )(x2, g, win, cw, wout, gffn, wr, brt)


def _row_copy(src_hbm, src_row, dst_hbm, dst_row, sem, chunks):
    return pltpu.make_async_copy(
        src_hbm.at[pl.ds(pl.multiple_of(src_row, chunks), chunks)],
        dst_hbm.at[pl.ds(pl.multiple_of(dst_row, chunks), chunks)], sem)


def _slot_rows_kernel(ri_ref, pstart_ref, o_ref):
    tb = ri_ref.shape[1]
    eio = lax.broadcasted_iota(jnp.int32, (N_EXPERTS, tb), 0)
    pstart = pstart_ref[...]
    slots = []
    for k in range(2):
        start = jnp.sum(jnp.where(eio == ri_ref[k:k + 1, :], pstart, 0), axis=0, keepdims=True)
        slots.append(start + ri_ref[2 + k:3 + k, :])
    zi = jnp.zeros_like(slots[0])
    o_ref[...] = jnp.concatenate(
        [s * XS_CHUNKS for s in slots] + [s * ROW_CHUNKS for s in slots] + [zi] * (SUBLANES - 4),
        axis=0)


def _slot_rows_call(ri, pstart):
    T = ri.shape[1]
    tb = TM_SLOT_ROWS
    return pl.pallas_call(
        _slot_rows_kernel,
        grid=(T // tb,),
        in_specs=[pl.BlockSpec((SUBLANES, tb), lambda i: (0, i)),
                  pl.BlockSpec((N_EXPERTS, 1), lambda i: (0, 0))],
        out_specs=pl.BlockSpec((SUBLANES, tb), lambda i: (0, i)),
        out_shape=jax.ShapeDtypeStruct((SUBLANES, T), jnp.int32),
        compiler_params=_params(("parallel",)),
        name="moe_slot_rows",
    )(ri, pstart[:, None])


def _dispatch_kernel(d0_ref, d1_ref, padrow_ref, x_ref, xs_hbm, zero_ref, sem, zsem):
    i = pl.program_id(0)
    tile_rows = TM_EXPERT * XS_CHUNKS

    @pl.when(i == 0)
    def _():
        zero_ref[...] = jnp.zeros_like(zero_ref)

        def zcopy(e):
            return pltpu.make_async_copy(
                zero_ref,
                xs_hbm.at[pl.ds(pl.multiple_of(padrow_ref[e], tile_rows), tile_rows)], zsem)

        def zstart(e, c):
            @pl.when(padrow_ref[e] >= 0)
            def _():
                zcopy(e).start()
            return c

        def zwait(e, c):
            @pl.when(padrow_ref[e] >= 0)
            def _():
                zcopy(e).wait()
            return c

        lax.fori_loop(0, N_EXPERTS, zstart, 0)
        lax.fori_loop(0, N_EXPERTS, zwait, 0)

        def tcopy(b):
            return pltpu.make_async_copy(
                zero_ref, xs_hbm.at[pl.ds(pl.multiple_of(b * tile_rows, tile_rows), tile_rows)], zsem)

        n_used = padrow_ref[N_EXPERTS]
        n_blk = xs_hbm.shape[0] // tile_rows
        lax.fori_loop(n_used, n_blk, lambda b, c: (tcopy(b).start(), c)[1], 0)
        lax.fori_loop(n_used, n_blk, lambda b, c: (tcopy(b).wait(), c)[1], 0)

    def issue(t, c):
        src = t * XS_CHUNKS
        _row_copy(x_ref, src, xs_hbm, d0_ref[t], sem, XS_CHUNKS).start(priority=0)
        _row_copy(x_ref, src, xs_hbm, d1_ref[t], sem, XS_CHUNKS).start(priority=1)
        return c

    lax.fori_loop(0, TM_DISPATCH, issue, 0, unroll=DMA_UNROLL)
    for _ in range(2):
        pltpu.make_async_copy(x_ref, xs_hbm.at[pl.ds(0, TM_DISPATCH * XS_CHUNKS)], sem).wait()


def _dispatch_call(d0, d1, padrow, xrows, capacity):
    T = d0.shape[0]
    return pl.pallas_call(
        _dispatch_kernel,
        grid=(T // TM_DISPATCH,),
        in_specs=[
            pl.BlockSpec((TM_DISPATCH,), lambda i: (i,), memory_space=pltpu.SMEM),
            pl.BlockSpec((TM_DISPATCH,), lambda i: (i,), memory_space=pltpu.SMEM),
            pl.BlockSpec(memory_space=pltpu.SMEM),
            pl.BlockSpec((TM_DISPATCH * XS_CHUNKS, LANES), lambda i: (i, 0)),
        ],
        out_specs=pl.BlockSpec(memory_space=pl.ANY),
        out_shape=jax.ShapeDtypeStruct((capacity * XS_CHUNKS, LANES), jnp.uint32),
        scratch_shapes=[pltpu.VMEM((TM_EXPERT * XS_CHUNKS, LANES), jnp.uint32),
                        pltpu.SemaphoreType.DMA, pltpu.SemaphoreType.DMA],
        compiler_params=_params(("arbitrary",)),
        name="moe_dispatch",
    )(d0, d1, padrow, xrows)


def _expert_kernel(tstart_ref, ntile_ref, xs_hbm, wgu_ref, wd_ref, ys_hbm,
                   xbuf, ybuf, wgu_bf, wd_bf, sem_in, sem_out):
    nb = EXPERT_BUFS
    e = pl.program_id(0)
    tm = TM_EXPERT
    tile_rows = tm * ROW_CHUNKS
    t0 = tstart_ref[e]
    nt = ntile_ref[e]
    n_used = tstart_ref[N_EXPERTS - 1] + ntile_ref[N_EXPERTS - 1]

    def tile_at(ref, g, rows):
        return ref.at[pl.ds(pl.multiple_of(g * rows, rows), rows)]

    def in_copy(g, slot):
        return pltpu.make_async_copy(tile_at(xs_hbm, g, tm * XS_CHUNKS), xbuf.at[slot], sem_in.at[slot])

    def out_copy(g, slot):
        return pltpu.make_async_copy(ybuf.at[slot], tile_at(ys_hbm, g, tile_rows), sem_out.at[slot])

    @pl.when(e == 0)
    def _():
        for k in range(nb - 1):
            @pl.when(k < n_used)
            def _():
                in_copy(k, k).start(priority=1)

    @pl.when(nt > 0)
    def _():
        wgu_bf[...] = wgu_ref[0, 0].astype(jnp.bfloat16)
        wd_bf[...] = wd_ref[0, 0].astype(jnp.bfloat16)

    def tile_body(g, carry):
        slot = g % nb
        in_copy(g, slot).wait()

        @pl.when(g + (nb - 1) < n_used)
        def _():
            in_copy(g + (nb - 1), (g + (nb - 1)) % nb).start(priority=1)

        @pl.when(g >= nb)
        def _():
            out_copy(g - nb, slot).wait()

        cols = []
        for c in range(XS_CHUNKS):
            w = xbuf[slot, pl.ds(c, tm, stride=XS_CHUNKS), :]
            cols.append(lax.bitcast_convert_type(w << 16, jnp.float32).astype(jnp.bfloat16))
            cols.append(lax.bitcast_convert_type(w & jnp.uint32(0xFFFF0000), jnp.float32
                                                 ).astype(jnp.bfloat16))
        x = jnp.concatenate(cols, axis=-1)
        gu = jnp.dot(x, wgu_bf[...], preferred_element_type=jnp.float32)
        g_act = gu[:, :EXPERT_FF]
        u = gu[:, EXPERT_FF:]
        act = (g_act / (1.0 + jnp.exp(-g_act))) * u
        y = jnp.dot(act.astype(jnp.bfloat16), wd_bf[...], preferred_element_type=jnp.float32)
        for c in range(ROW_CHUNKS):
            ybuf[slot, pl.ds(c, tm, stride=ROW_CHUNKS), :] = y[:, c * LANES:(c + 1) * LANES]
        out_copy(g, slot).start(priority=1)
        return carry

    lax.fori_loop(t0, t0 + nt, tile_body, 0)

    @pl.when(e == N_EXPERTS - 1)
    def _():
        for k in range(1, nb + 1):
            @pl.when(n_used >= k)
            def _():
                out_copy(n_used - k, (n_used - k) % nb).wait()

        n_blk = ys_hbm.shape[0] // tile_rows
        ybuf[0] = jnp.zeros(ybuf.shape[1:], ybuf.dtype)
        lax.fori_loop(n_used, n_blk, lambda b, c: (out_copy(b, 0).start(), c)[1], 0)
        lax.fori_loop(n_used, n_blk, lambda b, c: (out_copy(b, 0).wait(), c)[1], 0)


def _expert_call(layer, tstart, ntile, xs, wgu, wd, capacity):
    tile_rows = TM_EXPERT * ROW_CHUNKS

    def w_map(e, ts, nt):
        return (layer, e, 0, 0)

    return pl.pallas_call(
        _expert_kernel,
        grid_spec=pltpu.PrefetchScalarGridSpec(
            num_scalar_prefetch=2,
            grid=(N_EXPERTS,),
            in_specs=[
                pl.BlockSpec(memory_space=pl.ANY),
                pl.BlockSpec((1, 1, D_MODEL, 2 * EXPERT_FF), w_map),
                pl.BlockSpec((1, 1, EXPERT_FF, D_MODEL), w_map),
            ],
            out_specs=pl.BlockSpec(memory_space=pl.ANY),
            scratch_shapes=[pltpu.VMEM((EXPERT_BUFS, TM_EXPERT * XS_CHUNKS, LANES), jnp.uint32),
                            pltpu.VMEM((EXPERT_BUFS, tile_rows, LANES), jnp.float32),
                            pltpu.VMEM((D_MODEL, 2 * EXPERT_FF), jnp.bfloat16),
                            pltpu.VMEM((EXPERT_FF, D_MODEL), jnp.bfloat16),
                            pltpu.SemaphoreType.DMA((EXPERT_BUFS,)),
                            pltpu.SemaphoreType.DMA((EXPERT_BUFS,))],
        ),
        out_shape=jax.ShapeDtypeStruct((capacity * ROW_CHUNKS, LANES), jnp.float32),
        compiler_params=_params(("arbitrary",)),
        name="moe_experts",
    )(tstart, ntile, xs, wgu, wd)


def _gathered_residual(tm, d0_ref, d1_ref, d0n_ref, d1n_ref, ys_hbm, h_ref, gate_ref,
                       y0_ref, y1_ref, sem):
    i = pl.program_id(0)
    slot = i % 2

    def issue_tile(d0r, d1r, s):
        def issue(t, c):
            dst = t * ROW_CHUNKS
            _row_copy(ys_hbm, d0r[t], y0_ref.at[s], dst, sem.at[s], ROW_CHUNKS).start(priority=0)
            _row_copy(ys_hbm, d1r[t], y1_ref.at[s], dst, sem.at[s], ROW_CHUNKS).start(priority=1)
            return c
        lax.fori_loop(0, tm, issue, 0, unroll=DMA_UNROLL)

    @pl.when(i == 0)
    def _():
        issue_tile(d0_ref, d1_ref, 0)

    @pl.when(i + 1 < pl.num_programs(0))
    def _():
        issue_tile(d0n_ref, d1n_ref, 1 - slot)

    for buf in (y0_ref, y1_ref):
        pltpu.make_async_copy(ys_hbm.at[pl.ds(0, tm * ROW_CHUNKS)], buf.at[slot], sem.at[slot]).wait()

    g0 = gate_ref[:, 0:1]
    g1 = gate_ref[:, 1:2]
    cols = []
    for c in range(ROW_CHUNKS):
        y0 = y0_ref[slot, pl.ds(c, tm, stride=ROW_CHUNKS), :]
        y1 = y1_ref[slot, pl.ds(c, tm, stride=ROW_CHUNKS), :]
        cols.append(h_ref[:, c * LANES:(c + 1) * LANES] + (y0 * g0 + y1 * g1))
    return jnp.concatenate(cols, axis=-1)


def _gather_in_specs(tm, n_tiles):
    last = n_tiles - 1
    return [
        pl.BlockSpec((tm,), lambda i: (i,), memory_space=pltpu.SMEM),
        pl.BlockSpec((tm,), lambda i: (i,), memory_space=pltpu.SMEM),
        pl.BlockSpec((tm,), lambda i: (jnp.minimum(i + 1, last),), memory_space=pltpu.SMEM),
        pl.BlockSpec((tm,), lambda i: (jnp.minimum(i + 1, last),), memory_space=pltpu.SMEM),
        pl.BlockSpec(memory_space=pl.ANY),
        pl.BlockSpec((tm, D_MODEL), lambda i: (i, 0)),
        pl.BlockSpec((tm, 2), lambda i: (i, 0)),
    ]


def _gather_scratch(tm):
    return [pltpu.VMEM((2, tm * ROW_CHUNKS, LANES), jnp.float32),
            pltpu.VMEM((2, tm * ROW_CHUNKS, LANES), jnp.float32),
            pltpu.SemaphoreType.DMA((2,))]


def _combine_kernel(final_norm, d0_ref, d1_ref, d0n_ref, d1n_ref, ys_hbm, h_ref, gate_ref, gfin_ref,
                    o_ref, y0_ref, y1_ref, sem):
    h = _gathered_residual(TM_COMBINE, d0_ref, d1_ref, d0n_ref, d1n_ref, ys_hbm, h_ref, gate_ref,
                           y0_ref, y1_ref, sem)
    if final_norm:
        h = _rms(h, gfin_ref[...])
    o_ref[...] = h


def _combine_call(d0, d1, ys, h, gates, gfin, final_norm):
    T = h.shape[0]
    tm = TM_COMBINE
    return pl.pallas_call(
        functools.partial(_combine_kernel, final_norm),
        grid=(T // tm,),
        in_specs=_gather_in_specs(tm, T // tm) + [pl.BlockSpec((1, D_MODEL), lambda i: (0, 0))],
        out_specs=pl.BlockSpec((tm, D_MODEL), lambda i: (i, 0)),
        out_shape=jax.ShapeDtypeStruct((T, D_MODEL), jnp.float32),
        scratch_shapes=_gather_scratch(tm),
        compiler_params=_params(("arbitrary",)),
        name="moe_combine",
    )(d0, d1, d0, d1, ys, h, gates, gfin)


def _group_heads_last(w):
    lead = w.shape[:-1]
    w = w.reshape(lead + (N_KV_HEADS, GROUP, HEAD_DIM))
    w = jnp.swapaxes(w, -3, -2)
    return w.reshape(lead + (N_Q_HEADS * HEAD_DIM,))


def _rel_bucket_table():
    qi = np.arange(ATTN_BLOCK)[:, None]
    kj = np.arange(2 * ATTN_BLOCK)[None, :]
    dist = np.maximum(qi + ATTN_BLOCK - kj, 0)
    max_exact = N_REL_BUCKETS // 2
    d = np.maximum(dist, max_exact).astype(np.float32)
    large = max_exact + (np.log(d / np.float32(max_exact)) / np.float32(math.log(REL_MAX_DISTANCE / max_exact))
                         * np.float32(N_REL_BUCKETS - max_exact)).astype(np.int32)
    large = np.minimum(large, N_REL_BUCKETS - 1)
    return np.where(dist < max_exact, dist, large).astype(np.int32)


def _rel_bias_blocks(rel_bias, sinks):
    onehot = (jnp.asarray(_rel_bucket_table())[..., None] == jnp.arange(N_REL_BUCKETS)).astype(jnp.float32)
    bias = jnp.einsum("qkb,bh->hqk", onehot, rel_bias.astype(jnp.float32),
                      precision=lax.Precision.HIGHEST)
    sink_col = jnp.broadcast_to(sinks.astype(jnp.float32)[:, None, None], (N_Q_HEADS, ATTN_BLOCK, 1))
    return jnp.concatenate([sink_col, bias[:, :, 1:]], axis=2)


def _router_weights(w_group, b_group, w_expert, b_expert):
    zw = jnp.zeros((D_MODEL, SUBLANES - N_GROUPS), jnp.float32)
    zw2 = jnp.zeros((D_MODEL, LANES - SUBLANES - N_EXPERTS), jnp.float32)
    wr = jnp.concatenate([w_group, zw, w_expert, zw2], axis=1)
    wh = wr.astype(jnp.bfloat16)
    wl = (wr - wh.astype(jnp.float32)).astype(jnp.bfloat16)
    wr = jnp.concatenate([wh, wl], axis=1)
    brt = jnp.concatenate([b_group, jnp.zeros((SUBLANES - N_GROUPS,), jnp.float32), b_expert])[:, None]
    return wr, brt


def _moe(layer, h, xrows, ri, rf, cnt, w_gate_up, w_down, gfin, final_norm):
    T = h.shape[0]
    tm = TM_EXPERT
    capacity = -(-(2 * T) // tm) * tm + N_EXPERTS * tm
    counts = cnt[:, 0].astype(jnp.int32)
    ntile = (counts + tm - 1) // tm
    tend = jnp.cumsum(ntile)
    pstart = (tend - ntile) * tm
    n_used = tend[-1:]
    padrow = jnp.concatenate([jnp.where(ntile > 0, (tend - 1) * (tm * XS_CHUNKS), -1), n_used]
                             ).astype(jnp.int32)

    dst = _slot_rows_call(ri, pstart)
    xs = _dispatch_call(dst[0], dst[1], padrow, xrows, capacity)
    ys = _expert_call(layer, (tend - ntile).astype(jnp.int32), ntile.astype(jnp.int32), xs,
                      w_gate_up, w_down, capacity)
    gates = rf[:2].T
    return _combine_call(dst[2], dst[3], ys, h, gates, gfin, final_norm)


def kernel(x, norm_mix, norm_ffn, final_norm, rel_bias, attn_w_qkv, attn_b_qkv, attn_w_o, attn_b_o,
           attn_sinks, conv_w_in, conv_w, conv_w_out, moe_w_group, moe_b_group, moe_w_expert,
           moe_b_expert, moe_w_gate_up, moe_w_down):
    B, S, D = x.shape
    T = B * S
    x2 = x.reshape(T, D)
    nq = N_Q_HEADS * HEAD_DIM
    scale = HEAD_DIM ** -0.5

    wqkv = attn_w_qkv[0]
    bqkv = attn_b_qkv[0]
    w_all = jnp.concatenate([_group_heads_last(wqkv[:, :nq]) * scale, wqkv[:, nq:]],
                            axis=1).astype(jnp.bfloat16)
    b_all = jnp.concatenate([_group_heads_last(bqkv[:nq]) * scale, bqkv[nq:]])[None, :]
    q, kt, v = _qkv_call(x2, norm_mix[0][None, :], w_all, b_all)
    a = _attn_call(q, kt, v, _rel_bias_blocks(rel_bias, attn_sinks[0]), B, S)
    wr, brt = _router_weights(moe_w_group[0], moe_b_group[0], moe_w_expert[0], moe_b_expert[0])
    w_o = _group_heads_last(attn_w_o[0].T).T.astype(jnp.bfloat16)
    h, xrows, ri, rf, cnt = _attn_out_call(
        a, w_o, attn_b_o[0][None, :], x2, norm_ffn[0][None, :], wr, brt)
    h = _moe(0, h, xrows, ri, rf, cnt, moe_w_gate_up, moe_w_down, final_norm[None, :], False)

    wr, brt = _router_weights(moe_w_group[1], moe_b_group[1], moe_w_expert[1], moe_b_expert[1])
    h, xrows, ri, rf, cnt = _conv_mixer_call(
        h, norm_mix[1][None, :], conv_w_in[0].astype(jnp.bfloat16), conv_w[0],
        conv_w_out[0].astype(jnp.bfloat16), norm_ffn[1][None, :], wr, brt, S)
    out = _moe(1, h, xrows, ri, rf, cnt, moe_w_gate_up, moe_w_down, final_norm[None, :], True)
    return out.reshape(B, S, D)
```

```python
import functools
import math

import numpy as np
import jax
import jax.numpy as jnp
from jax import lax
from jax.experimental import pallas as pl
from jax.experimental.pallas import tpu as pltpu

D_MODEL = 1024
N_Q_HEADS = 16
N_KV_HEADS = 2
HEAD_DIM = 64
GROUP = N_Q_HEADS // N_KV_HEADS
WINDOW = 128
ATTN_BLOCK = 128
N_REL_BUCKETS = 32
REL_MAX_DISTANCE = 128
CONV_WIDTH = 3
N_GROUPS = 4
EXPERTS_PER_GROUP = 8
N_EXPERTS = N_GROUPS * EXPERTS_PER_GROUP
EXPERT_FF = 512
RMS_EPS = 1e-5

LANES = 128
SUBLANES = 8
ROW_CHUNKS = D_MODEL // LANES
XS_CHUNKS = ROW_CHUNKS // 2
VMEM_LIMIT = 56 * 1024 * 1024

TM_DENSE = 1024
TQ = 1024
TM_MIX1 = 1024
TM_EXPERT = 512
TM_DISPATCH = 1024
TM_COMBINE = 256
DMA_UNROLL = 8
TM_SLOT_ROWS = 4096
EXPERT_BUFS = 4
ROUTER_ROWS = 40


def _rms(x, g):
    return x * lax.rsqrt(jnp.mean(x * x, axis=-1, keepdims=True) + RMS_EPS) * g


def _params(sem):
    return pltpu.CompilerParams(dimension_semantics=sem, vmem_limit_bytes=VMEM_LIMIT)


def _qkv_kernel(x_ref, g_ref, w_ref, b_ref, q_ref, kt_ref, v_ref):
    xn = _rms(x_ref[...], g_ref[...]).astype(jnp.bfloat16)
    out = jnp.dot(xn, w_ref[...], preferred_element_type=jnp.float32) + b_ref[...]
    nq = N_Q_HEADS * HEAD_DIM
    nk = N_KV_HEADS * HEAD_DIM
    q_ref[...] = out[:, :nq].astype(jnp.bfloat16)
    kt_ref[...] = out[:, nq:nq + nk].T.astype(jnp.bfloat16)
    v_ref[...] = out[:, nq + nk:].astype(jnp.bfloat16)


def _qkv_call(x2, g, w, b):
    T = x2.shape[0]
    nq = N_Q_HEADS * HEAD_DIM
    nk = N_KV_HEADS * HEAD_DIM
    tm = TM_DENSE
    return pl.pallas_call(
        _qkv_kernel,
        grid=(T // tm,),
        in_specs=[
            pl.BlockSpec((tm, D_MODEL), lambda i: (i, 0)),
            pl.BlockSpec((1, D_MODEL), lambda i: (0, 0)),
            pl.BlockSpec((D_MODEL, nq + 2 * nk), lambda i: (0, 0)),
            pl.BlockSpec((1, nq + 2 * nk), lambda i: (0, 0)),
        ],
        out_specs=[
            pl.BlockSpec((tm, nq), lambda i: (i, 0)),
            pl.BlockSpec((nk, tm), lambda i: (0, i)),
            pl.BlockSpec((tm, nk), lambda i: (i, 0)),
        ],
        out_shape=[
            jax.ShapeDtypeStruct((T, nq), jnp.bfloat16),
            jax.ShapeDtypeStruct((nk, T), jnp.bfloat16),
            jax.ShapeDtypeStruct((T, nk), jnp.bfloat16),
        ],
        compiler_params=_params(("parallel",)),
        name="qkv_proj",
    )(x2, g, w, b)


def _attn_kernel(q_ref, kt_ref, ktp_ref, v_ref, vp_ref, bias_ref, o_ref):
    first_tile = pl.program_id(1) == 0
    nblk = TQ // ATTN_BLOCK
    qi_io = lax.broadcasted_iota(jnp.int32, (ATTN_BLOCK, 2 * ATTN_BLOCK), 0)
    kj_io = lax.broadcasted_iota(jnp.int32, (ATTN_BLOCK, 2 * ATTN_BLOCK), 1)
    dist = qi_io + ATTN_BLOCK - kj_io
    band = (dist >= 0) & (dist < WINDOW)
    sink_col = kj_io == 0
    kt_keep = lax.broadcasted_iota(jnp.int32, (N_KV_HEADS * HEAD_DIM, 2 * ATTN_BLOCK), 1) > 0
    v_keep = lax.broadcasted_iota(jnp.int32, (2 * ATTN_BLOCK, N_KV_HEADS * HEAD_DIM), 0) > 0
    low_half = lax.broadcasted_iota(jnp.int32, (ATTN_BLOCK, LANES), 1) < HEAD_DIM
    for qi in range(nblk):
        r0 = qi * ATTN_BLOCK
        if qi == 0:
            kt_blk = jnp.concatenate([ktp_ref[...], kt_ref[:, 0:ATTN_BLOCK]], axis=1)
            v_blk = jnp.concatenate([vp_ref[...], v_ref[0:ATTN_BLOCK, :]], axis=0)
            mask = (band & (jnp.logical_not(first_tile) | (kj_io >= ATTN_BLOCK))) | sink_col
        else:
            kt_blk = kt_ref[:, r0 - ATTN_BLOCK:r0 + ATTN_BLOCK]
            v_blk = v_ref[r0 - ATTN_BLOCK:r0 + ATTN_BLOCK, :]
            mask = band | sink_col
        kt_blk = jnp.where(kt_keep, kt_blk, jnp.zeros_like(kt_blk))
        v_blk = jnp.where(v_keep, v_blk, jnp.zeros_like(v_blk))
        for m in range(GROUP):
            qg = q_ref[r0:r0 + ATTN_BLOCK, m * LANES:(m + 1) * LANES]
            halves = []
            for half in range(N_KV_HEADS):
                h = m + GROUP * half
                keep = low_half if half == 0 else jnp.logical_not(low_half)
                qpad = jnp.where(keep, qg, jnp.zeros_like(qg))
                s = jnp.dot(qpad, kt_blk, preferred_element_type=jnp.float32)
                s = jnp.where(mask, s + bias_ref[h], -1e30)
                mx = jnp.max(s, axis=-1, keepdims=True)
                p = jnp.exp(s - mx)
                den = jnp.sum(p, axis=-1, keepdims=True)
                pv = jnp.dot(p.astype(jnp.bfloat16), v_blk, preferred_element_type=jnp.float32)
                halves.append(pv * (1.0 / den))
            og = jnp.where(low_half, halves[0], halves[1])
            o_ref[r0:r0 + ATTN_BLOCK, m * LANES:(m + 1) * LANES] = og.astype(jnp.bfloat16)


def _attn_call(q, kt, v, bias, batch, seq):
    T = q.shape[0]
    nq = N_Q_HEADS * HEAD_DIM
    nk = N_KV_HEADS * HEAD_DIM
    tiles = seq // TQ
    per = TQ // ATTN_BLOCK

    def cur(b, j):
        return b * tiles + j

    def prev(b, j):
        return jnp.maximum((b * tiles + j) * per - 1, b * tiles * per)

    return pl.pallas_call(
        _attn_kernel,
        grid=(batch, tiles),
        in_specs=[
            pl.BlockSpec((TQ, nq), lambda b, j: (cur(b, j), 0)),
            pl.BlockSpec((nk, TQ), lambda b, j: (0, cur(b, j))),
            pl.BlockSpec((nk, ATTN_BLOCK), lambda b, j: (0, prev(b, j))),
            pl.BlockSpec((TQ, nk), lambda b, j: (cur(b, j), 0)),
            pl.BlockSpec((ATTN_BLOCK, nk), lambda b, j: (prev(b, j), 0)),
            pl.BlockSpec((N_Q_HEADS, ATTN_BLOCK, 2 * ATTN_BLOCK), lambda b, j: (0, 0, 0)),
        ],
        out_specs=pl.BlockSpec((TQ, nq), lambda b, j: (cur(b, j), 0)),
        out_shape=jax.ShapeDtypeStruct((T, nq), jnp.bfloat16),
        compiler_params=_params(("parallel", "parallel")),
        name="swa_attention",
    )(q, kt, kt, v, v, bias)


def _router_epilogue(h, gffn_ref, wr_ref, brt_ref, carry_ref,
                     h_out_ref, xrows_ref, ri_ref, rf_ref, cnt_ref):
    tm = h.shape[0]
    h_out_ref[...] = h
    xn = _rms(h, gffn_ref[...])
    xh = xn.astype(jnp.bfloat16)
    xbits = lax.bitcast_convert_type(xh.astype(jnp.float32), jnp.uint32)
    for c in range(XS_CHUNKS):
        lo = xbits[:, (2 * c) * LANES:(2 * c + 1) * LANES] >> 16
        hi = xbits[:, (2 * c + 1) * LANES:(2 * c + 2) * LANES]
        xrows_ref[pl.ds(c, tm, stride=XS_CHUNKS), :] = lo | hi

    xl = (xn - xh.astype(jnp.float32)).astype(jnp.bfloat16)
    wcat = wr_ref[...]
    prod = jnp.dot(xh, wcat, preferred_element_type=jnp.float32)
    logits = (prod[:, :LANES] + prod[:, LANES:]) + jnp.dot(
        xl, wcat[:, :LANES], preferred_element_type=jnp.float32)
    lt = logits.T[:ROUTER_ROWS, :] + brt_ref[...]

    gl = [lt[g:g + 1, :] for g in range(N_GROUPS)]
    gmax = functools.reduce(jnp.maximum, gl)
    gexp = [jnp.exp(x - gmax) for x in gl]
    gsum = functools.reduce(lambda a, b: a + b, gexp)
    gprob = [x / gsum for x in gexp]
    g_prob = functools.reduce(jnp.maximum, gprob)
    g_idx = jnp.full(g_prob.shape, N_GROUPS - 1, jnp.int32)
    for g in range(N_GROUPS - 2, -1, -1):
        g_idx = jnp.where(gprob[g] == g_prob, g, g_idx)

    el = []
    for j in range(EXPERTS_PER_GROUP):
        x = lt[SUBLANES + j:SUBLANES + j + 1, :]
        for g in range(1, N_GROUPS):
            r = SUBLANES + g * EXPERTS_PER_GROUP + j
            x = jnp.where(g_idx == g, lt[r:r + 1, :], x)
        el.append(x)
    emax = functools.reduce(jnp.maximum, el)
    eexp = [jnp.exp(x - emax) for x in el]
    esum = functools.reduce(lambda a, b: a + b, eexp)
    eprob = [x / esum for x in eexp]
    p1 = functools.reduce(jnp.maximum, eprob)
    i1 = jnp.full(p1.shape, EXPERTS_PER_GROUP - 1, jnp.int32)
    for j in range(EXPERTS_PER_GROUP - 2, -1, -1):
        i1 = jnp.where(eprob[j] == p1, j, i1)
    rest = [jnp.where(i1 == j, -1.0, eprob[j]) for j in range(EXPERTS_PER_GROUP)]
    p2 = functools.reduce(jnp.maximum, rest)
    i2 = jnp.full(p2.shape, EXPERTS_PER_GROUP - 1, jnp.int32)
    for j in range(EXPERTS_PER_GROUP - 2, -1, -1):
        i2 = jnp.where(rest[j] == p2, j, i2)
    psum = p1 + p2
    gate0 = g_prob * (p1 / psum)
    gate1 = g_prob * (p2 / psum)
    e0 = g_idx * EXPERTS_PER_GROUP + i1
    e1 = g_idx * EXPERTS_PER_GROUP + i2

    eio = lax.broadcasted_iota(jnp.int32, (N_EXPERTS, tm), 0)
    oh0 = (eio == e0).astype(jnp.float32)
    oh1 = (eio == e1).astype(jnp.float32)
    both = oh0 + oh1
    tr = lax.broadcasted_iota(jnp.int32, (tm, tm), 0)
    tc = lax.broadcasted_iota(jnp.int32, (tm, tm), 1)
    upper = (tr < tc).astype(jnp.bfloat16)
    before = jnp.dot(both.astype(jnp.bfloat16), upper, preferred_element_type=jnp.float32)
    before = before + carry_ref[...]
    rank0 = jnp.sum(oh0 * before, axis=0, keepdims=True)
    rank1 = jnp.sum(oh1 * (before + oh0), axis=0, keepdims=True)
    carry_ref[...] = carry_ref[...] + jnp.sum(both, axis=1, keepdims=True)
    cnt_ref[...] = jnp.broadcast_to(carry_ref[...], cnt_ref.shape)

    zi = jnp.zeros_like(e0)
    ri_ref[...] = jnp.concatenate(
        [e0, e1, rank0.astype(jnp.int32), rank1.astype(jnp.int32), zi, zi, zi, zi], axis=0)
    zf = jnp.zeros_like(gate0)
    rf_ref[...] = jnp.concatenate([gate0, gate1, zf, zf, zf, zf, zf, zf], axis=0)


def _epilogue_out_specs(tm):
    return [
        pl.BlockSpec((tm, D_MODEL), lambda i: (i, 0)),
        pl.BlockSpec((tm * XS_CHUNKS, LANES), lambda i: (i, 0)),
        pl.BlockSpec((SUBLANES, tm), lambda i: (0, i)),
        pl.BlockSpec((SUBLANES, tm), lambda i: (0, i)),
        pl.BlockSpec((N_EXPERTS, LANES), lambda i: (0, 0)),
    ]


def _epilogue_out_shapes(T):
    return [
        jax.ShapeDtypeStruct((T, D_MODEL), jnp.float32),
        jax.ShapeDtypeStruct((T * XS_CHUNKS, LANES), jnp.uint32),
        jax.ShapeDtypeStruct((SUBLANES, T), jnp.int32),
        jax.ShapeDtypeStruct((SUBLANES, T), jnp.float32),
        jax.ShapeDtypeStruct((N_EXPERTS, LANES), jnp.float32),
    ]


def _epilogue_in_specs():
    return [
        pl.BlockSpec((1, D_MODEL), lambda i: (0, 0)),
        pl.BlockSpec((D_MODEL, 2 * LANES), lambda i: (0, 0)),
        pl.BlockSpec((ROUTER_ROWS, 1), lambda i: (0, 0)),
    ]


def _attn_out_kernel(a_ref, w_ref, b_ref, res_ref, gffn_ref, wr_ref, brt_ref,
                     h_out_ref, xrows_ref, ri_ref, rf_ref, cnt_ref, carry_ref):
    @pl.when(pl.program_id(0) == 0)
    def _():
        carry_ref[...] = jnp.zeros_like(carry_ref)

    mix = jnp.dot(a_ref[...], w_ref[...], preferred_element_type=jnp.float32) + b_ref[...]
    h = res_ref[...] + mix
    _router_epilogue(h, gffn_ref, wr_ref, brt_ref, carry_ref,
                     h_out_ref, xrows_ref, ri_ref, rf_ref, cnt_ref)


def _attn_out_call(a, w, b, res, gffn, wr, brt):
    T = a.shape[0]
    tm = TM_DENSE
    return pl.pallas_call(
        _attn_out_kernel,
        grid=(T // tm,),
        in_specs=[
            pl.BlockSpec((tm, D_MODEL), lambda i: (i, 0)),
            pl.BlockSpec((D_MODEL, D_MODEL), lambda i: (0, 0)),
            pl.BlockSpec((1, D_MODEL), lambda i: (0, 0)),
            pl.BlockSpec((tm, D_MODEL), lambda i: (i, 0)),
        ] + _epilogue_in_specs(),
        out_specs=_epilogue_out_specs(tm),
        out_shape=_epilogue_out_shapes(T),
        scratch_shapes=[pltpu.VMEM((N_EXPERTS, 1), jnp.float32)],
        compiler_params=_params(("arbitrary",)),
        name="attn_out_router",
    )(a, w, b, res, gffn, wr, brt)


def _conv_mixer_kernel(tiles_per_seq, x_ref, g_ref, win_ref, cw_ref, wout_ref,
                       gffn_ref, wr_ref, brt_ref,
                       h_out_ref, xrows_ref, ri_ref, rf_ref, cnt_ref, carry_ref, tail_ref):
    i = pl.program_id(0)
    tm = x_ref.shape[0]

    @pl.when(i == 0)
    def _():
        carry_ref[...] = jnp.zeros_like(carry_ref)

    @pl.when(i % tiles_per_seq == 0)
    def _():
        tail_ref[...] = jnp.zeros_like(tail_ref)

    x = x_ref[...]
    xn = _rms(x, g_ref[...]).astype(jnp.bfloat16)
    bcu = jnp.dot(xn, win_ref[...], preferred_element_type=jnp.float32)
    b_gate = bcu[:, :D_MODEL]
    z = bcu[:, D_MODEL:2 * D_MODEL] * bcu[:, 2 * D_MODEL:]
    row = lax.broadcasted_iota(jnp.int32, (tm, D_MODEL), 0)
    tail = tail_ref[...]
    t1 = tail[SUBLANES - 1:SUBLANES, :]
    t2 = tail[SUBLANES - 2:SUBLANES - 1, :]
    z1 = jnp.where(row == 0, t1, pltpu.roll(z, 1, axis=0))
    z2 = jnp.where(row == 0, t2, jnp.where(row == 1, t1, pltpu.roll(z, 2, axis=0)))
    tail_ref[...] = z[tm - SUBLANES:, :]
    conv = z2 * cw_ref[0:1, :]
    conv = conv + z1 * cw_ref[1:2, :]
    conv = conv + z * cw_ref[2:3, :]
    gated = (b_gate * conv).astype(jnp.bfloat16)
    h = x + jnp.dot(gated, wout_ref[...], preferred_element_type=jnp.float32)
    _router_epilogue(h, gffn_ref, wr_ref, brt_ref, carry_ref,
                     h_out_ref, xrows_ref, ri_ref, rf_ref, cnt_ref)


def _conv_mixer_call(x2, g, win, cw, wout, gffn, wr, brt, seq):
    T = x2.shape[0]
    tm = TM_MIX1
    return pl.pallas_call(
        functools.partial(_conv_mixer_kernel, seq // tm),
        grid=(T // tm,),
        in_specs=[
            pl.BlockSpec((tm, D_MODEL), lambda i: (i, 0)),
            pl.BlockSpec((1, D_MODEL), lambda i: (0, 0)),
            pl.BlockSpec((D_MODEL, 3 * D_MODEL), lambda i: (0, 0)),
            pl.BlockSpec((CONV_WIDTH, D_MODEL), lambda i: (0, 0)),
            pl.BlockSpec((D_MODEL, D_MODEL), lambda i: (0, 0)),
        ] + _epilogue_in_specs(),
        out_specs=_epilogue_out_specs(tm),
        out_shape=_epilogue_out_shapes(T),
        scratch_shapes=[pltpu.VMEM((N_EXPERTS, 1), jnp.float32),
                        pltpu.VMEM((SUBLANES, D_MODEL), jnp.float32)],
        compiler_params=_params(("arbitrary",)),
        name="conv_mixer_router",
    )(x2, g, win, cw, wout, gffn, wr, brt)


def _row_copy(src_hbm, src_row, dst_hbm, dst_row, sem, chunks):
    return pltpu.make_async_copy(
        src_hbm.at[pl.ds(pl.multiple_of(src_row, chunks), chunks)],
        dst_hbm.at[pl.ds(pl.multiple_of(dst_row, chunks), chunks)], sem)


def _slot_rows_kernel(ri_ref, pstart_ref, o_ref):
    tb = ri_ref.shape[1]
    eio = lax.broadcasted_iota(jnp.int32, (N_EXPERTS, tb), 0)
    pstart = pstart_ref[...]
    slots = []
    for k in range(2):
        start = jnp.sum(jnp.where(eio == ri_ref[k:k + 1, :], pstart, 0), axis=0, keepdims=True)
        slots.append(start + ri_ref[2 + k:3 + k, :])
    zi = jnp.zeros_like(slots[0])
    o_ref[...] = jnp.concatenate(
        [s * XS_CHUNKS for s in slots] + [s * ROW_CHUNKS for s in slots] + [zi] * (SUBLANES - 4),
        axis=0)


def _slot_rows_call(ri, pstart):
    T = ri.shape[1]
    tb = TM_SLOT_ROWS
    return pl.pallas_call(
        _slot_rows_kernel,
        grid=(T // tb,),
        in_specs=[pl.BlockSpec((SUBLANES, tb), lambda i: (0, i)),
                  pl.BlockSpec((N_EXPERTS, 1), lambda i: (0, 0))],
        out_specs=pl.BlockSpec((SUBLANES, tb), lambda i: (0, i)),
        out_shape=jax.ShapeDtypeStruct((SUBLANES, T), jnp.int32),
        compiler_params=_params(("parallel",)),
        name="moe_slot_rows",
    )(ri, pstart[:, None])


def _dispatch_kernel(d0_ref, d1_ref, padrow_ref, x_ref, xs_hbm, zero_ref, sem, zsem):
    i = pl.program_id(0)
    tile_rows = TM_EXPERT * XS_CHUNKS

    @pl.when(i == 0)
    def _():
        zero_ref[...] = jnp.zeros_like(zero_ref)

        def zcopy(e):
            return pltpu.make_async_copy(
                zero_ref,
                xs_hbm.at[pl.ds(pl.multiple_of(padrow_ref[e], tile_rows), tile_rows)], zsem)

        def zstart(e, c):
            @pl.when(padrow_ref[e] >= 0)
            def _():
                zcopy(e).start()
            return c

        def zwait(e, c):
            @pl.when(padrow_ref[e] >= 0)
            def _():
                zcopy(e).wait()
            return c

        lax.fori_loop(0, N_EXPERTS, zstart, 0)
        lax.fori_loop(0, N_EXPERTS, zwait, 0)

        def tcopy(b):
            return pltpu.make_async_copy(
                zero_ref, xs_hbm.at[pl.ds(pl.multiple_of(b * tile_rows, tile_rows), tile_rows)], zsem)

        n_used = padrow_ref[N_EXPERTS]
        n_blk = xs_hbm.shape[0] // tile_rows
        lax.fori_loop(n_used, n_blk, lambda b, c: (tcopy(b).start(), c)[1], 0)
        lax.fori_loop(n_used, n_blk, lambda b, c: (tcopy(b).wait(), c)[1], 0)

    def issue(t, c):
        src = t * XS_CHUNKS
        _row_copy(x_ref, src, xs_hbm, d0_ref[t], sem, XS_CHUNKS).start(priority=0)
        _row_copy(x_ref, src, xs_hbm, d1_ref[t], sem, XS_CHUNKS).start(priority=1)
        return c

    lax.fori_loop(0, TM_DISPATCH, issue, 0, unroll=DMA_UNROLL)
    for _ in range(2):
        pltpu.make_async_copy(x_ref, xs_hbm.at[pl.ds(0, TM_DISPATCH * XS_CHUNKS)], sem).wait()


def _dispatch_call(d0, d1, padrow, xrows, capacity):
    T = d0.shape[0]
    return pl.pallas_call(
        _dispatch_kernel,
        grid=(T // TM_DISPATCH,),
        in_specs=[
            pl.BlockSpec((TM_DISPATCH,), lambda i: (i,), memory_space=pltpu.SMEM),
            pl.BlockSpec((TM_DISPATCH,), lambda i: (i,), memory_space=pltpu.SMEM),
            pl.BlockSpec(memory_space=pltpu.SMEM),
            pl.BlockSpec((TM_DISPATCH * XS_CHUNKS, LANES), lambda i: (i, 0)),
        ],
        out_specs=pl.BlockSpec(memory_space=pl.ANY),
        out_shape=jax.ShapeDtypeStruct((capacity * XS_CHUNKS, LANES), jnp.uint32),
        scratch_shapes=[pltpu.VMEM((TM_EXPERT * XS_CHUNKS, LANES), jnp.uint32),
                        pltpu.SemaphoreType.DMA, pltpu.SemaphoreType.DMA],
        compiler_params=_params(("arbitrary",)),
        name="moe_dispatch",
    )(d0, d1, padrow, xrows)


def _expert_kernel(tstart_ref, ntile_ref, xs_hbm, wgu_ref, wd_ref, ys_hbm,
                   xbuf, ybuf, wgu_bf, wd_bf, sem_in, sem_out):
    nb = EXPERT_BUFS
    e = pl.program_id(0)
    tm = TM_EXPERT
    tile_rows = tm * ROW_CHUNKS
    t0 = tstart_ref[e]
    nt = ntile_ref[e]
    n_used = tstart_ref[N_EXPERTS - 1] + ntile_ref[N_EXPERTS - 1]

    def tile_at(ref, g, rows):
        return ref.at[pl.ds(pl.multiple_of(g * rows, rows), rows)]

    def in_copy(g, slot):
        return pltpu.make_async_copy(tile_at(xs_hbm, g, tm * XS_CHUNKS), xbuf.at[slot], sem_in.at[slot])

    def out_copy(g, slot):
        return pltpu.make_async_copy(ybuf.at[slot], tile_at(ys_hbm, g, tile_rows), sem_out.at[slot])

    @pl.when(e == 0)
    def _():
        for k in range(nb - 1):
            @pl.when(k < n_used)
            def _():
                in_copy(k, k).start(priority=1)

    @pl.when(nt > 0)
    def _():
        wgu_bf[...] = wgu_ref[0, 0].astype(jnp.bfloat16)
        wd_bf[...] = wd_ref[0, 0].astype(jnp.bfloat16)

    def tile_body(g, carry):
        slot = g % nb
        in_copy(g, slot).wait()

        @pl.when(g + (nb - 1) < n_used)
        def _():
            in_copy(g + (nb - 1), (g + (nb - 1)) % nb).start(priority=1)

        @pl.when(g >= nb)
        def _():
            out_copy(g - nb, slot).wait()

        cols = []
        for c in range(XS_CHUNKS):
            w = xbuf[slot, pl.ds(c, tm, stride=XS_CHUNKS), :]
            cols.append(lax.bitcast_convert_type(w << 16, jnp.float32).astype(jnp.bfloat16))
            cols.append(lax.bitcast_convert_type(w & jnp.uint32(0xFFFF0000), jnp.float32
                                                 ).astype(jnp.bfloat16))
        x = jnp.concatenate(cols, axis=-1)
        gu = jnp.dot(x, wgu_bf[...], preferred_element_type=jnp.float32)
        g_act = gu[:, :EXPERT_FF]
        u = gu[:, EXPERT_FF:]
        act = (g_act / (1.0 + jnp.exp(-g_act))) * u
        y = jnp.dot(act.astype(jnp.bfloat16), wd_bf[...], preferred_element_type=jnp.float32)
        for c in range(ROW_CHUNKS):
            ybuf[slot, pl.ds(c, tm, stride=ROW_CHUNKS), :] = y[:, c * LANES:(c + 1) * LANES]
        out_copy(g, slot).start(priority=1)
        return carry

    lax.fori_loop(t0, t0 + nt, tile_body, 0)

    @pl.when(e == N_EXPERTS - 1)
    def _():
        for k in range(1, nb + 1):
            @pl.when(n_used >= k)
            def _():
                out_copy(n_used - k, (n_used - k) % nb).wait()

        n_blk = ys_hbm.shape[0] // tile_rows
        ybuf[0] = jnp.zeros(ybuf.shape[1:], ybuf.dtype)
        lax.fori_loop(n_used, n_blk, lambda b, c: (out_copy(b, 0).start(), c)[1], 0)
        lax.fori_loop(n_used, n_blk, lambda b, c: (out_copy(b, 0).wait(), c)[1], 0)


def _expert_call(layer, tstart, ntile, xs, wgu, wd, capacity):
    tile_rows = TM_EXPERT * ROW_CHUNKS

    def w_map(e, ts, nt):
        return (layer, e, 0, 0)

    return pl.pallas_call(
        _expert_kernel,
        grid_spec=pltpu.PrefetchScalarGridSpec(
            num_scalar_prefetch=2,
            grid=(N_EXPERTS,),
            in_specs=[
                pl.BlockSpec(memory_space=pl.ANY),
                pl.BlockSpec((1, 1, D_MODEL, 2 * EXPERT_FF), w_map),
                pl.BlockSpec((1, 1, EXPERT_FF, D_MODEL), w_map),
            ],
            out_specs=pl.BlockSpec(memory_space=pl.ANY),
            scratch_shapes=[pltpu.VMEM((EXPERT_BUFS, TM_EXPERT * XS_CHUNKS, LANES), jnp.uint32),
                            pltpu.VMEM((EXPERT_BUFS, tile_rows, LANES), jnp.float32),
                            pltpu.VMEM((D_MODEL, 2 * EXPERT_FF), jnp.bfloat16),
                            pltpu.VMEM((EXPERT_FF, D_MODEL), jnp.bfloat16),
                            pltpu.SemaphoreType.DMA((EXPERT_BUFS,)),
                            pltpu.SemaphoreType.DMA((EXPERT_BUFS,))],
        ),
        out_shape=jax.ShapeDtypeStruct((capacity * ROW_CHUNKS, LANES), jnp.float32),
        compiler_params=_params(("arbitrary",)),
        name="moe_experts",
    )(tstart, ntile, xs, wgu, wd)


def _gathered_residual(tm, d0_ref, d1_ref, d0n_ref, d1n_ref, ys_hbm, h_ref, gate_ref,
                       y0_ref, y1_ref, sem):
    i = pl.program_id(0)
    slot = i % 2

    def issue_tile(d0r, d1r, s):
        def issue(t, c):
            dst = t * ROW_CHUNKS
            _row_copy(ys_hbm, d0r[t], y0_ref.at[s], dst, sem.at[s], ROW_CHUNKS).start(priority=0)
            _row_copy(ys_hbm, d1r[t], y1_ref.at[s], dst, sem.at[s], ROW_CHUNKS).start(priority=1)
            return c
        lax.fori_loop(0, tm, issue, 0, unroll=DMA_UNROLL)

    @pl.when(i == 0)
    def _():
        issue_tile(d0_ref, d1_ref, 0)

    @pl.when(i + 1 < pl.num_programs(0))
    def _():
        issue_tile(d0n_ref, d1n_ref, 1 - slot)

    for buf in (y0_ref, y1_ref):
        pltpu.make_async_copy(ys_hbm.at[pl.ds(0, tm * ROW_CHUNKS)], buf.at[slot], sem.at[slot]).wait()

    g0 = gate_ref[:, 0:1]
    g1 = gate_ref[:, 1:2]
    cols = []
    for c in range(ROW_CHUNKS):
        y0 = y0_ref[slot, pl.ds(c, tm, stride=ROW_CHUNKS), :]
        y1 = y1_ref[slot, pl.ds(c, tm, stride=ROW_CHUNKS), :]
        cols.append(h_ref[:, c * LANES:(c + 1) * LANES] + (y0 * g0 + y1 * g1))
    return jnp.concatenate(cols, axis=-1)


def _gather_in_specs(tm, n_tiles):
    last = n_tiles - 1
    return [
        pl.BlockSpec((tm,), lambda i: (i,), memory_space=pltpu.SMEM),
        pl.BlockSpec((tm,), lambda i: (i,), memory_space=pltpu.SMEM),
        pl.BlockSpec((tm,), lambda i: (jnp.minimum(i + 1, last),), memory_space=pltpu.SMEM),
        pl.BlockSpec((tm,), lambda i: (jnp.minimum(i + 1, last),), memory_space=pltpu.SMEM),
        pl.BlockSpec(memory_space=pl.ANY),
        pl.BlockSpec((tm, D_MODEL), lambda i: (i, 0)),
        pl.BlockSpec((tm, 2), lambda i: (i, 0)),
    ]


def _gather_scratch(tm):
    return [pltpu.VMEM((2, tm * ROW_CHUNKS, LANES), jnp.float32),
            pltpu.VMEM((2, tm * ROW_CHUNKS, LANES), jnp.float32),
            pltpu.SemaphoreType.DMA((2,))]


def _combine_kernel(final_norm, d0_ref, d1_ref, d0n_ref, d1n_ref, ys_hbm, h_ref, gate_ref, gfin_ref,
                    o_ref, y0_ref, y1_ref, sem):
    h = _gathered_residual(TM_COMBINE, d0_ref, d1_ref, d0n_ref, d1n_ref, ys_hbm, h_ref, gate_ref,
                           y0_ref, y1_ref, sem)
    if final_norm:
        h = _rms(h, gfin_ref[...])
    o_ref[...] = h


def _combine_call(d0, d1, ys, h, gates, gfin, final_norm):
    T = h.shape[0]
    tm = TM_COMBINE
    return pl.pallas_call(
        functools.partial(_combine_kernel, final_norm),
        grid=(T // tm,),
        in_specs=_gather_in_specs(tm, T // tm) + [pl.BlockSpec((1, D_MODEL), lambda i: (0, 0))],
        out_specs=pl.BlockSpec((tm, D_MODEL), lambda i: (i, 0)),
        out_shape=jax.ShapeDtypeStruct((T, D_MODEL), jnp.float32),
        scratch_shapes=_gather_scratch(tm),
        compiler_params=_params(("arbitrary",)),
        name="moe_combine",
    )(d0, d1, d0, d1, ys, h, gates, gfin)


def _group_heads_last(w):
    lead = w.shape[:-1]
    w = w.reshape(lead + (N_KV_HEADS, GROUP, HEAD_DIM))
    w = jnp.swapaxes(w, -3, -2)
    return w.reshape(lead + (N_Q_HEADS * HEAD_DIM,))


def _rel_bucket_table():
    qi = np.arange(ATTN_BLOCK)[:, None]
    kj = np.arange(2 * ATTN_BLOCK)[None, :]
    dist = np.maximum(qi + ATTN_BLOCK - kj, 0)
    max_exact = N_REL_BUCKETS // 2
    d = np.maximum(dist, max_exact).astype(np.float32)
    large = max_exact + (np.log(d / np.float32(max_exact)) / np.float32(math.log(REL_MAX_DISTANCE / max_exact))
                         * np.float32(N_REL_BUCKETS - max_exact)).astype(np.int32)
    large = np.minimum(large, N_REL_BUCKETS - 1)
    return np.where(dist < max_exact, dist, large).astype(np.int32)


def _rel_bias_blocks(rel_bias, sinks):
    onehot = (jnp.asarray(_rel_bucket_table())[..., None] == jnp.arange(N_REL_BUCKETS)).astype(jnp.float32)
    bias = jnp.einsum("qkb,bh->hqk", onehot, rel_bias.astype(jnp.float32),
                      precision=lax.Precision.HIGHEST)
    sink_col = jnp.broadcast_to(sinks.astype(jnp.float32)[:, None, None], (N_Q_HEADS, ATTN_BLOCK, 1))
    return jnp.concatenate([sink_col, bias[:, :, 1:]], axis=2)


def _router_weights(w_group, b_group, w_expert, b_expert):
    zw = jnp.zeros((D_MODEL, SUBLANES - N_GROUPS), jnp.float32)
    zw2 = jnp.zeros((D_MODEL, LANES - SUBLANES - N_EXPERTS), jnp.float32)
    wr = jnp.concatenate([w_group, zw, w_expert, zw2], axis=1)
    wh = wr.astype(jnp.bfloat16)
    wl = (wr - wh.astype(jnp.float32)).astype(jnp.bfloat16)
    wr = jnp.concatenate([wh, wl], axis=1)
    brt = jnp.concatenate([b_group, jnp.zeros((SUBLANES - N_GROUPS,), jnp.float32), b_expert])[:, None]
    return wr, brt


def _moe(layer, h, xrows, ri, rf, cnt, w_gate_up, w_down, gfin, final_norm):
    T = h.shape[0]
    tm = TM_EXPERT
    capacity = -(-(2 * T) // tm) * tm + N_EXPERTS * tm
    counts = cnt[:, 0].astype(jnp.int32)
    ntile = (counts + tm - 1) // tm
    tend = jnp.cumsum(ntile)
    pstart = (tend - ntile) * tm
    n_used = tend[-1:]
    padrow = jnp.concatenate([jnp.where(ntile > 0, (tend - 1) * (tm * XS_CHUNKS), -1), n_used]
                             ).astype(jnp.int32)

    dst = _slot_rows_call(ri, pstart)
    xs = _dispatch_call(dst[0], dst[1], padrow, xrows, capacity)
    ys = _expert_call(layer, (tend - ntile).astype(jnp.int32), ntile.astype(jnp.int32), xs,
                      w_gate_up, w_down, capacity)
    gates = rf[:2].T
    return _combine_call(dst[2], dst[3], ys, h, gates, gfin, final_norm)


def kernel(x, norm_mix, norm_ffn, final_norm, rel_bias, attn_w_qkv, attn_b_qkv, attn_w_o, attn_b_o,
           attn_sinks, conv_w_in, conv_w, conv_w_out, moe_w_group, moe_b_group, moe_w_expert,
           moe_b_expert, moe_w_gate_up, moe_w_down):
    B, S, D = x.shape
    T = B * S
    x2 = x.reshape(T, D)
    nq = N_Q_HEADS * HEAD_DIM
    scale = HEAD_DIM ** -0.5

    wqkv = attn_w_qkv[0]
    bqkv = attn_b_qkv[0]
    w_all = jnp.concatenate([_group_heads_last(wqkv[:, :nq]) * scale, wqkv[:, nq:]],
                            axis=1).astype(jnp.bfloat16)
    b_all = jnp.concatenate([_group_heads_last(bqkv[:nq]) * scale, bqkv[nq:]])[None, :]
    q, kt, v = _qkv_call(x2, norm_mix[0][None, :], w_all, b_all)
    a = _attn_call(q, kt, v, _rel_bias_blocks(rel_bias, attn_sinks[0]), B, S)
    wr, brt = _router_weights(moe_w_group[0], moe_b_group[0], moe_w_expert[0], moe_b_expert[0])
    w_o = _group_heads_last(attn_w_o[0].T).T.astype(jnp.bfloat16)
    h, xrows, ri, rf, cnt = _attn_out_call(
        a, w_o, attn_b_o[0][None, :], x2, norm_ffn[0][None, :], wr, brt)
    h = _moe(0, h, xrows, ri, rf, cnt, moe_w_gate_up, moe_w_down, final_norm[None, :], False)

    wr, brt = _router_weights(moe_w_group[1], moe_b_group[1], moe_w_expert[1], moe_b_expert[1])
    h, xrows, ri, rf, cnt = _conv_mixer_call(
        h, norm_mix[1][None, :], conv_w_in[0].astype(jnp.bfloat16), conv_w[0],
        conv_w_out[0].astype(jnp.bfloat16), norm_ffn[1][None, :], wr, brt, S)
    out = _moe(1, h, xrows, ri, rf, cnt, moe_w_gate_up, moe_w_down, final_norm[None, :], True)
    return out.reshape(B, S, D)
```

```python
import functools
import math

import numpy as np
import jax
import jax.numpy as jnp
from jax import lax
from jax.experimental import pallas as pl
from jax.experimental.pallas import tpu as pltpu

D_MODEL = 1024
N_Q_HEADS = 16
N_KV_HEADS = 2
HEAD_DIM = 64
GROUP = N_Q_HEADS // N_KV_HEADS
WINDOW = 128
ATTN_BLOCK = 128
N_REL_BUCKETS = 32
REL_MAX_DISTANCE = 128
CONV_WIDTH = 3
N_GROUPS = 4
EXPERTS_PER_GROUP = 8
N_EXPERTS = N_GROUPS * EXPERTS_PER_GROUP
EXPERT_FF = 512
RMS_EPS = 1e-5
LOG2_E = math.log2(math.e)

LANES = 128
SUBLANES = 8
ROW_CHUNKS = D_MODEL // LANES
XS_CHUNKS = ROW_CHUNKS // 2
VMEM_LIMIT = 56 * 1024 * 1024

TM_DENSE = 1024
TQ = 1024
TM_MIX1 = 1024
TM_EXPERT = 512
TM_DISPATCH = 1024
TM_COMBINE = 256
DMA_UNROLL = 8
TM_SLOT_ROWS = 4096
EXPERT_BUFS = 4
ROUTER_ROWS = 40


def _rms(x, g):
    return x * lax.rsqrt(jnp.mean(x * x, axis=-1, keepdims=True) + RMS_EPS) * g


def _params(sem):
    return pltpu.CompilerParams(dimension_semantics=sem, vmem_limit_bytes=VMEM_LIMIT)


def _qkv_kernel(x_ref, g_ref, w_ref, b_ref, q_ref, kt_ref, v_ref):
    xn = _rms(x_ref[...], g_ref[...]).astype(jnp.bfloat16)
    out = jnp.dot(xn, w_ref[...], preferred_element_type=jnp.float32) + b_ref[...]
    nq = N_Q_HEADS * HEAD_DIM
    nk = N_KV_HEADS * HEAD_DIM
    q_ref[...] = out[:, :nq].astype(jnp.bfloat16)
    kt_ref[...] = out[:, nq:nq + nk].T.astype(jnp.bfloat16)
    v_ref[...] = out[:, nq + nk:].astype(jnp.bfloat16)


def _qkv_call(x2, g, w, b):
    T = x2.shape[0]
    nq = N_Q_HEADS * HEAD_DIM
    nk = N_KV_HEADS * HEAD_DIM
    tm = TM_DENSE
    return pl.pallas_call(
        _qkv_kernel,
        grid=(T // tm,),
        in_specs=[
            pl.BlockSpec((tm, D_MODEL), lambda i: (i, 0)),
            pl.BlockSpec((1, D_MODEL), lambda i: (0, 0)),
            pl.BlockSpec((D_MODEL, nq + 2 * nk), lambda i: (0, 0)),
            pl.BlockSpec((1, nq + 2 * nk), lambda i: (0, 0)),
        ],
        out_specs=[
            pl.BlockSpec((tm, nq), lambda i: (i, 0)),
            pl.BlockSpec((nk, tm), lambda i: (0, i)),
            pl.BlockSpec((tm, nk), lambda i: (i, 0)),
        ],
        out_shape=[
            jax.ShapeDtypeStruct((T, nq), jnp.bfloat16),
            jax.ShapeDtypeStruct((nk, T), jnp.bfloat16),
            jax.ShapeDtypeStruct((T, nk), jnp.bfloat16),
        ],
        compiler_params=_params(("parallel",)),
        name="qkv_proj",
    )(x2, g, w, b)


def _attn_kernel(q_ref, kt_ref, ktp_ref, v_ref, vp_ref, bias_ref, o_ref):
    first_tile = pl.program_id(1) == 0
    nblk = TQ // ATTN_BLOCK
    qi_io = lax.broadcasted_iota(jnp.int32, (ATTN_BLOCK, 2 * ATTN_BLOCK), 0)
    kj_io = lax.broadcasted_iota(jnp.int32, (ATTN_BLOCK, 2 * ATTN_BLOCK), 1)
    dist = qi_io + ATTN_BLOCK - kj_io
    band = (dist >= 0) & (dist < WINDOW)
    sink_col = kj_io == 0
    kt_keep = lax.broadcasted_iota(jnp.int32, (N_KV_HEADS * HEAD_DIM, 2 * ATTN_BLOCK), 1) > 0
    v_keep = lax.broadcasted_iota(jnp.int32, (2 * ATTN_BLOCK, N_KV_HEADS * HEAD_DIM), 0) > 0
    low_half = lax.broadcasted_iota(jnp.int32, (ATTN_BLOCK, LANES), 1) < HEAD_DIM
    for qi in range(nblk):
        r0 = qi * ATTN_BLOCK
        if qi == 0:
            kt_blk = jnp.concatenate([ktp_ref[...], kt_ref[:, 0:ATTN_BLOCK]], axis=1)
            v_blk = jnp.concatenate([vp_ref[...], v_ref[0:ATTN_BLOCK, :]], axis=0)
            mask = (band & (jnp.logical_not(first_tile) | (kj_io >= ATTN_BLOCK))) | sink_col
        else:
            kt_blk = kt_ref[:, r0 - ATTN_BLOCK:r0 + ATTN_BLOCK]
            v_blk = v_ref[r0 - ATTN_BLOCK:r0 + ATTN_BLOCK, :]
            mask = band | sink_col
        kt_blk = jnp.where(kt_keep, kt_blk, jnp.zeros_like(kt_blk))
        v_blk = jnp.where(v_keep, v_blk, jnp.zeros_like(v_blk))
        for m in range(GROUP):
            qg = q_ref[r0:r0 + ATTN_BLOCK, m * LANES:(m + 1) * LANES]
            halves = []
            for half in range(N_KV_HEADS):
                h = m + GROUP * half
                keep = low_half if half == 0 else jnp.logical_not(low_half)
                qpad = jnp.where(keep, qg, jnp.zeros_like(qg))
                s = jnp.dot(qpad, kt_blk, preferred_element_type=jnp.float32)
                s = jnp.where(mask, s + bias_ref[h], -1e30)
                mx = jnp.max(s, axis=-1, keepdims=True)
                p = jnp.exp2(s - mx)
                den = jnp.sum(p, axis=-1, keepdims=True)
                pv = jnp.dot(p.astype(jnp.bfloat16), v_blk, preferred_element_type=jnp.float32)
                halves.append(pv * (1.0 / den))
            og = jnp.where(low_half, halves[0], halves[1])
            o_ref[r0:r0 + ATTN_BLOCK, m * LANES:(m + 1) * LANES] = og.astype(jnp.bfloat16)


def _attn_call(q, kt, v, bias, batch, seq):
    T = q.shape[0]
    nq = N_Q_HEADS * HEAD_DIM
    nk = N_KV_HEADS * HEAD_DIM
    tiles = seq // TQ
    per = TQ // ATTN_BLOCK

    def cur(b, j):
        return b * tiles + j

    def prev(b, j):
        return jnp.maximum((b * tiles + j) * per - 1, b * tiles * per)

    return pl.pallas_call(
        _attn_kernel,
        grid=(batch, tiles),
        in_specs=[
            pl.BlockSpec((TQ, nq), lambda b, j: (cur(b, j), 0)),
            pl.BlockSpec((nk, TQ), lambda b, j: (0, cur(b, j))),
            pl.BlockSpec((nk, ATTN_BLOCK), lambda b, j: (0, prev(b, j))),
            pl.BlockSpec((TQ, nk), lambda b, j: (cur(b, j), 0)),
            pl.BlockSpec((ATTN_BLOCK, nk), lambda b, j: (prev(b, j), 0)),
            pl.BlockSpec((N_Q_HEADS, ATTN_BLOCK, 2 * ATTN_BLOCK), lambda b, j: (0, 0, 0)),
        ],
        out_specs=pl.BlockSpec((TQ, nq), lambda b, j: (cur(b, j), 0)),
        out_shape=jax.ShapeDtypeStruct((T, nq), jnp.bfloat16),
        compiler_params=_params(("parallel", "parallel")),
        name="swa_attention",
    )(q, kt, kt, v, v, bias)


def _router_epilogue(h, gffn_ref, wr_ref, brt_ref, carry_ref,
                     h_out_ref, xrows_ref, ri_ref, rf_ref, cnt_ref):
    tm = h.shape[0]
    h_out_ref[...] = h
    xn = _rms(h, gffn_ref[...])
    xh = xn.astype(jnp.bfloat16)
    xbits = lax.bitcast_convert_type(xh.astype(jnp.float32), jnp.uint32)
    for c in range(XS_CHUNKS):
        lo = xbits[:, (2 * c) * LANES:(2 * c + 1) * LANES] >> 16
        hi = xbits[:, (2 * c + 1) * LANES:(2 * c + 2) * LANES]
        xrows_ref[pl.ds(c, tm, stride=XS_CHUNKS), :] = lo | hi

    xl = (xn - xh.astype(jnp.float32)).astype(jnp.bfloat16)
    wcat = wr_ref[...]
    prod = jnp.dot(xh, wcat, preferred_element_type=jnp.float32)
    logits = (prod[:, :LANES] + prod[:, LANES:]) + jnp.dot(
        xl, wcat[:, :LANES], preferred_element_type=jnp.float32)
    lt = logits.T[:ROUTER_ROWS, :] + brt_ref[...]

    gl = [lt[g:g + 1, :] for g in range(N_GROUPS)]
    gmax = functools.reduce(jnp.maximum, gl)
    gexp = [jnp.exp(x - gmax) for x in gl]
    gsum = functools.reduce(lambda a, b: a + b, gexp)
    gprob = [x / gsum for x in gexp]
    g_prob = functools.reduce(jnp.maximum, gprob)
    g_idx = jnp.full(g_prob.shape, N_GROUPS - 1, jnp.int32)
    for g in range(N_GROUPS - 2, -1, -1):
        g_idx = jnp.where(gprob[g] == g_prob, g, g_idx)

    el = []
    for j in range(EXPERTS_PER_GROUP):
        x = lt[SUBLANES + j:SUBLANES + j + 1, :]
        for g in range(1, N_GROUPS):
            r = SUBLANES + g * EXPERTS_PER_GROUP + j
            x = jnp.where(g_idx == g, lt[r:r + 1, :], x)
        el.append(x)
    emax = functools.reduce(jnp.maximum, el)
    eexp = [jnp.exp(x - emax) for x in el]
    esum = functools.reduce(lambda a, b: a + b, eexp)
    eprob = [x / esum for x in eexp]
    p1 = functools.reduce(jnp.maximum, eprob)
    i1 = jnp.full(p1.shape, EXPERTS_PER_GROUP - 1, jnp.int32)
    for j in range(EXPERTS_PER_GROUP - 2, -1, -1):
        i1 = jnp.where(eprob[j] == p1, j, i1)
    rest = [jnp.where(i1 == j, -1.0, eprob[j]) for j in range(EXPERTS_PER_GROUP)]
    p2 = functools.reduce(jnp.maximum, rest)
    i2 = jnp.full(p2.shape, EXPERTS_PER_GROUP - 1, jnp.int32)
    for j in range(EXPERTS_PER_GROUP - 2, -1, -1):
        i2 = jnp.where(rest[j] == p2, j, i2)
    psum = p1 + p2
    gate0 = g_prob * (p1 / psum)
    gate1 = g_prob * (p2 / psum)
    e0 = g_idx * EXPERTS_PER_GROUP + i1
    e1 = g_idx * EXPERTS_PER_GROUP + i2

    eio = lax.broadcasted_iota(jnp.int32, (N_EXPERTS, tm), 0)
    oh0 = (eio == e0).astype(jnp.float32)
    oh1 = (eio == e1).astype(jnp.float32)
    both = oh0 + oh1
    tr = lax.broadcasted_iota(jnp.int32, (tm, tm), 0)
    tc = lax.broadcasted_iota(jnp.int32, (tm, tm), 1)
    upper = (tr < tc).astype(jnp.bfloat16)
    before = jnp.dot(both.astype(jnp.bfloat16), upper, preferred_element_type=jnp.float32)
    before = before + carry_ref[...]
    rank0 = jnp.sum(oh0 * before, axis=0, keepdims=True)
    rank1 = jnp.sum(oh1 * (before + oh0), axis=0, keepdims=True)
    carry_ref[...] = carry_ref[...] + jnp.sum(both, axis=1, keepdims=True)
    cnt_ref[...] = jnp.broadcast_to(carry_ref[...], cnt_ref.shape)

    zi = jnp.zeros_like(e0)
    ri_ref[...] = jnp.concatenate(
        [e0, e1, rank0.astype(jnp.int32), rank1.astype(jnp.int32), zi, zi, zi, zi], axis=0)
    zf = jnp.zeros_like(gate0)
    rf_ref[...] = jnp.concatenate([gate0, gate1, zf, zf, zf, zf, zf, zf], axis=0)


def _epilogue_out_specs(tm):
    return [
        pl.BlockSpec((tm, D_MODEL), lambda i: (i, 0)),
        pl.BlockSpec((tm * XS_CHUNKS, LANES), lambda i: (i, 0)),
        pl.BlockSpec((SUBLANES, tm), lambda i: (0, i)),
        pl.BlockSpec((SUBLANES, tm), lambda i: (0, i)),
        pl.BlockSpec((N_EXPERTS, LANES), lambda i: (0, 0)),
    ]


def _epilogue_out_shapes(T):
    return [
        jax.ShapeDtypeStruct((T, D_MODEL), jnp.float32),
        jax.ShapeDtypeStruct((T * XS_CHUNKS, LANES), jnp.uint32),
        jax.ShapeDtypeStruct((SUBLANES, T), jnp.int32),
        jax.ShapeDtypeStruct((SUBLANES, T), jnp.float32),
        jax.ShapeDtypeStruct((N_EXPERTS, LANES), jnp.float32),
    ]


def _epilogue_in_specs():
    return [
        pl.BlockSpec((1, D_MODEL), lambda i: (0, 0)),
        pl.BlockSpec((D_MODEL, 2 * LANES), lambda i: (0, 0)),
        pl.BlockSpec((ROUTER_ROWS, 1), lambda i: (0, 0)),
    ]


def _attn_out_kernel(a_ref, w_ref, b_ref, res_ref, gffn_ref, wr_ref, brt_ref,
                     h_out_ref, xrows_ref, ri_ref, rf_ref, cnt_ref, carry_ref):
    @pl.when(pl.program_id(0) == 0)
    def _():
        carry_ref[...] = jnp.zeros_like(carry_ref)

    mix = jnp.dot(a_ref[...], w_ref[...], preferred_element_type=jnp.float32) + b_ref[...]
    h = res_ref[...] + mix
    _router_epilogue(h, gffn_ref, wr_ref, brt_ref, carry_ref,
                     h_out_ref, xrows_ref, ri_ref, rf_ref, cnt_ref)


def _attn_out_call(a, w, b, res, gffn, wr, brt):
    T = a.shape[0]
    tm = TM_DENSE
    return pl.pallas_call(
        _attn_out_kernel,
        grid=(T // tm,),
        in_specs=[
            pl.BlockSpec((tm, D_MODEL), lambda i: (i, 0)),
            pl.BlockSpec((D_MODEL, D_MODEL), lambda i: (0, 0)),
            pl.BlockSpec((1, D_MODEL), lambda i: (0, 0)),
            pl.BlockSpec((tm, D_MODEL), lambda i: (i, 0)),
        ] + _epilogue_in_specs(),
        out_specs=_epilogue_out_specs(tm),
        out_shape=_epilogue_out_shapes(T),
        scratch_shapes=[pltpu.VMEM((N_EXPERTS, 1), jnp.float32)],
        compiler_params=_params(("arbitrary",)),
        name="attn_out_router",
    )(a, w, b, res, gffn, wr, brt)


def _conv_mixer_kernel(tiles_per_seq, x_ref, g_ref, win_ref, cw_ref, wout_ref,
                       gffn_ref, wr_ref, brt_ref,
                       h_out_ref, xrows_ref, ri_ref, rf_ref, cnt_ref, carry_ref, tail_ref):
    i = pl.program_id(0)
    tm = x_ref.shape[0]

    @pl.when(i == 0)
    def _():
        carry_ref[...] = jnp.zeros_like(carry_ref)

    @pl.when(i % tiles_per_seq == 0)
    def _():
        tail_ref[...] = jnp.zeros_like(tail_ref)

    x = x_ref[...]
    xn = _rms(x, g_ref[...]).astype(jnp.bfloat16)
    bcu = jnp.dot(xn, win_ref[...], preferred_element_type=jnp.float32)
    b_gate = bcu[:, :D_MODEL]
    z = bcu[:, D_MODEL:2 * D_MODEL] * bcu[:, 2 * D_MODEL:]
    row = lax.broadcasted_iota(jnp.int32, (tm, D_MODEL), 0)
    tail = tail_ref[...]
    t1 = tail[SUBLANES - 1:SUBLANES, :]
    t2 = tail[SUBLANES - 2:SUBLANES - 1, :]
    z1 = jnp.where(row == 0, t1, pltpu.roll(z, 1, axis=0))
    z2 = jnp.where(row == 0, t2, jnp.where(row == 1, t1, pltpu.roll(z, 2, axis=0)))
    tail_ref[...] = z[tm - SUBLANES:, :]
    conv = z2 * cw_ref[0:1, :]
    conv = conv + z1 * cw_ref[1:2, :]
    conv = conv + z * cw_ref[2:3, :]
    gated = (b_gate * conv).astype(jnp.bfloat16)
    h = x + jnp.dot(gated, wout_ref[...], preferred_element_type=jnp.float32)
    _router_epilogue(h, gffn_ref, wr_ref, brt_ref, carry_ref,
                     h_out_ref, xrows_ref, ri_ref, rf_ref, cnt_ref)


def _conv_mixer_call(x2, g, win, cw, wout, gffn, wr, brt, seq):
    T = x2.shape[0]
    tm = TM_MIX1
    return pl.pallas_call(
        functools.partial(_conv_mixer_kernel, seq // tm),
        grid=(T // tm,),
        in_specs=[
            pl.BlockSpec((tm, D_MODEL), lambda i: (i, 0)),
            pl.BlockSpec((1, D_MODEL), lambda i: (0, 0)),
            pl.BlockSpec((D_MODEL, 3 * D_MODEL), lambda i: (0, 0)),
            pl.BlockSpec((CONV_WIDTH, D_MODEL), lambda i: (0, 0)),
            pl.BlockSpec((D_MODEL, D_MODEL), lambda i: (0, 0)),
        ] + _epilogue_in_specs(),
        out_specs=_epilogue_out_specs(tm),
        out_shape=_epilogue_out_shapes(T),
        scratch_shapes=[pltpu.VMEM((N_EXPERTS, 1), jnp.float32),
                        pltpu.VMEM((SUBLANES, D_MODEL), jnp.float32)],
        compiler_params=_params(("arbitrary",)),
        name="conv_mixer_router",
    )(x2, g, win, cw, wout, gffn, wr, brt)


def _row_copy(src_hbm, src_row, dst_hbm, dst_row, sem, chunks):
    return pltpu.make_async_copy(
        src_hbm.at[pl.ds(pl.multiple_of(src_row, chunks), chunks)],
        dst_hbm.at[pl.ds(pl.multiple_of(dst_row, chunks), chunks)], sem)


def _slot_rows_kernel(ri_ref, pstart_ref, o_ref):
    tb = ri_ref.shape[1]
    eio = lax.broadcasted_iota(jnp.int32, (N_EXPERTS, tb), 0)
    pstart = pstart_ref[...]
    slots = []
    for k in range(2):
        start = jnp.sum(jnp.where(eio == ri_ref[k:k + 1, :], pstart, 0), axis=0, keepdims=True)
        slots.append(start + ri_ref[2 + k:3 + k, :])
    zi = jnp.zeros_like(slots[0])
    o_ref[...] = jnp.concatenate(
        [s * XS_CHUNKS for s in slots] + [s * ROW_CHUNKS for s in slots] + [zi] * (SUBLANES - 4),
        axis=0)


def _slot_rows_call(ri, pstart):
    T = ri.shape[1]
    tb = TM_SLOT_ROWS
    return pl.pallas_call(
        _slot_rows_kernel,
        grid=(T // tb,),
        in_specs=[pl.BlockSpec((SUBLANES, tb), lambda i: (0, i)),
                  pl.BlockSpec((N_EXPERTS, 1), lambda i: (0, 0))],
        out_specs=pl.BlockSpec((SUBLANES, tb), lambda i: (0, i)),
        out_shape=jax.ShapeDtypeStruct((SUBLANES, T), jnp.int32),
        compiler_params=_params(("parallel",)),
        name="moe_slot_rows",
    )(ri, pstart[:, None])


def _dispatch_kernel(d0_ref, d1_ref, padrow_ref, x_ref, xs_hbm, zero_ref, sem, zsem):
    i = pl.program_id(0)
    tile_rows = TM_EXPERT * XS_CHUNKS

    @pl.when(i == 0)
    def _():
        zero_ref[...] = jnp.zeros_like(zero_ref)

        def zcopy(e):
            return pltpu.make_async_copy(
                zero_ref,
                xs_hbm.at[pl.ds(pl.multiple_of(padrow_ref[e], tile_rows), tile_rows)], zsem)

        def zstart(e, c):
            @pl.when(padrow_ref[e] >= 0)
            def _():
                zcopy(e).start()
            return c

        def zwait(e, c):
            @pl.when(padrow_ref[e] >= 0)
            def _():
                zcopy(e).wait()
            return c

        lax.fori_loop(0, N_EXPERTS, zstart, 0)
        lax.fori_loop(0, N_EXPERTS, zwait, 0)

        def tcopy(b):
            return pltpu.make_async_copy(
                zero_ref, xs_hbm.at[pl.ds(pl.multiple_of(b * tile_rows, tile_rows), tile_rows)], zsem)

        n_used = padrow_ref[N_EXPERTS]
        n_blk = xs_hbm.shape[0] // tile_rows
        lax.fori_loop(n_used, n_blk, lambda b, c: (tcopy(b).start(), c)[1], 0)
        lax.fori_loop(n_used, n_blk, lambda b, c: (tcopy(b).wait(), c)[1], 0)

    def issue(t, c):
        src = t * XS_CHUNKS
        _row_copy(x_ref, src, xs_hbm, d0_ref[t], sem, XS_CHUNKS).start(priority=0)
        _row_copy(x_ref, src, xs_hbm, d1_ref[t], sem, XS_CHUNKS).start(priority=1)
        return c

    lax.fori_loop(0, TM_DISPATCH, issue, 0, unroll=DMA_UNROLL)
    for _ in range(2):
        pltpu.make_async_copy(x_ref, xs_hbm.at[pl.ds(0, TM_DISPATCH * XS_CHUNKS)], sem).wait()


def _dispatch_call(d0, d1, padrow, xrows, capacity):
    T = d0.shape[0]
    return pl.pallas_call(
        _dispatch_kernel,
        grid=(T // TM_DISPATCH,),
        in_specs=[
            pl.BlockSpec((TM_DISPATCH,), lambda i: (i,), memory_space=pltpu.SMEM),
            pl.BlockSpec((TM_DISPATCH,), lambda i: (i,), memory_space=pltpu.SMEM),
            pl.BlockSpec(memory_space=pltpu.SMEM),
            pl.BlockSpec((TM_DISPATCH * XS_CHUNKS, LANES), lambda i: (i, 0)),
        ],
        out_specs=pl.BlockSpec(memory_space=pl.ANY),
        out_shape=jax.ShapeDtypeStruct((capacity * XS_CHUNKS, LANES), jnp.uint32),
        scratch_shapes=[pltpu.VMEM((TM_EXPERT * XS_CHUNKS, LANES), jnp.uint32),
                        pltpu.SemaphoreType.DMA, pltpu.SemaphoreType.DMA],
        compiler_params=_params(("arbitrary",)),
        name="moe_dispatch",
    )(d0, d1, padrow, xrows)


def _expert_kernel(tstart_ref, ntile_ref, xs_hbm, wgu_ref, wd_ref, ys_hbm,
                   xbuf, ybuf, wgu_bf, wd_bf, sem_in, sem_out):
    nb = EXPERT_BUFS
    e = pl.program_id(0)
    tm = TM_EXPERT
    tile_rows = tm * ROW_CHUNKS
    t0 = tstart_ref[e]
    nt = ntile_ref[e]
    n_used = tstart_ref[N_EXPERTS - 1] + ntile_ref[N_EXPERTS - 1]

    def tile_at(ref, g, rows):
        return ref.at[pl.ds(pl.multiple_of(g * rows, rows), rows)]

    def in_copy(g, slot):
        return pltpu.make_async_copy(tile_at(xs_hbm, g, tm * XS_CHUNKS), xbuf.at[slot], sem_in.at[slot])

    def out_copy(g, slot):
        return pltpu.make_async_copy(ybuf.at[slot], tile_at(ys_hbm, g, tile_rows), sem_out.at[slot])

    @pl.when(e == 0)
    def _():
        for k in range(nb - 1):
            @pl.when(k < n_used)
            def _():
                in_copy(k, k).start(priority=1)

    @pl.when(nt > 0)
    def _():
        wgu_bf[...] = wgu_ref[0, 0].astype(jnp.bfloat16)
        wd_bf[...] = wd_ref[0, 0].astype(jnp.bfloat16)

    def tile_body(g, carry):
        slot = g % nb
        in_copy(g, slot).wait()

        @pl.when(g + (nb - 1) < n_used)
        def _():
            in_copy(g + (nb - 1), (g + (nb - 1)) % nb).start(priority=1)

        @pl.when(g >= nb)
        def _():
            out_copy(g - nb, slot).wait()

        cols = []
        for c in range(XS_CHUNKS):
            w = xbuf[slot, pl.ds(c, tm, stride=XS_CHUNKS), :]
            cols.append(lax.bitcast_convert_type(w << 16, jnp.float32).astype(jnp.bfloat16))
            cols.append(lax.bitcast_convert_type(w & jnp.uint32(0xFFFF0000), jnp.float32
                                                 ).astype(jnp.bfloat16))
        x = jnp.concatenate(cols, axis=-1)
        gu = jnp.dot(x, wgu_bf[...], preferred_element_type=jnp.float32)
        g_act = gu[:, :EXPERT_FF]
        u = gu[:, EXPERT_FF:]
        act = (g_act / (1.0 + jnp.exp(-g_act))) * u
        y = jnp.dot(act.astype(jnp.bfloat16), wd_bf[...], preferred_element_type=jnp.float32)
        for c in range(ROW_CHUNKS):
            ybuf[slot, pl.ds(c, tm, stride=ROW_CHUNKS), :] = y[:, c * LANES:(c + 1) * LANES]
        out_copy(g, slot).start(priority=1)
        return carry

    lax.fori_loop(t0, t0 + nt, tile_body, 0)

    @pl.when(e == N_EXPERTS - 1)
    def _():
        for k in range(1, nb + 1):
            @pl.when(n_used >= k)
            def _():
                out_copy(n_used - k, (n_used - k) % nb).wait()

        n_blk = ys_hbm.shape[0] // tile_rows
        ybuf[0] = jnp.zeros(ybuf.shape[1:], ybuf.dtype)
        lax.fori_loop(n_used, n_blk, lambda b, c: (out_copy(b, 0).start(), c)[1], 0)
        lax.fori_loop(n_used, n_blk, lambda b, c: (out_copy(b, 0).wait(), c)[1], 0)


def _expert_call(layer, tstart, ntile, xs, wgu, wd, capacity):
    tile_rows = TM_EXPERT * ROW_CHUNKS

    def w_map(e, ts, nt):
        return (layer, e, 0, 0)

    return pl.pallas_call(
        _expert_kernel,
        grid_spec=pltpu.PrefetchScalarGridSpec(
            num_scalar_prefetch=2,
            grid=(N_EXPERTS,),
            in_specs=[
                pl.BlockSpec(memory_space=pl.ANY),
                pl.BlockSpec((1, 1, D_MODEL, 2 * EXPERT_FF), w_map),
                pl.BlockSpec((1, 1, EXPERT_FF, D_MODEL), w_map),
            ],
            out_specs=pl.BlockSpec(memory_space=pl.ANY),
            scratch_shapes=[pltpu.VMEM((EXPERT_BUFS, TM_EXPERT * XS_CHUNKS, LANES), jnp.uint32),
                            pltpu.VMEM((EXPERT_BUFS, tile_rows, LANES), jnp.float32),
                            pltpu.VMEM((D_MODEL, 2 * EXPERT_FF), jnp.bfloat16),
                            pltpu.VMEM((EXPERT_FF, D_MODEL), jnp.bfloat16),
                            pltpu.SemaphoreType.DMA((EXPERT_BUFS,)),
                            pltpu.SemaphoreType.DMA((EXPERT_BUFS,))],
        ),
        out_shape=jax.ShapeDtypeStruct((capacity * ROW_CHUNKS, LANES), jnp.float32),
        compiler_params=_params(("arbitrary",)),
        name="moe_experts",
    )(tstart, ntile, xs, wgu, wd)


def _gathered_residual(tm, d0_ref, d1_ref, d0n_ref, d1n_ref, ys_hbm, h_ref, gate_ref,
                       y0_ref, y1_ref, sem):
    i = pl.program_id(0)
    slot = i % 2

    def issue_tile(d0r, d1r, s):
        def issue(t, c):
            dst = t * ROW_CHUNKS
            _row_copy(ys_hbm, d0r[t], y0_ref.at[s], dst, sem.at[s], ROW_CHUNKS).start(priority=0)
            _row_copy(ys_hbm, d1r[t], y1_ref.at[s], dst, sem.at[s], ROW_CHUNKS).start(priority=1)
            return c
        lax.fori_loop(0, tm, issue, 0, unroll=DMA_UNROLL)

    @pl.when(i == 0)
    def _():
        issue_tile(d0_ref, d1_ref, 0)

    @pl.when(i + 1 < pl.num_programs(0))
    def _():
        issue_tile(d0n_ref, d1n_ref, 1 - slot)

    for buf in (y0_ref, y1_ref):
        pltpu.make_async_copy(ys_hbm.at[pl.ds(0, tm * ROW_CHUNKS)], buf.at[slot], sem.at[slot]).wait()

    g0 = gate_ref[:, 0:1]
    g1 = gate_ref[:, 1:2]
    cols = []
    for c in range(ROW_CHUNKS):
        y0 = y0_ref[slot, pl.ds(c, tm, stride=ROW_CHUNKS), :]
        y1 = y1_ref[slot, pl.ds(c, tm, stride=ROW_CHUNKS), :]
        cols.append(h_ref[:, c * LANES:(c + 1) * LANES] + (y0 * g0 + y1 * g1))
    return jnp.concatenate(cols, axis=-1)


def _gather_in_specs(tm, n_tiles):
    last = n_tiles - 1
    return [
        pl.BlockSpec((tm,), lambda i: (i,), memory_space=pltpu.SMEM),
        pl.BlockSpec((tm,), lambda i: (i,), memory_space=pltpu.SMEM),
        pl.BlockSpec((tm,), lambda i: (jnp.minimum(i + 1, last),), memory_space=pltpu.SMEM),
        pl.BlockSpec((tm,), lambda i: (jnp.minimum(i + 1, last),), memory_space=pltpu.SMEM),
        pl.BlockSpec(memory_space=pl.ANY),
        pl.BlockSpec((tm, D_MODEL), lambda i: (i, 0)),
        pl.BlockSpec((tm, 2), lambda i: (i, 0)),
    ]


def _gather_scratch(tm):
    return [pltpu.VMEM((2, tm * ROW_CHUNKS, LANES), jnp.float32),
            pltpu.VMEM((2, tm * ROW_CHUNKS, LANES), jnp.float32),
            pltpu.SemaphoreType.DMA((2,))]


def _combine_kernel(final_norm, d0_ref, d1_ref, d0n_ref, d1n_ref, ys_hbm, h_ref, gate_ref, gfin_ref,
                    o_ref, y0_ref, y1_ref, sem):
    h = _gathered_residual(TM_COMBINE, d0_ref, d1_ref, d0n_ref, d1n_ref, ys_hbm, h_ref, gate_ref,
                           y0_ref, y1_ref, sem)
    if final_norm:
        h = _rms(h, gfin_ref[...])
    o_ref[...] = h


def _combine_call(d0, d1, ys, h, gates, gfin, final_norm):
    T = h.shape[0]
    tm = TM_COMBINE
    return pl.pallas_call(
        functools.partial(_combine_kernel, final_norm),
        grid=(T // tm,),
        in_specs=_gather_in_specs(tm, T // tm) + [pl.BlockSpec((1, D_MODEL), lambda i: (0, 0))],
        out_specs=pl.BlockSpec((tm, D_MODEL), lambda i: (i, 0)),
        out_shape=jax.ShapeDtypeStruct((T, D_MODEL), jnp.float32),
        scratch_shapes=_gather_scratch(tm),
        compiler_params=_params(("arbitrary",)),
        name="moe_combine",
    )(d0, d1, d0, d1, ys, h, gates, gfin)


def _group_heads_last(w):
    lead = w.shape[:-1]
    w = w.reshape(lead + (N_KV_HEADS, GROUP, HEAD_DIM))
    w = jnp.swapaxes(w, -3, -2)
    return w.reshape(lead + (N_Q_HEADS * HEAD_DIM,))


def _rel_bucket_table():
    qi = np.arange(ATTN_BLOCK)[:, None]
    kj = np.arange(2 * ATTN_BLOCK)[None, :]
    dist = np.maximum(qi + ATTN_BLOCK - kj, 0)
    max_exact = N_REL_BUCKETS // 2
    d = np.maximum(dist, max_exact).astype(np.float32)
    large = max_exact + (np.log(d / np.float32(max_exact)) / np.float32(math.log(REL_MAX_DISTANCE / max_exact))
                         * np.float32(N_REL_BUCKETS - max_exact)).astype(np.int32)
    large = np.minimum(large, N_REL_BUCKETS - 1)
    return np.where(dist < max_exact, dist, large).astype(np.int32)


def _rel_bias_blocks(rel_bias, sinks):
    onehot = (jnp.asarray(_rel_bucket_table())[..., None] == jnp.arange(N_REL_BUCKETS)).astype(jnp.float32)
    bias = jnp.einsum("qkb,bh->hqk", onehot, rel_bias.astype(jnp.float32),
                      precision=lax.Precision.HIGHEST)
    sink_col = jnp.broadcast_to(sinks.astype(jnp.float32)[:, None, None], (N_Q_HEADS, ATTN_BLOCK, 1))
    return jnp.concatenate([sink_col, bias[:, :, 1:]], axis=2) * LOG2_E


def _router_weights(w_group, b_group, w_expert, b_expert):
    zw = jnp.zeros((D_MODEL, SUBLANES - N_GROUPS), jnp.float32)
    zw2 = jnp.zeros((D_MODEL, LANES - SUBLANES - N_EXPERTS), jnp.float32)
    wr = jnp.concatenate([w_group, zw, w_expert, zw2], axis=1)
    wh = wr.astype(jnp.bfloat16)
    wl = (wr - wh.astype(jnp.float32)).astype(jnp.bfloat16)
    wr = jnp.concatenate([wh, wl], axis=1)
    brt = jnp.concatenate([b_group, jnp.zeros((SUBLANES - N_GROUPS,), jnp.float32), b_expert])[:, None]
    return wr, brt


def _moe(layer, h, xrows, ri, rf, cnt, w_gate_up, w_down, gfin, final_norm):
    T = h.shape[0]
    tm = TM_EXPERT
    capacity = -(-(2 * T) // tm) * tm + N_EXPERTS * tm
    counts = cnt[:, 0].astype(jnp.int32)
    ntile = (counts + tm - 1) // tm
    tend = jnp.cumsum(ntile)
    pstart = (tend - ntile) * tm
    n_used = tend[-1:]
    padrow = jnp.concatenate([jnp.where(ntile > 0, (tend - 1) * (tm * XS_CHUNKS), -1), n_used]
                             ).astype(jnp.int32)

    dst = _slot_rows_call(ri, pstart)
    xs = _dispatch_call(dst[0], dst[1], padrow, xrows, capacity)
    ys = _expert_call(layer, (tend - ntile).astype(jnp.int32), ntile.astype(jnp.int32), xs,
                      w_gate_up, w_down, capacity)
    gates = rf[:2].T
    return _combine_call(dst[2], dst[3], ys, h, gates, gfin, final_norm)


def kernel(x, norm_mix, norm_ffn, final_norm, rel_bias, attn_w_qkv, attn_b_qkv, attn_w_o, attn_b_o,
           attn_sinks, conv_w_in, conv_w, conv_w_out, moe_w_group, moe_b_group, moe_w_expert,
           moe_b_expert, moe_w_gate_up, moe_w_down):
    B, S, D = x.shape
    T = B * S
    x2 = x.reshape(T, D)
    nq = N_Q_HEADS * HEAD_DIM
    scale = HEAD_DIM ** -0.5 * LOG2_E

    wqkv = attn_w_qkv[0]
    bqkv = attn_b_qkv[0]
    w_all = jnp.concatenate([_group_heads_last(wqkv[:, :nq]) * scale, wqkv[:, nq:]],
                            axis=1).astype(jnp.bfloat16)
    b_all = jnp.concatenate([_group_heads_last(bqkv[:nq]) * scale, bqkv[nq:]])[None, :]
    q, kt, v = _qkv_call(x2, norm_mix[0][None, :], w_all, b_all)
    a = _attn_call(q, kt, v, _rel_bias_blocks(rel_bias, attn_sinks[0]), B, S)
    wr, brt = _router_weights(moe_w_group[0], moe_b_group[0], moe_w_expert[0], moe_b_expert[0])
    w_o = _group_heads_last(attn_w_o[0].T).T.astype(jnp.bfloat16)
    h, xrows, ri, rf, cnt = _attn_out_call(
        a, w_o, attn_b_o[0][None, :], x2, norm_ffn[0][None, :], wr, brt)
    h = _moe(0, h, xrows, ri, rf, cnt, moe_w_gate_up, moe_w_down, final_norm[None, :], False)

    wr, brt = _router_weights(moe_w_group[1], moe_b_group[1], moe_w_expert[1], moe_b_expert[1])
    h, xrows, ri, rf, cnt = _conv_mixer_call(
        h, norm_mix[1][None, :], conv_w_in[0].astype(jnp.bfloat16), conv_w[0],
        conv_w_out[0].astype(jnp.bfloat16), norm_ffn[1][None, :], wr, brt, S)
    out = _moe(1, h, xrows, ri, rf, cnt, moe_w_gate_up, moe_w_down, final_norm[None, :], True)
    return out.reshape(B, S, D)
```

```python
import functools
import math

import numpy as np
import jax
import jax.numpy as jnp
from jax import lax
from jax.experimental import pallas as pl
from jax.experimental.pallas import tpu as pltpu

D_MODEL = 1024
N_Q_HEADS = 16
N_KV_HEADS = 2
HEAD_DIM = 64
GROUP = N_Q_HEADS // N_KV_HEADS
WINDOW = 128
ATTN_BLOCK = 128
N_REL_BUCKETS = 32
REL_MAX_DISTANCE = 128
CONV_WIDTH = 3
N_GROUPS = 4
EXPERTS_PER_GROUP = 8
N_EXPERTS = N_GROUPS * EXPERTS_PER_GROUP
EXPERT_FF = 512
RMS_EPS = 1e-5
LOG2_E = math.log2(math.e)

LANES = 128
SUBLANES = 8
ROW_CHUNKS = D_MODEL // LANES
XS_CHUNKS = ROW_CHUNKS // 2
VMEM_LIMIT = 56 * 1024 * 1024

TM_DENSE = 1024
TQ = 1024
TM_MIX1 = 1024
TM_EXPERT = 512
TM_DISPATCH = 1024
TM_COMBINE = 256
DMA_UNROLL = 8
TM_SLOT_ROWS = 4096
EXPERT_BUFS = 4
ROUTER_ROWS = 40


def _rms(x, g):
    return x * lax.rsqrt(jnp.mean(x * x, axis=-1, keepdims=True) + RMS_EPS) * g


def _pack_bf16_pairs(x):
    bits = lax.bitcast_convert_type(x.astype(jnp.bfloat16).astype(jnp.float32), jnp.uint32)
    return [(bits[:, (2 * c) * LANES:(2 * c + 1) * LANES] >> 16)
            | bits[:, (2 * c + 1) * LANES:(2 * c + 2) * LANES] for c in range(XS_CHUNKS)]


def _unpack_bf16_pairs(words):
    cols = []
    for w in words:
        cols.append(lax.bitcast_convert_type(w << 16, jnp.float32))
        cols.append(lax.bitcast_convert_type(w & jnp.uint32(0xFFFF0000), jnp.float32))
    return jnp.concatenate(cols, axis=-1)


def _params(sem):
    return pltpu.CompilerParams(dimension_semantics=sem, vmem_limit_bytes=VMEM_LIMIT)


def _qkv_kernel(x_ref, g_ref, w_ref, b_ref, q_ref, kt_ref, v_ref):
    xn = _rms(x_ref[...], g_ref[...]).astype(jnp.bfloat16)
    out = jnp.dot(xn, w_ref[...], preferred_element_type=jnp.float32) + b_ref[...]
    nq = N_Q_HEADS * HEAD_DIM
    nk = N_KV_HEADS * HEAD_DIM
    q_ref[...] = out[:, :nq].astype(jnp.bfloat16)
    kt_ref[...] = out[:, nq:nq + nk].T.astype(jnp.bfloat16)
    v_ref[...] = out[:, nq + nk:].astype(jnp.bfloat16)


def _qkv_call(x2, g, w, b):
    T = x2.shape[0]
    nq = N_Q_HEADS * HEAD_DIM
    nk = N_KV_HEADS * HEAD_DIM
    tm = TM_DENSE
    return pl.pallas_call(
        _qkv_kernel,
        grid=(T // tm,),
        in_specs=[
            pl.BlockSpec((tm, D_MODEL), lambda i: (i, 0)),
            pl.BlockSpec((1, D_MODEL), lambda i: (0, 0)),
            pl.BlockSpec((D_MODEL, nq + 2 * nk), lambda i: (0, 0)),
            pl.BlockSpec((1, nq + 2 * nk), lambda i: (0, 0)),
        ],
        out_specs=[
            pl.BlockSpec((tm, nq), lambda i: (i, 0)),
            pl.BlockSpec((nk, tm), lambda i: (0, i)),
            pl.BlockSpec((tm, nk), lambda i: (i, 0)),
        ],
        out_shape=[
            jax.ShapeDtypeStruct((T, nq), jnp.bfloat16),
            jax.ShapeDtypeStruct((nk, T), jnp.bfloat16),
            jax.ShapeDtypeStruct((T, nk), jnp.bfloat16),
        ],
        compiler_params=_params(("parallel",)),
        name="qkv_proj",
    )(x2, g, w, b)


def _attn_kernel(q_ref, kt_ref, ktp_ref, v_ref, vp_ref, bias_ref, o_ref):
    first_tile = pl.program_id(1) == 0
    nblk = TQ // ATTN_BLOCK
    qi_io = lax.broadcasted_iota(jnp.int32, (ATTN_BLOCK, 2 * ATTN_BLOCK), 0)
    kj_io = lax.broadcasted_iota(jnp.int32, (ATTN_BLOCK, 2 * ATTN_BLOCK), 1)
    dist = qi_io + ATTN_BLOCK - kj_io
    band = (dist >= 0) & (dist < WINDOW)
    sink_col = kj_io == 0
    kt_keep = lax.broadcasted_iota(jnp.int32, (N_KV_HEADS * HEAD_DIM, 2 * ATTN_BLOCK), 1) > 0
    v_keep = lax.broadcasted_iota(jnp.int32, (2 * ATTN_BLOCK, N_KV_HEADS * HEAD_DIM), 0) > 0
    low_half = lax.broadcasted_iota(jnp.int32, (ATTN_BLOCK, LANES), 1) < HEAD_DIM
    for qi in range(nblk):
        r0 = qi * ATTN_BLOCK
        if qi == 0:
            kt_blk = jnp.concatenate([ktp_ref[...], kt_ref[:, 0:ATTN_BLOCK]], axis=1)
            v_blk = jnp.concatenate([vp_ref[...], v_ref[0:ATTN_BLOCK, :]], axis=0)
            mask = (band & (jnp.logical_not(first_tile) | (kj_io >= ATTN_BLOCK))) | sink_col
        else:
            kt_blk = kt_ref[:, r0 - ATTN_BLOCK:r0 + ATTN_BLOCK]
            v_blk = v_ref[r0 - ATTN_BLOCK:r0 + ATTN_BLOCK, :]
            mask = band | sink_col
        kt_blk = jnp.where(kt_keep, kt_blk, jnp.zeros_like(kt_blk))
        v_blk = jnp.where(v_keep, v_blk, jnp.zeros_like(v_blk))
        for m in range(GROUP):
            qg = q_ref[r0:r0 + ATTN_BLOCK, m * LANES:(m + 1) * LANES]
            halves = []
            for half in range(N_KV_HEADS):
                h = m + GROUP * half
                keep = low_half if half == 0 else jnp.logical_not(low_half)
                qpad = jnp.where(keep, qg, jnp.zeros_like(qg))
                s = jnp.dot(qpad, kt_blk, preferred_element_type=jnp.float32)
                s = jnp.where(mask, s + bias_ref[h], -1e30)
                mx = jnp.max(s, axis=-1, keepdims=True)
                p = jnp.exp2(s - mx)
                den = jnp.sum(p, axis=-1, keepdims=True)
                pv = jnp.dot(p.astype(jnp.bfloat16), v_blk, preferred_element_type=jnp.float32)
                halves.append(pv * (1.0 / den))
            og = jnp.where(low_half, halves[0], halves[1])
            o_ref[r0:r0 + ATTN_BLOCK, m * LANES:(m + 1) * LANES] = og.astype(jnp.bfloat16)


def _attn_call(q, kt, v, bias, batch, seq):
    T = q.shape[0]
    nq = N_Q_HEADS * HEAD_DIM
    nk = N_KV_HEADS * HEAD_DIM
    tiles = seq // TQ
    per = TQ // ATTN_BLOCK

    def cur(b, j):
        return b * tiles + j

    def prev(b, j):
        return jnp.maximum((b * tiles + j) * per - 1, b * tiles * per)

    return pl.pallas_call(
        _attn_kernel,
        grid=(batch, tiles),
        in_specs=[
            pl.BlockSpec((TQ, nq), lambda b, j: (cur(b, j), 0)),
            pl.BlockSpec((nk, TQ), lambda b, j: (0, cur(b, j))),
            pl.BlockSpec((nk, ATTN_BLOCK), lambda b, j: (0, prev(b, j))),
            pl.BlockSpec((TQ, nk), lambda b, j: (cur(b, j), 0)),
            pl.BlockSpec((ATTN_BLOCK, nk), lambda b, j: (prev(b, j), 0)),
            pl.BlockSpec((N_Q_HEADS, ATTN_BLOCK, 2 * ATTN_BLOCK), lambda b, j: (0, 0, 0)),
        ],
        out_specs=pl.BlockSpec((TQ, nq), lambda b, j: (cur(b, j), 0)),
        out_shape=jax.ShapeDtypeStruct((T, nq), jnp.bfloat16),
        compiler_params=_params(("parallel", "parallel")),
        name="swa_attention",
    )(q, kt, kt, v, v, bias)


def _router_epilogue(h, gffn_ref, wr_ref, brt_ref, carry_ref,
                     h_out_ref, xrows_ref, ri_ref, rf_ref, cnt_ref):
    tm = h.shape[0]
    h_out_ref[...] = h
    xn = _rms(h, gffn_ref[...])
    xh = xn.astype(jnp.bfloat16)
    for c, w in enumerate(_pack_bf16_pairs(xn)):
        xrows_ref[pl.ds(c, tm, stride=XS_CHUNKS), :] = w

    xl = (xn - xh.astype(jnp.float32)).astype(jnp.bfloat16)
    wcat = wr_ref[...]
    prod = jnp.dot(xh, wcat, preferred_element_type=jnp.float32)
    logits = (prod[:, :LANES] + prod[:, LANES:]) + jnp.dot(
        xl, wcat[:, :LANES], preferred_element_type=jnp.float32)
    lt = logits.T[:ROUTER_ROWS, :] + brt_ref[...]

    gl = [lt[g:g + 1, :] for g in range(N_GROUPS)]
    gmax = functools.reduce(jnp.maximum, gl)
    gexp = [jnp.exp(x - gmax) for x in gl]
    gsum = functools.reduce(lambda a, b: a + b, gexp)
    gprob = [x / gsum for x in gexp]
    g_prob = functools.reduce(jnp.maximum, gprob)
    g_idx = jnp.full(g_prob.shape, N_GROUPS - 1, jnp.int32)
    for g in range(N_GROUPS - 2, -1, -1):
        g_idx = jnp.where(gprob[g] == g_prob, g, g_idx)

    el = []
    for j in range(EXPERTS_PER_GROUP):
        x = lt[SUBLANES + j:SUBLANES + j + 1, :]
        for g in range(1, N_GROUPS):
            r = SUBLANES + g * EXPERTS_PER_GROUP + j
            x = jnp.where(g_idx == g, lt[r:r + 1, :], x)
        el.append(x)
    emax = functools.reduce(jnp.maximum, el)
    eexp = [jnp.exp(x - emax) for x in el]
    esum = functools.reduce(lambda a, b: a + b, eexp)
    eprob = [x / esum for x in eexp]
    p1 = functools.reduce(jnp.maximum, eprob)
    i1 = jnp.full(p1.shape, EXPERTS_PER_GROUP - 1, jnp.int32)
    for j in range(EXPERTS_PER_GROUP - 2, -1, -1):
        i1 = jnp.where(eprob[j] == p1, j, i1)
    rest = [jnp.where(i1 == j, -1.0, eprob[j]) for j in range(EXPERTS_PER_GROUP)]
    p2 = functools.reduce(jnp.maximum, rest)
    i2 = jnp.full(p2.shape, EXPERTS_PER_GROUP - 1, jnp.int32)
    for j in range(EXPERTS_PER_GROUP - 2, -1, -1):
        i2 = jnp.where(rest[j] == p2, j, i2)
    psum = p1 + p2
    gate0 = g_prob * (p1 / psum)
    gate1 = g_prob * (p2 / psum)
    e0 = g_idx * EXPERTS_PER_GROUP + i1
    e1 = g_idx * EXPERTS_PER_GROUP + i2

    eio = lax.broadcasted_iota(jnp.int32, (N_EXPERTS, tm), 0)
    oh0 = (eio == e0).astype(jnp.float32)
    oh1 = (eio == e1).astype(jnp.float32)
    both = oh0 + oh1
    tr = lax.broadcasted_iota(jnp.int32, (tm, tm), 0)
    tc = lax.broadcasted_iota(jnp.int32, (tm, tm), 1)
    upper = (tr < tc).astype(jnp.bfloat16)
    before = jnp.dot(both.astype(jnp.bfloat16), upper, preferred_element_type=jnp.float32)
    before = before + carry_ref[...]
    rank0 = jnp.sum(oh0 * before, axis=0, keepdims=True)
    rank1 = jnp.sum(oh1 * (before + oh0), axis=0, keepdims=True)
    carry_ref[...] = carry_ref[...] + jnp.sum(both, axis=1, keepdims=True)
    cnt_ref[...] = jnp.broadcast_to(carry_ref[...], cnt_ref.shape)

    zi = jnp.zeros_like(e0)
    ri_ref[...] = jnp.concatenate(
        [e0, e1, rank0.astype(jnp.int32), rank1.astype(jnp.int32), zi, zi, zi, zi], axis=0)
    zf = jnp.zeros_like(gate0)
    rf_ref[...] = jnp.concatenate([gate0, gate1, zf, zf, zf, zf, zf, zf], axis=0)


def _epilogue_out_specs(tm):
    return [
        pl.BlockSpec((tm, D_MODEL), lambda i: (i, 0)),
        pl.BlockSpec((tm * XS_CHUNKS, LANES), lambda i: (i, 0)),
        pl.BlockSpec((SUBLANES, tm), lambda i: (0, i)),
        pl.BlockSpec((SUBLANES, tm), lambda i: (0, i)),
        pl.BlockSpec((N_EXPERTS, LANES), lambda i: (0, 0)),
    ]


def _epilogue_out_shapes(T):
    return [
        jax.ShapeDtypeStruct((T, D_MODEL), jnp.float32),
        jax.ShapeDtypeStruct((T * XS_CHUNKS, LANES), jnp.uint32),
        jax.ShapeDtypeStruct((SUBLANES, T), jnp.int32),
        jax.ShapeDtypeStruct((SUBLANES, T), jnp.float32),
        jax.ShapeDtypeStruct((N_EXPERTS, LANES), jnp.float32),
    ]


def _epilogue_in_specs():
    return [
        pl.BlockSpec((1, D_MODEL), lambda i: (0, 0)),
        pl.BlockSpec((D_MODEL, 2 * LANES), lambda i: (0, 0)),
        pl.BlockSpec((ROUTER_ROWS, 1), lambda i: (0, 0)),
    ]


def _attn_out_kernel(a_ref, w_ref, b_ref, res_ref, gffn_ref, wr_ref, brt_ref,
                     h_out_ref, xrows_ref, ri_ref, rf_ref, cnt_ref, carry_ref):
    @pl.when(pl.program_id(0) == 0)
    def _():
        carry_ref[...] = jnp.zeros_like(carry_ref)

    mix = jnp.dot(a_ref[...], w_ref[...], preferred_element_type=jnp.float32) + b_ref[...]
    h = res_ref[...] + mix
    _router_epilogue(h, gffn_ref, wr_ref, brt_ref, carry_ref,
                     h_out_ref, xrows_ref, ri_ref, rf_ref, cnt_ref)


def _attn_out_call(a, w, b, res, gffn, wr, brt):
    T = a.shape[0]
    tm = TM_DENSE
    return pl.pallas_call(
        _attn_out_kernel,
        grid=(T // tm,),
        in_specs=[
            pl.BlockSpec((tm, D_MODEL), lambda i: (i, 0)),
            pl.BlockSpec((D_MODEL, D_MODEL), lambda i: (0, 0)),
            pl.BlockSpec((1, D_MODEL), lambda i: (0, 0)),
            pl.BlockSpec((tm, D_MODEL), lambda i: (i, 0)),
        ] + _epilogue_in_specs(),
        out_specs=_epilogue_out_specs(tm),
        out_shape=_epilogue_out_shapes(T),
        scratch_shapes=[pltpu.VMEM((N_EXPERTS, 1), jnp.float32)],
        compiler_params=_params(("arbitrary",)),
        name="attn_out_router",
    )(a, w, b, res, gffn, wr, brt)


def _conv_mixer_kernel(tiles_per_seq, x_ref, g_ref, win_ref, cw_ref, wout_ref,
                       gffn_ref, wr_ref, brt_ref,
                       h_out_ref, xrows_ref, ri_ref, rf_ref, cnt_ref, carry_ref, tail_ref):
    i = pl.program_id(0)
    tm = x_ref.shape[0]

    @pl.when(i == 0)
    def _():
        carry_ref[...] = jnp.zeros_like(carry_ref)

    @pl.when(i % tiles_per_seq == 0)
    def _():
        tail_ref[...] = jnp.zeros_like(tail_ref)

    x = x_ref[...]
    xn = _rms(x, g_ref[...]).astype(jnp.bfloat16)
    bcu = jnp.dot(xn, win_ref[...], preferred_element_type=jnp.float32)
    b_gate = bcu[:, :D_MODEL]
    z = bcu[:, D_MODEL:2 * D_MODEL] * bcu[:, 2 * D_MODEL:]
    row = lax.broadcasted_iota(jnp.int32, (tm, D_MODEL), 0)
    tail = tail_ref[...]
    t1 = tail[SUBLANES - 1:SUBLANES, :]
    t2 = tail[SUBLANES - 2:SUBLANES - 1, :]
    z1 = jnp.where(row == 0, t1, pltpu.roll(z, 1, axis=0))
    z2 = jnp.where(row == 0, t2, jnp.where(row == 1, t1, pltpu.roll(z, 2, axis=0)))
    tail_ref[...] = z[tm - SUBLANES:, :]
    conv = z2 * cw_ref[0:1, :]
    conv = conv + z1 * cw_ref[1:2, :]
    conv = conv + z * cw_ref[2:3, :]
    gated = (b_gate * conv).astype(jnp.bfloat16)
    h = x + jnp.dot(gated, wout_ref[...], preferred_element_type=jnp.float32)
    _router_epilogue(h, gffn_ref, wr_ref, brt_ref, carry_ref,
                     h_out_ref, xrows_ref, ri_ref, rf_ref, cnt_ref)


def _conv_mixer_call(x2, g, win, cw, wout, gffn, wr, brt, seq):
    T = x2.shape[0]
    tm = TM_MIX1
    return pl.pallas_call(
        functools.partial(_conv_mixer_kernel, seq // tm),
        grid=(T // tm,),
        in_specs=[
            pl.BlockSpec((tm, D_MODEL), lambda i: (i, 0)),
            pl.BlockSpec((1, D_MODEL), lambda i: (0, 0)),
            pl.BlockSpec((D_MODEL, 3 * D_MODEL), lambda i: (0, 0)),
            pl.BlockSpec((CONV_WIDTH, D_MODEL), lambda i: (0, 0)),
            pl.BlockSpec((D_MODEL, D_MODEL), lambda i: (0, 0)),
        ] + _epilogue_in_specs(),
        out_specs=_epilogue_out_specs(tm),
        out_shape=_epilogue_out_shapes(T),
        scratch_shapes=[pltpu.VMEM((N_EXPERTS, 1), jnp.float32),
                        pltpu.VMEM((SUBLANES, D_MODEL), jnp.float32)],
        compiler_params=_params(("arbitrary",)),
        name="conv_mixer_router",
    )(x2, g, win, cw, wout, gffn, wr, brt)


def _row_copy(src_hbm, src_row, dst_hbm, dst_row, sem, chunks):
    return pltpu.make_async_copy(
        src_hbm.at[pl.ds(pl.multiple_of(src_row, chunks), chunks)],
        dst_hbm.at[pl.ds(pl.multiple_of(dst_row, chunks), chunks)], sem)


def _slot_rows_kernel(ri_ref, pstart_ref, o_ref):
    tb = ri_ref.shape[1]
    eio = lax.broadcasted_iota(jnp.int32, (N_EXPERTS, tb), 0)
    pstart = pstart_ref[...]
    slots = []
    for k in range(2):
        start = jnp.sum(jnp.where(eio == ri_ref[k:k + 1, :], pstart, 0), axis=0, keepdims=True)
        slots.append(start + ri_ref[2 + k:3 + k, :])
    zi = jnp.zeros_like(slots[0])
    o_ref[...] = jnp.concatenate(
        [s * XS_CHUNKS for s in slots] + [zi] * (SUBLANES - 2), axis=0)


def _slot_rows_call(ri, pstart):
    T = ri.shape[1]
    tb = TM_SLOT_ROWS
    return pl.pallas_call(
        _slot_rows_kernel,
        grid=(T // tb,),
        in_specs=[pl.BlockSpec((SUBLANES, tb), lambda i: (0, i)),
                  pl.BlockSpec((N_EXPERTS, 1), lambda i: (0, 0))],
        out_specs=pl.BlockSpec((SUBLANES, tb), lambda i: (0, i)),
        out_shape=jax.ShapeDtypeStruct((SUBLANES, T), jnp.int32),
        compiler_params=_params(("parallel",)),
        name="moe_slot_rows",
    )(ri, pstart[:, None])


def _dispatch_kernel(d0_ref, d1_ref, padrow_ref, x_ref, xs_hbm, zero_ref, sem, zsem):
    i = pl.program_id(0)
    tile_rows = TM_EXPERT * XS_CHUNKS

    @pl.when(i == 0)
    def _():
        zero_ref[...] = jnp.zeros_like(zero_ref)

        def zcopy(e):
            return pltpu.make_async_copy(
                zero_ref,
                xs_hbm.at[pl.ds(pl.multiple_of(padrow_ref[e], tile_rows), tile_rows)], zsem)

        def zstart(e, c):
            @pl.when(padrow_ref[e] >= 0)
            def _():
                zcopy(e).start()
            return c

        def zwait(e, c):
            @pl.when(padrow_ref[e] >= 0)
            def _():
                zcopy(e).wait()
            return c

        lax.fori_loop(0, N_EXPERTS, zstart, 0)
        lax.fori_loop(0, N_EXPERTS, zwait, 0)

        def tcopy(b):
            return pltpu.make_async_copy(
                zero_ref, xs_hbm.at[pl.ds(pl.multiple_of(b * tile_rows, tile_rows), tile_rows)], zsem)

        n_used = padrow_ref[N_EXPERTS]
        n_blk = xs_hbm.shape[0] // tile_rows
        lax.fori_loop(n_used, n_blk, lambda b, c: (tcopy(b).start(), c)[1], 0)
        lax.fori_loop(n_used, n_blk, lambda b, c: (tcopy(b).wait(), c)[1], 0)

    def issue(t, c):
        src = t * XS_CHUNKS
        _row_copy(x_ref, src, xs_hbm, d0_ref[t], sem, XS_CHUNKS).start(priority=0)
        _row_copy(x_ref, src, xs_hbm, d1_ref[t], sem, XS_CHUNKS).start(priority=1)
        return c

    lax.fori_loop(0, TM_DISPATCH, issue, 0, unroll=DMA_UNROLL)
    for _ in range(2):
        pltpu.make_async_copy(x_ref, xs_hbm.at[pl.ds(0, TM_DISPATCH * XS_CHUNKS)], sem).wait()


def _dispatch_call(d0, d1, padrow, xrows, capacity):
    T = d0.shape[0]
    return pl.pallas_call(
        _dispatch_kernel,
        grid=(T // TM_DISPATCH,),
        in_specs=[
            pl.BlockSpec((TM_DISPATCH,), lambda i: (i,), memory_space=pltpu.SMEM),
            pl.BlockSpec((TM_DISPATCH,), lambda i: (i,), memory_space=pltpu.SMEM),
            pl.BlockSpec(memory_space=pltpu.SMEM),
            pl.BlockSpec((TM_DISPATCH * XS_CHUNKS, LANES), lambda i: (i, 0)),
        ],
        out_specs=pl.BlockSpec(memory_space=pl.ANY),
        out_shape=jax.ShapeDtypeStruct((capacity * XS_CHUNKS, LANES), jnp.uint32),
        scratch_shapes=[pltpu.VMEM((TM_EXPERT * XS_CHUNKS, LANES), jnp.uint32),
                        pltpu.SemaphoreType.DMA, pltpu.SemaphoreType.DMA],
        compiler_params=_params(("arbitrary",)),
        name="moe_dispatch",
    )(d0, d1, padrow, xrows)


def _expert_kernel(tstart_ref, ntile_ref, xs_hbm, wgu_ref, wd_ref, ys_hbm,
                   xbuf, ybuf, wgu_bf, wd_bf, sem_in, sem_out):
    nb = EXPERT_BUFS
    e = pl.program_id(0)
    tm = TM_EXPERT
    tile_rows = tm * XS_CHUNKS
    t0 = tstart_ref[e]
    nt = ntile_ref[e]
    n_used = tstart_ref[N_EXPERTS - 1] + ntile_ref[N_EXPERTS - 1]

    def tile_at(ref, g, rows):
        return ref.at[pl.ds(pl.multiple_of(g * rows, rows), rows)]

    def in_copy(g, slot):
        return pltpu.make_async_copy(tile_at(xs_hbm, g, tile_rows), xbuf.at[slot], sem_in.at[slot])

    def out_copy(g, slot):
        return pltpu.make_async_copy(ybuf.at[slot], tile_at(ys_hbm, g, tile_rows), sem_out.at[slot])

    @pl.when(e == 0)
    def _():
        for k in range(nb - 1):
            @pl.when(k < n_used)
            def _():
                in_copy(k, k).start(priority=1)

    @pl.when(nt > 0)
    def _():
        wgu_bf[...] = wgu_ref[0, 0].astype(jnp.bfloat16)
        wd_bf[...] = wd_ref[0, 0].astype(jnp.bfloat16)

    def tile_body(g, carry):
        slot = g % nb
        in_copy(g, slot).wait()

        @pl.when(g + (nb - 1) < n_used)
        def _():
            in_copy(g + (nb - 1), (g + (nb - 1)) % nb).start(priority=1)

        @pl.when(g >= nb)
        def _():
            out_copy(g - nb, slot).wait()

        x = _unpack_bf16_pairs(
            [xbuf[slot, pl.ds(c, tm, stride=XS_CHUNKS), :] for c in range(XS_CHUNKS)])
        gu = jnp.dot(x.astype(jnp.bfloat16), wgu_bf[...], preferred_element_type=jnp.float32)
        g_act = gu[:, :EXPERT_FF]
        u = gu[:, EXPERT_FF:]
        act = (g_act / (1.0 + jnp.exp(-g_act))) * u
        y = jnp.dot(act.astype(jnp.bfloat16), wd_bf[...], preferred_element_type=jnp.float32)
        for c, w in enumerate(_pack_bf16_pairs(y)):
            ybuf[slot, pl.ds(c, tm, stride=XS_CHUNKS), :] = w
        out_copy(g, slot).start(priority=1)
        return carry

    lax.fori_loop(t0, t0 + nt, tile_body, 0)

    @pl.when(e == N_EXPERTS - 1)
    def _():
        for k in range(1, nb + 1):
            @pl.when(n_used >= k)
            def _():
                out_copy(n_used - k, (n_used - k) % nb).wait()

        n_blk = ys_hbm.shape[0] // tile_rows
        ybuf[0] = jnp.zeros(ybuf.shape[1:], ybuf.dtype)
        lax.fori_loop(n_used, n_blk, lambda b, c: (out_copy(b, 0).start(), c)[1], 0)
        lax.fori_loop(n_used, n_blk, lambda b, c: (out_copy(b, 0).wait(), c)[1], 0)


def _expert_call(layer, tstart, ntile, xs, wgu, wd, capacity):
    tile_rows = TM_EXPERT * XS_CHUNKS

    def w_map(e, ts, nt):
        return (layer, e, 0, 0)

    return pl.pallas_call(
        _expert_kernel,
        grid_spec=pltpu.PrefetchScalarGridSpec(
            num_scalar_prefetch=2,
            grid=(N_EXPERTS,),
            in_specs=[
                pl.BlockSpec(memory_space=pl.ANY),
                pl.BlockSpec((1, 1, D_MODEL, 2 * EXPERT_FF), w_map),
                pl.BlockSpec((1, 1, EXPERT_FF, D_MODEL), w_map),
            ],
            out_specs=pl.BlockSpec(memory_space=pl.ANY),
            scratch_shapes=[pltpu.VMEM((EXPERT_BUFS, tile_rows, LANES), jnp.uint32),
                            pltpu.VMEM((EXPERT_BUFS, tile_rows, LANES), jnp.uint32),
                            pltpu.VMEM((D_MODEL, 2 * EXPERT_FF), jnp.bfloat16),
                            pltpu.VMEM((EXPERT_FF, D_MODEL), jnp.bfloat16),
                            pltpu.SemaphoreType.DMA((EXPERT_BUFS,)),
                            pltpu.SemaphoreType.DMA((EXPERT_BUFS,))],
        ),
        out_shape=jax.ShapeDtypeStruct((capacity * XS_CHUNKS, LANES), jnp.uint32),
        compiler_params=_params(("arbitrary",)),
        name="moe_experts",
    )(tstart, ntile, xs, wgu, wd)


def _gathered_residual(tm, d0_ref, d1_ref, d0n_ref, d1n_ref, ys_hbm, h_ref, gate_ref,
                       y0_ref, y1_ref, sem):
    i = pl.program_id(0)
    slot = i % 2

    def issue_tile(d0r, d1r, s):
        def issue(t, c):
            dst = t * XS_CHUNKS
            _row_copy(ys_hbm, d0r[t], y0_ref.at[s], dst, sem.at[s], XS_CHUNKS).start(priority=0)
            _row_copy(ys_hbm, d1r[t], y1_ref.at[s], dst, sem.at[s], XS_CHUNKS).start(priority=1)
            return c
        lax.fori_loop(0, tm, issue, 0, unroll=DMA_UNROLL)

    @pl.when(i == 0)
    def _():
        issue_tile(d0_ref, d1_ref, 0)

    @pl.when(i + 1 < pl.num_programs(0))
    def _():
        issue_tile(d0n_ref, d1n_ref, 1 - slot)

    for buf in (y0_ref, y1_ref):
        pltpu.make_async_copy(ys_hbm.at[pl.ds(0, tm * XS_CHUNKS)], buf.at[slot], sem.at[slot]).wait()

    g0 = gate_ref[:, 0:1]
    g1 = gate_ref[:, 1:2]
    y0 = _unpack_bf16_pairs([y0_ref[slot, pl.ds(c, tm, stride=XS_CHUNKS), :] for c in range(XS_CHUNKS)])
    y1 = _unpack_bf16_pairs([y1_ref[slot, pl.ds(c, tm, stride=XS_CHUNKS), :] for c in range(XS_CHUNKS)])
    return h_ref[...] + (y0 * g0 + y1 * g1)


def _gather_in_specs(tm, n_tiles):
    last = n_tiles - 1
    return [
        pl.BlockSpec((tm,), lambda i: (i,), memory_space=pltpu.SMEM),
        pl.BlockSpec((tm,), lambda i: (i,), memory_space=pltpu.SMEM),
        pl.BlockSpec((tm,), lambda i: (jnp.minimum(i + 1, last),), memory_space=pltpu.SMEM),
        pl.BlockSpec((tm,), lambda i: (jnp.minimum(i + 1, last),), memory_space=pltpu.SMEM),
        pl.BlockSpec(memory_space=pl.ANY),
        pl.BlockSpec((tm, D_MODEL), lambda i: (i, 0)),
        pl.BlockSpec((tm, 2), lambda i: (i, 0)),
    ]


def _gather_scratch(tm):
    return [pltpu.VMEM((2, tm * XS_CHUNKS, LANES), jnp.uint32),
            pltpu.VMEM((2, tm * XS_CHUNKS, LANES), jnp.uint32),
            pltpu.SemaphoreType.DMA((2,))]


def _combine_kernel(final_norm, d0_ref, d1_ref, d0n_ref, d1n_ref, ys_hbm, h_ref, gate_ref, gfin_ref,
                    o_ref, y0_ref, y1_ref, sem):
    h = _gathered_residual(TM_COMBINE, d0_ref, d1_ref, d0n_ref, d1n_ref, ys_hbm, h_ref, gate_ref,
                           y0_ref, y1_ref, sem)
    if final_norm:
        h = _rms(h, gfin_ref[...])
    o_ref[...] = h


def _combine_call(d0, d1, ys, h, gates, gfin, final_norm):
    T = h.shape[0]
    tm = TM_COMBINE
    return pl.pallas_call(
        functools.partial(_combine_kernel, final_norm),
        grid=(T // tm,),
        in_specs=_gather_in_specs(tm, T // tm) + [pl.BlockSpec((1, D_MODEL), lambda i: (0, 0))],
        out_specs=pl.BlockSpec((tm, D_MODEL), lambda i: (i, 0)),
        out_shape=jax.ShapeDtypeStruct((T, D_MODEL), jnp.float32),
        scratch_shapes=_gather_scratch(tm),
        compiler_params=_params(("arbitrary",)),
        name="moe_combine",
    )(d0, d1, d0, d1, ys, h, gates, gfin)


def _group_heads_last(w):
    lead = w.shape[:-1]
    w = w.reshape(lead + (N_KV_HEADS, GROUP, HEAD_DIM))
    w = jnp.swapaxes(w, -3, -2)
    return w.reshape(lead + (N_Q_HEADS * HEAD_DIM,))


def _rel_bucket_table():
    qi = np.arange(ATTN_BLOCK)[:, None]
    kj = np.arange(2 * ATTN_BLOCK)[None, :]
    dist = np.maximum(qi + ATTN_BLOCK - kj, 0)
    max_exact = N_REL_BUCKETS // 2
    d = np.maximum(dist, max_exact).astype(np.float32)
    large = max_exact + (np.log(d / np.float32(max_exact)) / np.float32(math.log(REL_MAX_DISTANCE / max_exact))
                         * np.float32(N_REL_BUCKETS - max_exact)).astype(np.int32)
    large = np.minimum(large, N_REL_BUCKETS - 1)
    return np.where(dist < max_exact, dist, large).astype(np.int32)


def _rel_bias_blocks(rel_bias, sinks):
    onehot = (jnp.asarray(_rel_bucket_table())[..., None] == jnp.arange(N_REL_BUCKETS)).astype(jnp.float32)
    bias = jnp.einsum("qkb,bh->hqk", onehot, rel_bias.astype(jnp.float32),
                      precision=lax.Precision.HIGHEST)
    sink_col = jnp.broadcast_to(sinks.astype(jnp.float32)[:, None, None], (N_Q_HEADS, ATTN_BLOCK, 1))
    return jnp.concatenate([sink_col, bias[:, :, 1:]], axis=2) * LOG2_E


def _router_weights(w_group, b_group, w_expert, b_expert):
    zw = jnp.zeros((D_MODEL, SUBLANES - N_GROUPS), jnp.float32)
    zw2 = jnp.zeros((D_MODEL, LANES - SUBLANES - N_EXPERTS), jnp.float32)
    wr = jnp.concatenate([w_group, zw, w_expert, zw2], axis=1)
    wh = wr.astype(jnp.bfloat16)
    wl = (wr - wh.astype(jnp.float32)).astype(jnp.bfloat16)
    wr = jnp.concatenate([wh, wl], axis=1)
    brt = jnp.concatenate([b_group, jnp.zeros((SUBLANES - N_GROUPS,), jnp.float32), b_expert])[:, None]
    return wr, brt


def _moe(layer, h, xrows, ri, rf, cnt, w_gate_up, w_down, gfin, final_norm):
    T = h.shape[0]
    tm = TM_EXPERT
    capacity = -(-(2 * T) // tm) * tm + N_EXPERTS * tm
    counts = cnt[:, 0].astype(jnp.int32)
    ntile = (counts + tm - 1) // tm
    tend = jnp.cumsum(ntile)
    pstart = (tend - ntile) * tm
    n_used = tend[-1:]
    padrow = jnp.concatenate([jnp.where(ntile > 0, (tend - 1) * (tm * XS_CHUNKS), -1), n_used]
                             ).astype(jnp.int32)

    dst = _slot_rows_call(ri, pstart)
    xs = _dispatch_call(dst[0], dst[1], padrow, xrows, capacity)
    ys = _expert_call(layer, (tend - ntile).astype(jnp.int32), ntile.astype(jnp.int32), xs,
                      w_gate_up, w_down, capacity)
    gates = rf[:2].T
    return _combine_call(dst[0], dst[1], ys, h, gates, gfin, final_norm)


def kernel(x, norm_mix, norm_ffn, final_norm, rel_bias, attn_w_qkv, attn_b_qkv, attn_w_o, attn_b_o,
           attn_sinks, conv_w_in, conv_w, conv_w_out, moe_w_group, moe_b_group, moe_w_expert,
           moe_b_expert, moe_w_gate_up, moe_w_down):
    B, S, D = x.shape
    T = B * S
    x2 = x.reshape(T, D)
    nq = N_Q_HEADS * HEAD_DIM
    scale = HEAD_DIM ** -0.5 * LOG2_E

    wqkv = attn_w_qkv[0]
    bqkv = attn_b_qkv[0]
    w_all = jnp.concatenate([_group_heads_last(wqkv[:, :nq]) * scale, wqkv[:, nq:]],
                            axis=1).astype(jnp.bfloat16)
    b_all = jnp.concatenate([_group_heads_last(bqkv[:nq]) * scale, bqkv[nq:]])[None, :]
    q, kt, v = _qkv_call(x2, norm_mix[0][None, :], w_all, b_all)
    a = _attn_call(q, kt, v, _rel_bias_blocks(rel_bias, attn_sinks[0]), B, S)
    wr, brt = _router_weights(moe_w_group[0], moe_b_group[0], moe_w_expert[0], moe_b_expert[0])
    w_o = _group_heads_last(attn_w_o[0].T).T.astype(jnp.bfloat16)
    h, xrows, ri, rf, cnt = _attn_out_call(
        a, w_o, attn_b_o[0][None, :], x2, norm_ffn[0][None, :], wr, brt)
    h = _moe(0, h, xrows, ri, rf, cnt, moe_w_gate_up, moe_w_down, final_norm[None, :], False)

    wr, brt = _router_weights(moe_w_group[1], moe_b_group[1], moe_w_expert[1], moe_b_expert[1])
    h, xrows, ri, rf, cnt = _conv_mixer_call(
        h, norm_mix[1][None, :], conv_w_in[0].astype(jnp.bfloat16), conv_w[0],
        conv_w_out[0].astype(jnp.bfloat16), norm_ffn[1][None, :], wr, brt, S)
    out = _moe(1, h, xrows, ri, rf, cnt, moe_w_gate_up, moe_w_down, final_norm[None, :], True)
    return out.reshape(B, S, D)
```

```python
import functools
import math

import numpy as np
import jax
import jax.numpy as jnp
from jax import lax
from jax.experimental import pallas as pl
from jax.experimental.pallas import tpu as pltpu

D_MODEL = 1024
N_Q_HEADS = 16
N_KV_HEADS = 2
HEAD_DIM = 64
GROUP = N_Q_HEADS // N_KV_HEADS
WINDOW = 128
ATTN_BLOCK = 128
N_REL_BUCKETS = 32
REL_MAX_DISTANCE = 128
CONV_WIDTH = 3
N_GROUPS = 4
EXPERTS_PER_GROUP = 8
N_EXPERTS = N_GROUPS * EXPERTS_PER_GROUP
EXPERT_FF = 512
RMS_EPS = 1e-5
LOG2_E = math.log2(math.e)

LANES = 128
SUBLANES = 8
ROW_CHUNKS = D_MODEL // LANES
XS_CHUNKS = ROW_CHUNKS // 2
VMEM_LIMIT = 56 * 1024 * 1024

TM_DENSE = 1024
TQ = 1024
TM_MIX1 = 1024
TM_EXPERT = 512
TM_DISPATCH = 2048
TM_COMBINE = 128
DMA_UNROLL = 8
TM_SLOT_ROWS = 4096
EXPERT_BUFS = 4
ROUTER_ROWS = 40


def _rms(x, g):
    return x * lax.rsqrt(jnp.mean(x * x, axis=-1, keepdims=True) + RMS_EPS) * g


def _pack_bf16_pairs(x):
    bits = lax.bitcast_convert_type(x.astype(jnp.bfloat16).astype(jnp.float32), jnp.uint32)
    return [(bits[:, (2 * c) * LANES:(2 * c + 1) * LANES] >> 16)
            | bits[:, (2 * c + 1) * LANES:(2 * c + 2) * LANES] for c in range(XS_CHUNKS)]


def _unpack_bf16_pairs(words):
    cols = []
    for w in words:
        cols.append(lax.bitcast_convert_type(w << 16, jnp.float32))
        cols.append(lax.bitcast_convert_type(w & jnp.uint32(0xFFFF0000), jnp.float32))
    return jnp.concatenate(cols, axis=-1)


def _params(sem):
    return pltpu.CompilerParams(dimension_semantics=sem, vmem_limit_bytes=VMEM_LIMIT)


def _qkv_kernel(x_ref, g_ref, w_ref, b_ref, q_ref, kt_ref, v_ref):
    xn = _rms(x_ref[...], g_ref[...]).astype(jnp.bfloat16)
    out = jnp.dot(xn, w_ref[...], preferred_element_type=jnp.float32) + b_ref[...]
    nq = N_Q_HEADS * HEAD_DIM
    nk = N_KV_HEADS * HEAD_DIM
    q_ref[...] = out[:, :nq].astype(jnp.bfloat16)
    kt_ref[...] = out[:, nq:nq + nk].T.astype(jnp.bfloat16)
    v_ref[...] = out[:, nq + nk:].astype(jnp.bfloat16)


def _qkv_call(x2, g, w, b):
    T = x2.shape[0]
    nq = N_Q_HEADS * HEAD_DIM
    nk = N_KV_HEADS * HEAD_DIM
    tm = TM_DENSE
    return pl.pallas_call(
        _qkv_kernel,
        grid=(T // tm,),
        in_specs=[
            pl.BlockSpec((tm, D_MODEL), lambda i: (i, 0)),
            pl.BlockSpec((1, D_MODEL), lambda i: (0, 0)),
            pl.BlockSpec((D_MODEL, nq + 2 * nk), lambda i: (0, 0)),
            pl.BlockSpec((1, nq + 2 * nk), lambda i: (0, 0)),
        ],
        out_specs=[
            pl.BlockSpec((tm, nq), lambda i: (i, 0)),
            pl.BlockSpec((nk, tm), lambda i: (0, i)),
            pl.BlockSpec((tm, nk), lambda i: (i, 0)),
        ],
        out_shape=[
            jax.ShapeDtypeStruct((T, nq), jnp.bfloat16),
            jax.ShapeDtypeStruct((nk, T), jnp.bfloat16),
            jax.ShapeDtypeStruct((T, nk), jnp.bfloat16),
        ],
        compiler_params=_params(("parallel",)),
        name="qkv_proj",
    )(x2, g, w, b)


def _attn_kernel(q_ref, kt_ref, ktp_ref, v_ref, vp_ref, bias_ref, o_ref):
    first_tile = pl.program_id(1) == 0
    nblk = TQ // ATTN_BLOCK
    qi_io = lax.broadcasted_iota(jnp.int32, (ATTN_BLOCK, 2 * ATTN_BLOCK), 0)
    kj_io = lax.broadcasted_iota(jnp.int32, (ATTN_BLOCK, 2 * ATTN_BLOCK), 1)
    dist = qi_io + ATTN_BLOCK - kj_io
    band = (dist >= 0) & (dist < WINDOW)
    sink_col = kj_io == 0
    kt_keep = lax.broadcasted_iota(jnp.int32, (N_KV_HEADS * HEAD_DIM, 2 * ATTN_BLOCK), 1) > 0
    v_keep = lax.broadcasted_iota(jnp.int32, (2 * ATTN_BLOCK, N_KV_HEADS * HEAD_DIM), 0) > 0
    low_half = lax.broadcasted_iota(jnp.int32, (ATTN_BLOCK, LANES), 1) < HEAD_DIM
    for qi in range(nblk):
        r0 = qi * ATTN_BLOCK
        if qi == 0:
            kt_blk = jnp.concatenate([ktp_ref[...], kt_ref[:, 0:ATTN_BLOCK]], axis=1)
            v_blk = jnp.concatenate([vp_ref[...], v_ref[0:ATTN_BLOCK, :]], axis=0)
            mask = (band & (jnp.logical_not(first_tile) | (kj_io >= ATTN_BLOCK))) | sink_col
        else:
            kt_blk = kt_ref[:, r0 - ATTN_BLOCK:r0 + ATTN_BLOCK]
            v_blk = v_ref[r0 - ATTN_BLOCK:r0 + ATTN_BLOCK, :]
            mask = band | sink_col
        kt_blk = jnp.where(kt_keep, kt_blk, jnp.zeros_like(kt_blk))
        v_blk = jnp.where(v_keep, v_blk, jnp.zeros_like(v_blk))
        for m in range(GROUP):
            qg = q_ref[r0:r0 + ATTN_BLOCK, m * LANES:(m + 1) * LANES]
            halves = []
            for half in range(N_KV_HEADS):
                h = m + GROUP * half
                keep = low_half if half == 0 else jnp.logical_not(low_half)
                qpad = jnp.where(keep, qg, jnp.zeros_like(qg))
                s = jnp.dot(qpad, kt_blk, preferred_element_type=jnp.float32)
                s = jnp.where(mask, s + bias_ref[h], -1e30)
                mx = jnp.max(s, axis=-1, keepdims=True)
                p = jnp.exp2(s - mx)
                den = jnp.sum(p, axis=-1, keepdims=True)
                pv = jnp.dot(p.astype(jnp.bfloat16), v_blk, preferred_element_type=jnp.float32)
                halves.append(pv * (1.0 / den))
            og = jnp.where(low_half, halves[0], halves[1])
            o_ref[r0:r0 + ATTN_BLOCK, m * LANES:(m + 1) * LANES] = og.astype(jnp.bfloat16)


def _attn_call(q, kt, v, bias, batch, seq):
    T = q.shape[0]
    nq = N_Q_HEADS * HEAD_DIM
    nk = N_KV_HEADS * HEAD_DIM
    tiles = seq // TQ
    per = TQ // ATTN_BLOCK

    def cur(b, j):
        return b * tiles + j

    def prev(b, j):
        return jnp.maximum((b * tiles + j) * per - 1, b * tiles * per)

    return pl.pallas_call(
        _attn_kernel,
        grid=(batch, tiles),
        in_specs=[
            pl.BlockSpec((TQ, nq), lambda b, j: (cur(b, j), 0)),
            pl.BlockSpec((nk, TQ), lambda b, j: (0, cur(b, j))),
            pl.BlockSpec((nk, ATTN_BLOCK), lambda b, j: (0, prev(b, j))),
            pl.BlockSpec((TQ, nk), lambda b, j: (cur(b, j), 0)),
            pl.BlockSpec((ATTN_BLOCK, nk), lambda b, j: (prev(b, j), 0)),
            pl.BlockSpec((N_Q_HEADS, ATTN_BLOCK, 2 * ATTN_BLOCK), lambda b, j: (0, 0, 0)),
        ],
        out_specs=pl.BlockSpec((TQ, nq), lambda b, j: (cur(b, j), 0)),
        out_shape=jax.ShapeDtypeStruct((T, nq), jnp.bfloat16),
        compiler_params=_params(("parallel", "parallel")),
        name="swa_attention",
    )(q, kt, kt, v, v, bias)


def _router_epilogue(h, gffn_ref, wr_ref, brt_ref, carry_ref,
                     h_out_ref, xrows_ref, ri_ref, rf_ref, cnt_ref):
    tm = h.shape[0]
    h_out_ref[...] = h
    xn = _rms(h, gffn_ref[...])
    xh = xn.astype(jnp.bfloat16)
    for c, w in enumerate(_pack_bf16_pairs(xn)):
        xrows_ref[pl.ds(c, tm, stride=XS_CHUNKS), :] = w

    xl = (xn - xh.astype(jnp.float32)).astype(jnp.bfloat16)
    wcat = wr_ref[...]
    prod = jnp.dot(xh, wcat, preferred_element_type=jnp.float32)
    logits = (prod[:, :LANES] + prod[:, LANES:]) + jnp.dot(
        xl, wcat[:, :LANES], preferred_element_type=jnp.float32)
    lt = logits.T[:ROUTER_ROWS, :] + brt_ref[...]

    gl = [lt[g:g + 1, :] for g in range(N_GROUPS)]
    gmax = functools.reduce(jnp.maximum, gl)
    gexp = [jnp.exp(x - gmax) for x in gl]
    gsum = functools.reduce(lambda a, b: a + b, gexp)
    gprob = [x / gsum for x in gexp]
    g_prob = functools.reduce(jnp.maximum, gprob)
    g_idx = jnp.full(g_prob.shape, N_GROUPS - 1, jnp.int32)
    for g in range(N_GROUPS - 2, -1, -1):
        g_idx = jnp.where(gprob[g] == g_prob, g, g_idx)

    el = []
    for j in range(EXPERTS_PER_GROUP):
        x = lt[SUBLANES + j:SUBLANES + j + 1, :]
        for g in range(1, N_GROUPS):
            r = SUBLANES + g * EXPERTS_PER_GROUP + j
            x = jnp.where(g_idx == g, lt[r:r + 1, :], x)
        el.append(x)
    emax = functools.reduce(jnp.maximum, el)
    eexp = [jnp.exp(x - emax) for x in el]
    esum = functools.reduce(lambda a, b: a + b, eexp)
    eprob = [x / esum for x in eexp]
    p1 = functools.reduce(jnp.maximum, eprob)
    i1 = jnp.full(p1.shape, EXPERTS_PER_GROUP - 1, jnp.int32)
    for j in range(EXPERTS_PER_GROUP - 2, -1, -1):
        i1 = jnp.where(eprob[j] == p1, j, i1)
    rest = [jnp.where(i1 == j, -1.0, eprob[j]) for j in range(EXPERTS_PER_GROUP)]
    p2 = functools.reduce(jnp.maximum, rest)
    i2 = jnp.full(p2.shape, EXPERTS_PER_GROUP - 1, jnp.int32)
    for j in range(EXPERTS_PER_GROUP - 2, -1, -1):
        i2 = jnp.where(rest[j] == p2, j, i2)
    psum = p1 + p2
    gate0 = g_prob * (p1 / psum)
    gate1 = g_prob * (p2 / psum)
    e0 = g_idx * EXPERTS_PER_GROUP + i1
    e1 = g_idx * EXPERTS_PER_GROUP + i2

    eio = lax.broadcasted_iota(jnp.int32, (N_EXPERTS, tm), 0)
    oh0 = (eio == e0).astype(jnp.float32)
    oh1 = (eio == e1).astype(jnp.float32)
    both = oh0 + oh1
    tr = lax.broadcasted_iota(jnp.int32, (tm, tm), 0)
    tc = lax.broadcasted_iota(jnp.int32, (tm, tm), 1)
    upper = (tr < tc).astype(jnp.bfloat16)
    before = jnp.dot(both.astype(jnp.bfloat16), upper, preferred_element_type=jnp.float32)
    before = before + carry_ref[...]
    rank0 = jnp.sum(oh0 * before, axis=0, keepdims=True)
    rank1 = jnp.sum(oh1 * (before + oh0), axis=0, keepdims=True)
    carry_ref[...] = carry_ref[...] + jnp.sum(both, axis=1, keepdims=True)
    cnt_ref[...] = jnp.broadcast_to(carry_ref[...], cnt_ref.shape)

    zi = jnp.zeros_like(e0)
    ri_ref[...] = jnp.concatenate(
        [e0, e1, rank0.astype(jnp.int32), rank1.astype(jnp.int32), zi, zi, zi, zi], axis=0)
    zf = jnp.zeros_like(gate0)
    rf_ref[...] = jnp.concatenate([gate0, gate1, zf, zf, zf, zf, zf, zf], axis=0)


def _epilogue_out_specs(tm):
    return [
        pl.BlockSpec((tm, D_MODEL), lambda i: (i, 0)),
        pl.BlockSpec((tm * XS_CHUNKS, LANES), lambda i: (i, 0)),
        pl.BlockSpec((SUBLANES, tm), lambda i: (0, i)),
        pl.BlockSpec((SUBLANES, tm), lambda i: (0, i)),
        pl.BlockSpec((N_EXPERTS, LANES), lambda i: (0, 0)),
    ]


def _epilogue_out_shapes(T):
    return [
        jax.ShapeDtypeStruct((T, D_MODEL), jnp.float32),
        jax.ShapeDtypeStruct((T * XS_CHUNKS, LANES), jnp.uint32),
        jax.ShapeDtypeStruct((SUBLANES, T), jnp.int32),
        jax.ShapeDtypeStruct((SUBLANES, T), jnp.float32),
        jax.ShapeDtypeStruct((N_EXPERTS, LANES), jnp.float32),
    ]


def _epilogue_in_specs():
    return [
        pl.BlockSpec((1, D_MODEL), lambda i: (0, 0)),
        pl.BlockSpec((D_MODEL, 2 * LANES), lambda i: (0, 0)),
        pl.BlockSpec((ROUTER_ROWS, 1), lambda i: (0, 0)),
    ]


def _attn_out_kernel(a_ref, w_ref, b_ref, res_ref, gffn_ref, wr_ref, brt_ref,
                     h_out_ref, xrows_ref, ri_ref, rf_ref, cnt_ref, carry_ref):
    @pl.when(pl.program_id(0) == 0)
    def _():
        carry_ref[...] = jnp.zeros_like(carry_ref)

    mix = jnp.dot(a_ref[...], w_ref[...], preferred_element_type=jnp.float32) + b_ref[...]
    h = res_ref[...] + mix
    _router_epilogue(h, gffn_ref, wr_ref, brt_ref, carry_ref,
                     h_out_ref, xrows_ref, ri_ref, rf_ref, cnt_ref)


def _attn_out_call(a, w, b, res, gffn, wr, brt):
    T = a.shape[0]
    tm = TM_DENSE
    return pl.pallas_call(
        _attn_out_kernel,
        grid=(T // tm,),
        in_specs=[
            pl.BlockSpec((tm, D_MODEL), lambda i: (i, 0)),
            pl.BlockSpec((D_MODEL, D_MODEL), lambda i: (0, 0)),
            pl.BlockSpec((1, D_MODEL), lambda i: (0, 0)),
            pl.BlockSpec((tm, D_MODEL), lambda i: (i, 0)),
        ] + _epilogue_in_specs(),
        out_specs=_epilogue_out_specs(tm),
        out_shape=_epilogue_out_shapes(T),
        scratch_shapes=[pltpu.VMEM((N_EXPERTS, 1), jnp.float32)],
        compiler_params=_params(("arbitrary",)),
        name="attn_out_router",
    )(a, w, b, res, gffn, wr, brt)


def _conv_mixer_kernel(tiles_per_seq, x_ref, g_ref, win_ref, cw_ref, wout_ref,
                       gffn_ref, wr_ref, brt_ref,
                       h_out_ref, xrows_ref, ri_ref, rf_ref, cnt_ref, carry_ref, tail_ref):
    i = pl.program_id(0)
    tm = x_ref.shape[0]

    @pl.when(i == 0)
    def _():
        carry_ref[...] = jnp.zeros_like(carry_ref)

    @pl.when(i % tiles_per_seq == 0)
    def _():
        tail_ref[...] = jnp.zeros_like(tail_ref)

    x = x_ref[...]
    xn = _rms(x, g_ref[...]).astype(jnp.bfloat16)
    bcu = jnp.dot(xn, win_ref[...], preferred_element_type=jnp.float32)
    b_gate = bcu[:, :D_MODEL]
    z = bcu[:, D_MODEL:2 * D_MODEL] * bcu[:, 2 * D_MODEL:]
    row = lax.broadcasted_iota(jnp.int32, (tm, D_MODEL), 0)
    tail = tail_ref[...]
    t1 = tail[SUBLANES - 1:SUBLANES, :]
    t2 = tail[SUBLANES - 2:SUBLANES - 1, :]
    z1 = jnp.where(row == 0, t1, pltpu.roll(z, 1, axis=0))
    z2 = jnp.where(row == 0, t2, jnp.where(row == 1, t1, pltpu.roll(z, 2, axis=0)))
    tail_ref[...] = z[tm - SUBLANES:, :]
    conv = z2 * cw_ref[0:1, :]
    conv = conv + z1 * cw_ref[1:2, :]
    conv = conv + z * cw_ref[2:3, :]
    gated = (b_gate * conv).astype(jnp.bfloat16)
    h = x + jnp.dot(gated, wout_ref[...], preferred_element_type=jnp.float32)
    _router_epilogue(h, gffn_ref, wr_ref, brt_ref, carry_ref,
                     h_out_ref, xrows_ref, ri_ref, rf_ref, cnt_ref)


def _conv_mixer_call(x2, g, win, cw, wout, gffn, wr, brt, seq):
    T = x2.shape[0]
    tm = TM_MIX1
    return pl.pallas_call(
        functools.partial(_conv_mixer_kernel, seq // tm),
        grid=(T // tm,),
        in_specs=[
            pl.BlockSpec((tm, D_MODEL), lambda i: (i, 0)),
            pl.BlockSpec((1, D_MODEL), lambda i: (0, 0)),
            pl.BlockSpec((D_MODEL, 3 * D_MODEL), lambda i: (0, 0)),
            pl.BlockSpec((CONV_WIDTH, D_MODEL), lambda i: (0, 0)),
            pl.BlockSpec((D_MODEL, D_MODEL), lambda i: (0, 0)),
        ] + _epilogue_in_specs(),
        out_specs=_epilogue_out_specs(tm),
        out_shape=_epilogue_out_shapes(T),
        scratch_shapes=[pltpu.VMEM((N_EXPERTS, 1), jnp.float32),
                        pltpu.VMEM((SUBLANES, D_MODEL), jnp.float32)],
        compiler_params=_params(("arbitrary",)),
        name="conv_mixer_router",
    )(x2, g, win, cw, wout, gffn, wr, brt)


def _row_copy(src_hbm, src_row, dst_hbm, dst_row, sem, chunks):
    return pltpu.make_async_copy(
        src_hbm.at[pl.ds(pl.multiple_of(src_row, chunks), chunks)],
        dst_hbm.at[pl.ds(pl.multiple_of(dst_row, chunks), chunks)], sem)


def _slot_rows_kernel(ri_ref, pstart_ref, o_ref):
    tb = ri_ref.shape[1]
    eio = lax.broadcasted_iota(jnp.int32, (N_EXPERTS, tb), 0)
    pstart = pstart_ref[...]
    slots = []
    for k in range(2):
        start = jnp.sum(jnp.where(eio == ri_ref[k:k + 1, :], pstart, 0), axis=0, keepdims=True)
        slots.append(start + ri_ref[2 + k:3 + k, :])
    zi = jnp.zeros_like(slots[0])
    o_ref[...] = jnp.concatenate(
        [s * XS_CHUNKS for s in slots] + [zi] * (SUBLANES - 2), axis=0)


def _slot_rows_call(ri, pstart):
    T = ri.shape[1]
    tb = TM_SLOT_ROWS
    return pl.pallas_call(
        _slot_rows_kernel,
        grid=(T // tb,),
        in_specs=[pl.BlockSpec((SUBLANES, tb), lambda i: (0, i)),
                  pl.BlockSpec((N_EXPERTS, 1), lambda i: (0, 0))],
        out_specs=pl.BlockSpec((SUBLANES, tb), lambda i: (0, i)),
        out_shape=jax.ShapeDtypeStruct((SUBLANES, T), jnp.int32),
        compiler_params=_params(("parallel",)),
        name="moe_slot_rows",
    )(ri, pstart[:, None])


def _dispatch_kernel(d0_ref, d1_ref, padrow_ref, x_ref, xs_hbm, zero_ref, sem, zsem):
    i = pl.program_id(0)
    tile_rows = TM_EXPERT * XS_CHUNKS

    @pl.when(i == 0)
    def _():
        zero_ref[...] = jnp.zeros_like(zero_ref)

        def zcopy(e):
            return pltpu.make_async_copy(
                zero_ref,
                xs_hbm.at[pl.ds(pl.multiple_of(padrow_ref[e], tile_rows), tile_rows)], zsem)

        def zstart(e, c):
            @pl.when(padrow_ref[e] >= 0)
            def _():
                zcopy(e).start()
            return c

        def zwait(e, c):
            @pl.when(padrow_ref[e] >= 0)
            def _():
                zcopy(e).wait()
            return c

        lax.fori_loop(0, N_EXPERTS, zstart, 0)
        lax.fori_loop(0, N_EXPERTS, zwait, 0)

        def tcopy(b):
            return pltpu.make_async_copy(
                zero_ref, xs_hbm.at[pl.ds(pl.multiple_of(b * tile_rows, tile_rows), tile_rows)], zsem)

        n_used = padrow_ref[N_EXPERTS]
        n_blk = xs_hbm.shape[0] // tile_rows
        lax.fori_loop(n_used, n_blk, lambda b, c: (tcopy(b).start(), c)[1], 0)
        lax.fori_loop(n_used, n_blk, lambda b, c: (tcopy(b).wait(), c)[1], 0)

    def issue(t, c):
        src = t * XS_CHUNKS
        _row_copy(x_ref, src, xs_hbm, d0_ref[t], sem, XS_CHUNKS).start(priority=0)
        _row_copy(x_ref, src, xs_hbm, d1_ref[t], sem, XS_CHUNKS).start(priority=1)
        return c

    lax.fori_loop(0, TM_DISPATCH, issue, 0, unroll=DMA_UNROLL)
    for _ in range(2):
        pltpu.make_async_copy(x_ref, xs_hbm.at[pl.ds(0, TM_DISPATCH * XS_CHUNKS)], sem).wait()


def _dispatch_call(d0, d1, padrow, xrows, capacity):
    T = d0.shape[0]
    return pl.pallas_call(
        _dispatch_kernel,
        grid=(T // TM_DISPATCH,),
        in_specs=[
            pl.BlockSpec((TM_DISPATCH,), lambda i: (i,), memory_space=pltpu.SMEM),
            pl.BlockSpec((TM_DISPATCH,), lambda i: (i,), memory_space=pltpu.SMEM),
            pl.BlockSpec(memory_space=pltpu.SMEM),
            pl.BlockSpec((TM_DISPATCH * XS_CHUNKS, LANES), lambda i: (i, 0)),
        ],
        out_specs=pl.BlockSpec(memory_space=pl.ANY),
        out_shape=jax.ShapeDtypeStruct((capacity * XS_CHUNKS, LANES), jnp.uint32),
        scratch_shapes=[pltpu.VMEM((TM_EXPERT * XS_CHUNKS, LANES), jnp.uint32),
                        pltpu.SemaphoreType.DMA, pltpu.SemaphoreType.DMA],
        compiler_params=_params(("arbitrary",)),
        name="moe_dispatch",
    )(d0, d1, padrow, xrows)


def _expert_kernel(tstart_ref, ntile_ref, xs_hbm, wgu_ref, wd_ref, ys_hbm,
                   xbuf, ybuf, wgu_bf, wd_bf, sem_in, sem_out):
    nb = EXPERT_BUFS
    e = pl.program_id(0)
    tm = TM_EXPERT
    tile_rows = tm * XS_CHUNKS
    t0 = tstart_ref[e]
    nt = ntile_ref[e]
    n_used = tstart_ref[N_EXPERTS - 1] + ntile_ref[N_EXPERTS - 1]

    def tile_at(ref, g, rows):
        return ref.at[pl.ds(pl.multiple_of(g * rows, rows), rows)]

    def in_copy(g, slot):
        return pltpu.make_async_copy(tile_at(xs_hbm, g, tile_rows), xbuf.at[slot], sem_in.at[slot])

    def out_copy(g, slot):
        return pltpu.make_async_copy(ybuf.at[slot], tile_at(ys_hbm, g, tile_rows), sem_out.at[slot])

    @pl.when(e == 0)
    def _():
        for k in range(nb - 1):
            @pl.when(k < n_used)
            def _():
                in_copy(k, k).start(priority=1)

    @pl.when(nt > 0)
    def _():
        wgu_bf[...] = wgu_ref[0, 0].astype(jnp.bfloat16)
        wd_bf[...] = wd_ref[0, 0].astype(jnp.bfloat16)

    def tile_body(g, carry):
        slot = g % nb
        in_copy(g, slot).wait()

        @pl.when(g + (nb - 1) < n_used)
        def _():
            in_copy(g + (nb - 1), (g + (nb - 1)) % nb).start(priority=1)

        @pl.when(g >= nb)
        def _():
            out_copy(g - nb, slot).wait()

        x = _unpack_bf16_pairs(
            [xbuf[slot, pl.ds(c, tm, stride=XS_CHUNKS), :] for c in range(XS_CHUNKS)])
        gu = jnp.dot(x.astype(jnp.bfloat16), wgu_bf[...], preferred_element_type=jnp.float32)
        g_act = gu[:, :EXPERT_FF]
        u = gu[:, EXPERT_FF:]
        act = (g_act / (1.0 + jnp.exp(-g_act))) * u
        y = jnp.dot(act.astype(jnp.bfloat16), wd_bf[...], preferred_element_type=jnp.float32)
        for c, w in enumerate(_pack_bf16_pairs(y)):
            ybuf[slot, pl.ds(c, tm, stride=XS_CHUNKS), :] = w
        out_copy(g, slot).start(priority=1)
        return carry

    lax.fori_loop(t0, t0 + nt, tile_body, 0)

    @pl.when(e == N_EXPERTS - 1)
    def _():
        for k in range(1, nb + 1):
            @pl.when(n_used >= k)
            def _():
                out_copy(n_used - k, (n_used - k) % nb).wait()

        n_blk = ys_hbm.shape[0] // tile_rows
        ybuf[0] = jnp.zeros(ybuf.shape[1:], ybuf.dtype)
        lax.fori_loop(n_used, n_blk, lambda b, c: (out_copy(b, 0).start(), c)[1], 0)
        lax.fori_loop(n_used, n_blk, lambda b, c: (out_copy(b, 0).wait(), c)[1], 0)


def _expert_call(layer, tstart, ntile, xs, wgu, wd, capacity):
    tile_rows = TM_EXPERT * XS_CHUNKS

    def w_map(e, ts, nt):
        return (layer, e, 0, 0)

    return pl.pallas_call(
        _expert_kernel,
        grid_spec=pltpu.PrefetchScalarGridSpec(
            num_scalar_prefetch=2,
            grid=(N_EXPERTS,),
            in_specs=[
                pl.BlockSpec(memory_space=pl.ANY),
                pl.BlockSpec((1, 1, D_MODEL, 2 * EXPERT_FF), w_map),
                pl.BlockSpec((1, 1, EXPERT_FF, D_MODEL), w_map),
            ],
            out_specs=pl.BlockSpec(memory_space=pl.ANY),
            scratch_shapes=[pltpu.VMEM((EXPERT_BUFS, tile_rows, LANES), jnp.uint32),
                            pltpu.VMEM((EXPERT_BUFS, tile_rows, LANES), jnp.uint32),
                            pltpu.VMEM((D_MODEL, 2 * EXPERT_FF), jnp.bfloat16),
                            pltpu.VMEM((EXPERT_FF, D_MODEL), jnp.bfloat16),
                            pltpu.SemaphoreType.DMA((EXPERT_BUFS,)),
                            pltpu.SemaphoreType.DMA((EXPERT_BUFS,))],
        ),
        out_shape=jax.ShapeDtypeStruct((capacity * XS_CHUNKS, LANES), jnp.uint32),
        compiler_params=_params(("arbitrary",)),
        name="moe_experts",
    )(tstart, ntile, xs, wgu, wd)


def _gathered_residual(tm, d0_ref, d1_ref, d0n_ref, d1n_ref, ys_hbm, h_ref, gate_ref,
                       y0_ref, y1_ref, sem):
    i = pl.program_id(0)
    slot = i % 2

    def issue_tile(d0r, d1r, s):
        def issue(t, c):
            dst = t * XS_CHUNKS
            _row_copy(ys_hbm, d0r[t], y0_ref.at[s], dst, sem.at[s], XS_CHUNKS).start(priority=0)
            _row_copy(ys_hbm, d1r[t], y1_ref.at[s], dst, sem.at[s], XS_CHUNKS).start(priority=1)
            return c
        lax.fori_loop(0, tm, issue, 0, unroll=DMA_UNROLL)

    @pl.when(i == 0)
    def _():
        issue_tile(d0_ref, d1_ref, 0)

    @pl.when(i + 1 < pl.num_programs(0))
    def _():
        issue_tile(d0n_ref, d1n_ref, 1 - slot)

    for buf in (y0_ref, y1_ref):
        pltpu.make_async_copy(ys_hbm.at[pl.ds(0, tm * XS_CHUNKS)], buf.at[slot], sem.at[slot]).wait()

    g0 = gate_ref[:, 0:1]
    g1 = gate_ref[:, 1:2]
    y0 = _unpack_bf16_pairs([y0_ref[slot, pl.ds(c, tm, stride=XS_CHUNKS), :] for c in range(XS_CHUNKS)])
    y1 = _unpack_bf16_pairs([y1_ref[slot, pl.ds(c, tm, stride=XS_CHUNKS), :] for c in range(XS_CHUNKS)])
    return h_ref[...] + (y0 * g0 + y1 * g1)


def _gather_in_specs(tm, n_tiles):
    last = n_tiles - 1
    return [
        pl.BlockSpec((tm,), lambda i: (i,), memory_space=pltpu.SMEM),
        pl.BlockSpec((tm,), lambda i: (i,), memory_space=pltpu.SMEM),
        pl.BlockSpec((tm,), lambda i: (jnp.minimum(i + 1, last),), memory_space=pltpu.SMEM),
        pl.BlockSpec((tm,), lambda i: (jnp.minimum(i + 1, last),), memory_space=pltpu.SMEM),
        pl.BlockSpec(memory_space=pl.ANY),
        pl.BlockSpec((tm, D_MODEL), lambda i: (i, 0)),
        pl.BlockSpec((tm, 2), lambda i: (i, 0)),
    ]


def _gather_scratch(tm):
    return [pltpu.VMEM((2, tm * XS_CHUNKS, LANES), jnp.uint32),
            pltpu.VMEM((2, tm * XS_CHUNKS, LANES), jnp.uint32),
            pltpu.SemaphoreType.DMA((2,))]


def _combine_kernel(final_norm, d0_ref, d1_ref, d0n_ref, d1n_ref, ys_hbm, h_ref, gate_ref, gfin_ref,
                    o_ref, y0_ref, y1_ref, sem):
    h = _gathered_residual(TM_COMBINE, d0_ref, d1_ref, d0n_ref, d1n_ref, ys_hbm, h_ref, gate_ref,
                           y0_ref, y1_ref, sem)
    if final_norm:
        h = _rms(h, gfin_ref[...])
    o_ref[...] = h


def _combine_call(d0, d1, ys, h, gates, gfin, final_norm):
    T = h.shape[0]
    tm = TM_COMBINE
    return pl.pallas_call(
        functools.partial(_combine_kernel, final_norm),
        grid=(T // tm,),
        in_specs=_gather_in_specs(tm, T // tm) + [pl.BlockSpec((1, D_MODEL), lambda i: (0, 0))],
        out_specs=pl.BlockSpec((tm, D_MODEL), lambda i: (i, 0)),
        out_shape=jax.ShapeDtypeStruct((T, D_MODEL), jnp.float32),
        scratch_shapes=_gather_scratch(tm),
        compiler_params=_params(("arbitrary",)),
        name="moe_combine",
    )(d0, d1, d0, d1, ys, h, gates, gfin)


def _group_heads_last(w):
    lead = w.shape[:-1]
    w = w.reshape(lead + (N_KV_HEADS, GROUP, HEAD_DIM))
    w = jnp.swapaxes(w, -3, -2)
    return w.reshape(lead + (N_Q_HEADS * HEAD_DIM,))


def _rel_bucket_table():
    qi = np.arange(ATTN_BLOCK)[:, None]
    kj = np.arange(2 * ATTN_BLOCK)[None, :]
    dist = np.maximum(qi + ATTN_BLOCK - kj, 0)
    max_exact = N_REL_BUCKETS // 2
    d = np.maximum(dist, max_exact).astype(np.float32)
    large = max_exact + (np.log(d / np.float32(max_exact)) / np.float32(math.log(REL_MAX_DISTANCE / max_exact))
                         * np.float32(N_REL_BUCKETS - max_exact)).astype(np.int32)
    large = np.minimum(large, N_REL_BUCKETS - 1)
    return np.where(dist < max_exact, dist, large).astype(np.int32)


def _rel_bias_blocks(rel_bias, sinks):
    onehot = (jnp.asarray(_rel_bucket_table())[..., None] == jnp.arange(N_REL_BUCKETS)).astype(jnp.float32)
    bias = jnp.einsum("qkb,bh->hqk", onehot, rel_bias.astype(jnp.float32),
                      precision=lax.Precision.HIGHEST)
    sink_col = jnp.broadcast_to(sinks.astype(jnp.float32)[:, None, None], (N_Q_HEADS, ATTN_BLOCK, 1))
    return jnp.concatenate([sink_col, bias[:, :, 1:]], axis=2) * LOG2_E


def _router_weights(w_group, b_group, w_expert, b_expert):
    zw = jnp.zeros((D_MODEL, SUBLANES - N_GROUPS), jnp.float32)
    zw2 = jnp.zeros((D_MODEL, LANES - SUBLANES - N_EXPERTS), jnp.float32)
    wr = jnp.concatenate([w_group, zw, w_expert, zw2], axis=1)
    wh = wr.astype(jnp.bfloat16)
    wl = (wr - wh.astype(jnp.float32)).astype(jnp.bfloat16)
    wr = jnp.concatenate([wh, wl], axis=1)
    brt = jnp.concatenate([b_group, jnp.zeros((SUBLANES - N_GROUPS,), jnp.float32), b_expert])[:, None]
    return wr, brt


def _moe(layer, h, xrows, ri, rf, cnt, w_gate_up, w_down, gfin, final_norm):
    T = h.shape[0]
    tm = TM_EXPERT
    capacity = -(-(2 * T) // tm) * tm + N_EXPERTS * tm
    counts = cnt[:, 0].astype(jnp.int32)
    ntile = (counts + tm - 1) // tm
    tend = jnp.cumsum(ntile)
    pstart = (tend - ntile) * tm
    n_used = tend[-1:]
    padrow = jnp.concatenate([jnp.where(ntile > 0, (tend - 1) * (tm * XS_CHUNKS), -1), n_used]
                             ).astype(jnp.int32)

    dst = _slot_rows_call(ri, pstart)
    xs = _dispatch_call(dst[0], dst[1], padrow, xrows, capacity)
    ys = _expert_call(layer, (tend - ntile).astype(jnp.int32), ntile.astype(jnp.int32), xs,
                      w_gate_up, w_down, capacity)
    gates = rf[:2].T
    return _combine_call(dst[0], dst[1], ys, h, gates, gfin, final_norm)


def kernel(x, norm_mix, norm_ffn, final_norm, rel_bias, attn_w_qkv, attn_b_qkv, attn_w_o, attn_b_o,
           attn_sinks, conv_w_in, conv_w, conv_w_out, moe_w_group, moe_b_group, moe_w_expert,
           moe_b_expert, moe_w_gate_up, moe_w_down):
    B, S, D = x.shape
    T = B * S
    x2 = x.reshape(T, D)
    nq = N_Q_HEADS * HEAD_DIM
    scale = HEAD_DIM ** -0.5 * LOG2_E

    wqkv = attn_w_qkv[0]
    bqkv = attn_b_qkv[0]
    w_all = jnp.concatenate([_group_heads_last(wqkv[:, :nq]) * scale, wqkv[:, nq:]],
                            axis=1).astype(jnp.bfloat16)
    b_all = jnp.concatenate([_group_heads_last(bqkv[:nq]) * scale, bqkv[nq:]])[None, :]
    q, kt, v = _qkv_call(x2, norm_mix[0][None, :], w_all, b_all)
    a = _attn_call(q, kt, v, _rel_bias_blocks(rel_bias, attn_sinks[0]), B, S)
    wr, brt = _router_weights(moe_w_group[0], moe_b_group[0], moe_w_expert[0], moe_b_expert[0])
    w_o = _group_heads_last(attn_w_o[0].T).T.astype(jnp.bfloat16)
    h, xrows, ri, rf, cnt = _attn_out_call(
        a, w_o, attn_b_o[0][None, :], x2, norm_ffn[0][None, :], wr, brt)
    h = _moe(0, h, xrows, ri, rf, cnt, moe_w_gate_up, moe_w_down, final_norm[None, :], False)

    wr, brt = _router_weights(moe_w_group[1], moe_b_group[1], moe_w_expert[1], moe_b_expert[1])
    h, xrows, ri, rf, cnt = _conv_mixer_call(
        h, norm_mix[1][None, :], conv_w_in[0].astype(jnp.bfloat16), conv_w[0],
        conv_w_out[0].astype(jnp.bfloat16), norm_ffn[1][None, :], wr, brt, S)
    out = _moe(1, h, xrows, ri, rf, cnt, moe_w_gate_up, moe_w_down, final_norm[None, :], True)
    return out.reshape(B, S, D)
```

```python
import functools
import math

import numpy as np
import jax
import jax.numpy as jnp
from jax import lax
from jax.experimental import pallas as pl
from jax.experimental.pallas import tpu as pltpu

D_MODEL = 1024
N_Q_HEADS = 16
N_KV_HEADS = 2
HEAD_DIM = 64
GROUP = N_Q_HEADS // N_KV_HEADS
WINDOW = 128
ATTN_BLOCK = 128
N_REL_BUCKETS = 32
REL_MAX_DISTANCE = 128
CONV_WIDTH = 3
N_GROUPS = 4
EXPERTS_PER_GROUP = 8
N_EXPERTS = N_GROUPS * EXPERTS_PER_GROUP
EXPERT_FF = 512
RMS_EPS = 1e-5
LOG2_E = math.log2(math.e)

LANES = 128
SUBLANES = 8
ROW_CHUNKS = D_MODEL // LANES
XS_CHUNKS = ROW_CHUNKS // 2
VMEM_LIMIT = 56 * 1024 * 1024

TM_DENSE = 1024
TQ = 1024
TM_MIX1 = 1024
TM_EXPERT = 512
TM_DISPATCH = 4096
TM_COMBINE = 256
DMA_UNROLL = 8
TM_SLOT_ROWS = 4096
EXPERT_BUFS = 4
ROUTER_ROWS = 40


def _rms(x, g):
    return x * lax.rsqrt(jnp.mean(x * x, axis=-1, keepdims=True) + RMS_EPS) * g


def _pack_bf16_pairs(x):
    bits = lax.bitcast_convert_type(x.astype(jnp.bfloat16).astype(jnp.float32), jnp.uint32)
    return [(bits[:, (2 * c) * LANES:(2 * c + 1) * LANES] >> 16)
            | bits[:, (2 * c + 1) * LANES:(2 * c + 2) * LANES] for c in range(XS_CHUNKS)]


def _unpack_bf16_pairs(words):
    cols = []
    for w in words:
        cols.append(lax.bitcast_convert_type(w << 16, jnp.float32))
        cols.append(lax.bitcast_convert_type(w & jnp.uint32(0xFFFF0000), jnp.float32))
    return jnp.concatenate(cols, axis=-1)


def _params(sem):
    return pltpu.CompilerParams(dimension_semantics=sem, vmem_limit_bytes=VMEM_LIMIT)


def _qkv_kernel(x_ref, g_ref, w_ref, b_ref, q_ref, kt_ref, v_ref):
    xn = _rms(x_ref[...], g_ref[...]).astype(jnp.bfloat16)
    out = jnp.dot(xn, w_ref[...], preferred_element_type=jnp.float32) + b_ref[...]
    nq = N_Q_HEADS * HEAD_DIM
    nk = N_KV_HEADS * HEAD_DIM
    q_ref[...] = out[:, :nq].astype(jnp.bfloat16)
    kt_ref[...] = out[:, nq:nq + nk].T.astype(jnp.bfloat16)
    v_ref[...] = out[:, nq + nk:].astype(jnp.bfloat16)


def _qkv_call(x2, g, w, b):
    T = x2.shape[0]
    nq = N_Q_HEADS * HEAD_DIM
    nk = N_KV_HEADS * HEAD_DIM
    tm = TM_DENSE
    return pl.pallas_call(
        _qkv_kernel,
        grid=(T // tm,),
        in_specs=[
            pl.BlockSpec((tm, D_MODEL), lambda i: (i, 0)),
            pl.BlockSpec((1, D_MODEL), lambda i: (0, 0)),
            pl.BlockSpec((D_MODEL, nq + 2 * nk), lambda i: (0, 0)),
            pl.BlockSpec((1, nq + 2 * nk), lambda i: (0, 0)),
        ],
        out_specs=[
            pl.BlockSpec((tm, nq), lambda i: (i, 0)),
            pl.BlockSpec((nk, tm), lambda i: (0, i)),
            pl.BlockSpec((tm, nk), lambda i: (i, 0)),
        ],
        out_shape=[
            jax.ShapeDtypeStruct((T, nq), jnp.bfloat16),
            jax.ShapeDtypeStruct((nk, T), jnp.bfloat16),
            jax.ShapeDtypeStruct((T, nk), jnp.bfloat16),
        ],
        compiler_params=_params(("parallel",)),
        name="qkv_proj",
    )(x2, g, w, b)


def _attn_kernel(q_ref, kt_ref, ktp_ref, v_ref, vp_ref, bias_ref, o_ref):
    first_tile = pl.program_id(1) == 0
    nblk = TQ // ATTN_BLOCK
    qi_io = lax.broadcasted_iota(jnp.int32, (ATTN_BLOCK, 2 * ATTN_BLOCK), 0)
    kj_io = lax.broadcasted_iota(jnp.int32, (ATTN_BLOCK, 2 * ATTN_BLOCK), 1)
    dist = qi_io + ATTN_BLOCK - kj_io
    band = (dist >= 0) & (dist < WINDOW)
    sink_col = kj_io == 0
    kt_keep = lax.broadcasted_iota(jnp.int32, (N_KV_HEADS * HEAD_DIM, 2 * ATTN_BLOCK), 1) > 0
    v_keep = lax.broadcasted_iota(jnp.int32, (2 * ATTN_BLOCK, N_KV_HEADS * HEAD_DIM), 0) > 0
    low_half = lax.broadcasted_iota(jnp.int32, (ATTN_BLOCK, LANES), 1) < HEAD_DIM
    for qi in range(nblk):
        r0 = qi * ATTN_BLOCK
        if qi == 0:
            kt_blk = jnp.concatenate([ktp_ref[...], kt_ref[:, 0:ATTN_BLOCK]], axis=1)
            v_blk = jnp.concatenate([vp_ref[...], v_ref[0:ATTN_BLOCK, :]], axis=0)
            mask = (band & (jnp.logical_not(first_tile) | (kj_io >= ATTN_BLOCK))) | sink_col
        else:
            kt_blk = kt_ref[:, r0 - ATTN_BLOCK:r0 + ATTN_BLOCK]
            v_blk = v_ref[r0 - ATTN_BLOCK:r0 + ATTN_BLOCK, :]
            mask = band | sink_col
        kt_blk = jnp.where(kt_keep, kt_blk, jnp.zeros_like(kt_blk))
        v_blk = jnp.where(v_keep, v_blk, jnp.zeros_like(v_blk))
        for m in range(GROUP):
            qg = q_ref[r0:r0 + ATTN_BLOCK, m * LANES:(m + 1) * LANES]
            halves = []
            for half in range(N_KV_HEADS):
                h = m + GROUP * half
                keep = low_half if half == 0 else jnp.logical_not(low_half)
                qpad = jnp.where(keep, qg, jnp.zeros_like(qg))
                s = jnp.dot(qpad, kt_blk, preferred_element_type=jnp.float32)
                s = jnp.where(mask, s + bias_ref[h], -1e30)
                mx = jnp.max(s, axis=-1, keepdims=True)
                p = jnp.exp2(s - mx)
                den = jnp.sum(p, axis=-1, keepdims=True)
                pv = jnp.dot(p.astype(jnp.bfloat16), v_blk, preferred_element_type=jnp.float32)
                halves.append(pv * (1.0 / den))
            og = jnp.where(low_half, halves[0], halves[1])
            o_ref[r0:r0 + ATTN_BLOCK, m * LANES:(m + 1) * LANES] = og.astype(jnp.bfloat16)


def _attn_call(q, kt, v, bias, batch, seq):
    T = q.shape[0]
    nq = N_Q_HEADS * HEAD_DIM
    nk = N_KV_HEADS * HEAD_DIM
    tiles = seq // TQ
    per = TQ // ATTN_BLOCK

    def cur(b, j):
        return b * tiles + j

    def prev(b, j):
        return jnp.maximum((b * tiles + j) * per - 1, b * tiles * per)

    return pl.pallas_call(
        _attn_kernel,
        grid=(batch, tiles),
        in_specs=[
            pl.BlockSpec((TQ, nq), lambda b, j: (cur(b, j), 0)),
            pl.BlockSpec((nk, TQ), lambda b, j: (0, cur(b, j))),
            pl.BlockSpec((nk, ATTN_BLOCK), lambda b, j: (0, prev(b, j))),
            pl.BlockSpec((TQ, nk), lambda b, j: (cur(b, j), 0)),
            pl.BlockSpec((ATTN_BLOCK, nk), lambda b, j: (prev(b, j), 0)),
            pl.BlockSpec((N_Q_HEADS, ATTN_BLOCK, 2 * ATTN_BLOCK), lambda b, j: (0, 0, 0)),
        ],
        out_specs=pl.BlockSpec((TQ, nq), lambda b, j: (cur(b, j), 0)),
        out_shape=jax.ShapeDtypeStruct((T, nq), jnp.bfloat16),
        compiler_params=_params(("parallel", "parallel")),
        name="swa_attention",
    )(q, kt, kt, v, v, bias)


def _router_epilogue(h, gffn_ref, wr_ref, brt_ref, carry_ref,
                     h_out_ref, xrows_ref, ri_ref, rf_ref, cnt_ref):
    tm = h.shape[0]
    h_out_ref[...] = h
    xn = _rms(h, gffn_ref[...])
    xh = xn.astype(jnp.bfloat16)
    for c, w in enumerate(_pack_bf16_pairs(xn)):
        xrows_ref[pl.ds(c, tm, stride=XS_CHUNKS), :] = w

    xl = (xn - xh.astype(jnp.float32)).astype(jnp.bfloat16)
    wcat = wr_ref[...]
    prod = jnp.dot(xh, wcat, preferred_element_type=jnp.float32)
    logits = (prod[:, :LANES] + prod[:, LANES:]) + jnp.dot(
        xl, wcat[:, :LANES], preferred_element_type=jnp.float32)
    lt = logits.T[:ROUTER_ROWS, :] + brt_ref[...]

    gl = [lt[g:g + 1, :] for g in range(N_GROUPS)]
    gmax = functools.reduce(jnp.maximum, gl)
    gexp = [jnp.exp(x - gmax) for x in gl]
    gsum = functools.reduce(lambda a, b: a + b, gexp)
    gprob = [x / gsum for x in gexp]
    g_prob = functools.reduce(jnp.maximum, gprob)
    g_idx = jnp.full(g_prob.shape, N_GROUPS - 1, jnp.int32)
    for g in range(N_GROUPS - 2, -1, -1):
        g_idx = jnp.where(gprob[g] == g_prob, g, g_idx)

    el = []
    for j in range(EXPERTS_PER_GROUP):
        x = lt[SUBLANES + j:SUBLANES + j + 1, :]
        for g in range(1, N_GROUPS):
            r = SUBLANES + g * EXPERTS_PER_GROUP + j
            x = jnp.where(g_idx == g, lt[r:r + 1, :], x)
        el.append(x)
    emax = functools.reduce(jnp.maximum, el)
    eexp = [jnp.exp(x - emax) for x in el]
    esum = functools.reduce(lambda a, b: a + b, eexp)
    eprob = [x / esum for x in eexp]
    p1 = functools.reduce(jnp.maximum, eprob)
    i1 = jnp.full(p1.shape, EXPERTS_PER_GROUP - 1, jnp.int32)
    for j in range(EXPERTS_PER_GROUP - 2, -1, -1):
        i1 = jnp.where(eprob[j] == p1, j, i1)
    rest = [jnp.where(i1 == j, -1.0, eprob[j]) for j in range(EXPERTS_PER_GROUP)]
    p2 = functools.reduce(jnp.maximum, rest)
    i2 = jnp.full(p2.shape, EXPERTS_PER_GROUP - 1, jnp.int32)
    for j in range(EXPERTS_PER_GROUP - 2, -1, -1):
        i2 = jnp.where(rest[j] == p2, j, i2)
    psum = p1 + p2
    gate0 = g_prob * (p1 / psum)
    gate1 = g_prob * (p2 / psum)
    e0 = g_idx * EXPERTS_PER_GROUP + i1
    e1 = g_idx * EXPERTS_PER_GROUP + i2

    eio = lax.broadcasted_iota(jnp.int32, (N_EXPERTS, tm), 0)
    oh0 = (eio == e0).astype(jnp.float32)
    oh1 = (eio == e1).astype(jnp.float32)
    both = oh0 + oh1
    tr = lax.broadcasted_iota(jnp.int32, (tm, tm), 0)
    tc = lax.broadcasted_iota(jnp.int32, (tm, tm), 1)
    upper = (tr < tc).astype(jnp.bfloat16)
    before = jnp.dot(both.astype(jnp.bfloat16), upper, preferred_element_type=jnp.float32)
    before = before + carry_ref[...]
    rank0 = jnp.sum(oh0 * before, axis=0, keepdims=True)
    rank1 = jnp.sum(oh1 * (before + oh0), axis=0, keepdims=True)
    carry_ref[...] = carry_ref[...] + jnp.sum(both, axis=1, keepdims=True)
    cnt_ref[...] = jnp.broadcast_to(carry_ref[...], cnt_ref.shape)

    zi = jnp.zeros_like(e0)
    ri_ref[...] = jnp.concatenate(
        [e0, e1, rank0.astype(jnp.int32), rank1.astype(jnp.int32), zi, zi, zi, zi], axis=0)
    zf = jnp.zeros_like(gate0)
    rf_ref[...] = jnp.concatenate([gate0, gate1, zf, zf, zf, zf, zf, zf], axis=0)


def _epilogue_out_specs(tm):
    return [
        pl.BlockSpec((tm, D_MODEL), lambda i: (i, 0)),
        pl.BlockSpec((tm * XS_CHUNKS, LANES), lambda i: (i, 0)),
        pl.BlockSpec((SUBLANES, tm), lambda i: (0, i)),
        pl.BlockSpec((SUBLANES, tm), lambda i: (0, i)),
        pl.BlockSpec((N_EXPERTS, LANES), lambda i: (0, 0)),
    ]


def _epilogue_out_shapes(T):
    return [
        jax.ShapeDtypeStruct((T, D_MODEL), jnp.float32),
        jax.ShapeDtypeStruct((T * XS_CHUNKS, LANES), jnp.uint32),
        jax.ShapeDtypeStruct((SUBLANES, T), jnp.int32),
        jax.ShapeDtypeStruct((SUBLANES, T), jnp.float32),
        jax.ShapeDtypeStruct((N_EXPERTS, LANES), jnp.float32),
    ]


def _epilogue_in_specs():
    return [
        pl.BlockSpec((1, D_MODEL), lambda i: (0, 0)),
        pl.BlockSpec((D_MODEL, 2 * LANES), lambda i: (0, 0)),
        pl.BlockSpec((ROUTER_ROWS, 1), lambda i: (0, 0)),
    ]


def _attn_out_kernel(a_ref, w_ref, b_ref, res_ref, gffn_ref, wr_ref, brt_ref,
                     h_out_ref, xrows_ref, ri_ref, rf_ref, cnt_ref, carry_ref):
    @pl.when(pl.program_id(0) == 0)
    def _():
        carry_ref[...] = jnp.zeros_like(carry_ref)

    mix = jnp.dot(a_ref[...], w_ref[...], preferred_element_type=jnp.float32) + b_ref[...]
    h = res_ref[...] + mix
    _router_epilogue(h, gffn_ref, wr_ref, brt_ref, carry_ref,
                     h_out_ref, xrows_ref, ri_ref, rf_ref, cnt_ref)


def _attn_out_call(a, w, b, res, gffn, wr, brt):
    T = a.shape[0]
    tm = TM_DENSE
    return pl.pallas_call(
        _attn_out_kernel,
        grid=(T // tm,),
        in_specs=[
            pl.BlockSpec((tm, D_MODEL), lambda i: (i, 0)),
            pl.BlockSpec((D_MODEL, D_MODEL), lambda i: (0, 0)),
            pl.BlockSpec((1, D_MODEL), lambda i: (0, 0)),
            pl.BlockSpec((tm, D_MODEL), lambda i: (i, 0)),
        ] + _epilogue_in_specs(),
        out_specs=_epilogue_out_specs(tm),
        out_shape=_epilogue_out_shapes(T),
        scratch_shapes=[pltpu.VMEM((N_EXPERTS, 1), jnp.float32)],
        compiler_params=_params(("arbitrary",)),
        name="attn_out_router",
    )(a, w, b, res, gffn, wr, brt)


def _conv_mixer_kernel(tiles_per_seq, x_ref, g_ref, win_ref, cw_ref, wout_ref,
                       gffn_ref, wr_ref, brt_ref,
                       h_out_ref, xrows_ref, ri_ref, rf_ref, cnt_ref, carry_ref, tail_ref):
    i = pl.program_id(0)
    tm = x_ref.shape[0]

    @pl.when(i == 0)
    def _():
        carry_ref[...] = jnp.zeros_like(carry_ref)

    @pl.when(i % tiles_per_seq == 0)
    def _():
        tail_ref[...] = jnp.zeros_like(tail_ref)

    x = x_ref[...]
    xn = _rms(x, g_ref[...]).astype(jnp.bfloat16)
    bcu = jnp.dot(xn, win_ref[...], preferred_element_type=jnp.float32)
    b_gate = bcu[:, :D_MODEL]
    z = bcu[:, D_MODEL:2 * D_MODEL] * bcu[:, 2 * D_MODEL:]
    row = lax.broadcasted_iota(jnp.int32, (tm, D_MODEL), 0)
    tail = tail_ref[...]
    t1 = tail[SUBLANES - 1:SUBLANES, :]
    t2 = tail[SUBLANES - 2:SUBLANES - 1, :]
    z1 = jnp.where(row == 0, t1, pltpu.roll(z, 1, axis=0))
    z2 = jnp.where(row == 0, t2, jnp.where(row == 1, t1, pltpu.roll(z, 2, axis=0)))
    tail_ref[...] = z[tm - SUBLANES:, :]
    conv = z2 * cw_ref[0:1, :]
    conv = conv + z1 * cw_ref[1:2, :]
    conv = conv + z * cw_ref[2:3, :]
    gated = (b_gate * conv).astype(jnp.bfloat16)
    h = x + jnp.dot(gated, wout_ref[...], preferred_element_type=jnp.float32)
    _router_epilogue(h, gffn_ref, wr_ref, brt_ref, carry_ref,
                     h_out_ref, xrows_ref, ri_ref, rf_ref, cnt_ref)


def _conv_mixer_call(x2, g, win, cw, wout, gffn, wr, brt, seq):
    T = x2.shape[0]
    tm = TM_MIX1
    return pl.pallas_call(
        functools.partial(_conv_mixer_kernel, seq // tm),
        grid=(T // tm,),
        in_specs=[
            pl.BlockSpec((tm, D_MODEL), lambda i: (i, 0)),
            pl.BlockSpec((1, D_MODEL), lambda i: (0, 0)),
            pl.BlockSpec((D_MODEL, 3 * D_MODEL), lambda i: (0, 0)),
            pl.BlockSpec((CONV_WIDTH, D_MODEL), lambda i: (0, 0)),
            pl.BlockSpec((D_MODEL, D_MODEL), lambda i: (0, 0)),
        ] + _epilogue_in_specs(),
        out_specs=_epilogue_out_specs(tm),
        out_shape=_epilogue_out_shapes(T),
        scratch_shapes=[pltpu.VMEM((N_EXPERTS, 1), jnp.float32),
                        pltpu.VMEM((SUBLANES, D_MODEL), jnp.float32)],
        compiler_params=_params(("arbitrary",)),
        name="conv_mixer_router",
    )(x2, g, win, cw, wout, gffn, wr, brt)


def _row_copy(src_hbm, src_row, dst_hbm, dst_row, sem, chunks):
    return pltpu.make_async_copy(
        src_hbm.at[pl.ds(pl.multiple_of(src_row, chunks), chunks)],
        dst_hbm.at[pl.ds(pl.multiple_of(dst_row, chunks), chunks)], sem)


def _slot_rows_kernel(ri_ref, pstart_ref, o_ref):
    tb = ri_ref.shape[1]
    eio = lax.broadcasted_iota(jnp.int32, (N_EXPERTS, tb), 0)
    pstart = pstart_ref[...]
    slots = []
    for k in range(2):
        start = jnp.sum(jnp.where(eio == ri_ref[k:k + 1, :], pstart, 0), axis=0, keepdims=True)
        slots.append(start + ri_ref[2 + k:3 + k, :])
    zi = jnp.zeros_like(slots[0])
    o_ref[...] = jnp.concatenate(
        [s * XS_CHUNKS for s in slots] + [zi] * (SUBLANES - 2), axis=0)


def _slot_rows_call(ri, pstart):
    T = ri.shape[1]
    tb = TM_SLOT_ROWS
    return pl.pallas_call(
        _slot_rows_kernel,
        grid=(T // tb,),
        in_specs=[pl.BlockSpec((SUBLANES, tb), lambda i: (0, i)),
                  pl.BlockSpec((N_EXPERTS, 1), lambda i: (0, 0))],
        out_specs=pl.BlockSpec((SUBLANES, tb), lambda i: (0, i)),
        out_shape=jax.ShapeDtypeStruct((SUBLANES, T), jnp.int32),
        compiler_params=_params(("parallel",)),
        name="moe_slot_rows",
    )(ri, pstart[:, None])


def _dispatch_kernel(d0_ref, d1_ref, padrow_ref, x_ref, xs_hbm, zero_ref, sem, zsem):
    i = pl.program_id(0)
    tile_rows = TM_EXPERT * XS_CHUNKS

    @pl.when(i == 0)
    def _():
        zero_ref[...] = jnp.zeros_like(zero_ref)

        def zcopy(e):
            return pltpu.make_async_copy(
                zero_ref,
                xs_hbm.at[pl.ds(pl.multiple_of(padrow_ref[e], tile_rows), tile_rows)], zsem)

        def zstart(e, c):
            @pl.when(padrow_ref[e] >= 0)
            def _():
                zcopy(e).start()
            return c

        def zwait(e, c):
            @pl.when(padrow_ref[e] >= 0)
            def _():
                zcopy(e).wait()
            return c

        lax.fori_loop(0, N_EXPERTS, zstart, 0)
        lax.fori_loop(0, N_EXPERTS, zwait, 0)

        def tcopy(b):
            return pltpu.make_async_copy(
                zero_ref, xs_hbm.at[pl.ds(pl.multiple_of(b * tile_rows, tile_rows), tile_rows)], zsem)

        n_used = padrow_ref[N_EXPERTS]
        n_blk = xs_hbm.shape[0] // tile_rows
        lax.fori_loop(n_used, n_blk, lambda b, c: (tcopy(b).start(), c)[1], 0)
        lax.fori_loop(n_used, n_blk, lambda b, c: (tcopy(b).wait(), c)[1], 0)

    def issue(t, c):
        src = t * XS_CHUNKS
        _row_copy(x_ref, src, xs_hbm, d0_ref[t], sem, XS_CHUNKS).start(priority=0)
        _row_copy(x_ref, src, xs_hbm, d1_ref[t], sem, XS_CHUNKS).start(priority=1)
        return c

    lax.fori_loop(0, TM_DISPATCH, issue, 0, unroll=DMA_UNROLL)
    for _ in range(2):
        pltpu.make_async_copy(x_ref, xs_hbm.at[pl.ds(0, TM_DISPATCH * XS_CHUNKS)], sem).wait()


def _dispatch_call(d0, d1, padrow, xrows, capacity):
    T = d0.shape[0]
    return pl.pallas_call(
        _dispatch_kernel,
        grid=(T // TM_DISPATCH,),
        in_specs=[
            pl.BlockSpec((TM_DISPATCH,), lambda i: (i,), memory_space=pltpu.SMEM),
            pl.BlockSpec((TM_DISPATCH,), lambda i: (i,), memory_space=pltpu.SMEM),
            pl.BlockSpec(memory_space=pltpu.SMEM),
            pl.BlockSpec((TM_DISPATCH * XS_CHUNKS, LANES), lambda i: (i, 0)),
        ],
        out_specs=pl.BlockSpec(memory_space=pl.ANY),
        out_shape=jax.ShapeDtypeStruct((capacity * XS_CHUNKS, LANES), jnp.uint32),
        scratch_shapes=[pltpu.VMEM((TM_EXPERT * XS_CHUNKS, LANES), jnp.uint32),
                        pltpu.SemaphoreType.DMA, pltpu.SemaphoreType.DMA],
        compiler_params=_params(("arbitrary",)),
        name="moe_dispatch",
    )(d0, d1, padrow, xrows)


def _expert_kernel(tstart_ref, ntile_ref, cnt_ref, xs_hbm, wgu_ref, wd_ref, ys_hbm,
                   xbuf, ybuf, wgu_bf, wd_bf, sem_in, sem_out):
    nb = EXPERT_BUFS
    e = pl.program_id(0)
    tm = TM_EXPERT
    tile_rows = tm * XS_CHUNKS
    t0 = tstart_ref[e]
    nt = ntile_ref[e]
    n_used = tstart_ref[N_EXPERTS - 1] + ntile_ref[N_EXPERTS - 1]

    def tile_at(ref, g, rows):
        return ref.at[pl.ds(pl.multiple_of(g * rows, rows), rows)]

    def in_copy(g, slot):
        return pltpu.make_async_copy(tile_at(xs_hbm, g, tile_rows), xbuf.at[slot], sem_in.at[slot])

    def out_copy(g, slot):
        return pltpu.make_async_copy(ybuf.at[slot], tile_at(ys_hbm, g, tile_rows), sem_out.at[slot])

    @pl.when(e == 0)
    def _():
        for k in range(nb - 1):
            @pl.when(k < n_used)
            def _():
                in_copy(k, k).start(priority=1)

    @pl.when(nt > 0)
    def _():
        wgu_bf[...] = wgu_ref[0, 0].astype(jnp.bfloat16)
        wd_bf[...] = wd_ref[0, 0].astype(jnp.bfloat16)

    def tile_body(g, carry):
        slot = g % nb
        in_copy(g, slot).wait()

        @pl.when(g + (nb - 1) < n_used)
        def _():
            in_copy(g + (nb - 1), (g + (nb - 1)) % nb).start(priority=1)

        @pl.when(g >= nb)
        def _():
            out_copy(g - nb, slot).wait()

        def swiglu_rows(rows):
            x = _unpack_bf16_pairs(
                [xbuf[slot, pl.ds(c, rows, stride=XS_CHUNKS), :] for c in range(XS_CHUNKS)])
            gu = jnp.dot(x.astype(jnp.bfloat16), wgu_bf[...], preferred_element_type=jnp.float32)
            g_act = gu[:, :EXPERT_FF]
            u = gu[:, EXPERT_FF:]
            act = (g_act / (1.0 + jnp.exp(-g_act))) * u
            y = jnp.dot(act.astype(jnp.bfloat16), wd_bf[...], preferred_element_type=jnp.float32)
            for c, w in enumerate(_pack_bf16_pairs(y)):
                ybuf[slot, pl.ds(c, rows, stride=XS_CHUNKS), :] = w

        half = tm // 2
        live_rows = cnt_ref[e] - (g - t0) * tm

        @pl.when(live_rows > half)
        def _():
            swiglu_rows(tm)

        @pl.when(live_rows <= half)
        def _():
            swiglu_rows(half)
            ybuf[slot, pl.ds(half * XS_CHUNKS, half * XS_CHUNKS), :] = jnp.zeros(
                (half * XS_CHUNKS, LANES), ybuf.dtype)

        out_copy(g, slot).start(priority=1)
        return carry

    lax.fori_loop(t0, t0 + nt, tile_body, 0)

    @pl.when(e == N_EXPERTS - 1)
    def _():
        for k in range(1, nb + 1):
            @pl.when(n_used >= k)
            def _():
                out_copy(n_used - k, (n_used - k) % nb).wait()

        n_blk = ys_hbm.shape[0] // tile_rows
        ybuf[0] = jnp.zeros(ybuf.shape[1:], ybuf.dtype)
        lax.fori_loop(n_used, n_blk, lambda b, c: (out_copy(b, 0).start(), c)[1], 0)
        lax.fori_loop(n_used, n_blk, lambda b, c: (out_copy(b, 0).wait(), c)[1], 0)


def _expert_call(layer, tstart, ntile, counts, xs, wgu, wd, capacity):
    tile_rows = TM_EXPERT * XS_CHUNKS

    def w_map(e, ts, nt, cn):
        return (layer, e, 0, 0)

    return pl.pallas_call(
        _expert_kernel,
        grid_spec=pltpu.PrefetchScalarGridSpec(
            num_scalar_prefetch=3,
            grid=(N_EXPERTS,),
            in_specs=[
                pl.BlockSpec(memory_space=pl.ANY),
                pl.BlockSpec((1, 1, D_MODEL, 2 * EXPERT_FF), w_map),
                pl.BlockSpec((1, 1, EXPERT_FF, D_MODEL), w_map),
            ],
            out_specs=pl.BlockSpec(memory_space=pl.ANY),
            scratch_shapes=[pltpu.VMEM((EXPERT_BUFS, tile_rows, LANES), jnp.uint32),
                            pltpu.VMEM((EXPERT_BUFS, tile_rows, LANES), jnp.uint32),
                            pltpu.VMEM((D_MODEL, 2 * EXPERT_FF), jnp.bfloat16),
                            pltpu.VMEM((EXPERT_FF, D_MODEL), jnp.bfloat16),
                            pltpu.SemaphoreType.DMA((EXPERT_BUFS,)),
                            pltpu.SemaphoreType.DMA((EXPERT_BUFS,))],
        ),
        out_shape=jax.ShapeDtypeStruct((capacity * XS_CHUNKS, LANES), jnp.uint32),
        compiler_params=_params(("arbitrary",)),
        name="moe_experts",
    )(tstart, ntile, counts, xs, wgu, wd)


def _gathered_residual(tm, d0_ref, d1_ref, d0n_ref, d1n_ref, ys_hbm, h_ref, gate_ref,
                       y0_ref, y1_ref, sem):
    i = pl.program_id(0)
    slot = i % 2

    def issue_tile(d0r, d1r, s):
        def issue(t, c):
            dst = t * XS_CHUNKS
            _row_copy(ys_hbm, d0r[t], y0_ref.at[s], dst, sem.at[s], XS_CHUNKS).start(priority=0)
            _row_copy(ys_hbm, d1r[t], y1_ref.at[s], dst, sem.at[s], XS_CHUNKS).start(priority=1)
            return c
        lax.fori_loop(0, tm, issue, 0, unroll=DMA_UNROLL)

    @pl.when(i == 0)
    def _():
        issue_tile(d0_ref, d1_ref, 0)

    @pl.when(i + 1 < pl.num_programs(0))
    def _():
        issue_tile(d0n_ref, d1n_ref, 1 - slot)

    for buf in (y0_ref, y1_ref):
        pltpu.make_async_copy(ys_hbm.at[pl.ds(0, tm * XS_CHUNKS)], buf.at[slot], sem.at[slot]).wait()

    g0 = gate_ref[:, 0:1]
    g1 = gate_ref[:, 1:2]
    y0 = _unpack_bf16_pairs([y0_ref[slot, pl.ds(c, tm, stride=XS_CHUNKS), :] for c in range(XS_CHUNKS)])
    y1 = _unpack_bf16_pairs([y1_ref[slot, pl.ds(c, tm, stride=XS_CHUNKS), :] for c in range(XS_CHUNKS)])
    return h_ref[...] + (y0 * g0 + y1 * g1)


def _gather_in_specs(tm, n_tiles):
    last = n_tiles - 1
    return [
        pl.BlockSpec((tm,), lambda i: (i,), memory_space=pltpu.SMEM),
        pl.BlockSpec((tm,), lambda i: (i,), memory_space=pltpu.SMEM),
        pl.BlockSpec((tm,), lambda i: (jnp.minimum(i + 1, last),), memory_space=pltpu.SMEM),
        pl.BlockSpec((tm,), lambda i: (jnp.minimum(i + 1, last),), memory_space=pltpu.SMEM),
        pl.BlockSpec(memory_space=pl.ANY),
        pl.BlockSpec((tm, D_MODEL), lambda i: (i, 0)),
        pl.BlockSpec((tm, 2), lambda i: (i, 0)),
    ]


def _gather_scratch(tm):
    return [pltpu.VMEM((2, tm * XS_CHUNKS, LANES), jnp.uint32),
            pltpu.VMEM((2, tm * XS_CHUNKS, LANES), jnp.uint32),
            pltpu.SemaphoreType.DMA((2,))]


def _combine_kernel(final_norm, d0_ref, d1_ref, d0n_ref, d1n_ref, ys_hbm, h_ref, gate_ref, gfin_ref,
                    o_ref, y0_ref, y1_ref, sem):
    h = _gathered_residual(TM_COMBINE, d0_ref, d1_ref, d0n_ref, d1n_ref, ys_hbm, h_ref, gate_ref,
                           y0_ref, y1_ref, sem)
    if final_norm:
        h = _rms(h, gfin_ref[...])
    o_ref[...] = h


def _combine_call(d0, d1, ys, h, gates, gfin, final_norm):
    T = h.shape[0]
    tm = TM_COMBINE
    return pl.pallas_call(
        functools.partial(_combine_kernel, final_norm),
        grid=(T // tm,),
        in_specs=_gather_in_specs(tm, T // tm) + [pl.BlockSpec((1, D_MODEL), lambda i: (0, 0))],
        out_specs=pl.BlockSpec((tm, D_MODEL), lambda i: (i, 0)),
        out_shape=jax.ShapeDtypeStruct((T, D_MODEL), jnp.float32),
        scratch_shapes=_gather_scratch(tm),
        compiler_params=_params(("arbitrary",)),
        name="moe_combine",
    )(d0, d1, d0, d1, ys, h, gates, gfin)


def _group_heads_last(w):
    lead = w.shape[:-1]
    w = w.reshape(lead + (N_KV_HEADS, GROUP, HEAD_DIM))
    w = jnp.swapaxes(w, -3, -2)
    return w.reshape(lead + (N_Q_HEADS * HEAD_DIM,))


def _rel_bucket_table():
    qi = np.arange(ATTN_BLOCK)[:, None]
    kj = np.arange(2 * ATTN_BLOCK)[None, :]
    dist = np.maximum(qi + ATTN_BLOCK - kj, 0)
    max_exact = N_REL_BUCKETS // 2
    d = np.maximum(dist, max_exact).astype(np.float32)
    large = max_exact + (np.log(d / np.float32(max_exact)) / np.float32(math.log(REL_MAX_DISTANCE / max_exact))
                         * np.float32(N_REL_BUCKETS - max_exact)).astype(np.int32)
    large = np.minimum(large, N_REL_BUCKETS - 1)
    return np.where(dist < max_exact, dist, large).astype(np.int32)


def _rel_bias_blocks(rel_bias, sinks):
    onehot = (jnp.asarray(_rel_bucket_table())[..., None] == jnp.arange(N_REL_BUCKETS)).astype(jnp.float32)
    bias = jnp.einsum("qkb,bh->hqk", onehot, rel_bias.astype(jnp.float32),
                      precision=lax.Precision.HIGHEST)
    sink_col = jnp.broadcast_to(sinks.astype(jnp.float32)[:, None, None], (N_Q_HEADS, ATTN_BLOCK, 1))
    return jnp.concatenate([sink_col, bias[:, :, 1:]], axis=2) * LOG2_E


def _router_weights(w_group, b_group, w_expert, b_expert):
    zw = jnp.zeros((D_MODEL, SUBLANES - N_GROUPS), jnp.float32)
    zw2 = jnp.zeros((D_MODEL, LANES - SUBLANES - N_EXPERTS), jnp.float32)
    wr = jnp.concatenate([w_group, zw, w_expert, zw2], axis=1)
    wh = wr.astype(jnp.bfloat16)
    wl = (wr - wh.astype(jnp.float32)).astype(jnp.bfloat16)
    wr = jnp.concatenate([wh, wl], axis=1)
    brt = jnp.concatenate([b_group, jnp.zeros((SUBLANES - N_GROUPS,), jnp.float32), b_expert])[:, None]
    return wr, brt


def _moe(layer, h, xrows, ri, rf, cnt, w_gate_up, w_down, gfin, final_norm):
    T = h.shape[0]
    tm = TM_EXPERT
    capacity = -(-(2 * T) // tm) * tm + N_EXPERTS * tm
    counts = cnt[:, 0].astype(jnp.int32)
    ntile = (counts + tm - 1) // tm
    tend = jnp.cumsum(ntile)
    pstart = (tend - ntile) * tm
    n_used = tend[-1:]
    padrow = jnp.concatenate([jnp.where(ntile > 0, (tend - 1) * (tm * XS_CHUNKS), -1), n_used]
                             ).astype(jnp.int32)

    dst = _slot_rows_call(ri, pstart)
    xs = _dispatch_call(dst[0], dst[1], padrow, xrows, capacity)
    ys = _expert_call(layer, (tend - ntile).astype(jnp.int32), ntile.astype(jnp.int32), counts, xs,
                      w_gate_up, w_down, capacity)
    gates = rf[:2].T
    return _combine_call(dst[0], dst[1], ys, h, gates, gfin, final_norm)


def kernel(x, norm_mix, norm_ffn, final_norm, rel_bias, attn_w_qkv, attn_b_qkv, attn_w_o, attn_b_o,
           attn_sinks, conv_w_in, conv_w, conv_w_out, moe_w_group, moe_b_group, moe_w_expert,
           moe_b_expert, moe_w_gate_up, moe_w_down):
    B, S, D = x.shape
    T = B * S
    x2 = x.reshape(T, D)
    nq = N_Q_HEADS * HEAD_DIM
    scale = HEAD_DIM ** -0.5 * LOG2_E

    wqkv = attn_w_qkv[0]
    bqkv = attn_b_qkv[0]
    w_all = jnp.concatenate([_group_heads_last(wqkv[:, :nq]) * scale, wqkv[:, nq:]],
                            axis=1).astype(jnp.bfloat16)
    b_all = jnp.concatenate([_group_heads_last(bqkv[:nq]) * scale, bqkv[nq:]])[None, :]
    q, kt, v = _qkv_call(x2, norm_mix[0][None, :], w_all, b_all)
    a = _attn_call(q, kt, v, _rel_bias_blocks(rel_bias, attn_sinks[0]), B, S)
    wr, brt = _router_weights(moe_w_group[0], moe_b_group[0], moe_w_expert[0], moe_b_expert[0])
    w_o = _group_heads_last(attn_w_o[0].T).T.astype(jnp.bfloat16)
    h, xrows, ri, rf, cnt = _attn_out_call(
        a, w_o, attn_b_o[0][None, :], x2, norm_ffn[0][None, :], wr, brt)
    h = _moe(0, h, xrows, ri, rf, cnt, moe_w_gate_up, moe_w_down, final_norm[None, :], False)

    wr, brt = _router_weights(moe_w_group[1], moe_b_group[1], moe_w_expert[1], moe_b_expert[1])
    h, xrows, ri, rf, cnt = _conv_mixer_call(
        h, norm_mix[1][None, :], conv_w_in[0].astype(jnp.bfloat16), conv_w[0],
        conv_w_out[0].astype(jnp.bfloat16), norm_ffn[1][None, :], wr, brt, S)
    out = _moe(1, h, xrows, ri, rf, cnt, moe_w_gate_up, moe_w_down, final_norm[None, :], True)
    return out.reshape(B, S, D)
```

```python
import functools
import math

import numpy as np
import jax
import jax.numpy as jnp
from jax import lax
from jax.experimental import pallas as pl
from jax.experimental.pallas import tpu as pltpu

D_MODEL = 1024
N_Q_HEADS = 16
N_KV_HEADS = 2
HEAD_DIM = 64
GROUP = N_Q_HEADS // N_KV_HEADS
WINDOW = 128
ATTN_BLOCK = 128
N_REL_BUCKETS = 32
REL_MAX_DISTANCE = 128
CONV_WIDTH = 3
N_GROUPS = 4
EXPERTS_PER_GROUP = 8
N_EXPERTS = N_GROUPS * EXPERTS_PER_GROUP
EXPERT_FF = 512
RMS_EPS = 1e-5
LOG2_E = math.log2(math.e)

LANES = 128
SUBLANES = 8
ROW_CHUNKS = D_MODEL // LANES
XS_CHUNKS = ROW_CHUNKS // 2
VMEM_LIMIT = 56 * 1024 * 1024

TM_DENSE = 1024
TQ = 1024
TM_MIX1 = 1024
TM_EXPERT = 512
TM_DISPATCH = 4096
TM_COMBINE = 256
DMA_UNROLL = 8
TM_SLOT_ROWS = 4096
EXPERT_BUFS = 4
COMBINE_BUFS = 3
ROUTER_ROWS = 40


def _rms(x, g):
    return x * lax.rsqrt(jnp.mean(x * x, axis=-1, keepdims=True) + RMS_EPS) * g


def _pack_bf16_pairs(x):
    bits = lax.bitcast_convert_type(x.astype(jnp.bfloat16).astype(jnp.float32), jnp.uint32)
    return [(bits[:, (2 * c) * LANES:(2 * c + 1) * LANES] >> 16)
            | bits[:, (2 * c + 1) * LANES:(2 * c + 2) * LANES] for c in range(XS_CHUNKS)]


def _unpack_bf16_pairs(words):
    cols = []
    for w in words:
        cols.append(lax.bitcast_convert_type(w << 16, jnp.float32))
        cols.append(lax.bitcast_convert_type(w & jnp.uint32(0xFFFF0000), jnp.float32))
    return jnp.concatenate(cols, axis=-1)


def _params(sem):
    return pltpu.CompilerParams(dimension_semantics=sem, vmem_limit_bytes=VMEM_LIMIT)


def _qkv_kernel(x_ref, g_ref, w_ref, b_ref, q_ref, kt_ref, v_ref):
    xn = _rms(x_ref[...], g_ref[...]).astype(jnp.bfloat16)
    out = jnp.dot(xn, w_ref[...], preferred_element_type=jnp.float32) + b_ref[...]
    nq = N_Q_HEADS * HEAD_DIM
    nk = N_KV_HEADS * HEAD_DIM
    q_ref[...] = out[:, :nq].astype(jnp.bfloat16)
    kt_ref[...] = out[:, nq:nq + nk].T.astype(jnp.bfloat16)
    v_ref[...] = out[:, nq + nk:].astype(jnp.bfloat16)


def _qkv_call(x2, g, w, b):
    T = x2.shape[0]
    nq = N_Q_HEADS * HEAD_DIM
    nk = N_KV_HEADS * HEAD_DIM
    tm = TM_DENSE
    return pl.pallas_call(
        _qkv_kernel,
        grid=(T // tm,),
        in_specs=[
            pl.BlockSpec((tm, D_MODEL), lambda i: (i, 0)),
            pl.BlockSpec((1, D_MODEL), lambda i: (0, 0)),
            pl.BlockSpec((D_MODEL, nq + 2 * nk), lambda i: (0, 0)),
            pl.BlockSpec((1, nq + 2 * nk), lambda i: (0, 0)),
        ],
        out_specs=[
            pl.BlockSpec((tm, nq), lambda i: (i, 0)),
            pl.BlockSpec((nk, tm), lambda i: (0, i)),
            pl.BlockSpec((tm, nk), lambda i: (i, 0)),
        ],
        out_shape=[
            jax.ShapeDtypeStruct((T, nq), jnp.bfloat16),
            jax.ShapeDtypeStruct((nk, T), jnp.bfloat16),
            jax.ShapeDtypeStruct((T, nk), jnp.bfloat16),
        ],
        compiler_params=_params(("parallel",)),
        name="qkv_proj",
    )(x2, g, w, b)


def _attn_kernel(q_ref, kt_ref, ktp_ref, v_ref, vp_ref, bias_ref, o_ref):
    first_tile = pl.program_id(1) == 0
    nblk = TQ // ATTN_BLOCK
    qi_io = lax.broadcasted_iota(jnp.int32, (ATTN_BLOCK, 2 * ATTN_BLOCK), 0)
    kj_io = lax.broadcasted_iota(jnp.int32, (ATTN_BLOCK, 2 * ATTN_BLOCK), 1)
    dist = qi_io + ATTN_BLOCK - kj_io
    band = (dist >= 0) & (dist < WINDOW)
    sink_col = kj_io == 0
    kt_keep = lax.broadcasted_iota(jnp.int32, (N_KV_HEADS * HEAD_DIM, 2 * ATTN_BLOCK), 1) > 0
    v_keep = lax.broadcasted_iota(jnp.int32, (2 * ATTN_BLOCK, N_KV_HEADS * HEAD_DIM), 0) > 0
    low_half = lax.broadcasted_iota(jnp.int32, (ATTN_BLOCK, LANES), 1) < HEAD_DIM
    for qi in range(nblk):
        r0 = qi * ATTN_BLOCK
        if qi == 0:
            kt_blk = jnp.concatenate([ktp_ref[...], kt_ref[:, 0:ATTN_BLOCK]], axis=1)
            v_blk = jnp.concatenate([vp_ref[...], v_ref[0:ATTN_BLOCK, :]], axis=0)
            mask = (band & (jnp.logical_not(first_tile) | (kj_io >= ATTN_BLOCK))) | sink_col
        else:
            kt_blk = kt_ref[:, r0 - ATTN_BLOCK:r0 + ATTN_BLOCK]
            v_blk = v_ref[r0 - ATTN_BLOCK:r0 + ATTN_BLOCK, :]
            mask = band | sink_col
        kt_blk = jnp.where(kt_keep, kt_blk, jnp.zeros_like(kt_blk))
        v_blk = jnp.where(v_keep, v_blk, jnp.zeros_like(v_blk))
        for m in range(GROUP):
            qg = q_ref[r0:r0 + ATTN_BLOCK, m * LANES:(m + 1) * LANES]
            halves = []
            for half in range(N_KV_HEADS):
                h = m + GROUP * half
                keep = low_half if half == 0 else jnp.logical_not(low_half)
                qpad = jnp.where(keep, qg, jnp.zeros_like(qg))
                s = jnp.dot(qpad, kt_blk, preferred_element_type=jnp.float32)
                s = jnp.where(mask, s + bias_ref[h], -1e30)
                mx = jnp.max(s, axis=-1, keepdims=True)
                p = jnp.exp2(s - mx)
                den = jnp.sum(p, axis=-1, keepdims=True)
                pv = jnp.dot(p.astype(jnp.bfloat16), v_blk, preferred_element_type=jnp.float32)
                halves.append(pv * (1.0 / den))
            og = jnp.where(low_half, halves[0], halves[1])
            o_ref[r0:r0 + ATTN_BLOCK, m * LANES:(m + 1) * LANES] = og.astype(jnp.bfloat16)


def _attn_call(q, kt, v, bias, batch, seq):
    T = q.shape[0]
    nq = N_Q_HEADS * HEAD_DIM
    nk = N_KV_HEADS * HEAD_DIM
    tiles = seq // TQ
    per = TQ // ATTN_BLOCK

    def cur(b, j):
        return b * tiles + j

    def prev(b, j):
        return jnp.maximum((b * tiles + j) * per - 1, b * tiles * per)

    return pl.pallas_call(
        _attn_kernel,
        grid=(batch, tiles),
        in_specs=[
            pl.BlockSpec((TQ, nq), lambda b, j: (cur(b, j), 0)),
            pl.BlockSpec((nk, TQ), lambda b, j: (0, cur(b, j))),
            pl.BlockSpec((nk, ATTN_BLOCK), lambda b, j: (0, prev(b, j))),
            pl.BlockSpec((TQ, nk), lambda b, j: (cur(b, j), 0)),
            pl.BlockSpec((ATTN_BLOCK, nk), lambda b, j: (prev(b, j), 0)),
            pl.BlockSpec((N_Q_HEADS, ATTN_BLOCK, 2 * ATTN_BLOCK), lambda b, j: (0, 0, 0)),
        ],
        out_specs=pl.BlockSpec((TQ, nq), lambda b, j: (cur(b, j), 0)),
        out_shape=jax.ShapeDtypeStruct((T, nq), jnp.bfloat16),
        compiler_params=_params(("parallel", "parallel")),
        name="swa_attention",
    )(q, kt, kt, v, v, bias)


def _router_epilogue(h, gffn_ref, wr_ref, brt_ref, carry_ref,
                     h_out_ref, xrows_ref, ri_ref, rf_ref, cnt_ref):
    tm = h.shape[0]
    h_out_ref[...] = h
    xn = _rms(h, gffn_ref[...])
    xh = xn.astype(jnp.bfloat16)
    for c, w in enumerate(_pack_bf16_pairs(xn)):
        xrows_ref[pl.ds(c, tm, stride=XS_CHUNKS), :] = w

    xl = (xn - xh.astype(jnp.float32)).astype(jnp.bfloat16)
    wcat = wr_ref[...]
    prod = jnp.dot(xh, wcat, preferred_element_type=jnp.float32)
    logits = (prod[:, :LANES] + prod[:, LANES:]) + jnp.dot(
        xl, wcat[:, :LANES], preferred_element_type=jnp.float32)
    lt = logits.T[:ROUTER_ROWS, :] + brt_ref[...]

    gl = [lt[g:g + 1, :] for g in range(N_GROUPS)]
    gmax = functools.reduce(jnp.maximum, gl)
    gexp = [jnp.exp(x - gmax) for x in gl]
    gsum = functools.reduce(lambda a, b: a + b, gexp)
    gprob = [x / gsum for x in gexp]
    g_prob = functools.reduce(jnp.maximum, gprob)
    g_idx = jnp.full(g_prob.shape, N_GROUPS - 1, jnp.int32)
    for g in range(N_GROUPS - 2, -1, -1):
        g_idx = jnp.where(gprob[g] == g_prob, g, g_idx)

    el = []
    for j in range(EXPERTS_PER_GROUP):
        x = lt[SUBLANES + j:SUBLANES + j + 1, :]
        for g in range(1, N_GROUPS):
            r = SUBLANES + g * EXPERTS_PER_GROUP + j
            x = jnp.where(g_idx == g, lt[r:r + 1, :], x)
        el.append(x)
    emax = functools.reduce(jnp.maximum, el)
    eexp = [jnp.exp(x - emax) for x in el]
    esum = functools.reduce(lambda a, b: a + b, eexp)
    eprob = [x / esum for x in eexp]
    p1 = functools.reduce(jnp.maximum, eprob)
    i1 = jnp.full(p1.shape, EXPERTS_PER_GROUP - 1, jnp.int32)
    for j in range(EXPERTS_PER_GROUP - 2, -1, -1):
        i1 = jnp.where(eprob[j] == p1, j, i1)
    rest = [jnp.where(i1 == j, -1.0, eprob[j]) for j in range(EXPERTS_PER_GROUP)]
    p2 = functools.reduce(jnp.maximum, rest)
    i2 = jnp.full(p2.shape, EXPERTS_PER_GROUP - 1, jnp.int32)
    for j in range(EXPERTS_PER_GROUP - 2, -1, -1):
        i2 = jnp.where(rest[j] == p2, j, i2)
    psum = p1 + p2
    gate0 = g_prob * (p1 / psum)
    gate1 = g_prob * (p2 / psum)
    e0 = g_idx * EXPERTS_PER_GROUP + i1
    e1 = g_idx * EXPERTS_PER_GROUP + i2

    eio = lax.broadcasted_iota(jnp.int32, (N_EXPERTS, tm), 0)
    oh0 = (eio == e0).astype(jnp.float32)
    oh1 = (eio == e1).astype(jnp.float32)
    both = oh0 + oh1
    tr = lax.broadcasted_iota(jnp.int32, (tm, tm), 0)
    tc = lax.broadcasted_iota(jnp.int32, (tm, tm), 1)
    upper = (tr < tc).astype(jnp.bfloat16)
    before = jnp.dot(both.astype(jnp.bfloat16), upper, preferred_element_type=jnp.float32)
    before = before + carry_ref[...]
    rank0 = jnp.sum(oh0 * before, axis=0, keepdims=True)
    rank1 = jnp.sum(oh1 * (before + oh0), axis=0, keepdims=True)
    carry_ref[...] = carry_ref[...] + jnp.sum(both, axis=1, keepdims=True)
    cnt_ref[...] = jnp.broadcast_to(carry_ref[...], cnt_ref.shape)

    zi = jnp.zeros_like(e0)
    ri_ref[...] = jnp.concatenate(
        [e0, e1, rank0.astype(jnp.int32), rank1.astype(jnp.int32), zi, zi, zi, zi], axis=0)
    zf = jnp.zeros_like(gate0)
    rf_ref[...] = jnp.concatenate([gate0, gate1, zf, zf, zf, zf, zf, zf], axis=0)


def _epilogue_out_specs(tm):
    return [
        pl.BlockSpec((tm, D_MODEL), lambda i: (i, 0)),
        pl.BlockSpec((tm * XS_CHUNKS, LANES), lambda i: (i, 0)),
        pl.BlockSpec((SUBLANES, tm), lambda i: (0, i)),
        pl.BlockSpec((SUBLANES, tm), lambda i: (0, i)),
        pl.BlockSpec((N_EXPERTS, LANES), lambda i: (0, 0)),
    ]


def _epilogue_out_shapes(T):
    return [
        jax.ShapeDtypeStruct((T, D_MODEL), jnp.float32),
        jax.ShapeDtypeStruct((T * XS_CHUNKS, LANES), jnp.uint32),
        jax.ShapeDtypeStruct((SUBLANES, T), jnp.int32),
        jax.ShapeDtypeStruct((SUBLANES, T), jnp.float32),
        jax.ShapeDtypeStruct((N_EXPERTS, LANES), jnp.float32),
    ]


def _epilogue_in_specs():
    return [
        pl.BlockSpec((1, D_MODEL), lambda i: (0, 0)),
        pl.BlockSpec((D_MODEL, 2 * LANES), lambda i: (0, 0)),
        pl.BlockSpec((ROUTER_ROWS, 1), lambda i: (0, 0)),
    ]


def _attn_out_kernel(a_ref, w_ref, b_ref, res_ref, gffn_ref, wr_ref, brt_ref,
                     h_out_ref, xrows_ref, ri_ref, rf_ref, cnt_ref, carry_ref):
    @pl.when(pl.program_id(0) == 0)
    def _():
        carry_ref[...] = jnp.zeros_like(carry_ref)

    mix = jnp.dot(a_ref[...], w_ref[...], preferred_element_type=jnp.float32) + b_ref[...]
    h = res_ref[...] + mix
    _router_epilogue(h, gffn_ref, wr_ref, brt_ref, carry_ref,
                     h_out_ref, xrows_ref, ri_ref, rf_ref, cnt_ref)


def _attn_out_call(a, w, b, res, gffn, wr, brt):
    T = a.shape[0]
    tm = TM_DENSE
    return pl.pallas_call(
        _attn_out_kernel,
        grid=(T // tm,),
        in_specs=[
            pl.BlockSpec((tm, D_MODEL), lambda i: (i, 0)),
            pl.BlockSpec((D_MODEL, D_MODEL), lambda i: (0, 0)),
            pl.BlockSpec((1, D_MODEL), lambda i: (0, 0)),
            pl.BlockSpec((tm, D_MODEL), lambda i: (i, 0)),
        ] + _epilogue_in_specs(),
        out_specs=_epilogue_out_specs(tm),
        out_shape=_epilogue_out_shapes(T),
        scratch_shapes=[pltpu.VMEM((N_EXPERTS, 1), jnp.float32)],
        compiler_params=_params(("arbitrary",)),
        name="attn_out_router",
    )(a, w, b, res, gffn, wr, brt)


def _conv_mixer_kernel(tiles_per_seq, x_ref, g_ref, win_ref, cw_ref, wout_ref,
                       gffn_ref, wr_ref, brt_ref,
                       h_out_ref, xrows_ref, ri_ref, rf_ref, cnt_ref, carry_ref, tail_ref):
    i = pl.program_id(0)
    tm = x_ref.shape[0]

    @pl.when(i == 0)
    def _():
        carry_ref[...] = jnp.zeros_like(carry_ref)

    @pl.when(i % tiles_per_seq == 0)
    def _():
        tail_ref[...] = jnp.zeros_like(tail_ref)

    x = x_ref[...]
    xn = _rms(x, g_ref[...]).astype(jnp.bfloat16)
    bcu = jnp.dot(xn, win_ref[...], preferred_element_type=jnp.float32)
    b_gate = bcu[:, :D_MODEL]
    z = bcu[:, D_MODEL:2 * D_MODEL] * bcu[:, 2 * D_MODEL:]
    row = lax.broadcasted_iota(jnp.int32, (tm, D_MODEL), 0)
    tail = tail_ref[...]
    t1 = tail[SUBLANES - 1:SUBLANES, :]
    t2 = tail[SUBLANES - 2:SUBLANES - 1, :]
    z1 = jnp.where(row == 0, t1, pltpu.roll(z, 1, axis=0))
    z2 = jnp.where(row == 0, t2, jnp.where(row == 1, t1, pltpu.roll(z, 2, axis=0)))
    tail_ref[...] = z[tm - SUBLANES:, :]
    conv = z2 * cw_ref[0:1, :]
    conv = conv + z1 * cw_ref[1:2, :]
    conv = conv + z * cw_ref[2:3, :]
    gated = (b_gate * conv).astype(jnp.bfloat16)
    h = x + jnp.dot(gated, wout_ref[...], preferred_element_type=jnp.float32)
    _router_epilogue(h, gffn_ref, wr_ref, brt_ref, carry_ref,
                     h_out_ref, xrows_ref, ri_ref, rf_ref, cnt_ref)


def _conv_mixer_call(x2, g, win, cw, wout, gffn, wr, brt, seq):
    T = x2.shape[0]
    tm = TM_MIX1
    return pl.pallas_call(
        functools.partial(_conv_mixer_kernel, seq // tm),
        grid=(T // tm,),
        in_specs=[
            pl.BlockSpec((tm, D_MODEL), lambda i: (i, 0)),
            pl.BlockSpec((1, D_MODEL), lambda i: (0, 0)),
            pl.BlockSpec((D_MODEL, 3 * D_MODEL), lambda i: (0, 0)),
            pl.BlockSpec((CONV_WIDTH, D_MODEL), lambda i: (0, 0)),
            pl.BlockSpec((D_MODEL, D_MODEL), lambda i: (0, 0)),
        ] + _epilogue_in_specs(),
        out_specs=_epilogue_out_specs(tm),
        out_shape=_epilogue_out_shapes(T),
        scratch_shapes=[pltpu.VMEM((N_EXPERTS, 1), jnp.float32),
                        pltpu.VMEM((SUBLANES, D_MODEL), jnp.float32)],
        compiler_params=_params(("arbitrary",)),
        name="conv_mixer_router",
    )(x2, g, win, cw, wout, gffn, wr, brt)


def _row_copy(src_hbm, src_row, dst_hbm, dst_row, sem, chunks):
    return pltpu.make_async_copy(
        src_hbm.at[pl.ds(pl.multiple_of(src_row, chunks), chunks)],
        dst_hbm.at[pl.ds(pl.multiple_of(dst_row, chunks), chunks)], sem)


def _slot_rows_kernel(ri_ref, pstart_ref, o_ref):
    tb = ri_ref.shape[1]
    eio = lax.broadcasted_iota(jnp.int32, (N_EXPERTS, tb), 0)
    pstart = pstart_ref[...]
    slots = []
    for k in range(2):
        start = jnp.sum(jnp.where(eio == ri_ref[k:k + 1, :], pstart, 0), axis=0, keepdims=True)
        slots.append(start + ri_ref[2 + k:3 + k, :])
    zi = jnp.zeros_like(slots[0])
    o_ref[...] = jnp.concatenate(
        [s * XS_CHUNKS for s in slots] + [zi] * (SUBLANES - 2), axis=0)


def _slot_rows_call(ri, pstart):
    T = ri.shape[1]
    tb = TM_SLOT_ROWS
    return pl.pallas_call(
        _slot_rows_kernel,
        grid=(T // tb,),
        in_specs=[pl.BlockSpec((SUBLANES, tb), lambda i: (0, i)),
                  pl.BlockSpec((N_EXPERTS, 1), lambda i: (0, 0))],
        out_specs=pl.BlockSpec((SUBLANES, tb), lambda i: (0, i)),
        out_shape=jax.ShapeDtypeStruct((SUBLANES, T), jnp.int32),
        compiler_params=_params(("parallel",)),
        name="moe_slot_rows",
    )(ri, pstart[:, None])


def _dispatch_kernel(d0_ref, d1_ref, padrow_ref, x_ref, xs_hbm, zero_ref, sem, zsem):
    i = pl.program_id(0)
    tile_rows = TM_EXPERT * XS_CHUNKS

    @pl.when(i == 0)
    def _():
        zero_ref[...] = jnp.zeros_like(zero_ref)

        def zcopy(e):
            return pltpu.make_async_copy(
                zero_ref,
                xs_hbm.at[pl.ds(pl.multiple_of(padrow_ref[e], tile_rows), tile_rows)], zsem)

        def zstart(e, c):
            @pl.when(padrow_ref[e] >= 0)
            def _():
                zcopy(e).start()
            return c

        def zwait(e, c):
            @pl.when(padrow_ref[e] >= 0)
            def _():
                zcopy(e).wait()
            return c

        lax.fori_loop(0, N_EXPERTS, zstart, 0)
        lax.fori_loop(0, N_EXPERTS, zwait, 0)

        def tcopy(b):
            return pltpu.make_async_copy(
                zero_ref, xs_hbm.at[pl.ds(pl.multiple_of(b * tile_rows, tile_rows), tile_rows)], zsem)

        n_used = padrow_ref[N_EXPERTS]
        n_blk = xs_hbm.shape[0] // tile_rows
        lax.fori_loop(n_used, n_blk, lambda b, c: (tcopy(b).start(), c)[1], 0)
        lax.fori_loop(n_used, n_blk, lambda b, c: (tcopy(b).wait(), c)[1], 0)

    def issue(t, c):
        src = t * XS_CHUNKS
        _row_copy(x_ref, src, xs_hbm, d0_ref[t], sem, XS_CHUNKS).start(priority=0)
        _row_copy(x_ref, src, xs_hbm, d1_ref[t], sem, XS_CHUNKS).start(priority=1)
        return c

    lax.fori_loop(0, TM_DISPATCH, issue, 0, unroll=DMA_UNROLL)
    for _ in range(2):
        pltpu.make_async_copy(x_ref, xs_hbm.at[pl.ds(0, TM_DISPATCH * XS_CHUNKS)], sem).wait()


def _dispatch_call(d0, d1, padrow, xrows, capacity):
    T = d0.shape[0]
    return pl.pallas_call(
        _dispatch_kernel,
        grid=(T // TM_DISPATCH,),
        in_specs=[
            pl.BlockSpec((TM_DISPATCH,), lambda i: (i,), memory_space=pltpu.SMEM),
            pl.BlockSpec((TM_DISPATCH,), lambda i: (i,), memory_space=pltpu.SMEM),
            pl.BlockSpec(memory_space=pltpu.SMEM),
            pl.BlockSpec((TM_DISPATCH * XS_CHUNKS, LANES), lambda i: (i, 0)),
        ],
        out_specs=pl.BlockSpec(memory_space=pl.ANY),
        out_shape=jax.ShapeDtypeStruct((capacity * XS_CHUNKS, LANES), jnp.uint32),
        scratch_shapes=[pltpu.VMEM((TM_EXPERT * XS_CHUNKS, LANES), jnp.uint32),
                        pltpu.SemaphoreType.DMA, pltpu.SemaphoreType.DMA],
        compiler_params=_params(("arbitrary",)),
        name="moe_dispatch",
    )(d0, d1, padrow, xrows)


def _expert_kernel(tstart_ref, ntile_ref, cnt_ref, xs_hbm, wgu_ref, wd_ref, ys_hbm,
                   xbuf, ybuf, wgu_bf, wd_bf, sem_in, sem_out):
    nb = EXPERT_BUFS
    e = pl.program_id(0)
    tm = TM_EXPERT
    tile_rows = tm * XS_CHUNKS
    t0 = tstart_ref[e]
    nt = ntile_ref[e]
    n_used = tstart_ref[N_EXPERTS - 1] + ntile_ref[N_EXPERTS - 1]

    def tile_at(ref, g, rows):
        return ref.at[pl.ds(pl.multiple_of(g * rows, rows), rows)]

    def in_copy(g, slot):
        return pltpu.make_async_copy(tile_at(xs_hbm, g, tile_rows), xbuf.at[slot], sem_in.at[slot])

    def out_copy(g, slot):
        return pltpu.make_async_copy(ybuf.at[slot], tile_at(ys_hbm, g, tile_rows), sem_out.at[slot])

    @pl.when(e == 0)
    def _():
        for k in range(nb - 1):
            @pl.when(k < n_used)
            def _():
                in_copy(k, k).start(priority=1)

    @pl.when(nt > 0)
    def _():
        wgu_bf[...] = wgu_ref[0, 0].astype(jnp.bfloat16)
        wd_bf[...] = wd_ref[0, 0].astype(jnp.bfloat16)

    def tile_body(g, carry):
        slot = g % nb
        in_copy(g, slot).wait()

        @pl.when(g + (nb - 1) < n_used)
        def _():
            in_copy(g + (nb - 1), (g + (nb - 1)) % nb).start(priority=1)

        @pl.when(g >= nb)
        def _():
            out_copy(g - nb, slot).wait()

        def swiglu_rows(rows):
            x = _unpack_bf16_pairs(
                [xbuf[slot, pl.ds(c, rows, stride=XS_CHUNKS), :] for c in range(XS_CHUNKS)])
            gu = jnp.dot(x.astype(jnp.bfloat16), wgu_bf[...], preferred_element_type=jnp.float32)
            g_act = gu[:, :EXPERT_FF]
            u = gu[:, EXPERT_FF:]
            act = (g_act / (1.0 + jnp.exp(-g_act))) * u
            y = jnp.dot(act.astype(jnp.bfloat16), wd_bf[...], preferred_element_type=jnp.float32)
            for c, w in enumerate(_pack_bf16_pairs(y)):
                ybuf[slot, pl.ds(c, rows, stride=XS_CHUNKS), :] = w

        half = tm // 2
        live_rows = cnt_ref[e] - (g - t0) * tm

        @pl.when(live_rows > half)
        def _():
            swiglu_rows(tm)

        @pl.when(live_rows <= half)
        def _():
            swiglu_rows(half)
            ybuf[slot, pl.ds(half * XS_CHUNKS, half * XS_CHUNKS), :] = jnp.zeros(
                (half * XS_CHUNKS, LANES), ybuf.dtype)

        out_copy(g, slot).start(priority=1)
        return carry

    lax.fori_loop(t0, t0 + nt, tile_body, 0)

    @pl.when(e == N_EXPERTS - 1)
    def _():
        for k in range(1, nb + 1):
            @pl.when(n_used >= k)
            def _():
                out_copy(n_used - k, (n_used - k) % nb).wait()

        n_blk = ys_hbm.shape[0] // tile_rows
        ybuf[0] = jnp.zeros(ybuf.shape[1:], ybuf.dtype)
        lax.fori_loop(n_used, n_blk, lambda b, c: (out_copy(b, 0).start(), c)[1], 0)
        lax.fori_loop(n_used, n_blk, lambda b, c: (out_copy(b, 0).wait(), c)[1], 0)


def _expert_call(layer, tstart, ntile, counts, xs, wgu, wd, capacity):
    tile_rows = TM_EXPERT * XS_CHUNKS

    def w_map(e, ts, nt, cn):
        return (layer, e, 0, 0)

    return pl.pallas_call(
        _expert_kernel,
        grid_spec=pltpu.PrefetchScalarGridSpec(
            num_scalar_prefetch=3,
            grid=(N_EXPERTS,),
            in_specs=[
                pl.BlockSpec(memory_space=pl.ANY),
                pl.BlockSpec((1, 1, D_MODEL, 2 * EXPERT_FF), w_map),
                pl.BlockSpec((1, 1, EXPERT_FF, D_MODEL), w_map),
            ],
            out_specs=pl.BlockSpec(memory_space=pl.ANY),
            scratch_shapes=[pltpu.VMEM((EXPERT_BUFS, tile_rows, LANES), jnp.uint32),
                            pltpu.VMEM((EXPERT_BUFS, tile_rows, LANES), jnp.uint32),
                            pltpu.VMEM((D_MODEL, 2 * EXPERT_FF), jnp.bfloat16),
                            pltpu.VMEM((EXPERT_FF, D_MODEL), jnp.bfloat16),
                            pltpu.SemaphoreType.DMA((EXPERT_BUFS,)),
                            pltpu.SemaphoreType.DMA((EXPERT_BUFS,))],
        ),
        out_shape=jax.ShapeDtypeStruct((capacity * XS_CHUNKS, LANES), jnp.uint32),
        compiler_params=_params(("arbitrary",)),
        name="moe_experts",
    )(tstart, ntile, counts, xs, wgu, wd)


def _gathered_residual(tm, addr_refs, ys_hbm, h_ref, gate_ref, y0_ref, y1_ref, sem):
    nb = COMBINE_BUFS
    i = pl.program_id(0)
    n = pl.num_programs(0)
    slot = i % nb

    def issue_tile(k, s):
        d0r, d1r = addr_refs[2 * k], addr_refs[2 * k + 1]

        def issue(t, c):
            dst = t * XS_CHUNKS
            _row_copy(ys_hbm, d0r[t], y0_ref.at[s], dst, sem.at[s], XS_CHUNKS).start(priority=0)
            _row_copy(ys_hbm, d1r[t], y1_ref.at[s], dst, sem.at[s], XS_CHUNKS).start(priority=1)
            return c
        lax.fori_loop(0, tm, issue, 0, unroll=DMA_UNROLL)

    @pl.when(i == 0)
    def _():
        for k in range(nb - 1):
            @pl.when(k < n)
            def _():
                issue_tile(k, k)

    @pl.when(i + (nb - 1) < n)
    def _():
        issue_tile(nb - 1, (i + (nb - 1)) % nb)

    for buf in (y0_ref, y1_ref):
        pltpu.make_async_copy(ys_hbm.at[pl.ds(0, tm * XS_CHUNKS)], buf.at[slot], sem.at[slot]).wait()

    g0 = gate_ref[:, 0:1]
    g1 = gate_ref[:, 1:2]
    y0 = _unpack_bf16_pairs([y0_ref[slot, pl.ds(c, tm, stride=XS_CHUNKS), :] for c in range(XS_CHUNKS)])
    y1 = _unpack_bf16_pairs([y1_ref[slot, pl.ds(c, tm, stride=XS_CHUNKS), :] for c in range(XS_CHUNKS)])
    return h_ref[...] + (y0 * g0 + y1 * g1)


def _gather_in_specs(tm, n_tiles):
    last = n_tiles - 1
    addr = []
    for k in range(COMBINE_BUFS):
        addr += [pl.BlockSpec((tm,), lambda i, k=k: (jnp.minimum(i + k, last),),
                              memory_space=pltpu.SMEM)] * 2
    return addr + [
        pl.BlockSpec(memory_space=pl.ANY),
        pl.BlockSpec((tm, D_MODEL), lambda i: (i, 0)),
        pl.BlockSpec((tm, 2), lambda i: (i, 0)),
    ]


def _gather_scratch(tm):
    return [pltpu.VMEM((COMBINE_BUFS, tm * XS_CHUNKS, LANES), jnp.uint32),
            pltpu.VMEM((COMBINE_BUFS, tm * XS_CHUNKS, LANES), jnp.uint32),
            pltpu.SemaphoreType.DMA((COMBINE_BUFS,))]


def _combine_kernel(final_norm, *refs):
    addr_refs = refs[:2 * COMBINE_BUFS]
    ys_hbm, h_ref, gate_ref, gfin_ref, o_ref, y0_ref, y1_ref, sem = refs[2 * COMBINE_BUFS:]
    h = _gathered_residual(TM_COMBINE, addr_refs, ys_hbm, h_ref, gate_ref, y0_ref, y1_ref, sem)
    if final_norm:
        h = _rms(h, gfin_ref[...])
    o_ref[...] = h


def _combine_call(d0, d1, ys, h, gates, gfin, final_norm):
    T = h.shape[0]
    tm = TM_COMBINE
    return pl.pallas_call(
        functools.partial(_combine_kernel, final_norm),
        grid=(T // tm,),
        in_specs=_gather_in_specs(tm, T // tm) + [pl.BlockSpec((1, D_MODEL), lambda i: (0, 0))],
        out_specs=pl.BlockSpec((tm, D_MODEL), lambda i: (i, 0)),
        out_shape=jax.ShapeDtypeStruct((T, D_MODEL), jnp.float32),
        scratch_shapes=_gather_scratch(tm),
        compiler_params=_params(("arbitrary",)),
        name="moe_combine",
    )(*([d0, d1] * COMBINE_BUFS), ys, h, gates, gfin)


def _group_heads_last(w):
    lead = w.shape[:-1]
    w = w.reshape(lead + (N_KV_HEADS, GROUP, HEAD_DIM))
    w = jnp.swapaxes(w, -3, -2)
    return w.reshape(lead + (N_Q_HEADS * HEAD_DIM,))


def _rel_bucket_table():
    qi = np.arange(ATTN_BLOCK)[:, None]
    kj = np.arange(2 * ATTN_BLOCK)[None, :]
    dist = np.maximum(qi + ATTN_BLOCK - kj, 0)
    max_exact = N_REL_BUCKETS // 2
    d = np.maximum(dist, max_exact).astype(np.float32)
    large = max_exact + (np.log(d / np.float32(max_exact)) / np.float32(math.log(REL_MAX_DISTANCE / max_exact))
                         * np.float32(N_REL_BUCKETS - max_exact)).astype(np.int32)
    large = np.minimum(large, N_REL_BUCKETS - 1)
    return np.where(dist < max_exact, dist, large).astype(np.int32)


def _rel_bias_blocks(rel_bias, sinks):
    onehot = (jnp.asarray(_rel_bucket_table())[..., None] == jnp.arange(N_REL_BUCKETS)).astype(jnp.float32)
    bias = jnp.einsum("qkb,bh->hqk", onehot, rel_bias.astype(jnp.float32),
                      precision=lax.Precision.HIGHEST)
    sink_col = jnp.broadcast_to(sinks.astype(jnp.float32)[:, None, None], (N_Q_HEADS, ATTN_BLOCK, 1))
    return jnp.concatenate([sink_col, bias[:, :, 1:]], axis=2) * LOG2_E


def _router_weights(w_group, b_group, w_expert, b_expert):
    zw = jnp.zeros((D_MODEL, SUBLANES - N_GROUPS), jnp.float32)
    zw2 = jnp.zeros((D_MODEL, LANES - SUBLANES - N_EXPERTS), jnp.float32)
    wr = jnp.concatenate([w_group, zw, w_expert, zw2], axis=1)
    wh = wr.astype(jnp.bfloat16)
    wl = (wr - wh.astype(jnp.float32)).astype(jnp.bfloat16)
    wr = jnp.concatenate([wh, wl], axis=1)
    brt = jnp.concatenate([b_group, jnp.zeros((SUBLANES - N_GROUPS,), jnp.float32), b_expert])[:, None]
    return wr, brt


def _moe(layer, h, xrows, ri, rf, cnt, w_gate_up, w_down, gfin, final_norm):
    T = h.shape[0]
    tm = TM_EXPERT
    capacity = -(-(2 * T) // tm) * tm + N_EXPERTS * tm
    counts = cnt[:, 0].astype(jnp.int32)
    ntile = (counts + tm - 1) // tm
    tend = jnp.cumsum(ntile)
    pstart = (tend - ntile) * tm
    n_used = tend[-1:]
    padrow = jnp.concatenate([jnp.where(ntile > 0, (tend - 1) * (tm * XS_CHUNKS), -1), n_used]
                             ).astype(jnp.int32)

    dst = _slot_rows_call(ri, pstart)
    xs = _dispatch_call(dst[0], dst[1], padrow, xrows, capacity)
    ys = _expert_call(layer, (tend - ntile).astype(jnp.int32), ntile.astype(jnp.int32), counts, xs,
                      w_gate_up, w_down, capacity)
    gates = rf[:2].T
    return _combine_call(dst[0], dst[1], ys, h, gates, gfin, final_norm)


def kernel(x, norm_mix, norm_ffn, final_norm, rel_bias, attn_w_qkv, attn_b_qkv, attn_w_o, attn_b_o,
           attn_sinks, conv_w_in, conv_w, conv_w_out, moe_w_group, moe_b_group, moe_w_expert,
           moe_b_expert, moe_w_gate_up, moe_w_down):
    B, S, D = x.shape
    T = B * S
    x2 = x.reshape(T, D)
    nq = N_Q_HEADS * HEAD_DIM
    scale = HEAD_DIM ** -0.5 * LOG2_E

    wqkv = attn_w_qkv[0]
    bqkv = attn_b_qkv[0]
    w_all = jnp.concatenate([_group_heads_last(wqkv[:, :nq]) * scale, wqkv[:, nq:]],
                            axis=1).astype(jnp.bfloat16)
    b_all = jnp.concatenate([_group_heads_last(bqkv[:nq]) * scale, bqkv[nq:]])[None, :]
    q, kt, v = _qkv_call(x2, norm_mix[0][None, :], w_all, b_all)
    a = _attn_call(q, kt, v, _rel_bias_blocks(rel_bias, attn_sinks[0]), B, S)
    wr, brt = _router_weights(moe_w_group[0], moe_b_group[0], moe_w_expert[0], moe_b_expert[0])
    w_o = _group_heads_last(attn_w_o[0].T).T.astype(jnp.bfloat16)
    h, xrows, ri, rf, cnt = _attn_out_call(
        a, w_o, attn_b_o[0][None, :], x2, norm_ffn[0][None, :], wr, brt)
    h = _moe(0, h, xrows, ri, rf, cnt, moe_w_gate_up, moe_w_down, final_norm[None, :], False)

    wr, brt = _router_weights(moe_w_group[1], moe_b_group[1], moe_w_expert[1], moe_b_expert[1])
    h, xrows, ri, rf, cnt = _conv_mixer_call(
        h, norm_mix[1][None, :], conv_w_in[0].astype(jnp.bfloat16), conv_w[0],
        conv_w_out[0].astype(jnp.bfloat16), norm_ffn[1][None, :], wr, brt, S)
    out = _moe(1, h, xrows, ri, rf, cnt, moe_w_gate_up, moe_w_down, final_norm[None, :], True)
    return out.reshape(B, S, D)
```

```python
import functools
import math

import numpy as np
import jax
import jax.numpy as jnp
from jax import lax
from jax.experimental import pallas as pl
from jax.experimental.pallas import tpu as pltpu

D_MODEL = 1024
N_Q_HEADS = 16
N_KV_HEADS = 2
HEAD_DIM = 64
GROUP = N_Q_HEADS // N_KV_HEADS
WINDOW = 128
ATTN_BLOCK = 128
N_REL_BUCKETS = 32
REL_MAX_DISTANCE = 128
CONV_WIDTH = 3
N_GROUPS = 4
EXPERTS_PER_GROUP = 8
N_EXPERTS = N_GROUPS * EXPERTS_PER_GROUP
EXPERT_FF = 512
RMS_EPS = 1e-5
LOG2_E = math.log2(math.e)

LANES = 128
SUBLANES = 8
ROW_CHUNKS = D_MODEL // LANES
XS_CHUNKS = ROW_CHUNKS // 2
VMEM_LIMIT = 56 * 1024 * 1024

TM_DENSE = 1024
TM_QKV = 2048
TQ = 2048
TM_MIX1 = 1024
TM_EXPERT = 512
TM_DISPATCH = 4096
TM_COMBINE = 256
DMA_UNROLL = 8
TM_SLOT_ROWS = 4096
EXPERT_BUFS = 4
EXPERT_TAIL_PARTS = 4
ROUTER_ROWS = 40


def _rms(x, g):
    return x * lax.rsqrt(jnp.mean(x * x, axis=-1, keepdims=True) + RMS_EPS) * g


def _pack_bf16_pairs(x):
    bits = lax.bitcast_convert_type(x.astype(jnp.bfloat16).astype(jnp.float32), jnp.uint32)
    return [(bits[:, (2 * c) * LANES:(2 * c + 1) * LANES] >> 16)
            | bits[:, (2 * c + 1) * LANES:(2 * c + 2) * LANES] for c in range(XS_CHUNKS)]


def _unpack_bf16_pairs(words):
    cols = []
    for w in words:
        cols.append(lax.bitcast_convert_type(w << 16, jnp.float32))
        cols.append(lax.bitcast_convert_type(w & jnp.uint32(0xFFFF0000), jnp.float32))
    return jnp.concatenate(cols, axis=-1)


def _params(sem):
    return pltpu.CompilerParams(dimension_semantics=sem, vmem_limit_bytes=VMEM_LIMIT)


def _qkv_kernel(x_ref, g_ref, w_ref, b_ref, q_ref, kt_ref, v_ref):
    xn = _rms(x_ref[...], g_ref[...]).astype(jnp.bfloat16)
    out = jnp.dot(xn, w_ref[...], preferred_element_type=jnp.float32) + b_ref[...]
    nq = N_Q_HEADS * HEAD_DIM
    nk = N_KV_HEADS * HEAD_DIM
    q_ref[...] = out[:, :nq].astype(jnp.bfloat16)
    kt_ref[...] = out[:, nq:nq + nk].T.astype(jnp.bfloat16)
    v_ref[...] = out[:, nq + nk:].astype(jnp.bfloat16)


def _qkv_call(x2, g, w, b):
    T = x2.shape[0]
    nq = N_Q_HEADS * HEAD_DIM
    nk = N_KV_HEADS * HEAD_DIM
    tm = TM_QKV
    return pl.pallas_call(
        _qkv_kernel,
        grid=(T // tm,),
        in_specs=[
            pl.BlockSpec((tm, D_MODEL), lambda i: (i, 0)),
            pl.BlockSpec((1, D_MODEL), lambda i: (0, 0)),
            pl.BlockSpec((D_MODEL, nq + 2 * nk), lambda i: (0, 0)),
            pl.BlockSpec((1, nq + 2 * nk), lambda i: (0, 0)),
        ],
        out_specs=[
            pl.BlockSpec((tm, nq), lambda i: (i, 0)),
            pl.BlockSpec((nk, tm), lambda i: (0, i)),
            pl.BlockSpec((tm, nk), lambda i: (i, 0)),
        ],
        out_shape=[
            jax.ShapeDtypeStruct((T, nq), jnp.bfloat16),
            jax.ShapeDtypeStruct((nk, T), jnp.bfloat16),
            jax.ShapeDtypeStruct((T, nk), jnp.bfloat16),
        ],
        compiler_params=_params(("parallel",)),
        name="qkv_proj",
    )(x2, g, w, b)


def _attn_kernel(q_ref, kt_ref, ktp_ref, v_ref, vp_ref, bias_ref, o_ref):
    first_tile = pl.program_id(1) == 0
    nblk = TQ // ATTN_BLOCK
    qi_io = lax.broadcasted_iota(jnp.int32, (ATTN_BLOCK, 2 * ATTN_BLOCK), 0)
    kj_io = lax.broadcasted_iota(jnp.int32, (ATTN_BLOCK, 2 * ATTN_BLOCK), 1)
    dist = qi_io + ATTN_BLOCK - kj_io
    band = (dist >= 0) & (dist < WINDOW)
    sink_col = kj_io == 0
    kt_keep = lax.broadcasted_iota(jnp.int32, (N_KV_HEADS * HEAD_DIM, 2 * ATTN_BLOCK), 1) > 0
    v_keep = lax.broadcasted_iota(jnp.int32, (2 * ATTN_BLOCK, N_KV_HEADS * HEAD_DIM), 0) > 0
    low_half = lax.broadcasted_iota(jnp.int32, (ATTN_BLOCK, LANES), 1) < HEAD_DIM
    for qi in range(nblk):
        r0 = qi * ATTN_BLOCK
        if qi == 0:
            kt_blk = jnp.concatenate([ktp_ref[...], kt_ref[:, 0:ATTN_BLOCK]], axis=1)
            v_blk = jnp.concatenate([vp_ref[...], v_ref[0:ATTN_BLOCK, :]], axis=0)
            mask = (band & (jnp.logical_not(first_tile) | (kj_io >= ATTN_BLOCK))) | sink_col
        else:
            kt_blk = kt_ref[:, r0 - ATTN_BLOCK:r0 + ATTN_BLOCK]
            v_blk = v_ref[r0 - ATTN_BLOCK:r0 + ATTN_BLOCK, :]
            mask = band | sink_col
        kt_blk = jnp.where(kt_keep, kt_blk, jnp.zeros_like(kt_blk))
        v_blk = jnp.where(v_keep, v_blk, jnp.zeros_like(v_blk))
        for m in range(GROUP):
            qg = q_ref[r0:r0 + ATTN_BLOCK, m * LANES:(m + 1) * LANES]
            halves = []
            for half in range(N_KV_HEADS):
                h = m + GROUP * half
                keep = low_half if half == 0 else jnp.logical_not(low_half)
                qpad = jnp.where(keep, qg, jnp.zeros_like(qg))
                s = jnp.dot(qpad, kt_blk, preferred_element_type=jnp.float32)
                s = jnp.where(mask, s + bias_ref[h], -1e30)
                mx = jnp.max(s, axis=-1, keepdims=True)
                p = jnp.exp2(s - mx)
                den = jnp.sum(p, axis=-1, keepdims=True)
                pv = jnp.dot(p.astype(jnp.bfloat16), v_blk, preferred_element_type=jnp.float32)
                halves.append(pv * (1.0 / den))
            og = jnp.where(low_half, halves[0], halves[1])
            o_ref[r0:r0 + ATTN_BLOCK, m * LANES:(m + 1) * LANES] = og.astype(jnp.bfloat16)


def _attn_call(q, kt, v, bias, batch, seq):
    T = q.shape[0]
    nq = N_Q_HEADS * HEAD_DIM
    nk = N_KV_HEADS * HEAD_DIM
    tiles = seq // TQ
    per = TQ // ATTN_BLOCK

    def cur(b, j):
        return b * tiles + j

    def prev(b, j):
        return jnp.maximum((b * tiles + j) * per - 1, b * tiles * per)

    return pl.pallas_call(
        _attn_kernel,
        grid=(batch, tiles),
        in_specs=[
            pl.BlockSpec((TQ, nq), lambda b, j: (cur(b, j), 0)),
            pl.BlockSpec((nk, TQ), lambda b, j: (0, cur(b, j))),
            pl.BlockSpec((nk, ATTN_BLOCK), lambda b, j: (0, prev(b, j))),
            pl.BlockSpec((TQ, nk), lambda b, j: (cur(b, j), 0)),
            pl.BlockSpec((ATTN_BLOCK, nk), lambda b, j: (prev(b, j), 0)),
            pl.BlockSpec((N_Q_HEADS, ATTN_BLOCK, 2 * ATTN_BLOCK), lambda b, j: (0, 0, 0)),
        ],
        out_specs=pl.BlockSpec((TQ, nq), lambda b, j: (cur(b, j), 0)),
        out_shape=jax.ShapeDtypeStruct((T, nq), jnp.bfloat16),
        compiler_params=_params(("parallel", "parallel")),
        name="swa_attention",
    )(q, kt, kt, v, v, bias)


def _router_epilogue(h, gffn_ref, wr_ref, brt_ref, carry_ref,
                     h_out_ref, xrows_ref, ri_ref, rf_ref, cnt_ref):
    tm = h.shape[0]
    h_out_ref[...] = h
    xn = _rms(h, gffn_ref[...])
    xh = xn.astype(jnp.bfloat16)
    for c, w in enumerate(_pack_bf16_pairs(xn)):
        xrows_ref[pl.ds(c, tm, stride=XS_CHUNKS), :] = w

    xl = (xn - xh.astype(jnp.float32)).astype(jnp.bfloat16)
    wcat = wr_ref[...]
    prod = jnp.dot(xh, wcat, preferred_element_type=jnp.float32)
    logits = (prod[:, :LANES] + prod[:, LANES:]) + jnp.dot(
        xl, wcat[:, :LANES], preferred_element_type=jnp.float32)
    lt = logits.T[:ROUTER_ROWS, :] + brt_ref[...]

    gl = [lt[g:g + 1, :] for g in range(N_GROUPS)]
    gmax = functools.reduce(jnp.maximum, gl)
    gexp = [jnp.exp(x - gmax) for x in gl]
    gsum = functools.reduce(lambda a, b: a + b, gexp)
    gprob = [x / gsum for x in gexp]
    g_prob = functools.reduce(jnp.maximum, gprob)
    g_idx = jnp.full(g_prob.shape, N_GROUPS - 1, jnp.int32)
    for g in range(N_GROUPS - 2, -1, -1):
        g_idx = jnp.where(gprob[g] == g_prob, g, g_idx)

    el = []
    for j in range(EXPERTS_PER_GROUP):
        x = lt[SUBLANES + j:SUBLANES + j + 1, :]
        for g in range(1, N_GROUPS):
            r = SUBLANES + g * EXPERTS_PER_GROUP + j
            x = jnp.where(g_idx == g, lt[r:r + 1, :], x)
        el.append(x)
    emax = functools.reduce(jnp.maximum, el)
    eexp = [jnp.exp(x - emax) for x in el]
    esum = functools.reduce(lambda a, b: a + b, eexp)
    eprob = [x / esum for x in eexp]
    p1 = functools.reduce(jnp.maximum, eprob)
    i1 = jnp.full(p1.shape, EXPERTS_PER_GROUP - 1, jnp.int32)
    for j in range(EXPERTS_PER_GROUP - 2, -1, -1):
        i1 = jnp.where(eprob[j] == p1, j, i1)
    rest = [jnp.where(i1 == j, -1.0, eprob[j]) for j in range(EXPERTS_PER_GROUP)]
    p2 = functools.reduce(jnp.maximum, rest)
    i2 = jnp.full(p2.shape, EXPERTS_PER_GROUP - 1, jnp.int32)
    for j in range(EXPERTS_PER_GROUP - 2, -1, -1):
        i2 = jnp.where(rest[j] == p2, j, i2)
    psum = p1 + p2
    gate0 = g_prob * (p1 / psum)
    gate1 = g_prob * (p2 / psum)
    e0 = g_idx * EXPERTS_PER_GROUP + i1
    e1 = g_idx * EXPERTS_PER_GROUP + i2

    eio = lax.broadcasted_iota(jnp.int32, (N_EXPERTS, tm), 0)
    oh0 = (eio == e0).astype(jnp.float32)
    oh1 = (eio == e1).astype(jnp.float32)
    both = oh0 + oh1
    tr = lax.broadcasted_iota(jnp.int32, (tm, tm), 0)
    tc = lax.broadcasted_iota(jnp.int32, (tm, tm), 1)
    upper = (tr < tc).astype(jnp.bfloat16)
    before = jnp.dot(both.astype(jnp.bfloat16), upper, preferred_element_type=jnp.float32)
    before = before + carry_ref[...]
    rank0 = jnp.sum(oh0 * before, axis=0, keepdims=True)
    rank1 = jnp.sum(oh1 * (before + oh0), axis=0, keepdims=True)
    carry_ref[...] = carry_ref[...] + jnp.sum(both, axis=1, keepdims=True)
    cnt_ref[...] = jnp.broadcast_to(carry_ref[...], cnt_ref.shape)

    zi = jnp.zeros_like(e0)
    ri_ref[...] = jnp.concatenate(
        [e0, e1, rank0.astype(jnp.int32), rank1.astype(jnp.int32), zi, zi, zi, zi], axis=0)
    zf = jnp.zeros_like(gate0)
    rf_ref[...] = jnp.concatenate([gate0, gate1, zf, zf, zf, zf, zf, zf], axis=0)


def _epilogue_out_specs(tm):
    return [
        pl.BlockSpec((tm, D_MODEL), lambda i: (i, 0)),
        pl.BlockSpec((tm * XS_CHUNKS, LANES), lambda i: (i, 0)),
        pl.BlockSpec((SUBLANES, tm), lambda i: (0, i)),
        pl.BlockSpec((SUBLANES, tm), lambda i: (0, i)),
        pl.BlockSpec((N_EXPERTS, LANES), lambda i: (0, 0)),
    ]


def _epilogue_out_shapes(T):
    return [
        jax.ShapeDtypeStruct((T, D_MODEL), jnp.float32),
        jax.ShapeDtypeStruct((T * XS_CHUNKS, LANES), jnp.uint32),
        jax.ShapeDtypeStruct((SUBLANES, T), jnp.int32),
        jax.ShapeDtypeStruct((SUBLANES, T), jnp.float32),
        jax.ShapeDtypeStruct((N_EXPERTS, LANES), jnp.float32),
    ]


def _epilogue_in_specs():
    return [
        pl.BlockSpec((1, D_MODEL), lambda i: (0, 0)),
        pl.BlockSpec((D_MODEL, 2 * LANES), lambda i: (0, 0)),
        pl.BlockSpec((ROUTER_ROWS, 1), lambda i: (0, 0)),
    ]


def _attn_out_kernel(a_ref, w_ref, b_ref, res_ref, gffn_ref, wr_ref, brt_ref,
                     h_out_ref, xrows_ref, ri_ref, rf_ref, cnt_ref, carry_ref):
    @pl.when(pl.program_id(0) == 0)
    def _():
        carry_ref[...] = jnp.zeros_like(carry_ref)

    mix = jnp.dot(a_ref[...], w_ref[...], preferred_element_type=jnp.float32) + b_ref[...]
    h = res_ref[...] + mix
    _router_epilogue(h, gffn_ref, wr_ref, brt_ref, carry_ref,
                     h_out_ref, xrows_ref, ri_ref, rf_ref, cnt_ref)


def _attn_out_call(a, w, b, res, gffn, wr, brt):
    T = a.shape[0]
    tm = TM_DENSE
    return pl.pallas_call(
        _attn_out_kernel,
        grid=(T // tm,),
        in_specs=[
            pl.BlockSpec((tm, D_MODEL), lambda i: (i, 0)),
            pl.BlockSpec((D_MODEL, D_MODEL), lambda i: (0, 0)),
            pl.BlockSpec((1, D_MODEL), lambda i: (0, 0)),
            pl.BlockSpec((tm, D_MODEL), lambda i: (i, 0)),
        ] + _epilogue_in_specs(),
        out_specs=_epilogue_out_specs(tm),
        out_shape=_epilogue_out_shapes(T),
        scratch_shapes=[pltpu.VMEM((N_EXPERTS, 1), jnp.float32)],
        compiler_params=_params(("arbitrary",)),
        name="attn_out_router",
    )(a, w, b, res, gffn, wr, brt)


def _conv_mixer_kernel(tiles_per_seq, x_ref, g_ref, win_ref, cw_ref, wout_ref,
                       gffn_ref, wr_ref, brt_ref,
                       h_out_ref, xrows_ref, ri_ref, rf_ref, cnt_ref, carry_ref, tail_ref):
    i = pl.program_id(0)
    tm = x_ref.shape[0]

    @pl.when(i == 0)
    def _():
        carry_ref[...] = jnp.zeros_like(carry_ref)

    @pl.when(i % tiles_per_seq == 0)
    def _():
        tail_ref[...] = jnp.zeros_like(tail_ref)

    x = x_ref[...]
    xn = _rms(x, g_ref[...]).astype(jnp.bfloat16)
    bcu = jnp.dot(xn, win_ref[...], preferred_element_type=jnp.float32)
    b_gate = bcu[:, :D_MODEL]
    z = bcu[:, D_MODEL:2 * D_MODEL] * bcu[:, 2 * D_MODEL:]
    row = lax.broadcasted_iota(jnp.int32, (tm, D_MODEL), 0)
    tail = tail_ref[...]
    t1 = tail[SUBLANES - 1:SUBLANES, :]
    t2 = tail[SUBLANES - 2:SUBLANES - 1, :]
    z1 = jnp.where(row == 0, t1, pltpu.roll(z, 1, axis=0))
    z2 = jnp.where(row == 0, t2, jnp.where(row == 1, t1, pltpu.roll(z, 2, axis=0)))
    tail_ref[...] = z[tm - SUBLANES:, :]
    conv = z2 * cw_ref[0:1, :]
    conv = conv + z1 * cw_ref[1:2, :]
    conv = conv + z * cw_ref[2:3, :]
    gated = (b_gate * conv).astype(jnp.bfloat16)
    h = x + jnp.dot(gated, wout_ref[...], preferred_element_type=jnp.float32)
    _router_epilogue(h, gffn_ref, wr_ref, brt_ref, carry_ref,
                     h_out_ref, xrows_ref, ri_ref, rf_ref, cnt_ref)


def _conv_mixer_call(x2, g, win, cw, wout, gffn, wr, brt, seq):
    T = x2.shape[0]
    tm = TM_MIX1
    return pl.pallas_call(
        functools.partial(_conv_mixer_kernel, seq // tm),
        grid=(T // tm,),
        in_specs=[
            pl.BlockSpec((tm, D_MODEL), lambda i: (i, 0)),
            pl.BlockSpec((1, D_MODEL), lambda i: (0, 0)),
            pl.BlockSpec((D_MODEL, 3 * D_MODEL), lambda i: (0, 0)),
            pl.BlockSpec((CONV_WIDTH, D_MODEL), lambda i: (0, 0)),
            pl.BlockSpec((D_MODEL, D_MODEL), lambda i: (0, 0)),
        ] + _epilogue_in_specs(),
        out_specs=_epilogue_out_specs(tm),
        out_shape=_epilogue_out_shapes(T),
        scratch_shapes=[pltpu.VMEM((N_EXPERTS, 1), jnp.float32),
                        pltpu.VMEM((SUBLANES, D_MODEL), jnp.float32)],
        compiler_params=_params(("arbitrary",)),
        name="conv_mixer_router",
    )(x2, g, win, cw, wout, gffn, wr, brt)


def _row_copy(src_hbm, src_row, dst_hbm, dst_row, sem, chunks):
    return pltpu.make_async_copy(
        src_hbm.at[pl.ds(pl.multiple_of(src_row, chunks), chunks)],
        dst_hbm.at[pl.ds(pl.multiple_of(dst_row, chunks), chunks)], sem)


def _slot_rows_kernel(ri_ref, pstart_ref, o_ref):
    tb = ri_ref.shape[1]
    eio = lax.broadcasted_iota(jnp.int32, (N_EXPERTS, tb), 0)
    pstart = pstart_ref[...]
    slots = []
    for k in range(2):
        start = jnp.sum(jnp.where(eio == ri_ref[k:k + 1, :], pstart, 0), axis=0, keepdims=True)
        slots.append(start + ri_ref[2 + k:3 + k, :])
    zi = jnp.zeros_like(slots[0])
    o_ref[...] = jnp.concatenate(
        [s * XS_CHUNKS for s in slots] + [zi] * (SUBLANES - 2), axis=0)


def _slot_rows_call(ri, pstart):
    T = ri.shape[1]
    tb = TM_SLOT_ROWS
    return pl.pallas_call(
        _slot_rows_kernel,
        grid=(T // tb,),
        in_specs=[pl.BlockSpec((SUBLANES, tb), lambda i: (0, i)),
                  pl.BlockSpec((N_EXPERTS, 1), lambda i: (0, 0))],
        out_specs=pl.BlockSpec((SUBLANES, tb), lambda i: (0, i)),
        out_shape=jax.ShapeDtypeStruct((SUBLANES, T), jnp.int32),
        compiler_params=_params(("parallel",)),
        name="moe_slot_rows",
    )(ri, pstart[:, None])


def _dispatch_kernel(d0_ref, d1_ref, padrow_ref, x_ref, xs_hbm, zero_ref, sem, zsem):
    i = pl.program_id(0)
    tile_rows = TM_EXPERT * XS_CHUNKS

    @pl.when(i == 0)
    def _():
        zero_ref[...] = jnp.zeros_like(zero_ref)

        def zcopy(e):
            return pltpu.make_async_copy(
                zero_ref,
                xs_hbm.at[pl.ds(pl.multiple_of(padrow_ref[e], tile_rows), tile_rows)], zsem)

        def zstart(e, c):
            @pl.when(padrow_ref[e] >= 0)
            def _():
                zcopy(e).start()
            return c

        def zwait(e, c):
            @pl.when(padrow_ref[e] >= 0)
            def _():
                zcopy(e).wait()
            return c

        lax.fori_loop(0, N_EXPERTS, zstart, 0)
        lax.fori_loop(0, N_EXPERTS, zwait, 0)

        def tcopy(b):
            return pltpu.make_async_copy(
                zero_ref, xs_hbm.at[pl.ds(pl.multiple_of(b * tile_rows, tile_rows), tile_rows)], zsem)

        n_used = padrow_ref[N_EXPERTS]
        n_blk = xs_hbm.shape[0] // tile_rows
        lax.fori_loop(n_used, n_blk, lambda b, c: (tcopy(b).start(), c)[1], 0)
        lax.fori_loop(n_used, n_blk, lambda b, c: (tcopy(b).wait(), c)[1], 0)

    def issue(t, c):
        src = t * XS_CHUNKS
        _row_copy(x_ref, src, xs_hbm, d0_ref[t], sem, XS_CHUNKS).start(priority=0)
        _row_copy(x_ref, src, xs_hbm, d1_ref[t], sem, XS_CHUNKS).start(priority=1)
        return c

    lax.fori_loop(0, TM_DISPATCH, issue, 0, unroll=DMA_UNROLL)
    for _ in range(2):
        pltpu.make_async_copy(x_ref, xs_hbm.at[pl.ds(0, TM_DISPATCH * XS_CHUNKS)], sem).wait()


def _dispatch_call(d0, d1, padrow, xrows, capacity):
    T = d0.shape[0]
    return pl.pallas_call(
        _dispatch_kernel,
        grid=(T // TM_DISPATCH,),
        in_specs=[
            pl.BlockSpec((TM_DISPATCH,), lambda i: (i,), memory_space=pltpu.SMEM),
            pl.BlockSpec((TM_DISPATCH,), lambda i: (i,), memory_space=pltpu.SMEM),
            pl.BlockSpec(memory_space=pltpu.SMEM),
            pl.BlockSpec((TM_DISPATCH * XS_CHUNKS, LANES), lambda i: (i, 0)),
        ],
        out_specs=pl.BlockSpec(memory_space=pl.ANY),
        out_shape=jax.ShapeDtypeStruct((capacity * XS_CHUNKS, LANES), jnp.uint32),
        scratch_shapes=[pltpu.VMEM((TM_EXPERT * XS_CHUNKS, LANES), jnp.uint32),
                        pltpu.SemaphoreType.DMA, pltpu.SemaphoreType.DMA],
        compiler_params=_params(("arbitrary",)),
        name="moe_dispatch",
    )(d0, d1, padrow, xrows)


def _expert_kernel(tstart_ref, ntile_ref, cnt_ref, xs_hbm, wgu_ref, wd_ref, ys_hbm,
                   xbuf, ybuf, wgu_bf, wd_bf, sem_in, sem_out):
    nb = EXPERT_BUFS
    e = pl.program_id(0)
    tm = TM_EXPERT
    tile_rows = tm * XS_CHUNKS
    t0 = tstart_ref[e]
    nt = ntile_ref[e]
    n_used = tstart_ref[N_EXPERTS - 1] + ntile_ref[N_EXPERTS - 1]

    def tile_at(ref, g, rows):
        return ref.at[pl.ds(pl.multiple_of(g * rows, rows), rows)]

    def in_copy(g, slot):
        return pltpu.make_async_copy(tile_at(xs_hbm, g, tile_rows), xbuf.at[slot], sem_in.at[slot])

    def out_copy(g, slot):
        return pltpu.make_async_copy(ybuf.at[slot], tile_at(ys_hbm, g, tile_rows), sem_out.at[slot])

    @pl.when(e == 0)
    def _():
        for k in range(nb - 1):
            @pl.when(k < n_used)
            def _():
                in_copy(k, k).start(priority=1)

    @pl.when(nt > 0)
    def _():
        wgu_bf[...] = wgu_ref[0, 0].astype(jnp.bfloat16)
        wd_bf[...] = wd_ref[0, 0].astype(jnp.bfloat16)

    def tile_body(g, carry):
        slot = g % nb
        in_copy(g, slot).wait()

        @pl.when(g + (nb - 1) < n_used)
        def _():
            in_copy(g + (nb - 1), (g + (nb - 1)) % nb).start(priority=1)

        @pl.when(g >= nb)
        def _():
            out_copy(g - nb, slot).wait()

        def swiglu_rows(rows):
            x = _unpack_bf16_pairs(
                [xbuf[slot, pl.ds(c, rows, stride=XS_CHUNKS), :] for c in range(XS_CHUNKS)])
            gu = jnp.dot(x.astype(jnp.bfloat16), wgu_bf[...], preferred_element_type=jnp.float32)
            g_act = gu[:, :EXPERT_FF]
            u = gu[:, EXPERT_FF:]
            act = (g_act / (1.0 + jnp.exp(-g_act))) * u
            y = jnp.dot(act.astype(jnp.bfloat16), wd_bf[...], preferred_element_type=jnp.float32)
            for c, w in enumerate(_pack_bf16_pairs(y)):
                ybuf[slot, pl.ds(c, rows, stride=XS_CHUNKS), :] = w

        quarter = tm // EXPERT_TAIL_PARTS
        live_rows = cnt_ref[e] - (g - t0) * tm
        for q in range(1, EXPERT_TAIL_PARTS + 1):
            rows = q * quarter
            in_part = live_rows > rows - quarter
            if q < EXPERT_TAIL_PARTS:
                in_part = in_part & (live_rows <= rows)

            @pl.when(in_part)
            def _():
                swiglu_rows(rows)
                if rows < tm:
                    ybuf[slot, pl.ds(rows * XS_CHUNKS, (tm - rows) * XS_CHUNKS), :] = jnp.zeros(
                        ((tm - rows) * XS_CHUNKS, LANES), ybuf.dtype)

        out_copy(g, slot).start(priority=1)
        return carry

    lax.fori_loop(t0, t0 + nt, tile_body, 0)

    @pl.when(e == N_EXPERTS - 1)
    def _():
        for k in range(1, nb + 1):
            @pl.when(n_used >= k)
            def _():
                out_copy(n_used - k, (n_used - k) % nb).wait()

        n_blk = ys_hbm.shape[0] // tile_rows
        ybuf[0] = jnp.zeros(ybuf.shape[1:], ybuf.dtype)
        lax.fori_loop(n_used, n_blk, lambda b, c: (out_copy(b, 0).start(), c)[1], 0)
        lax.fori_loop(n_used, n_blk, lambda b, c: (out_copy(b, 0).wait(), c)[1], 0)


def _expert_call(layer, tstart, ntile, counts, xs, wgu, wd, capacity):
    tile_rows = TM_EXPERT * XS_CHUNKS

    def w_map(e, ts, nt, cn):
        return (layer, e, 0, 0)

    return pl.pallas_call(
        _expert_kernel,
        grid_spec=pltpu.PrefetchScalarGridSpec(
            num_scalar_prefetch=3,
            grid=(N_EXPERTS,),
            in_specs=[
                pl.BlockSpec(memory_space=pl.ANY),
                pl.BlockSpec((1, 1, D_MODEL, 2 * EXPERT_FF), w_map),
                pl.BlockSpec((1, 1, EXPERT_FF, D_MODEL), w_map),
            ],
            out_specs=pl.BlockSpec(memory_space=pl.ANY),
            scratch_shapes=[pltpu.VMEM((EXPERT_BUFS, tile_rows, LANES), jnp.uint32),
                            pltpu.VMEM((EXPERT_BUFS, tile_rows, LANES), jnp.uint32),
                            pltpu.VMEM((D_MODEL, 2 * EXPERT_FF), jnp.bfloat16),
                            pltpu.VMEM((EXPERT_FF, D_MODEL), jnp.bfloat16),
                            pltpu.SemaphoreType.DMA((EXPERT_BUFS,)),
                            pltpu.SemaphoreType.DMA((EXPERT_BUFS,))],
        ),
        out_shape=jax.ShapeDtypeStruct((capacity * XS_CHUNKS, LANES), jnp.uint32),
        compiler_params=_params(("arbitrary",)),
        name="moe_experts",
    )(tstart, ntile, counts, xs, wgu, wd)


def _gathered_residual(tm, d0_ref, d1_ref, d0n_ref, d1n_ref, ys_hbm, h_ref, gate_ref,
                       y0_ref, y1_ref, sem):
    i = pl.program_id(0)
    slot = i % 2

    def issue_tile(d0r, d1r, s):
        def issue(t, c):
            dst = t * XS_CHUNKS
            _row_copy(ys_hbm, d0r[t], y0_ref.at[s], dst, sem.at[s], XS_CHUNKS).start(priority=0)
            _row_copy(ys_hbm, d1r[t], y1_ref.at[s], dst, sem.at[s], XS_CHUNKS).start(priority=1)
            return c
        lax.fori_loop(0, tm, issue, 0, unroll=DMA_UNROLL)

    @pl.when(i == 0)
    def _():
        issue_tile(d0_ref, d1_ref, 0)

    @pl.when(i + 1 < pl.num_programs(0))
    def _():
        issue_tile(d0n_ref, d1n_ref, 1 - slot)

    for buf in (y0_ref, y1_ref):
        pltpu.make_async_copy(ys_hbm.at[pl.ds(0, tm * XS_CHUNKS)], buf.at[slot], sem.at[slot]).wait()

    g0 = gate_ref[:, 0:1]
    g1 = gate_ref[:, 1:2]
    y0 = _unpack_bf16_pairs([y0_ref[slot, pl.ds(c, tm, stride=XS_CHUNKS), :] for c in range(XS_CHUNKS)])
    y1 = _unpack_bf16_pairs([y1_ref[slot, pl.ds(c, tm, stride=XS_CHUNKS), :] for c in range(XS_CHUNKS)])
    return h_ref[...] + (y0 * g0 + y1 * g1)


def _gather_in_specs(tm, n_tiles):
    last = n_tiles - 1
    return [
        pl.BlockSpec((tm,), lambda i: (i,), memory_space=pltpu.SMEM),
        pl.BlockSpec((tm,), lambda i: (i,), memory_space=pltpu.SMEM),
        pl.BlockSpec((tm,), lambda i: (jnp.minimum(i + 1, last),), memory_space=pltpu.SMEM),
        pl.BlockSpec((tm,), lambda i: (jnp.minimum(i + 1, last),), memory_space=pltpu.SMEM),
        pl.BlockSpec(memory_space=pl.ANY),
        pl.BlockSpec((tm, D_MODEL), lambda i: (i, 0)),
        pl.BlockSpec((tm, 2), lambda i: (i, 0)),
    ]


def _gather_scratch(tm):
    return [pltpu.VMEM((2, tm * XS_CHUNKS, LANES), jnp.uint32),
            pltpu.VMEM((2, tm * XS_CHUNKS, LANES), jnp.uint32),
            pltpu.SemaphoreType.DMA((2,))]


def _combine_kernel(final_norm, d0_ref, d1_ref, d0n_ref, d1n_ref, ys_hbm, h_ref, gate_ref, gfin_ref,
                    o_ref, y0_ref, y1_ref, sem):
    h = _gathered_residual(TM_COMBINE, d0_ref, d1_ref, d0n_ref, d1n_ref, ys_hbm, h_ref, gate_ref,
                           y0_ref, y1_ref, sem)
    if final_norm:
        h = _rms(h, gfin_ref[...])
    o_ref[...] = h


def _combine_call(d0, d1, ys, h, gates, gfin, final_norm):
    T = h.shape[0]
    tm = TM_COMBINE
    return pl.pallas_call(
        functools.partial(_combine_kernel, final_norm),
        grid=(T // tm,),
        in_specs=_gather_in_specs(tm, T // tm) + [pl.BlockSpec((1, D_MODEL), lambda i: (0, 0))],
        out_specs=pl.BlockSpec((tm, D_MODEL), lambda i: (i, 0)),
        out_shape=jax.ShapeDtypeStruct((T, D_MODEL), jnp.float32),
        scratch_shapes=_gather_scratch(tm),
        compiler_params=_params(("arbitrary",)),
        name="moe_combine",
    )(d0, d1, d0, d1, ys, h, gates, gfin)


def _group_heads_last(w):
    lead = w.shape[:-1]
    w = w.reshape(lead + (N_KV_HEADS, GROUP, HEAD_DIM))
    w = jnp.swapaxes(w, -3, -2)
    return w.reshape(lead + (N_Q_HEADS * HEAD_DIM,))


def _rel_bucket_table():
    qi = np.arange(ATTN_BLOCK)[:, None]
    kj = np.arange(2 * ATTN_BLOCK)[None, :]
    dist = np.maximum(qi + ATTN_BLOCK - kj, 0)
    max_exact = N_REL_BUCKETS // 2
    d = np.maximum(dist, max_exact).astype(np.float32)
    large = max_exact + (np.log(d / np.float32(max_exact)) / np.float32(math.log(REL_MAX_DISTANCE / max_exact))
                         * np.float32(N_REL_BUCKETS - max_exact)).astype(np.int32)
    large = np.minimum(large, N_REL_BUCKETS - 1)
    return np.where(dist < max_exact, dist, large).astype(np.int32)


def _rel_bias_blocks(rel_bias, sinks):
    onehot = (jnp.asarray(_rel_bucket_table())[..., None] == jnp.arange(N_REL_BUCKETS)).astype(jnp.float32)
    bias = jnp.einsum("qkb,bh->hqk", onehot, rel_bias.astype(jnp.float32),
                      precision=lax.Precision.HIGHEST)
    sink_col = jnp.broadcast_to(sinks.astype(jnp.float32)[:, None, None], (N_Q_HEADS, ATTN_BLOCK, 1))
    return jnp.concatenate([sink_col, bias[:, :, 1:]], axis=2) * LOG2_E


def _router_weights(w_group, b_group, w_expert, b_expert):
    zw = jnp.zeros((D_MODEL, SUBLANES - N_GROUPS), jnp.float32)
    zw2 = jnp.zeros((D_MODEL, LANES - SUBLANES - N_EXPERTS), jnp.float32)
    wr = jnp.concatenate([w_group, zw, w_expert, zw2], axis=1)
    wh = wr.astype(jnp.bfloat16)
    wl = (wr - wh.astype(jnp.float32)).astype(jnp.bfloat16)
    wr = jnp.concatenate([wh, wl], axis=1)
    brt = jnp.concatenate([b_group, jnp.zeros((SUBLANES - N_GROUPS,), jnp.float32), b_expert])[:, None]
    return wr, brt


def _moe(layer, h, xrows, ri, rf, cnt, w_gate_up, w_down, gfin, final_norm):
    T = h.shape[0]
    tm = TM_EXPERT
    capacity = -(-(2 * T) // tm) * tm + N_EXPERTS * tm
    counts = cnt[:, 0].astype(jnp.int32)
    ntile = (counts + tm - 1) // tm
    tend = jnp.cumsum(ntile)
    pstart = (tend - ntile) * tm
    n_used = tend[-1:]
    padrow = jnp.concatenate([jnp.where(ntile > 0, (tend - 1) * (tm * XS_CHUNKS), -1), n_used]
                             ).astype(jnp.int32)

    dst = _slot_rows_call(ri, pstart)
    xs = _dispatch_call(dst[0], dst[1], padrow, xrows, capacity)
    ys = _expert_call(layer, (tend - ntile).astype(jnp.int32), ntile.astype(jnp.int32), counts, xs,
                      w_gate_up, w_down, capacity)
    gates = rf[:2].T
    return _combine_call(dst[0], dst[1], ys, h, gates, gfin, final_norm)


def kernel(x, norm_mix, norm_ffn, final_norm, rel_bias, attn_w_qkv, attn_b_qkv, attn_w_o, attn_b_o,
           attn_sinks, conv_w_in, conv_w, conv_w_out, moe_w_group, moe_b_group, moe_w_expert,
           moe_b_expert, moe_w_gate_up, moe_w_down):
    B, S, D = x.shape
    T = B * S
    x2 = x.reshape(T, D)
    nq = N_Q_HEADS * HEAD_DIM
    scale = HEAD_DIM ** -0.5 * LOG2_E

    wqkv = attn_w_qkv[0]
    bqkv = attn_b_qkv[0]
    w_all = jnp.concatenate([_group_heads_last(wqkv[:, :nq]) * scale, wqkv[:, nq:]],
                            axis=1).astype(jnp.bfloat16)
    b_all = jnp.concatenate([_group_heads_last(bqkv[:nq]) * scale, bqkv[nq:]])[None, :]
    q, kt, v = _qkv_call(x2, norm_mix[0][None, :], w_all, b_all)
    a = _attn_call(q, kt, v, _rel_bias_blocks(rel_bias, attn_sinks[0]), B, S)
    wr, brt = _router_weights(moe_w_group[0], moe_b_group[0], moe_w_expert[0], moe_b_expert[0])
    w_o = _group_heads_last(attn_w_o[0].T).T.astype(jnp.bfloat16)
    h, xrows, ri, rf, cnt = _attn_out_call(
        a, w_o, attn_b_o[0][None, :], x2, norm_ffn[0][None, :], wr, brt)
    h = _moe(0, h, xrows, ri, rf, cnt, moe_w_gate_up, moe_w_down, final_norm[None, :], False)

    wr, brt = _router_weights(moe_w_group[1], moe_b_group[1], moe_w_expert[1], moe_b_expert[1])
    h, xrows, ri, rf, cnt = _conv_mixer_call(
        h, norm_mix[1][None, :], conv_w_in[0].astype(jnp.bfloat16), conv_w[0],
        conv_w_out[0].astype(jnp.bfloat16), norm_ffn[1][None, :], wr, brt, S)
    out = _moe(1, h, xrows, ri, rf, cnt, moe_w_gate_up, moe_w_down, final_norm[None, :], True)
    return out.reshape(B, S, D)
```

```python
import functools
import math

import numpy as np
import jax
import jax.numpy as jnp
from jax import lax
from jax.experimental import pallas as pl
from jax.experimental.pallas import tpu as pltpu

D_MODEL = 1024
N_Q_HEADS = 16
N_KV_HEADS = 2
HEAD_DIM = 64
GROUP = N_Q_HEADS // N_KV_HEADS
WINDOW = 128
ATTN_BLOCK = 128
N_REL_BUCKETS = 32
REL_MAX_DISTANCE = 128
CONV_WIDTH = 3
N_GROUPS = 4
EXPERTS_PER_GROUP = 8
N_EXPERTS = N_GROUPS * EXPERTS_PER_GROUP
EXPERT_FF = 512
RMS_EPS = 1e-5
LOG2_E = math.log2(math.e)

LANES = 128
SUBLANES = 8
ROW_CHUNKS = D_MODEL // LANES
XS_CHUNKS = ROW_CHUNKS // 2
VMEM_LIMIT = 56 * 1024 * 1024

TM_DENSE = 1024
TM_QKV = 2048
TQ = 2048
TM_MIX1 = 1024
TM_EXPERT = 512
TM_DISPATCH = 4096
TM_COMBINE = 256
DMA_UNROLL = 8
TM_SLOT_ROWS = 4096
EXPERT_BUFS = 4
EXPERT_TAIL_PARTS = 4
GATHER_QUEUES = ((0, 1), (1, 0), (0, 1), (1, 0), (0, 1), (1, 0), (1, 1), (1, 1))
ROUTER_ROWS = 40


def _rms(x, g):
    return x * lax.rsqrt(jnp.mean(x * x, axis=-1, keepdims=True) + RMS_EPS) * g


def _pack_bf16_pairs(x):
    bits = lax.bitcast_convert_type(x.astype(jnp.bfloat16).astype(jnp.float32), jnp.uint32)
    return [(bits[:, (2 * c) * LANES:(2 * c + 1) * LANES] >> 16)
            | bits[:, (2 * c + 1) * LANES:(2 * c + 2) * LANES] for c in range(XS_CHUNKS)]


def _unpack_bf16_pairs(words):
    cols = []
    for w in words:
        cols.append(lax.bitcast_convert_type(w << 16, jnp.float32))
        cols.append(lax.bitcast_convert_type(w & jnp.uint32(0xFFFF0000), jnp.float32))
    return jnp.concatenate(cols, axis=-1)


def _params(sem):
    return pltpu.CompilerParams(dimension_semantics=sem, vmem_limit_bytes=VMEM_LIMIT)


def _qkv_kernel(x_ref, g_ref, w_ref, b_ref, q_ref, kt_ref, v_ref):
    xn = _rms(x_ref[...], g_ref[...]).astype(jnp.bfloat16)
    out = jnp.dot(xn, w_ref[...], preferred_element_type=jnp.float32) + b_ref[...]
    nq = N_Q_HEADS * HEAD_DIM
    nk = N_KV_HEADS * HEAD_DIM
    q_ref[...] = out[:, :nq].astype(jnp.bfloat16)
    kt_ref[...] = out[:, nq:nq + nk].T.astype(jnp.bfloat16)
    v_ref[...] = out[:, nq + nk:].astype(jnp.bfloat16)


def _qkv_call(x2, g, w, b):
    T = x2.shape[0]
    nq = N_Q_HEADS * HEAD_DIM
    nk = N_KV_HEADS * HEAD_DIM
    tm = TM_QKV
    return pl.pallas_call(
        _qkv_kernel,
        grid=(T // tm,),
        in_specs=[
            pl.BlockSpec((tm, D_MODEL), lambda i: (i, 0)),
            pl.BlockSpec((1, D_MODEL), lambda i: (0, 0)),
            pl.BlockSpec((D_MODEL, nq + 2 * nk), lambda i: (0, 0)),
            pl.BlockSpec((1, nq + 2 * nk), lambda i: (0, 0)),
        ],
        out_specs=[
            pl.BlockSpec((tm, nq), lambda i: (i, 0)),
            pl.BlockSpec((nk, tm), lambda i: (0, i)),
            pl.BlockSpec((tm, nk), lambda i: (i, 0)),
        ],
        out_shape=[
            jax.ShapeDtypeStruct((T, nq), jnp.bfloat16),
            jax.ShapeDtypeStruct((nk, T), jnp.bfloat16),
            jax.ShapeDtypeStruct((T, nk), jnp.bfloat16),
        ],
        compiler_params=_params(("parallel",)),
        name="qkv_proj",
    )(x2, g, w, b)


def _attn_kernel(q_ref, kt_ref, ktp_ref, v_ref, vp_ref, bias_ref, o_ref):
    first_tile = pl.program_id(1) == 0
    nblk = TQ // ATTN_BLOCK
    qi_io = lax.broadcasted_iota(jnp.int32, (ATTN_BLOCK, 2 * ATTN_BLOCK), 0)
    kj_io = lax.broadcasted_iota(jnp.int32, (ATTN_BLOCK, 2 * ATTN_BLOCK), 1)
    dist = qi_io + ATTN_BLOCK - kj_io
    band = (dist >= 0) & (dist < WINDOW)
    sink_col = kj_io == 0
    kt_keep = lax.broadcasted_iota(jnp.int32, (N_KV_HEADS * HEAD_DIM, 2 * ATTN_BLOCK), 1) > 0
    v_keep = lax.broadcasted_iota(jnp.int32, (2 * ATTN_BLOCK, N_KV_HEADS * HEAD_DIM), 0) > 0
    low_half = lax.broadcasted_iota(jnp.int32, (ATTN_BLOCK, LANES), 1) < HEAD_DIM
    for qi in range(nblk):
        r0 = qi * ATTN_BLOCK
        if qi == 0:
            kt_blk = jnp.concatenate([ktp_ref[...], kt_ref[:, 0:ATTN_BLOCK]], axis=1)
            v_blk = jnp.concatenate([vp_ref[...], v_ref[0:ATTN_BLOCK, :]], axis=0)
            mask = (band & (jnp.logical_not(first_tile) | (kj_io >= ATTN_BLOCK))) | sink_col
        else:
            kt_blk = kt_ref[:, r0 - ATTN_BLOCK:r0 + ATTN_BLOCK]
            v_blk = v_ref[r0 - ATTN_BLOCK:r0 + ATTN_BLOCK, :]
            mask = band | sink_col
        kt_blk = jnp.where(kt_keep, kt_blk, jnp.zeros_like(kt_blk))
        v_blk = jnp.where(v_keep, v_blk, jnp.zeros_like(v_blk))
        for m in range(GROUP):
            qg = q_ref[r0:r0 + ATTN_BLOCK, m * LANES:(m + 1) * LANES]
            halves = []
            for half in range(N_KV_HEADS):
                h = m + GROUP * half
                keep = low_half if half == 0 else jnp.logical_not(low_half)
                qpad = jnp.where(keep, qg, jnp.zeros_like(qg))
                s = jnp.dot(qpad, kt_blk, preferred_element_type=jnp.float32)
                s = jnp.where(mask, s + bias_ref[h], -1e30)
                mx = jnp.max(s, axis=-1, keepdims=True)
                p = jnp.exp2(s - mx)
                den = jnp.sum(p, axis=-1, keepdims=True)
                pv = jnp.dot(p.astype(jnp.bfloat16), v_blk, preferred_element_type=jnp.float32)
                halves.append(pv * (1.0 / den))
            og = jnp.where(low_half, halves[0], halves[1])
            o_ref[r0:r0 + ATTN_BLOCK, m * LANES:(m + 1) * LANES] = og.astype(jnp.bfloat16)


def _attn_call(q, kt, v, bias, batch, seq):
    T = q.shape[0]
    nq = N_Q_HEADS * HEAD_DIM
    nk = N_KV_HEADS * HEAD_DIM
    tiles = seq // TQ
    per = TQ // ATTN_BLOCK

    def cur(b, j):
        return b * tiles + j

    def prev(b, j):
        return jnp.maximum((b * tiles + j) * per - 1, b * tiles * per)

    return pl.pallas_call(
        _attn_kernel,
        grid=(batch, tiles),
        in_specs=[
            pl.BlockSpec((TQ, nq), lambda b, j: (cur(b, j), 0)),
            pl.BlockSpec((nk, TQ), lambda b, j: (0, cur(b, j))),
            pl.BlockSpec((nk, ATTN_BLOCK), lambda b, j: (0, prev(b, j))),
            pl.BlockSpec((TQ, nk), lambda b, j: (cur(b, j), 0)),
            pl.BlockSpec((ATTN_BLOCK, nk), lambda b, j: (prev(b, j), 0)),
            pl.BlockSpec((N_Q_HEADS, ATTN_BLOCK, 2 * ATTN_BLOCK), lambda b, j: (0, 0, 0)),
        ],
        out_specs=pl.BlockSpec((TQ, nq), lambda b, j: (cur(b, j), 0)),
        out_shape=jax.ShapeDtypeStruct((T, nq), jnp.bfloat16),
        compiler_params=_params(("parallel", "parallel")),
        name="swa_attention",
    )(q, kt, kt, v, v, bias)


def _router_epilogue(h, gffn_ref, wr_ref, brt_ref, carry_ref,
                     h_out_ref, xrows_ref, ri_ref, rf_ref, cnt_ref):
    tm = h.shape[0]
    h_out_ref[...] = h
    xn = _rms(h, gffn_ref[...])
    xh = xn.astype(jnp.bfloat16)
    for c, w in enumerate(_pack_bf16_pairs(xn)):
        xrows_ref[pl.ds(c, tm, stride=XS_CHUNKS), :] = w

    xl = (xn - xh.astype(jnp.float32)).astype(jnp.bfloat16)
    wcat = wr_ref[...]
    prod = jnp.dot(xh, wcat, preferred_element_type=jnp.float32)
    logits = (prod[:, :LANES] + prod[:, LANES:]) + jnp.dot(
        xl, wcat[:, :LANES], preferred_element_type=jnp.float32)
    lt = logits.T[:ROUTER_ROWS, :] + brt_ref[...]

    gl = [lt[g:g + 1, :] for g in range(N_GROUPS)]
    gmax = functools.reduce(jnp.maximum, gl)
    gexp = [jnp.exp(x - gmax) for x in gl]
    gsum = functools.reduce(lambda a, b: a + b, gexp)
    gprob = [x / gsum for x in gexp]
    g_prob = functools.reduce(jnp.maximum, gprob)
    g_idx = jnp.full(g_prob.shape, N_GROUPS - 1, jnp.int32)
    for g in range(N_GROUPS - 2, -1, -1):
        g_idx = jnp.where(gprob[g] == g_prob, g, g_idx)

    el = []
    for j in range(EXPERTS_PER_GROUP):
        x = lt[SUBLANES + j:SUBLANES + j + 1, :]
        for g in range(1, N_GROUPS):
            r = SUBLANES + g * EXPERTS_PER_GROUP + j
            x = jnp.where(g_idx == g, lt[r:r + 1, :], x)
        el.append(x)
    emax = functools.reduce(jnp.maximum, el)
    eexp = [jnp.exp(x - emax) for x in el]
    esum = functools.reduce(lambda a, b: a + b, eexp)
    eprob = [x / esum for x in eexp]
    p1 = functools.reduce(jnp.maximum, eprob)
    i1 = jnp.full(p1.shape, EXPERTS_PER_GROUP - 1, jnp.int32)
    for j in range(EXPERTS_PER_GROUP - 2, -1, -1):
        i1 = jnp.where(eprob[j] == p1, j, i1)
    rest = [jnp.where(i1 == j, -1.0, eprob[j]) for j in range(EXPERTS_PER_GROUP)]
    p2 = functools.reduce(jnp.maximum, rest)
    i2 = jnp.full(p2.shape, EXPERTS_PER_GROUP - 1, jnp.int32)
    for j in range(EXPERTS_PER_GROUP - 2, -1, -1):
        i2 = jnp.where(rest[j] == p2, j, i2)
    psum = p1 + p2
    gate0 = g_prob * (p1 / psum)
    gate1 = g_prob * (p2 / psum)
    e0 = g_idx * EXPERTS_PER_GROUP + i1
    e1 = g_idx * EXPERTS_PER_GROUP + i2

    eio = lax.broadcasted_iota(jnp.int32, (N_EXPERTS, tm), 0)
    oh0 = (eio == e0).astype(jnp.float32)
    oh1 = (eio == e1).astype(jnp.float32)
    both = oh0 + oh1
    tr = lax.broadcasted_iota(jnp.int32, (tm, tm), 0)
    tc = lax.broadcasted_iota(jnp.int32, (tm, tm), 1)
    upper = (tr < tc).astype(jnp.bfloat16)
    before = jnp.dot(both.astype(jnp.bfloat16), upper, preferred_element_type=jnp.float32)
    before = before + carry_ref[...]
    rank0 = jnp.sum(oh0 * before, axis=0, keepdims=True)
    rank1 = jnp.sum(oh1 * (before + oh0), axis=0, keepdims=True)
    carry_ref[...] = carry_ref[...] + jnp.sum(both, axis=1, keepdims=True)
    cnt_ref[...] = jnp.broadcast_to(carry_ref[...], cnt_ref.shape)

    zi = jnp.zeros_like(e0)
    ri_ref[...] = jnp.concatenate(
        [e0, e1, rank0.astype(jnp.int32), rank1.astype(jnp.int32), zi, zi, zi, zi], axis=0)
    zf = jnp.zeros_like(gate0)
    rf_ref[...] = jnp.concatenate([gate0, gate1, zf, zf, zf, zf, zf, zf], axis=0)


def _epilogue_out_specs(tm):
    return [
        pl.BlockSpec((tm, D_MODEL), lambda i: (i, 0)),
        pl.BlockSpec((tm * XS_CHUNKS, LANES), lambda i: (i, 0)),
        pl.BlockSpec((SUBLANES, tm), lambda i: (0, i)),
        pl.BlockSpec((SUBLANES, tm), lambda i: (0, i)),
        pl.BlockSpec((N_EXPERTS, LANES), lambda i: (0, 0)),
    ]


def _epilogue_out_shapes(T):
    return [
        jax.ShapeDtypeStruct((T, D_MODEL), jnp.float32),
        jax.ShapeDtypeStruct((T * XS_CHUNKS, LANES), jnp.uint32),
        jax.ShapeDtypeStruct((SUBLANES, T), jnp.int32),
        jax.ShapeDtypeStruct((SUBLANES, T), jnp.float32),
        jax.ShapeDtypeStruct((N_EXPERTS, LANES), jnp.float32),
    ]


def _epilogue_in_specs():
    return [
        pl.BlockSpec((1, D_MODEL), lambda i: (0, 0)),
        pl.BlockSpec((D_MODEL, 2 * LANES), lambda i: (0, 0)),
        pl.BlockSpec((ROUTER_ROWS, 1), lambda i: (0, 0)),
    ]


def _attn_out_kernel(a_ref, w_ref, b_ref, res_ref, gffn_ref, wr_ref, brt_ref,
                     h_out_ref, xrows_ref, ri_ref, rf_ref, cnt_ref, carry_ref):
    @pl.when(pl.program_id(0) == 0)
    def _():
        carry_ref[...] = jnp.zeros_like(carry_ref)

    mix = jnp.dot(a_ref[...], w_ref[...], preferred_element_type=jnp.float32) + b_ref[...]
    h = res_ref[...] + mix
    _router_epilogue(h, gffn_ref, wr_ref, brt_ref, carry_ref,
                     h_out_ref, xrows_ref, ri_ref, rf_ref, cnt_ref)


def _attn_out_call(a, w, b, res, gffn, wr, brt):
    T = a.shape[0]
    tm = TM_DENSE
    return pl.pallas_call(
        _attn_out_kernel,
        grid=(T // tm,),
        in_specs=[
            pl.BlockSpec((tm, D_MODEL), lambda i: (i, 0)),
            pl.BlockSpec((D_MODEL, D_MODEL), lambda i: (0, 0)),
            pl.BlockSpec((1, D_MODEL), lambda i: (0, 0)),
            pl.BlockSpec((tm, D_MODEL), lambda i: (i, 0)),
        ] + _epilogue_in_specs(),
        out_specs=_epilogue_out_specs(tm),
        out_shape=_epilogue_out_shapes(T),
        scratch_shapes=[pltpu.VMEM((N_EXPERTS, 1), jnp.float32)],
        compiler_params=_params(("arbitrary",)),
        name="attn_out_router",
    )(a, w, b, res, gffn, wr, brt)


def _conv_mixer_kernel(tiles_per_seq, x_ref, g_ref, win_ref, cw_ref, wout_ref,
                       gffn_ref, wr_ref, brt_ref,
                       h_out_ref, xrows_ref, ri_ref, rf_ref, cnt_ref, carry_ref, tail_ref):
    i = pl.program_id(0)
    tm = x_ref.shape[0]

    @pl.when(i == 0)
    def _():
        carry_ref[...] = jnp.zeros_like(carry_ref)

    @pl.when(i % tiles_per_seq == 0)
    def _():
        tail_ref[...] = jnp.zeros_like(tail_ref)

    x = x_ref[...]
    xn = _rms(x, g_ref[...]).astype(jnp.bfloat16)
    bcu = jnp.dot(xn, win_ref[...], preferred_element_type=jnp.float32)
    b_gate = bcu[:, :D_MODEL]
    z = bcu[:, D_MODEL:2 * D_MODEL] * bcu[:, 2 * D_MODEL:]
    row = lax.broadcasted_iota(jnp.int32, (tm, D_MODEL), 0)
    tail = tail_ref[...]
    t1 = tail[SUBLANES - 1:SUBLANES, :]
    t2 = tail[SUBLANES - 2:SUBLANES - 1, :]
    z1 = jnp.where(row == 0, t1, pltpu.roll(z, 1, axis=0))
    z2 = jnp.where(row == 0, t2, jnp.where(row == 1, t1, pltpu.roll(z, 2, axis=0)))
    tail_ref[...] = z[tm - SUBLANES:, :]
    conv = z2 * cw_ref[0:1, :]
    conv = conv + z1 * cw_ref[1:2, :]
    conv = conv + z * cw_ref[2:3, :]
    gated = (b_gate * conv).astype(jnp.bfloat16)
    h = x + jnp.dot(gated, wout_ref[...], preferred_element_type=jnp.float32)
    _router_epilogue(h, gffn_ref, wr_ref, brt_ref, carry_ref,
                     h_out_ref, xrows_ref, ri_ref, rf_ref, cnt_ref)


def _conv_mixer_call(x2, g, win, cw, wout, gffn, wr, brt, seq):
    T = x2.shape[0]
    tm = TM_MIX1
    return pl.pallas_call(
        functools.partial(_conv_mixer_kernel, seq // tm),
        grid=(T // tm,),
        in_specs=[
            pl.BlockSpec((tm, D_MODEL), lambda i: (i, 0)),
            pl.BlockSpec((1, D_MODEL), lambda i: (0, 0)),
            pl.BlockSpec((D_MODEL, 3 * D_MODEL), lambda i: (0, 0)),
            pl.BlockSpec((CONV_WIDTH, D_MODEL), lambda i: (0, 0)),
            pl.BlockSpec((D_MODEL, D_MODEL), lambda i: (0, 0)),
        ] + _epilogue_in_specs(),
        out_specs=_epilogue_out_specs(tm),
        out_shape=_epilogue_out_shapes(T),
        scratch_shapes=[pltpu.VMEM((N_EXPERTS, 1), jnp.float32),
                        pltpu.VMEM((SUBLANES, D_MODEL), jnp.float32)],
        compiler_params=_params(("arbitrary",)),
        name="conv_mixer_router",
    )(x2, g, win, cw, wout, gffn, wr, brt)


def _row_copy(src_hbm, src_row, dst_hbm, dst_row, sem, chunks):
    return pltpu.make_async_copy(
        src_hbm.at[pl.ds(pl.multiple_of(src_row, chunks), chunks)],
        dst_hbm.at[pl.ds(pl.multiple_of(dst_row, chunks), chunks)], sem)


def _slot_rows_kernel(ri_ref, pstart_ref, o_ref):
    tb = ri_ref.shape[1]
    eio = lax.broadcasted_iota(jnp.int32, (N_EXPERTS, tb), 0)
    pstart = pstart_ref[...]
    slots = []
    for k in range(2):
        start = jnp.sum(jnp.where(eio == ri_ref[k:k + 1, :], pstart, 0), axis=0, keepdims=True)
        slots.append(start + ri_ref[2 + k:3 + k, :])
    zi = jnp.zeros_like(slots[0])
    o_ref[...] = jnp.concatenate(
        [s * XS_CHUNKS for s in slots] + [zi] * (SUBLANES - 2), axis=0)


def _slot_rows_call(ri, pstart):
    T = ri.shape[1]
    tb = TM_SLOT_ROWS
    return pl.pallas_call(
        _slot_rows_kernel,
        grid=(T // tb,),
        in_specs=[pl.BlockSpec((SUBLANES, tb), lambda i: (0, i)),
                  pl.BlockSpec((N_EXPERTS, 1), lambda i: (0, 0))],
        out_specs=pl.BlockSpec((SUBLANES, tb), lambda i: (0, i)),
        out_shape=jax.ShapeDtypeStruct((SUBLANES, T), jnp.int32),
        compiler_params=_params(("parallel",)),
        name="moe_slot_rows",
    )(ri, pstart[:, None])


def _dispatch_kernel(d0_ref, d1_ref, padrow_ref, x_ref, xs_hbm, zero_ref, sem, zsem):
    i = pl.program_id(0)
    tile_rows = TM_EXPERT * XS_CHUNKS

    @pl.when(i == 0)
    def _():
        zero_ref[...] = jnp.zeros_like(zero_ref)

        def zcopy(e):
            return pltpu.make_async_copy(
                zero_ref,
                xs_hbm.at[pl.ds(pl.multiple_of(padrow_ref[e], tile_rows), tile_rows)], zsem)

        def zstart(e, c):
            @pl.when(padrow_ref[e] >= 0)
            def _():
                zcopy(e).start()
            return c

        def zwait(e, c):
            @pl.when(padrow_ref[e] >= 0)
            def _():
                zcopy(e).wait()
            return c

        lax.fori_loop(0, N_EXPERTS, zstart, 0)
        lax.fori_loop(0, N_EXPERTS, zwait, 0)

        def tcopy(b):
            return pltpu.make_async_copy(
                zero_ref, xs_hbm.at[pl.ds(pl.multiple_of(b * tile_rows, tile_rows), tile_rows)], zsem)

        n_used = padrow_ref[N_EXPERTS]
        n_blk = xs_hbm.shape[0] // tile_rows
        lax.fori_loop(n_used, n_blk, lambda b, c: (tcopy(b).start(), c)[1], 0)
        lax.fori_loop(n_used, n_blk, lambda b, c: (tcopy(b).wait(), c)[1], 0)

    def issue(t, c):
        src = t * XS_CHUNKS
        _row_copy(x_ref, src, xs_hbm, d0_ref[t], sem, XS_CHUNKS).start(priority=0)
        _row_copy(x_ref, src, xs_hbm, d1_ref[t], sem, XS_CHUNKS).start(priority=1)
        return c

    lax.fori_loop(0, TM_DISPATCH, issue, 0, unroll=DMA_UNROLL)
    for _ in range(2):
        pltpu.make_async_copy(x_ref, xs_hbm.at[pl.ds(0, TM_DISPATCH * XS_CHUNKS)], sem).wait()


def _dispatch_call(d0, d1, padrow, xrows, capacity):
    T = d0.shape[0]
    return pl.pallas_call(
        _dispatch_kernel,
        grid=(T // TM_DISPATCH,),
        in_specs=[
            pl.BlockSpec((TM_DISPATCH,), lambda i: (i,), memory_space=pltpu.SMEM),
            pl.BlockSpec((TM_DISPATCH,), lambda i: (i,), memory_space=pltpu.SMEM),
            pl.BlockSpec(memory_space=pltpu.SMEM),
            pl.BlockSpec((TM_DISPATCH * XS_CHUNKS, LANES), lambda i: (i, 0)),
        ],
        out_specs=pl.BlockSpec(memory_space=pl.ANY),
        out_shape=jax.ShapeDtypeStruct((capacity * XS_CHUNKS, LANES), jnp.uint32),
        scratch_shapes=[pltpu.VMEM((TM_EXPERT * XS_CHUNKS, LANES), jnp.uint32),
                        pltpu.SemaphoreType.DMA, pltpu.SemaphoreType.DMA],
        compiler_params=_params(("arbitrary",)),
        name="moe_dispatch",
    )(d0, d1, padrow, xrows)


def _expert_kernel(tstart_ref, ntile_ref, cnt_ref, xs_hbm, wgu_ref, wd_ref, ys_hbm,
                   xbuf, ybuf, wgu_bf, wd_bf, sem_in, sem_out):
    nb = EXPERT_BUFS
    e = pl.program_id(0)
    tm = TM_EXPERT
    tile_rows = tm * XS_CHUNKS
    t0 = tstart_ref[e]
    nt = ntile_ref[e]
    n_used = tstart_ref[N_EXPERTS - 1] + ntile_ref[N_EXPERTS - 1]

    def tile_at(ref, g, rows):
        return ref.at[pl.ds(pl.multiple_of(g * rows, rows), rows)]

    def in_copy(g, slot):
        return pltpu.make_async_copy(tile_at(xs_hbm, g, tile_rows), xbuf.at[slot], sem_in.at[slot])

    def out_copy(g, slot):
        return pltpu.make_async_copy(ybuf.at[slot], tile_at(ys_hbm, g, tile_rows), sem_out.at[slot])

    @pl.when(e == 0)
    def _():
        for k in range(nb - 1):
            @pl.when(k < n_used)
            def _():
                in_copy(k, k).start(priority=1)

    @pl.when(nt > 0)
    def _():
        wgu_bf[...] = wgu_ref[0, 0].astype(jnp.bfloat16)
        wd_bf[...] = wd_ref[0, 0].astype(jnp.bfloat16)

    def tile_body(g, carry):
        slot = g % nb
        in_copy(g, slot).wait()

        @pl.when(g + (nb - 1) < n_used)
        def _():
            in_copy(g + (nb - 1), (g + (nb - 1)) % nb).start(priority=1)

        @pl.when(g >= nb)
        def _():
            out_copy(g - nb, slot).wait()

        def swiglu_rows(rows):
            x = _unpack_bf16_pairs(
                [xbuf[slot, pl.ds(c, rows, stride=XS_CHUNKS), :] for c in range(XS_CHUNKS)])
            gu = jnp.dot(x.astype(jnp.bfloat16), wgu_bf[...], preferred_element_type=jnp.float32)
            g_act = gu[:, :EXPERT_FF]
            u = gu[:, EXPERT_FF:]
            act = (g_act / (1.0 + jnp.exp(-g_act))) * u
            y = jnp.dot(act.astype(jnp.bfloat16), wd_bf[...], preferred_element_type=jnp.float32)
            for c, w in enumerate(_pack_bf16_pairs(y)):
                ybuf[slot, pl.ds(c, rows, stride=XS_CHUNKS), :] = w

        quarter = tm // EXPERT_TAIL_PARTS
        live_rows = cnt_ref[e] - (g - t0) * tm
        for q in range(1, EXPERT_TAIL_PARTS + 1):
            rows = q * quarter
            in_part = live_rows > rows - quarter
            if q < EXPERT_TAIL_PARTS:
                in_part = in_part & (live_rows <= rows)

            @pl.when(in_part)
            def _():
                swiglu_rows(rows)
                if rows < tm:
                    ybuf[slot, pl.ds(rows * XS_CHUNKS, (tm - rows) * XS_CHUNKS), :] = jnp.zeros(
                        ((tm - rows) * XS_CHUNKS, LANES), ybuf.dtype)

        out_copy(g, slot).start(priority=1)
        return carry

    lax.fori_loop(t0, t0 + nt, tile_body, 0)

    @pl.when(e == N_EXPERTS - 1)
    def _():
        for k in range(1, nb + 1):
            @pl.when(n_used >= k)
            def _():
                out_copy(n_used - k, (n_used - k) % nb).wait()

        n_blk = ys_hbm.shape[0] // tile_rows
        ybuf[0] = jnp.zeros(ybuf.shape[1:], ybuf.dtype)
        lax.fori_loop(n_used, n_blk, lambda b, c: (out_copy(b, 0).start(), c)[1], 0)
        lax.fori_loop(n_used, n_blk, lambda b, c: (out_copy(b, 0).wait(), c)[1], 0)


def _expert_call(layer, tstart, ntile, counts, xs, wgu, wd, capacity):
    tile_rows = TM_EXPERT * XS_CHUNKS

    def w_map(e, ts, nt, cn):
        return (layer, e, 0, 0)

    return pl.pallas_call(
        _expert_kernel,
        grid_spec=pltpu.PrefetchScalarGridSpec(
            num_scalar_prefetch=3,
            grid=(N_EXPERTS,),
            in_specs=[
                pl.BlockSpec(memory_space=pl.ANY),
                pl.BlockSpec((1, 1, D_MODEL, 2 * EXPERT_FF), w_map),
                pl.BlockSpec((1, 1, EXPERT_FF, D_MODEL), w_map),
            ],
            out_specs=pl.BlockSpec(memory_space=pl.ANY),
            scratch_shapes=[pltpu.VMEM((EXPERT_BUFS, tile_rows, LANES), jnp.uint32),
                            pltpu.VMEM((EXPERT_BUFS, tile_rows, LANES), jnp.uint32),
                            pltpu.VMEM((D_MODEL, 2 * EXPERT_FF), jnp.bfloat16),
                            pltpu.VMEM((EXPERT_FF, D_MODEL), jnp.bfloat16),
                            pltpu.SemaphoreType.DMA((EXPERT_BUFS,)),
                            pltpu.SemaphoreType.DMA((EXPERT_BUFS,))],
        ),
        out_shape=jax.ShapeDtypeStruct((capacity * XS_CHUNKS, LANES), jnp.uint32),
        compiler_params=_params(("arbitrary",)),
        name="moe_experts",
    )(tstart, ntile, counts, xs, wgu, wd)


def _gathered_residual(tm, d0_ref, d1_ref, d0n_ref, d1n_ref, ys_hbm, h_ref, gate_ref,
                       y0_ref, y1_ref, sem):
    i = pl.program_id(0)
    slot = i % 2

    def issue_tile(d0r, d1r, s):
        def issue(tg, c):
            for j, (q0, q1) in enumerate(GATHER_QUEUES):
                t = tg * len(GATHER_QUEUES) + j
                dst = t * XS_CHUNKS
                _row_copy(ys_hbm, d0r[t], y0_ref.at[s], dst, sem.at[s], XS_CHUNKS).start(priority=q0)
                _row_copy(ys_hbm, d1r[t], y1_ref.at[s], dst, sem.at[s], XS_CHUNKS).start(priority=q1)
            return c
        lax.fori_loop(0, tm // len(GATHER_QUEUES), issue, 0)

    @pl.when(i == 0)
    def _():
        issue_tile(d0_ref, d1_ref, 0)

    @pl.when(i + 1 < pl.num_programs(0))
    def _():
        issue_tile(d0n_ref, d1n_ref, 1 - slot)

    for buf in (y0_ref, y1_ref):
        pltpu.make_async_copy(ys_hbm.at[pl.ds(0, tm * XS_CHUNKS)], buf.at[slot], sem.at[slot]).wait()

    g0 = gate_ref[:, 0:1]
    g1 = gate_ref[:, 1:2]
    y0 = _unpack_bf16_pairs([y0_ref[slot, pl.ds(c, tm, stride=XS_CHUNKS), :] for c in range(XS_CHUNKS)])
    y1 = _unpack_bf16_pairs([y1_ref[slot, pl.ds(c, tm, stride=XS_CHUNKS), :] for c in range(XS_CHUNKS)])
    return h_ref[...] + (y0 * g0 + y1 * g1)


def _gather_in_specs(tm, n_tiles):
    last = n_tiles - 1
    return [
        pl.BlockSpec((tm,), lambda i: (i,), memory_space=pltpu.SMEM),
        pl.BlockSpec((tm,), lambda i: (i,), memory_space=pltpu.SMEM),
        pl.BlockSpec((tm,), lambda i: (jnp.minimum(i + 1, last),), memory_space=pltpu.SMEM),
        pl.BlockSpec((tm,), lambda i: (jnp.minimum(i + 1, last),), memory_space=pltpu.SMEM),
        pl.BlockSpec(memory_space=pl.ANY),
        pl.BlockSpec((tm, D_MODEL), lambda i: (i, 0)),
        pl.BlockSpec((tm, 2), lambda i: (i, 0)),
    ]


def _gather_scratch(tm):
    return [pltpu.VMEM((2, tm * XS_CHUNKS, LANES), jnp.uint32),
            pltpu.VMEM((2, tm * XS_CHUNKS, LANES), jnp.uint32),
            pltpu.SemaphoreType.DMA((2,))]


def _combine_kernel(final_norm, d0_ref, d1_ref, d0n_ref, d1n_ref, ys_hbm, h_ref, gate_ref, gfin_ref,
                    o_ref, y0_ref, y1_ref, sem):
    h = _gathered_residual(TM_COMBINE, d0_ref, d1_ref, d0n_ref, d1n_ref, ys_hbm, h_ref, gate_ref,
                           y0_ref, y1_ref, sem)
    if final_norm:
        h = _rms(h, gfin_ref[...])
    o_ref[...] = h


def _combine_call(d0, d1, ys, h, gates, gfin, final_norm):
    T = h.shape[0]
    tm = TM_COMBINE
    return pl.pallas_call(
        functools.partial(_combine_kernel, final_norm),
        grid=(T // tm,),
        in_specs=_gather_in_specs(tm, T // tm) + [pl.BlockSpec((1, D_MODEL), lambda i: (0, 0))],
        out_specs=pl.BlockSpec((tm, D_MODEL), lambda i: (i, 0)),
        out_shape=jax.ShapeDtypeStruct((T, D_MODEL), jnp.float32),
        scratch_shapes=_gather_scratch(tm),
        compiler_params=_params(("arbitrary",)),
        name="moe_combine",
    )(d0, d1, d0, d1, ys, h, gates, gfin)


def _group_heads_last(w):
    lead = w.shape[:-1]
    w = w.reshape(lead + (N_KV_HEADS, GROUP, HEAD_DIM))
    w = jnp.swapaxes(w, -3, -2)
    return w.reshape(lead + (N_Q_HEADS * HEAD_DIM,))


def _rel_bucket_table():
    qi = np.arange(ATTN_BLOCK)[:, None]
    kj = np.arange(2 * ATTN_BLOCK)[None, :]
    dist = np.maximum(qi + ATTN_BLOCK - kj, 0)
    max_exact = N_REL_BUCKETS // 2
    d = np.maximum(dist, max_exact).astype(np.float32)
    large = max_exact + (np.log(d / np.float32(max_exact)) / np.float32(math.log(REL_MAX_DISTANCE / max_exact))
                         * np.float32(N_REL_BUCKETS - max_exact)).astype(np.int32)
    large = np.minimum(large, N_REL_BUCKETS - 1)
    return np.where(dist < max_exact, dist, large).astype(np.int32)


def _rel_bias_blocks(rel_bias, sinks):
    onehot = (jnp.asarray(_rel_bucket_table())[..., None] == jnp.arange(N_REL_BUCKETS)).astype(jnp.float32)
    bias = jnp.einsum("qkb,bh->hqk", onehot, rel_bias.astype(jnp.float32),
                      precision=lax.Precision.HIGHEST)
    sink_col = jnp.broadcast_to(sinks.astype(jnp.float32)[:, None, None], (N_Q_HEADS, ATTN_BLOCK, 1))
    return jnp.concatenate([sink_col, bias[:, :, 1:]], axis=2) * LOG2_E


def _router_weights(w_group, b_group, w_expert, b_expert):
    zw = jnp.zeros((D_MODEL, SUBLANES - N_GROUPS), jnp.float32)
    zw2 = jnp.zeros((D_MODEL, LANES - SUBLANES - N_EXPERTS), jnp.float32)
    wr = jnp.concatenate([w_group, zw, w_expert, zw2], axis=1)
    wh = wr.astype(jnp.bfloat16)
    wl = (wr - wh.astype(jnp.float32)).astype(jnp.bfloat16)
    wr = jnp.concatenate([wh, wl], axis=1)
    brt = jnp.concatenate([b_group, jnp.zeros((SUBLANES - N_GROUPS,), jnp.float32), b_expert])[:, None]
    return wr, brt


def _moe(layer, h, xrows, ri, rf, cnt, w_gate_up, w_down, gfin, final_norm):
    T = h.shape[0]
    tm = TM_EXPERT
    capacity = -(-(2 * T) // tm) * tm + N_EXPERTS * tm
    counts = cnt[:, 0].astype(jnp.int32)
    ntile = (counts + tm - 1) // tm
    tend = jnp.cumsum(ntile)
    pstart = (tend - ntile) * tm
    n_used = tend[-1:]
    padrow = jnp.concatenate([jnp.where(ntile > 0, (tend - 1) * (tm * XS_CHUNKS), -1), n_used]
                             ).astype(jnp.int32)

    dst = _slot_rows_call(ri, pstart)
    xs = _dispatch_call(dst[0], dst[1], padrow, xrows, capacity)
    ys = _expert_call(layer, (tend - ntile).astype(jnp.int32), ntile.astype(jnp.int32), counts, xs,
                      w_gate_up, w_down, capacity)
    gates = rf[:2].T
    return _combine_call(dst[0], dst[1], ys, h, gates, gfin, final_norm)


def kernel(x, norm_mix, norm_ffn, final_norm, rel_bias, attn_w_qkv, attn_b_qkv, attn_w_o, attn_b_o,
           attn_sinks, conv_w_in, conv_w, conv_w_out, moe_w_group, moe_b_group, moe_w_expert,
           moe_b_expert, moe_w_gate_up, moe_w_down):
    B, S, D = x.shape
    T = B * S
    x2 = x.reshape(T, D)
    nq = N_Q_HEADS * HEAD_DIM
    scale = HEAD_DIM ** -0.5 * LOG2_E

    wqkv = attn_w_qkv[0]
    bqkv = attn_b_qkv[0]
    w_all = jnp.concatenate([_group_heads_last(wqkv[:, :nq]) * scale, wqkv[:, nq:]],
                            axis=1).astype(jnp.bfloat16)
    b_all = jnp.concatenate([_group_heads_last(bqkv[:nq]) * scale, bqkv[nq:]])[None, :]
    q, kt, v = _qkv_call(x2, norm_mix[0][None, :], w_all, b_all)
    a = _attn_call(q, kt, v, _rel_bias_blocks(rel_bias, attn_sinks[0]), B, S)
    wr, brt = _router_weights(moe_w_group[0], moe_b_group[0], moe_w_expert[0], moe_b_expert[0])
    w_o = _group_heads_last(attn_w_o[0].T).T.astype(jnp.bfloat16)
    h, xrows, ri, rf, cnt = _attn_out_call(
        a, w_o, attn_b_o[0][None, :], x2, norm_ffn[0][None, :], wr, brt)
    h = _moe(0, h, xrows, ri, rf, cnt, moe_w_gate_up, moe_w_down, final_norm[None, :], False)

    wr, brt = _router_weights(moe_w_group[1], moe_b_group[1], moe_w_expert[1], moe_b_expert[1])
    h, xrows, ri, rf, cnt = _conv_mixer_call(
        h, norm_mix[1][None, :], conv_w_in[0].astype(jnp.bfloat16), conv_w[0],
        conv_w_out[0].astype(jnp.bfloat16), norm_ffn[1][None, :], wr, brt, S)
    out = _moe(1, h, xrows, ri, rf, cnt, moe_w_gate_up, moe_w_down, final_norm[None, :], True)
    return out.reshape(B, S, D)
```

```python
import functools
import math

import numpy as np
import jax
import jax.numpy as jnp
from jax import lax
from jax.experimental import pallas as pl
from jax.experimental.pallas import tpu as pltpu

D_MODEL = 1024
N_Q_HEADS = 16
N_KV_HEADS = 2
HEAD_DIM = 64
GROUP = N_Q_HEADS // N_KV_HEADS
WINDOW = 128
ATTN_BLOCK = 128
N_REL_BUCKETS = 32
REL_MAX_DISTANCE = 128
CONV_WIDTH = 3
N_GROUPS = 4
EXPERTS_PER_GROUP = 8
N_EXPERTS = N_GROUPS * EXPERTS_PER_GROUP
EXPERT_FF = 512
RMS_EPS = 1e-5
LOG2_E = math.log2(math.e)

LANES = 128
SUBLANES = 8
ROW_CHUNKS = D_MODEL // LANES
XS_CHUNKS = ROW_CHUNKS // 2
VMEM_LIMIT = 56 * 1024 * 1024

TM_DENSE = 1024
TM_QKV = 2048
TQ = 2048
TM_MIX1 = 1024
TM_EXPERT = 512
TM_DISPATCH = 4096
TM_COMBINE = 256
DMA_UNROLL = 8
TM_SLOT_ROWS = 4096
EXPERT_BUFS = 4
EXPERT_TAIL_PARTS = 4
GATHER_QUEUES = ((0, 1), (1, 0), (0, 1), (1, 0), (0, 1), (1, 0), (0, 0), (0, 0))
ROUTER_ROWS = 40


def _rms(x, g):
    return x * lax.rsqrt(jnp.mean(x * x, axis=-1, keepdims=True) + RMS_EPS) * g


def _pack_bf16_pairs(x):
    bits = lax.bitcast_convert_type(x.astype(jnp.bfloat16).astype(jnp.float32), jnp.uint32)
    return [(bits[:, (2 * c) * LANES:(2 * c + 1) * LANES] >> 16)
            | bits[:, (2 * c + 1) * LANES:(2 * c + 2) * LANES] for c in range(XS_CHUNKS)]


def _unpack_bf16_pairs(words):
    cols = []
    for w in words:
        cols.append(lax.bitcast_convert_type(w << 16, jnp.float32))
        cols.append(lax.bitcast_convert_type(w & jnp.uint32(0xFFFF0000), jnp.float32))
    return jnp.concatenate(cols, axis=-1)


def _params(sem):
    return pltpu.CompilerParams(dimension_semantics=sem, vmem_limit_bytes=VMEM_LIMIT)


def _qkv_kernel(x_ref, g_ref, w_ref, b_ref, q_ref, kt_ref, v_ref):
    xn = _rms(x_ref[...], g_ref[...]).astype(jnp.bfloat16)
    out = jnp.dot(xn, w_ref[...], preferred_element_type=jnp.float32) + b_ref[...]
    nq = N_Q_HEADS * HEAD_DIM
    nk = N_KV_HEADS * HEAD_DIM
    q_ref[...] = out[:, :nq].astype(jnp.bfloat16)
    kt_ref[...] = out[:, nq:nq + nk].T.astype(jnp.bfloat16)
    v_ref[...] = out[:, nq + nk:].astype(jnp.bfloat16)


def _qkv_call(x2, g, w, b):
    T = x2.shape[0]
    nq = N_Q_HEADS * HEAD_DIM
    nk = N_KV_HEADS * HEAD_DIM
    tm = TM_QKV
    return pl.pallas_call(
        _qkv_kernel,
        grid=(T // tm,),
        in_specs=[
            pl.BlockSpec((tm, D_MODEL), lambda i: (i, 0)),
            pl.BlockSpec((1, D_MODEL), lambda i: (0, 0)),
            pl.BlockSpec((D_MODEL, nq + 2 * nk), lambda i: (0, 0)),
            pl.BlockSpec((1, nq + 2 * nk), lambda i: (0, 0)),
        ],
        out_specs=[
            pl.BlockSpec((tm, nq), lambda i: (i, 0)),
            pl.BlockSpec((nk, tm), lambda i: (0, i)),
            pl.BlockSpec((tm, nk), lambda i: (i, 0)),
        ],
        out_shape=[
            jax.ShapeDtypeStruct((T, nq), jnp.bfloat16),
            jax.ShapeDtypeStruct((nk, T), jnp.bfloat16),
            jax.ShapeDtypeStruct((T, nk), jnp.bfloat16),
        ],
        compiler_params=_params(("parallel",)),
        name="qkv_proj",
    )(x2, g, w, b)


def _attn_kernel(q_ref, kt_ref, ktp_ref, v_ref, vp_ref, bias_ref, o_ref):
    first_tile = pl.program_id(1) == 0
    nblk = TQ // ATTN_BLOCK
    qi_io = lax.broadcasted_iota(jnp.int32, (ATTN_BLOCK, 2 * ATTN_BLOCK), 0)
    kj_io = lax.broadcasted_iota(jnp.int32, (ATTN_BLOCK, 2 * ATTN_BLOCK), 1)
    dist = qi_io + ATTN_BLOCK - kj_io
    band = (dist >= 0) & (dist < WINDOW)
    sink_col = kj_io == 0
    kt_keep = lax.broadcasted_iota(jnp.int32, (N_KV_HEADS * HEAD_DIM, 2 * ATTN_BLOCK), 1) > 0
    v_keep = lax.broadcasted_iota(jnp.int32, (2 * ATTN_BLOCK, N_KV_HEADS * HEAD_DIM), 0) > 0
    low_half = lax.broadcasted_iota(jnp.int32, (ATTN_BLOCK, LANES), 1) < HEAD_DIM
    for qi in range(nblk):
        r0 = qi * ATTN_BLOCK
        if qi == 0:
            kt_blk = jnp.concatenate([ktp_ref[...], kt_ref[:, 0:ATTN_BLOCK]], axis=1)
            v_blk = jnp.concatenate([vp_ref[...], v_ref[0:ATTN_BLOCK, :]], axis=0)
            mask = (band & (jnp.logical_not(first_tile) | (kj_io >= ATTN_BLOCK))) | sink_col
        else:
            kt_blk = kt_ref[:, r0 - ATTN_BLOCK:r0 + ATTN_BLOCK]
            v_blk = v_ref[r0 - ATTN_BLOCK:r0 + ATTN_BLOCK, :]
            mask = band | sink_col
        kt_blk = jnp.where(kt_keep, kt_blk, jnp.zeros_like(kt_blk))
        v_blk = jnp.where(v_keep, v_blk, jnp.zeros_like(v_blk))
        for m in range(GROUP):
            qg = q_ref[r0:r0 + ATTN_BLOCK, m * LANES:(m + 1) * LANES]
            halves = []
            for half in range(N_KV_HEADS):
                h = m + GROUP * half
                keep = low_half if half == 0 else jnp.logical_not(low_half)
                qpad = jnp.where(keep, qg, jnp.zeros_like(qg))
                s = jnp.dot(qpad, kt_blk, preferred_element_type=jnp.float32)
                s = jnp.where(mask, s + bias_ref[h], -1e30)
                mx = jnp.max(s, axis=-1, keepdims=True)
                p = jnp.exp2(s - mx)
                den = jnp.sum(p, axis=-1, keepdims=True)
                pv = jnp.dot(p.astype(jnp.bfloat16), v_blk, preferred_element_type=jnp.float32)
                halves.append(pv * (1.0 / den))
            og = jnp.where(low_half, halves[0], halves[1])
            o_ref[r0:r0 + ATTN_BLOCK, m * LANES:(m + 1) * LANES] = og.astype(jnp.bfloat16)


def _attn_call(q, kt, v, bias, batch, seq):
    T = q.shape[0]
    nq = N_Q_HEADS * HEAD_DIM
    nk = N_KV_HEADS * HEAD_DIM
    tiles = seq // TQ
    per = TQ // ATTN_BLOCK

    def cur(b, j):
        return b * tiles + j

    def prev(b, j):
        return jnp.maximum((b * tiles + j) * per - 1, b * tiles * per)

    return pl.pallas_call(
        _attn_kernel,
        grid=(batch, tiles),
        in_specs=[
            pl.BlockSpec((TQ, nq), lambda b, j: (cur(b, j), 0)),
            pl.BlockSpec((nk, TQ), lambda b, j: (0, cur(b, j))),
            pl.BlockSpec((nk, ATTN_BLOCK), lambda b, j: (0, prev(b, j))),
            pl.BlockSpec((TQ, nk), lambda b, j: (cur(b, j), 0)),
            pl.BlockSpec((ATTN_BLOCK, nk), lambda b, j: (prev(b, j), 0)),
            pl.BlockSpec((N_Q_HEADS, ATTN_BLOCK, 2 * ATTN_BLOCK), lambda b, j: (0, 0, 0)),
        ],
        out_specs=pl.BlockSpec((TQ, nq), lambda b, j: (cur(b, j), 0)),
        out_shape=jax.ShapeDtypeStruct((T, nq), jnp.bfloat16),
        compiler_params=_params(("parallel", "parallel")),
        name="swa_attention",
    )(q, kt, kt, v, v, bias)


def _router_epilogue(h, gffn_ref, wr_ref, brt_ref, carry_ref,
                     h_out_ref, xrows_ref, ri_ref, rf_ref, cnt_ref):
    tm = h.shape[0]
    h_out_ref[...] = h
    xn = _rms(h, gffn_ref[...])
    xh = xn.astype(jnp.bfloat16)
    for c, w in enumerate(_pack_bf16_pairs(xn)):
        xrows_ref[pl.ds(c, tm, stride=XS_CHUNKS), :] = w

    xl = (xn - xh.astype(jnp.float32)).astype(jnp.bfloat16)
    wcat = wr_ref[...]
    prod = jnp.dot(xh, wcat, preferred_element_type=jnp.float32)
    logits = (prod[:, :LANES] + prod[:, LANES:]) + jnp.dot(
        xl, wcat[:, :LANES], preferred_element_type=jnp.float32)
    lt = logits.T[:ROUTER_ROWS, :] + brt_ref[...]

    gl = [lt[g:g + 1, :] for g in range(N_GROUPS)]
    gmax = functools.reduce(jnp.maximum, gl)
    gexp = [jnp.exp(x - gmax) for x in gl]
    gsum = functools.reduce(lambda a, b: a + b, gexp)
    gprob = [x / gsum for x in gexp]
    g_prob = functools.reduce(jnp.maximum, gprob)
    g_idx = jnp.full(g_prob.shape, N_GROUPS - 1, jnp.int32)
    for g in range(N_GROUPS - 2, -1, -1):
        g_idx = jnp.where(gprob[g] == g_prob, g, g_idx)

    el = []
    for j in range(EXPERTS_PER_GROUP):
        x = lt[SUBLANES + j:SUBLANES + j + 1, :]
        for g in range(1, N_GROUPS):
            r = SUBLANES + g * EXPERTS_PER_GROUP + j
            x = jnp.where(g_idx == g, lt[r:r + 1, :], x)
        el.append(x)
    emax = functools.reduce(jnp.maximum, el)
    eexp = [jnp.exp(x - emax) for x in el]
    esum = functools.reduce(lambda a, b: a + b, eexp)
    eprob = [x / esum for x in eexp]
    p1 = functools.reduce(jnp.maximum, eprob)
    i1 = jnp.full(p1.shape, EXPERTS_PER_GROUP - 1, jnp.int32)
    for j in range(EXPERTS_PER_GROUP - 2, -1, -1):
        i1 = jnp.where(eprob[j] == p1, j, i1)
    rest = [jnp.where(i1 == j, -1.0, eprob[j]) for j in range(EXPERTS_PER_GROUP)]
    p2 = functools.reduce(jnp.maximum, rest)
    i2 = jnp.full(p2.shape, EXPERTS_PER_GROUP - 1, jnp.int32)
    for j in range(EXPERTS_PER_GROUP - 2, -1, -1):
        i2 = jnp.where(rest[j] == p2, j, i2)
    psum = p1 + p2
    gate0 = g_prob * (p1 / psum)
    gate1 = g_prob * (p2 / psum)
    e0 = g_idx * EXPERTS_PER_GROUP + i1
    e1 = g_idx * EXPERTS_PER_GROUP + i2

    eio = lax.broadcasted_iota(jnp.int32, (N_EXPERTS, tm), 0)
    oh0 = (eio == e0).astype(jnp.float32)
    oh1 = (eio == e1).astype(jnp.float32)
    both = oh0 + oh1
    tr = lax.broadcasted_iota(jnp.int32, (tm, tm), 0)
    tc = lax.broadcasted_iota(jnp.int32, (tm, tm), 1)
    upper = (tr < tc).astype(jnp.bfloat16)
    before = jnp.dot(both.astype(jnp.bfloat16), upper, preferred_element_type=jnp.float32)
    before = before + carry_ref[...]
    rank0 = jnp.sum(oh0 * before, axis=0, keepdims=True)
    rank1 = jnp.sum(oh1 * (before + oh0), axis=0, keepdims=True)
    carry_ref[...] = carry_ref[...] + jnp.sum(both, axis=1, keepdims=True)
    cnt_ref[...] = jnp.broadcast_to(carry_ref[...], cnt_ref.shape)

    zi = jnp.zeros_like(e0)
    ri_ref[...] = jnp.concatenate(
        [e0, e1, rank0.astype(jnp.int32), rank1.astype(jnp.int32), zi, zi, zi, zi], axis=0)
    zf = jnp.zeros_like(gate0)
    rf_ref[...] = jnp.concatenate([gate0, gate1, zf, zf, zf, zf, zf, zf], axis=0)


def _epilogue_out_specs(tm):
    return [
        pl.BlockSpec((tm, D_MODEL), lambda i: (i, 0)),
        pl.BlockSpec((tm * XS_CHUNKS, LANES), lambda i: (i, 0)),
        pl.BlockSpec((SUBLANES, tm), lambda i: (0, i)),
        pl.BlockSpec((SUBLANES, tm), lambda i: (0, i)),
        pl.BlockSpec((N_EXPERTS, LANES), lambda i: (0, 0)),
    ]


def _epilogue_out_shapes(T):
    return [
        jax.ShapeDtypeStruct((T, D_MODEL), jnp.float32),
        jax.ShapeDtypeStruct((T * XS_CHUNKS, LANES), jnp.uint32),
        jax.ShapeDtypeStruct((SUBLANES, T), jnp.int32),
        jax.ShapeDtypeStruct((SUBLANES, T), jnp.float32),
        jax.ShapeDtypeStruct((N_EXPERTS, LANES), jnp.float32),
    ]


def _epilogue_in_specs():
    return [
        pl.BlockSpec((1, D_MODEL), lambda i: (0, 0)),
        pl.BlockSpec((D_MODEL, 2 * LANES), lambda i: (0, 0)),
        pl.BlockSpec((ROUTER_ROWS, 1), lambda i: (0, 0)),
    ]


def _attn_out_kernel(a_ref, w_ref, b_ref, res_ref, gffn_ref, wr_ref, brt_ref,
                     h_out_ref, xrows_ref, ri_ref, rf_ref, cnt_ref, carry_ref):
    @pl.when(pl.program_id(0) == 0)
    def _():
        carry_ref[...] = jnp.zeros_like(carry_ref)

    mix = jnp.dot(a_ref[...], w_ref[...], preferred_element_type=jnp.float32) + b_ref[...]
    h = res_ref[...] + mix
    _router_epilogue(h, gffn_ref, wr_ref, brt_ref, carry_ref,
                     h_out_ref, xrows_ref, ri_ref, rf_ref, cnt_ref)


def _attn_out_call(a, w, b, res, gffn, wr, brt):
    T = a.shape[0]
    tm = TM_DENSE
    return pl.pallas_call(
        _attn_out_kernel,
        grid=(T // tm,),
        in_specs=[
            pl.BlockSpec((tm, D_MODEL), lambda i: (i, 0)),
            pl.BlockSpec((D_MODEL, D_MODEL), lambda i: (0, 0)),
            pl.BlockSpec((1, D_MODEL), lambda i: (0, 0)),
            pl.BlockSpec((tm, D_MODEL), lambda i: (i, 0)),
        ] + _epilogue_in_specs(),
        out_specs=_epilogue_out_specs(tm),
        out_shape=_epilogue_out_shapes(T),
        scratch_shapes=[pltpu.VMEM((N_EXPERTS, 1), jnp.float32)],
        compiler_params=_params(("arbitrary",)),
        name="attn_out_router",
    )(a, w, b, res, gffn, wr, brt)


def _conv_mixer_kernel(tiles_per_seq, x_ref, g_ref, win_ref, cw_ref, wout_ref,
                       gffn_ref, wr_ref, brt_ref,
                       h_out_ref, xrows_ref, ri_ref, rf_ref, cnt_ref, carry_ref, tail_ref):
    i = pl.program_id(0)
    tm = x_ref.shape[0]

    @pl.when(i == 0)
    def _():
        carry_ref[...] = jnp.zeros_like(carry_ref)

    @pl.when(i % tiles_per_seq == 0)
    def _():
        tail_ref[...] = jnp.zeros_like(tail_ref)

    x = x_ref[...]
    xn = _rms(x, g_ref[...]).astype(jnp.bfloat16)
    bcu = jnp.dot(xn, win_ref[...], preferred_element_type=jnp.float32)
    b_gate = bcu[:, :D_MODEL]
    z = bcu[:, D_MODEL:2 * D_MODEL] * bcu[:, 2 * D_MODEL:]
    row = lax.broadcasted_iota(jnp.int32, (tm, D_MODEL), 0)
    tail = tail_ref[...]
    t1 = tail[SUBLANES - 1:SUBLANES, :]
    t2 = tail[SUBLANES - 2:SUBLANES - 1, :]
    z1 = jnp.where(row == 0, t1, pltpu.roll(z, 1, axis=0))
    z2 = jnp.where(row == 0, t2, jnp.where(row == 1, t1, pltpu.roll(z, 2, axis=0)))
    tail_ref[...] = z[tm - SUBLANES:, :]
    conv = z2 * cw_ref[0:1, :]
    conv = conv + z1 * cw_ref[1:2, :]
    conv = conv + z * cw_ref[2:3, :]
    gated = (b_gate * conv).astype(jnp.bfloat16)
    h = x + jnp.dot(gated, wout_ref[...], preferred_element_type=jnp.float32)
    _router_epilogue(h, gffn_ref, wr_ref, brt_ref, carry_ref,
                     h_out_ref, xrows_ref, ri_ref, rf_ref, cnt_ref)


def _conv_mixer_call(x2, g, win, cw, wout, gffn, wr, brt, seq):
    T = x2.shape[0]
    tm = TM_MIX1
    return pl.pallas_call(
        functools.partial(_conv_mixer_kernel, seq // tm),
        grid=(T // tm,),
        in_specs=[
            pl.BlockSpec((tm, D_MODEL), lambda i: (i, 0)),
            pl.BlockSpec((1, D_MODEL), lambda i: (0, 0)),
            pl.BlockSpec((D_MODEL, 3 * D_MODEL), lambda i: (0, 0)),
            pl.BlockSpec((CONV_WIDTH, D_MODEL), lambda i: (0, 0)),
            pl.BlockSpec((D_MODEL, D_MODEL), lambda i: (0, 0)),
        ] + _epilogue_in_specs(),
        out_specs=_epilogue_out_specs(tm),
        out_shape=_epilogue_out_shapes(T),
        scratch_shapes=[pltpu.VMEM((N_EXPERTS, 1), jnp.float32),
                        pltpu.VMEM((SUBLANES, D_MODEL), jnp.float32)],
        compiler_params=_params(("arbitrary",)),
        name="conv_mixer_router",
    )(x2, g, win, cw, wout, gffn, wr, brt)


def _row_copy(src_hbm, src_row, dst_hbm, dst_row, sem, chunks):
    return pltpu.make_async_copy(
        src_hbm.at[pl.ds(pl.multiple_of(src_row, chunks), chunks)],
        dst_hbm.at[pl.ds(pl.multiple_of(dst_row, chunks), chunks)], sem)


def _slot_rows_kernel(ri_ref, pstart_ref, o_ref):
    tb = ri_ref.shape[1]
    eio = lax.broadcasted_iota(jnp.int32, (N_EXPERTS, tb), 0)
    pstart = pstart_ref[...]
    slots = []
    for k in range(2):
        start = jnp.sum(jnp.where(eio == ri_ref[k:k + 1, :], pstart, 0), axis=0, keepdims=True)
        slots.append(start + ri_ref[2 + k:3 + k, :])
    zi = jnp.zeros_like(slots[0])
    o_ref[...] = jnp.concatenate(
        [s * XS_CHUNKS for s in slots] + [zi] * (SUBLANES - 2), axis=0)


def _slot_rows_call(ri, pstart):
    T = ri.shape[1]
    tb = TM_SLOT_ROWS
    return pl.pallas_call(
        _slot_rows_kernel,
        grid=(T // tb,),
        in_specs=[pl.BlockSpec((SUBLANES, tb), lambda i: (0, i)),
                  pl.BlockSpec((N_EXPERTS, 1), lambda i: (0, 0))],
        out_specs=pl.BlockSpec((SUBLANES, tb), lambda i: (0, i)),
        out_shape=jax.ShapeDtypeStruct((SUBLANES, T), jnp.int32),
        compiler_params=_params(("parallel",)),
        name="moe_slot_rows",
    )(ri, pstart[:, None])


def _dispatch_kernel(d0_ref, d1_ref, padrow_ref, x_ref, xs_hbm, zero_ref, sem, zsem):
    i = pl.program_id(0)
    tile_rows = TM_EXPERT * XS_CHUNKS

    @pl.when(i == 0)
    def _():
        zero_ref[...] = jnp.zeros_like(zero_ref)

        def zcopy(e):
            return pltpu.make_async_copy(
                zero_ref,
                xs_hbm.at[pl.ds(pl.multiple_of(padrow_ref[e], tile_rows), tile_rows)], zsem)

        def zstart(e, c):
            @pl.when(padrow_ref[e] >= 0)
            def _():
                zcopy(e).start()
            return c

        def zwait(e, c):
            @pl.when(padrow_ref[e] >= 0)
            def _():
                zcopy(e).wait()
            return c

        lax.fori_loop(0, N_EXPERTS, zstart, 0)
        lax.fori_loop(0, N_EXPERTS, zwait, 0)

        def tcopy(b):
            return pltpu.make_async_copy(
                zero_ref, xs_hbm.at[pl.ds(pl.multiple_of(b * tile_rows, tile_rows), tile_rows)], zsem)

        n_used = padrow_ref[N_EXPERTS]
        n_blk = xs_hbm.shape[0] // tile_rows
        lax.fori_loop(n_used, n_blk, lambda b, c: (tcopy(b).start(), c)[1], 0)
        lax.fori_loop(n_used, n_blk, lambda b, c: (tcopy(b).wait(), c)[1], 0)

    def issue(t, c):
        src = t * XS_CHUNKS
        _row_copy(x_ref, src, xs_hbm, d0_ref[t], sem, XS_CHUNKS).start(priority=0)
        _row_copy(x_ref, src, xs_hbm, d1_ref[t], sem, XS_CHUNKS).start(priority=1)
        return c

    lax.fori_loop(0, TM_DISPATCH, issue, 0, unroll=DMA_UNROLL)
    for _ in range(2):
        pltpu.make_async_copy(x_ref, xs_hbm.at[pl.ds(0, TM_DISPATCH * XS_CHUNKS)], sem).wait()


def _dispatch_call(d0, d1, padrow, xrows, capacity):
    T = d0.shape[0]
    return pl.pallas_call(
        _dispatch_kernel,
        grid=(T // TM_DISPATCH,),
        in_specs=[
            pl.BlockSpec((TM_DISPATCH,), lambda i: (i,), memory_space=pltpu.SMEM),
            pl.BlockSpec((TM_DISPATCH,), lambda i: (i,), memory_space=pltpu.SMEM),
            pl.BlockSpec(memory_space=pltpu.SMEM),
            pl.BlockSpec((TM_DISPATCH * XS_CHUNKS, LANES), lambda i: (i, 0)),
        ],
        out_specs=pl.BlockSpec(memory_space=pl.ANY),
        out_shape=jax.ShapeDtypeStruct((capacity * XS_CHUNKS, LANES), jnp.uint32),
        scratch_shapes=[pltpu.VMEM((TM_EXPERT * XS_CHUNKS, LANES), jnp.uint32),
                        pltpu.SemaphoreType.DMA, pltpu.SemaphoreType.DMA],
        compiler_params=_params(("arbitrary",)),
        name="moe_dispatch",
    )(d0, d1, padrow, xrows)


def _expert_kernel(tstart_ref, ntile_ref, cnt_ref, xs_hbm, wgu_ref, wd_ref, ys_hbm,
                   xbuf, ybuf, wgu_bf, wd_bf, sem_in, sem_out):
    nb = EXPERT_BUFS
    e = pl.program_id(0)
    tm = TM_EXPERT
    tile_rows = tm * XS_CHUNKS
    t0 = tstart_ref[e]
    nt = ntile_ref[e]
    n_used = tstart_ref[N_EXPERTS - 1] + ntile_ref[N_EXPERTS - 1]

    def tile_at(ref, g, rows):
        return ref.at[pl.ds(pl.multiple_of(g * rows, rows), rows)]

    def in_copy(g, slot):
        return pltpu.make_async_copy(tile_at(xs_hbm, g, tile_rows), xbuf.at[slot], sem_in.at[slot])

    def out_copy(g, slot):
        return pltpu.make_async_copy(ybuf.at[slot], tile_at(ys_hbm, g, tile_rows), sem_out.at[slot])

    @pl.when(e == 0)
    def _():
        for k in range(nb - 1):
            @pl.when(k < n_used)
            def _():
                in_copy(k, k).start(priority=1)

    @pl.when(nt > 0)
    def _():
        wgu_bf[...] = wgu_ref[0, 0].astype(jnp.bfloat16)
        wd_bf[...] = wd_ref[0, 0].astype(jnp.bfloat16)

    def tile_body(g, carry):
        slot = g % nb
        in_copy(g, slot).wait()

        @pl.when(g + (nb - 1) < n_used)
        def _():
            in_copy(g + (nb - 1), (g + (nb - 1)) % nb).start(priority=1)

        @pl.when(g >= nb)
        def _():
            out_copy(g - nb, slot).wait()

        def swiglu_rows(rows):
            x = _unpack_bf16_pairs(
                [xbuf[slot, pl.ds(c, rows, stride=XS_CHUNKS), :] for c in range(XS_CHUNKS)])
            gu = jnp.dot(x.astype(jnp.bfloat16), wgu_bf[...], preferred_element_type=jnp.float32)
            g_act = gu[:, :EXPERT_FF]
            u = gu[:, EXPERT_FF:]
            act = (g_act / (1.0 + jnp.exp(-g_act))) * u
            y = jnp.dot(act.astype(jnp.bfloat16), wd_bf[...], preferred_element_type=jnp.float32)
            for c, w in enumerate(_pack_bf16_pairs(y)):
                ybuf[slot, pl.ds(c, rows, stride=XS_CHUNKS), :] = w

        quarter = tm // EXPERT_TAIL_PARTS
        live_rows = cnt_ref[e] - (g - t0) * tm
        for q in range(1, EXPERT_TAIL_PARTS + 1):
            rows = q * quarter
            in_part = live_rows > rows - quarter
            if q < EXPERT_TAIL_PARTS:
                in_part = in_part & (live_rows <= rows)

            @pl.when(in_part)
            def _():
                swiglu_rows(rows)
                if rows < tm:
                    ybuf[slot, pl.ds(rows * XS_CHUNKS, (tm - rows) * XS_CHUNKS), :] = jnp.zeros(
                        ((tm - rows) * XS_CHUNKS, LANES), ybuf.dtype)

        out_copy(g, slot).start(priority=1)
        return carry

    lax.fori_loop(t0, t0 + nt, tile_body, 0)

    @pl.when(e == N_EXPERTS - 1)
    def _():
        for k in range(1, nb + 1):
            @pl.when(n_used >= k)
            def _():
                out_copy(n_used - k, (n_used - k) % nb).wait()

        n_blk = ys_hbm.shape[0] // tile_rows
        ybuf[0] = jnp.zeros(ybuf.shape[1:], ybuf.dtype)
        lax.fori_loop(n_used, n_blk, lambda b, c: (out_copy(b, 0).start(), c)[1], 0)
        lax.fori_loop(n_used, n_blk, lambda b, c: (out_copy(b, 0).wait(), c)[1], 0)


def _expert_call(layer, tstart, ntile, counts, xs, wgu, wd, capacity):
    tile_rows = TM_EXPERT * XS_CHUNKS

    def w_map(e, ts, nt, cn):
        return (layer, e, 0, 0)

    return pl.pallas_call(
        _expert_kernel,
        grid_spec=pltpu.PrefetchScalarGridSpec(
            num_scalar_prefetch=3,
            grid=(N_EXPERTS,),
            in_specs=[
                pl.BlockSpec(memory_space=pl.ANY),
                pl.BlockSpec((1, 1, D_MODEL, 2 * EXPERT_FF), w_map),
                pl.BlockSpec((1, 1, EXPERT_FF, D_MODEL), w_map),
            ],
            out_specs=pl.BlockSpec(memory_space=pl.ANY),
            scratch_shapes=[pltpu.VMEM((EXPERT_BUFS, tile_rows, LANES), jnp.uint32),
                            pltpu.VMEM((EXPERT_BUFS, tile_rows, LANES), jnp.uint32),
                            pltpu.VMEM((D_MODEL, 2 * EXPERT_FF), jnp.bfloat16),
                            pltpu.VMEM((EXPERT_FF, D_MODEL), jnp.bfloat16),
                            pltpu.SemaphoreType.DMA((EXPERT_BUFS,)),
                            pltpu.SemaphoreType.DMA((EXPERT_BUFS,))],
        ),
        out_shape=jax.ShapeDtypeStruct((capacity * XS_CHUNKS, LANES), jnp.uint32),
        compiler_params=_params(("arbitrary",)),
        name="moe_experts",
    )(tstart, ntile, counts, xs, wgu, wd)


def _gathered_residual(tm, d0_ref, d1_ref, d0n_ref, d1n_ref, ys_hbm, h_ref, gate_ref,
                       y0_ref, y1_ref, sem):
    i = pl.program_id(0)
    slot = i % 2

    def issue_tile(d0r, d1r, s):
        def issue(tg, c):
            for j, (q0, q1) in enumerate(GATHER_QUEUES):
                t = tg * len(GATHER_QUEUES) + j
                dst = t * XS_CHUNKS
                _row_copy(ys_hbm, d0r[t], y0_ref.at[s], dst, sem.at[s], XS_CHUNKS).start(priority=q0)
                _row_copy(ys_hbm, d1r[t], y1_ref.at[s], dst, sem.at[s], XS_CHUNKS).start(priority=q1)
            return c
        lax.fori_loop(0, tm // len(GATHER_QUEUES), issue, 0)

    @pl.when(i == 0)
    def _():
        issue_tile(d0_ref, d1_ref, 0)

    @pl.when(i + 1 < pl.num_programs(0))
    def _():
        issue_tile(d0n_ref, d1n_ref, 1 - slot)

    for buf in (y0_ref, y1_ref):
        pltpu.make_async_copy(ys_hbm.at[pl.ds(0, tm * XS_CHUNKS)], buf.at[slot], sem.at[slot]).wait()

    g0 = gate_ref[:, 0:1]
    g1 = gate_ref[:, 1:2]
    y0 = _unpack_bf16_pairs([y0_ref[slot, pl.ds(c, tm, stride=XS_CHUNKS), :] for c in range(XS_CHUNKS)])
    y1 = _unpack_bf16_pairs([y1_ref[slot, pl.ds(c, tm, stride=XS_CHUNKS), :] for c in range(XS_CHUNKS)])
    return h_ref[...] + (y0 * g0 + y1 * g1)


def _gather_in_specs(tm, n_tiles):
    last = n_tiles - 1
    return [
        pl.BlockSpec((tm,), lambda i: (i,), memory_space=pltpu.SMEM),
        pl.BlockSpec((tm,), lambda i: (i,), memory_space=pltpu.SMEM),
        pl.BlockSpec((tm,), lambda i: (jnp.minimum(i + 1, last),), memory_space=pltpu.SMEM),
        pl.BlockSpec((tm,), lambda i: (jnp.minimum(i + 1, last),), memory_space=pltpu.SMEM),
        pl.BlockSpec(memory_space=pl.ANY),
        pl.BlockSpec((tm, D_MODEL), lambda i: (i, 0)),
        pl.BlockSpec((tm, 2), lambda i: (i, 0)),
    ]


def _gather_scratch(tm):
    return [pltpu.VMEM((2, tm * XS_CHUNKS, LANES), jnp.uint32),
            pltpu.VMEM((2, tm * XS_CHUNKS, LANES), jnp.uint32),
            pltpu.SemaphoreType.DMA((2,))]


def _combine_kernel(final_norm, d0_ref, d1_ref, d0n_ref, d1n_ref, ys_hbm, h_ref, gate_ref, gfin_ref,
                    o_ref, y0_ref, y1_ref, sem):
    h = _gathered_residual(TM_COMBINE, d0_ref, d1_ref, d0n_ref, d1n_ref, ys_hbm, h_ref, gate_ref,
                           y0_ref, y1_ref, sem)
    if final_norm:
        h = _rms(h, gfin_ref[...])
    o_ref[...] = h


def _combine_call(d0, d1, ys, h, gates, gfin, final_norm):
    T = h.shape[0]
    tm = TM_COMBINE
    return pl.pallas_call(
        functools.partial(_combine_kernel, final_norm),
        grid=(T // tm,),
        in_specs=_gather_in_specs(tm, T // tm) + [pl.BlockSpec((1, D_MODEL), lambda i: (0, 0))],
        out_specs=pl.BlockSpec((tm, D_MODEL), lambda i: (i, 0)),
        out_shape=jax.ShapeDtypeStruct((T, D_MODEL), jnp.float32),
        scratch_shapes=_gather_scratch(tm),
        compiler_params=_params(("arbitrary",)),
        name="moe_combine",
    )(d0, d1, d0, d1, ys, h, gates, gfin)


def _group_heads_last(w):
    lead = w.shape[:-1]
    w = w.reshape(lead + (N_KV_HEADS, GROUP, HEAD_DIM))
    w = jnp.swapaxes(w, -3, -2)
    return w.reshape(lead + (N_Q_HEADS * HEAD_DIM,))


def _rel_bucket_table():
    qi = np.arange(ATTN_BLOCK)[:, None]
    kj = np.arange(2 * ATTN_BLOCK)[None, :]
    dist = np.maximum(qi + ATTN_BLOCK - kj, 0)
    max_exact = N_REL_BUCKETS // 2
    d = np.maximum(dist, max_exact).astype(np.float32)
    large = max_exact + (np.log(d / np.float32(max_exact)) / np.float32(math.log(REL_MAX_DISTANCE / max_exact))
                         * np.float32(N_REL_BUCKETS - max_exact)).astype(np.int32)
    large = np.minimum(large, N_REL_BUCKETS - 1)
    return np.where(dist < max_exact, dist, large).astype(np.int32)


def _rel_bias_blocks(rel_bias, sinks):
    onehot = (jnp.asarray(_rel_bucket_table())[..., None] == jnp.arange(N_REL_BUCKETS)).astype(jnp.float32)
    bias = jnp.einsum("qkb,bh->hqk", onehot, rel_bias.astype(jnp.float32),
                      precision=lax.Precision.HIGHEST)
    sink_col = jnp.broadcast_to(sinks.astype(jnp.float32)[:, None, None], (N_Q_HEADS, ATTN_BLOCK, 1))
    return jnp.concatenate([sink_col, bias[:, :, 1:]], axis=2) * LOG2_E


def _router_weights(w_group, b_group, w_expert, b_expert):
    zw = jnp.zeros((D_MODEL, SUBLANES - N_GROUPS), jnp.float32)
    zw2 = jnp.zeros((D_MODEL, LANES - SUBLANES - N_EXPERTS), jnp.float32)
    wr = jnp.concatenate([w_group, zw, w_expert, zw2], axis=1)
    wh = wr.astype(jnp.bfloat16)
    wl = (wr - wh.astype(jnp.float32)).astype(jnp.bfloat16)
    wr = jnp.concatenate([wh, wl], axis=1)
    brt = jnp.concatenate([b_group, jnp.zeros((SUBLANES - N_GROUPS,), jnp.float32), b_expert])[:, None]
    return wr, brt


def _moe(layer, h, xrows, ri, rf, cnt, w_gate_up, w_down, gfin, final_norm):
    T = h.shape[0]
    tm = TM_EXPERT
    capacity = -(-(2 * T) // tm) * tm + N_EXPERTS * tm
    counts = cnt[:, 0].astype(jnp.int32)
    ntile = (counts + tm - 1) // tm
    tend = jnp.cumsum(ntile)
    pstart = (tend - ntile) * tm
    n_used = tend[-1:]
    padrow = jnp.concatenate([jnp.where(ntile > 0, (tend - 1) * (tm * XS_CHUNKS), -1), n_used]
                             ).astype(jnp.int32)

    dst = _slot_rows_call(ri, pstart)
    xs = _dispatch_call(dst[0], dst[1], padrow, xrows, capacity)
    ys = _expert_call(layer, (tend - ntile).astype(jnp.int32), ntile.astype(jnp.int32), counts, xs,
                      w_gate_up, w_down, capacity)
    gates = rf[:2].T
    return _combine_call(dst[0], dst[1], ys, h, gates, gfin, final_norm)


def kernel(x, norm_mix, norm_ffn, final_norm, rel_bias, attn_w_qkv, attn_b_qkv, attn_w_o, attn_b_o,
           attn_sinks, conv_w_in, conv_w, conv_w_out, moe_w_group, moe_b_group, moe_w_expert,
           moe_b_expert, moe_w_gate_up, moe_w_down):
    B, S, D = x.shape
    T = B * S
    x2 = x.reshape(T, D)
    nq = N_Q_HEADS * HEAD_DIM
    scale = HEAD_DIM ** -0.5 * LOG2_E

    wqkv = attn_w_qkv[0]
    bqkv = attn_b_qkv[0]
    w_all = jnp.concatenate([_group_heads_last(wqkv[:, :nq]) * scale, wqkv[:, nq:]],
                            axis=1).astype(jnp.bfloat16)
    b_all = jnp.concatenate([_group_heads_last(bqkv[:nq]) * scale, bqkv[nq:]])[None, :]
    q, kt, v = _qkv_call(x2, norm_mix[0][None, :], w_all, b_all)
    a = _attn_call(q, kt, v, _rel_bias_blocks(rel_bias, attn_sinks[0]), B, S)
    wr, brt = _router_weights(moe_w_group[0], moe_b_group[0], moe_w_expert[0], moe_b_expert[0])
    w_o = _group_heads_last(attn_w_o[0].T).T.astype(jnp.bfloat16)
    h, xrows, ri, rf, cnt = _attn_out_call(
        a, w_o, attn_b_o[0][None, :], x2, norm_ffn[0][None, :], wr, brt)
    h = _moe(0, h, xrows, ri, rf, cnt, moe_w_gate_up, moe_w_down, final_norm[None, :], False)

    wr, brt = _router_weights(moe_w_group[1], moe_b_group[1], moe_w_expert[1], moe_b_expert[1])
    h, xrows, ri, rf, cnt = _conv_mixer_call(
        h, norm_mix[1][None, :], conv_w_in[0].astype(jnp.bfloat16), conv_w[0],
        conv_w_out[0].astype(jnp.bfloat16), norm_ffn[1][None, :], wr, brt, S)
    out = _moe(1, h, xrows, ri, rf, cnt, moe_w_gate_up, moe_w_down, final_norm[None, :], True)
    return out.reshape(B, S, D)
```
